```python
import jax
import jax.numpy as jnp
from jax import lax
import numpy as np

D_MODEL = 1024
BATCH = 4
SEQ = 4096
DEPTH = 2

MEM_LEN = 256
N_MIXERS = 4
GROUP_WIDTH = D_MODEL // N_MIXERS
HEAD_DIM = 64
GROUP_HEADS = GROUP_WIDTH // HEAD_DIM
Q_BLOCK = 128
RET_CHUNK = 128
RET_THETA = 10000.0
CONV_WIDTH = 4
LRU_C = 8.0
RWKV_W_RANK = 64
RWKV_A_RANK = 64
RWKV_V_RANK = 32
RWKV_G_RANK = 128
RWKV_GN_EPS = 64e-5
XATTN_HEADS = 4
XATTN_HEAD_DIM = D_MODEL // XATTN_HEADS
D_FF = 4 * D_MODEL
NORM_EPS = 1e-6

FOX_WIDTH = 3 * GROUP_WIDTH + GROUP_HEADS
LRU_WIDTH = 2 * GROUP_WIDTH
RWKV_WIDTH = 3 * GROUP_WIDTH + RWKV_W_RANK + RWKV_A_RANK + RWKV_G_RANK
RET_WIDTH = 4 * GROUP_WIDTH
FOX_OFF = 0
LRU_OFF = FOX_OFF + FOX_WIDTH
RWKV_OFF = LRU_OFF + LRU_WIDTH
RET_OFF = RWKV_OFF + RWKV_WIDTH
IN_WIDTH = RET_OFF + RET_WIDTH

kernel_name = 'hybrid_parallel_heads_decoder'

F32 = jnp.float32


def rms_norm(x, g):
    xf = x.astype(F32)
    y = xf * lax.rsqrt(jnp.mean(xf * xf, axis=-1, keepdims=True) + NORM_EPS)
    return (y * g.astype(F32)).astype(x.dtype)


def split_heads(t):
    return t.reshape(t.shape[0], t.shape[1], GROUP_HEADS, HEAD_DIM)


def head_layer_norm(y, w, b, eps):
    mu = jnp.mean(y, axis=-1, keepdims=True)
    var = jnp.mean(jnp.square(y - mu), axis=-1, keepdims=True)
    y = ((y - mu) * lax.rsqrt(var + eps)).reshape(y.shape[0], y.shape[1], -1)
    return y * w.astype(F32) + b.astype(F32)


def head_rms_norm(y, w):
    y = (y * lax.rsqrt(jnp.mean(y * y, axis=-1, keepdims=True) + NORM_EPS)).reshape(y.shape[0], y.shape[1], -1)
    return y * w.astype(F32)


def fox_mixer(q, k, v, f_logit, f_bias):
    B, S, _ = q.shape
    q = split_heads(q).transpose(0, 2, 1, 3)
    k = split_heads(k).transpose(0, 2, 1, 3)
    v = split_heads(v).transpose(0, 2, 1, 3)
    log_f = jax.nn.log_sigmoid(f_logit.astype(F32) + f_bias.astype(F32))
    cum = jnp.cumsum(log_f, axis=1).transpose(0, 2, 1)
    nb = S // Q_BLOCK
    q_blocks = q.reshape(B, GROUP_HEADS, nb, Q_BLOCK, HEAD_DIM).transpose(2, 0, 1, 3, 4)
    c_blocks = cum.reshape(B, GROUP_HEADS, nb, Q_BLOCK).transpose(2, 0, 1, 3)
    q_pos = jnp.arange(S).reshape(nb, Q_BLOCK)
    k_pos = jnp.arange(S)
    scale = HEAD_DIM ** -0.5

    def block(args):
        q_i, c_i, p_i = args
        s = jnp.einsum('bhqd,bhkd->bhqk', q_i, k).astype(F32) * scale
        s = s + c_i[..., :, None] - cum[..., None, :]
        s = jnp.where(k_pos[None, :] <= p_i[:, None], s, -jnp.inf)
        p = jax.nn.softmax(s, axis=-1)
        return jnp.einsum('bhqk,bhkd->bhqd', p.astype(v.dtype), v)

    o = lax.map(block, (q_blocks, c_blocks, q_pos))
    return o.transpose(1, 0, 3, 2, 4).reshape(B, S, GROUP_WIDTH)


def rglru_mixer(xb, yb, conv_w, conv_b, ra_w, ra_b, ri_w, ri_b, lam):
    B, S, W = xb.shape
    xp = jnp.pad(xb, ((0, 0), (CONV_WIDTH - 1, 0), (0, 0)))
    xc = conv_b + xp[:, 0:S] * conv_w[0]
    for j in range(1, CONV_WIDTH):
        xc = xc + xp[:, j:j + S] * conv_w[j]
    xh = split_heads(xc)
    r = jax.nn.sigmoid(jnp.einsum('bshi,hij->bshj', xh, ra_w).reshape(B, S, W) + ra_b)
    i = jax.nn.sigmoid(jnp.einsum('bshi,hij->bshj', xh, ri_w).reshape(B, S, W) + ri_b)
    log_a = LRU_C * r.astype(F32) * jax.nn.log_sigmoid(lam.astype(F32))
    a = jnp.exp(log_a)
    mult = jnp.sqrt(jnp.maximum(-jnp.expm1(2.0 * log_a), 0.0))
    u = mult * (i * xc).astype(F32)

    def combine(left, right):
        a1, b1 = left
        a2, b2 = right
        return a1 * a2, a2 * b1 + b2

    _, h = lax.associative_scan(combine, (a, u), axis=1)
    return (h * jax.nn.gelu(yb.astype(F32))).astype(xb.dtype)


def rwkv7_mixer(slab, v_first, mu, w0, w2, a0, a2, g2, k_k, k_a, r_k, gn_w, gn_b, vres):
    out_dtype = slab.dtype
    s = slab.astype(F32)
    B, S, _ = s.shape
    shifted = jnp.pad(s, ((0, 0), (1, 0), (0, 0)))[:, :-1]
    s = s + (shifted - s) * mu.astype(F32)
    G = GROUP_WIDTH
    r = s[..., 0:G]
    k = s[..., G:2 * G]
    v = s[..., 2 * G:3 * G]
    o = 3 * G
    wd = s[..., o:o + RWKV_W_RANK]
    o = o + RWKV_W_RANK
    ad = s[..., o:o + RWKV_A_RANK]
    o = o + RWKV_A_RANK
    gd = s[..., o:o + RWKV_G_RANK]
    w_log = -jax.nn.softplus(-(w0.astype(F32) + jnp.tanh(wd) @ w2.astype(F32))) - 0.5
    decay = jnp.exp(-jnp.exp(w_log))
    a = jax.nn.sigmoid(a0.astype(F32) + ad @ a2.astype(F32))
    g = jax.nn.sigmoid(gd) @ g2.astype(F32)
    if vres is not None:
        v0, v1, v2 = vres
        v = v + (v_first - v) * jax.nn.sigmoid(v0.astype(F32) + (v @ v1.astype(F32)) @ v2.astype(F32))
    v_out = v
    kk = split_heads(k * k_k.astype(F32))
    kk = kk / jnp.maximum(jnp.sqrt(jnp.sum(kk * kk, axis=-1, keepdims=True)), 1e-12)
    k = k * (1.0 + (a - 1.0) * k_a.astype(F32))
    rh, wh, kh, vh, ah = (split_heads(t) for t in (r, decay, k, v, a))
    a_vec = -kk
    b_vec = kk * ah

    def step(state, inp):
        r_t, w_t, k_t, v_t, a_t, b_t = inp
        sa = jnp.einsum('bhij,bhj->bhi', state, a_t)
        state = state * w_t[:, :, None, :] + sa[..., None] * b_t[:, :, None, :] + v_t[..., None] * k_t[:, :, None, :]
        return state, jnp.einsum('bhij,bhj->bhi', state, r_t)

    xs = tuple(t.transpose(1, 0, 2, 3) for t in (rh, wh, kh, vh, a_vec, b_vec))
    state0 = jnp.zeros((B, GROUP_HEADS, HEAD_DIM, HEAD_DIM), F32)
    _, y = lax.scan(step, state0, xs)
    y = head_layer_norm(y.transpose(1, 0, 2, 3), gn_w, gn_b, RWKV_GN_EPS)
    bonus = jnp.sum(rh * kh * r_k.astype(F32), axis=-1, keepdims=True) * vh
    y = y + bonus.reshape(B, S, G)
    return (y * g).astype(out_dtype), v_out


def rotary(t):
    half = HEAD_DIM // 2
    inv = 1.0 / (RET_THETA ** jnp.linspace(0.0, 1.0, half, dtype=F32))
    ang = jnp.arange(t.shape[2], dtype=F32)[:, None] * inv[None, :]
    cos, sin = jnp.cos(ang), jnp.sin(ang)
    t1, t2 = t[..., :half], t[..., half:]
    return jnp.concatenate([t1 * cos - t2 * sin, t1 * sin + t2 * cos], axis=-1)


def retention_mixer(q, k, v, g, gn_w):
    B, S, _ = q.shape
    H, C = GROUP_HEADS, RET_CHUNK
    nc = S // C
    qh = rotary(split_heads(q).astype(F32).transpose(0, 2, 1, 3))
    kh = rotary(split_heads(k).astype(F32).transpose(0, 2, 1, 3)) * (HEAD_DIM ** -0.5)
    vh = split_heads(v).astype(F32).transpose(0, 2, 1, 3)
    qc = qh.reshape(B, H, nc, C, HEAD_DIM)
    kc = kh.reshape(B, H, nc, C, HEAD_DIM)
    vc = vh.reshape(B, H, nc, C, HEAD_DIM)
    lg = jnp.log(1.0 - 2.0 ** (-5.0 - jnp.arange(H, dtype=F32)))
    n = jnp.arange(C, dtype=F32)
    diff = n[:, None] - n[None, :]
    dmat = jnp.where(diff >= 0, jnp.exp(lg[:, None, None] * jnp.maximum(diff, 0.0)), 0.0)
    inner = jnp.einsum('bhcnd,bhcmd->bhcnm', qc, kc) * dmat[None, :, None]
    intra = jnp.einsum('bhcnm,bhcme->bhcne', inner, vc)
    zeta = jnp.exp(lg[:, None] * (C - 1.0 - n)[None, :])
    kv = jnp.einsum('bhcmd,bhcme->bhcde', kc * zeta[None, :, None, :, None], vc)
    chunk_decay = jnp.exp(lg * C)[None, :, None, None]

    def step(R, kv_c):
        return R * chunk_decay + kv_c, R

    _, r_prev = lax.scan(step, jnp.zeros((B, H, HEAD_DIM, HEAD_DIM), F32), kv.transpose(2, 0, 1, 3, 4))
    xi = jnp.exp(lg[:, None] * (n + 1.0)[None, :])
    cross = jnp.einsum('bhcnd,cbhde->bhcne', qc * xi[None, :, None, :, None], r_prev)
    o = (intra + cross).transpose(0, 2, 3, 1, 4).reshape(B, S, H, HEAD_DIM)
    o = head_rms_norm(o, gn_w)
    return (o * jax.nn.silu(g.astype(F32))).astype(q.dtype)


def cross_attention(xn, memn, wq, wk, wv, wo):
    B, S, _ = xn.shape
    M = memn.shape[1]
    q = (xn @ wq).reshape(B, S, XATTN_HEADS, XATTN_HEAD_DIM)
    k = (memn @ wk).reshape(B, M, XATTN_HEADS, XATTN_HEAD_DIM)
    v = (memn @ wv).reshape(B, M, XATTN_HEADS, XATTN_HEAD_DIM)
    s = jnp.einsum('bshd,bmhd->bhsm', q, k).astype(F32) * (XATTN_HEAD_DIM ** -0.5)
    p = jax.nn.softmax(s, axis=-1)
    o = jnp.einsum('bhsm,bmhd->bshd', p.astype(v.dtype), v).reshape(B, S, D_MODEL)
    return o @ wo


def setup_inputs(seed: int = 0) -> dict:
    key = jax.random.key(seed)
    ks = iter(jax.random.split(key, 64))
    nrm = lambda shape, scale: jax.random.normal(next(ks), shape, F32) * scale
    gain = lambda shape: 1.0 + nrm(shape, 0.05)
    L, D, G, H, N = DEPTH, D_MODEL, GROUP_WIDTH, GROUP_HEADS, HEAD_DIM
    a8 = jax.random.uniform(next(ks), (L, G), F32, minval=0.9, maxval=0.999)
    a_lru = a8 ** (1.0 / LRU_C)
    return {
        'x': nrm((BATCH, SEQ, D), 1.0),
        'mem': nrm((BATCH, MEM_LEN, D), 1.0),
        'norm_mix_pre': gain((L, D)),
        'norm_mix_post': gain((L, D)),
        'norm_xa_pre': gain((L, D)),
        'norm_xa_post': gain((L, D)),
        'norm_mem': gain((L, D)),
        'norm_mlp_pre': gain((L, D)),
        'norm_mlp_post': gain((L, D)),
        'w_in': nrm((L, D, IN_WIDTH), D ** -0.5),
        'w_out': nrm((L, D, D), D ** -0.5),
        'fox_f_bias': 2.0 + nrm((L, H), 0.1),
        'lru_conv_w': nrm((L, CONV_WIDTH, G), CONV_WIDTH ** -0.5),
        'lru_conv_b': nrm((L, G), 0.01),
        'lru_ra_w': nrm((L, H, N, N), N ** -0.5),
        'lru_ra_b': nrm((L, G), 0.01),
        'lru_ri_w': nrm((L, H, N, N), N ** -0.5),
        'lru_ri_b': nrm((L, G), 0.01),
        'lru_lambda': jnp.log(a_lru) - jnp.log1p(-a_lru),
        'rwkv_mu': jax.random.uniform(next(ks), (L, RWKV_WIDTH), F32),
        'rwkv_w0': jnp.linspace(-6.5, -1.5, G, dtype=F32)[None, :] + nrm((L, G), 0.3),
        'rwkv_w2': nrm((L, RWKV_W_RANK, G), RWKV_W_RANK ** -0.5),
        'rwkv_a0': nrm((L, G), 0.1),
        'rwkv_a2': nrm((L, RWKV_A_RANK, G), RWKV_A_RANK ** -0.5),
        'rwkv_g2': nrm((L, RWKV_G_RANK, G), RWKV_G_RANK ** -0.5),
        'rwkv_k_k': 0.85 + nrm((L, G), 0.05),
        'rwkv_k_a': gain((L, G)),
        'rwkv_r_k': nrm((L, H, N), 0.1),
        'rwkv_gn_w': gain((L, G)),
        'rwkv_gn_b': nrm((L, G), 0.01),
        'rwkv_v0': nrm((L - 1, G), 0.1),
        'rwkv_v1': nrm((L - 1, G, RWKV_V_RANK), G ** -0.5),
        'rwkv_v2': nrm((L - 1, RWKV_V_RANK, G), RWKV_V_RANK ** -0.5),
        'ret_gn_w': gain((L, G)),
        'xa_wq': nrm((L, D, D), D ** -0.5),
        'xa_wk': nrm((L, D, D), D ** -0.5),
        'xa_wv': nrm((L, D, D), D ** -0.5),
        'xa_wo': nrm((L, D, D), D ** -0.5),
        'mlp_w1': nrm((L, D, D_FF), D ** -0.5),
        'mlp_w2': nrm((L, D_FF, D), D_FF ** -0.5),
    }


def reference(x, mem, norm_mix_pre, norm_mix_post, norm_xa_pre, norm_xa_post, norm_mem, norm_mlp_pre, norm_mlp_post,
              w_in, w_out, fox_f_bias, lru_conv_w, lru_conv_b, lru_ra_w, lru_ra_b, lru_ri_w, lru_ri_b, lru_lambda,
              rwkv_mu, rwkv_w0, rwkv_w2, rwkv_a0, rwkv_a2, rwkv_g2, rwkv_k_k, rwkv_k_a, rwkv_r_k, rwkv_gn_w, rwkv_gn_b,
              rwkv_v0, rwkv_v1, rwkv_v2, ret_gn_w, xa_wq, xa_wk, xa_wv, xa_wo, mlp_w1, mlp_w2):
    G = GROUP_WIDTH
    v_first = None
    for l in range(DEPTH):
        h = rms_norm(x, norm_mix_pre[l])
        proj = h @ w_in[l]
        fox_out = fox_mixer(proj[..., FOX_OFF:FOX_OFF + G], proj[..., FOX_OFF + G:FOX_OFF + 2 * G],
                            proj[..., FOX_OFF + 2 * G:FOX_OFF + 3 * G], proj[..., FOX_OFF + 3 * G:FOX_OFF + FOX_WIDTH],
                            fox_f_bias[l])
        lru_out = rglru_mixer(proj[..., LRU_OFF:LRU_OFF + G], proj[..., LRU_OFF + G:LRU_OFF + 2 * G],
                              lru_conv_w[l], lru_conv_b[l], lru_ra_w[l], lru_ra_b[l], lru_ri_w[l], lru_ri_b[l],
                              lru_lambda[l])
        vres = None if l == 0 else (rwkv_v0[l - 1], rwkv_v1[l - 1], rwkv_v2[l - 1])
        rwkv_out, v_l = rwkv7_mixer(proj[..., RWKV_OFF:RWKV_OFF + RWKV_WIDTH], v_first, rwkv_mu[l], rwkv_w0[l],
                                    rwkv_w2[l], rwkv_a0[l], rwkv_a2[l], rwkv_g2[l], rwkv_k_k[l], rwkv_k_a[l],
                                    rwkv_r_k[l], rwkv_gn_w[l], rwkv_gn_b[l], vres)
        if l == 0:
            v_first = v_l
        ret_out = retention_mixer(proj[..., RET_OFF:RET_OFF + G], proj[..., RET_OFF + G:RET_OFF + 2 * G],
                                  proj[..., RET_OFF + 2 * G:RET_OFF + 3 * G], proj[..., RET_OFF + 3 * G:RET_OFF + 4 * G],
                                  ret_gn_w[l])
        mixed = jnp.concatenate([fox_out, lru_out, rwkv_out, ret_out], axis=-1) @ w_out[l]
        x = x + rms_norm(mixed, norm_mix_post[l])
        xa = cross_attention(rms_norm(x, norm_xa_pre[l]), rms_norm(mem, norm_mem[l]),
                             xa_wq[l], xa_wk[l], xa_wv[l], xa_wo[l])
        x = x + rms_norm(xa, norm_xa_post[l])
        hid = jnp.square(jax.nn.relu(rms_norm(x, norm_mlp_pre[l]) @ mlp_w1[l]))
        x = x + rms_norm(hid @ mlp_w2[l], norm_mlp_post[l])
    return x
```

```python
import functools
import math

import jax
import jax.numpy as jnp
import numpy as np
from jax import lax
from jax.experimental import pallas as pl
from jax.experimental.pallas import tpu as pltpu

F32 = jnp.float32
BF16 = jnp.bfloat16
HIGHEST = lax.Precision.HIGHEST

D_MODEL = 1024
GROUP = 256
HEADS = 4
HEAD_DIM = 64
CONV_WIDTH = 4
LRU_C = 8.0
RET_THETA = 10000.0
RET_CHUNK = 128
RWKV_W_RANK, RWKV_A_RANK, RWKV_G_RANK, RWKV_V_RANK = 64, 64, 128, 32
RWKV_GN_EPS = 64e-5
XATTN_HEADS = 4
XATTN_HEAD_DIM = D_MODEL // XATTN_HEADS
NORM_EPS = 1e-6
NEG_BIG = -1e30

FOX_Q, FOX_K, FOX_V, FOX_F = 0, 256, 512, 768
LRU_X, LRU_Y = 1024, 1280
RWKV_R, RWKV_K, RWKV_V, RWKV_LR = 1536, 1792, 2048, 2304
RET_Q, RET_K, RET_V, RET_G = 2560, 2816, 3072, 3328
IN_PAD = 3584
FOX_REAL = 3 * GROUP + HEADS

VMEM_LIMIT = 56 * 1024 * 1024
RWKV_CHUNK = 64


def _cparams(sem):
    return pltpu.CompilerParams(dimension_semantics=sem, vmem_limit_bytes=VMEM_LIMIT)


def _rms(x, g):
    return x * lax.rsqrt(jnp.mean(x * x, axis=-1, keepdims=True) + NORM_EPS) * g


def _log_sigmoid(x):
    return jnp.minimum(x, 0.0) - jnp.log1p(jnp.exp(-jnp.abs(x)))


def _sigmoid(x):
    return 1.0 / (1.0 + jnp.exp(-x))


def _dot(a, b, **kw):
    return jnp.dot(a, b, preferred_element_type=F32, **kw)


def _dot_nt(a, b, **kw):
    return lax.dot_general(a, b, (((1,), (1,)), ((), ())), preferred_element_type=F32, **kw)


def _dot_tn(a, b, **kw):
    return lax.dot_general(a, b, (((0,), (0,)), ((), ())), preferred_element_type=F32, **kw)


def _tri(n, strict=False):
    r = lax.broadcasted_iota(jnp.int32, (n, n), 0)
    c = lax.broadcasted_iota(jnp.int32, (n, n), 1)
    return (r > c) if strict else (r >= c)


def _norm_matmul_kernel(x_ref, g_ref, w_ref, o_ref, xn_ref):
    @pl.when(pl.program_id(1) == 0)
    def _():
        xn_ref[...] = _rms(x_ref[...], g_ref[...]).astype(BF16)

    o_ref[...] = _dot(xn_ref[...], w_ref[...]).astype(o_ref.dtype)


def norm_matmul(x, g, w, *, tm, tn, out_dtype):
    n, d = x.shape
    width = w.shape[1]
    return pl.pallas_call(
        _norm_matmul_kernel,
        grid=(n // tm, width // tn),
        in_specs=[pl.BlockSpec((tm, d), lambda i, j: (i, 0)),
                  pl.BlockSpec((1, d), lambda i, j: (0, 0)),
                  pl.BlockSpec((d, tn), lambda i, j: (0, j))],
        out_specs=pl.BlockSpec((tm, tn), lambda i, j: (i, j)),
        out_shape=jax.ShapeDtypeStruct((n, width), out_dtype),
        scratch_shapes=[pltpu.VMEM((tm, d), BF16)],
        compiler_params=_cparams(("parallel", "arbitrary")),
        name="norm_matmul",
    )(x, g.reshape(1, d), w)


def _fox_cum_kernel(f_ref, b_ref, o_ref, carry_ref):
    @pl.when(pl.program_id(1) == 0)
    def _():
        carry_ref[...] = jnp.zeros_like(carry_ref)

    lf = _log_sigmoid(f_ref[0] + b_ref[...])
    tc = lf.shape[0]
    cum = _dot(_tri(tc).astype(F32), lf, precision=HIGHEST) + carry_ref[...]
    o_ref[0] = cum
    carry_ref[...] = cum[tc - 1:tc, :]


def fox_cum(proj3, f_bias_pad, *, tc):
    b, s, _ = proj3.shape
    return pl.pallas_call(
        _fox_cum_kernel,
        grid=(b, s // tc),
        in_specs=[pl.BlockSpec((1, tc, GROUP), lambda i, j: (i, j, FOX_F // GROUP)),
                  pl.BlockSpec((1, GROUP), lambda i, j: (0, 0))],
        out_specs=pl.BlockSpec((1, tc, GROUP), lambda i, j: (i, j, 0)),
        out_shape=jax.ShapeDtypeStruct((b, s, GROUP), F32),
        scratch_shapes=[pltpu.VMEM((1, GROUP), F32)],
        compiler_params=_cparams(("parallel", "arbitrary")),
        name="fox_cum",
    )(proj3, f_bias_pad)


def _fox_attn_kernel(q_ref, k_ref, v_ref, cq_ref, ck_ref, o_ref, *, tq):
    i = pl.program_id(1)
    cq = cq_ref[0]
    causal = _tri(tq)
    outs = []
    for h in range(HEADS):
        q = q_ref[0, h]
        cqh = cq[:, h:h + 1]

        def scores(j):
            ks = k_ref[0, h, pl.ds(pl.multiple_of(j * tq, tq), tq), :]
            vs = v_ref[0, h, pl.ds(pl.multiple_of(j * tq, tq), tq), :]
            ck = ck_ref[0, h, pl.ds(j, 1), :]
            return _dot_nt(q, ks) + (cqh - ck), vs

        s, vs = scores(i)
        s = jnp.where(causal, s, NEG_BIG)
        m = jnp.max(s, axis=-1, keepdims=True)
        p = jnp.exp(s - m)
        l = jnp.sum(p, axis=-1, keepdims=True)
        acc = _dot(p.astype(BF16), vs)

        def body(j, carry):
            m, l, acc = carry
            s, vs = scores(j)
            m_new = jnp.maximum(m, jnp.max(s, axis=-1, keepdims=True))
            alpha = jnp.exp(m - m_new)
            p = jnp.exp(s - m_new)
            l = alpha * l + jnp.sum(p, axis=-1, keepdims=True)
            acc = alpha * acc + _dot(p.astype(BF16), vs)
            return m_new, l, acc

        m, l, acc = lax.fori_loop(0, i, body, (m, l, acc))
        outs.append(acc / l)
    o_ref[0] = jnp.concatenate(outs, axis=-1).astype(o_ref.dtype)


def fox_attention(q, k, v, cum, ck, *, tq):
    b, h, s, dh = q.shape
    nk = s // tq
    return pl.pallas_call(
        functools.partial(_fox_attn_kernel, tq=tq),
        grid=(b, s // tq),
        in_specs=[pl.BlockSpec((1, h, tq, dh), lambda bi, i: (bi, 0, i, 0)),
                  pl.BlockSpec((1, h, s, dh), lambda bi, i: (bi, 0, 0, 0)),
                  pl.BlockSpec((1, h, s, dh), lambda bi, i: (bi, 0, 0, 0)),
                  pl.BlockSpec((1, tq, GROUP), lambda bi, i: (bi, i, 0)),
                  pl.BlockSpec((1, h, nk, tq), lambda bi, i: (bi, 0, 0, 0))],
        out_specs=pl.BlockSpec((1, tq, GROUP), lambda bi, i: (bi, i, 0)),
        out_shape=jax.ShapeDtypeStruct((b, s, GROUP), BF16),
        compiler_params=_cparams(("parallel", "arbitrary")),
        name="fox_attention",
    )(q, k, v, cum, ck)


def _lru_kernel(x_ref, y_ref, cw_ref, cb_ref, wra_ref, bra_ref, wri_ref, bri_ref, lam_ref, o_ref,
                buf_ref, h_ref, *, ts):
    @pl.when(pl.program_id(1) == 0)
    def _():
        buf_ref[0:8, :] = jnp.zeros((8, GROUP), F32)
        h_ref[...] = jnp.zeros_like(h_ref)

    xb = x_ref[0]
    buf_ref[8:8 + ts, :] = xb
    xc = cb_ref[...] + buf_ref[5:5 + ts, :] * cw_ref[0:1, :]
    for j in range(1, CONV_WIDTH):
        xc = xc + buf_ref[5 + j:5 + j + ts, :] * cw_ref[j:j + 1, :]
    buf_ref[0:8, :] = xb[ts - 8:ts, :]

    xcb = xc.astype(BF16)
    r = _sigmoid(_dot(xcb, wra_ref[...]) + bra_ref[...])
    gate_i = _sigmoid(_dot(xcb, wri_ref[...]) + bri_ref[...])
    log_a = LRU_C * r * _log_sigmoid(lam_ref[...])
    a = jnp.exp(log_a)
    z = 2.0 * log_a
    mult = jnp.sqrt(jnp.maximum(-jnp.tanh(0.5 * z) * (jnp.exp(z) + 1.0), 0.0))
    u = mult * (gate_i * xc)

    row = lax.broadcasted_iota(jnp.int32, (ts, GROUP), 0)
    pa, pb = a, u
    d = 1
    while d < ts:
        sa = pltpu.roll(pa, d, axis=0)
        sb = pltpu.roll(pb, d, axis=0)
        valid = row >= d
        pb = jnp.where(valid, pa * sb + pb, pb)
        pa = jnp.where(valid, pa * sa, pa)
        d *= 2
    hseq = pa * h_ref[...] + pb
    h_ref[...] = hseq[ts - 1:ts, :]

    y = y_ref[0]
    gelu = 0.5 * y * (1.0 + jnp.tanh(math.sqrt(2.0 / math.pi) * (y + 0.044715 * (y * y * y))))
    o_ref[0] = (hseq * gelu).astype(o_ref.dtype)


def rglru(proj3, conv_w, conv_b, wra_bd, ra_b, wri_bd, ri_b, lam, *, ts):
    b, s, _ = proj3.shape
    vec = lambda: pl.BlockSpec((1, GROUP), lambda i, j: (0, 0))
    mat = lambda: pl.BlockSpec((GROUP, GROUP), lambda i, j: (0, 0))
    return pl.pallas_call(
        functools.partial(_lru_kernel, ts=ts),
        grid=(b, s // ts),
        in_specs=[pl.BlockSpec((1, ts, GROUP), lambda i, j: (i, j, LRU_X // GROUP)),
                  pl.BlockSpec((1, ts, GROUP), lambda i, j: (i, j, LRU_Y // GROUP)),
                  pl.BlockSpec((CONV_WIDTH, GROUP), lambda i, j: (0, 0)),
                  vec(), mat(), vec(), mat(), vec(), vec()],
        out_specs=pl.BlockSpec((1, ts, GROUP), lambda i, j: (i, j, 0)),
        out_shape=jax.ShapeDtypeStruct((b, s, GROUP), BF16),
        scratch_shapes=[pltpu.VMEM((ts + 8, GROUP), F32), pltpu.VMEM((1, GROUP), F32)],
        compiler_params=_cparams(("parallel", "arbitrary")),
        name="rglru",
    )(proj3, proj3, conv_w, conv_b.reshape(1, GROUP), wra_bd, ra_b.reshape(1, GROUP), wri_bd,
      ri_b.reshape(1, GROUP), lam.reshape(1, GROUP))


def _rwkv_prep_kernel(*refs, ts, has_vres):
    if has_vres:
        (sr_ref, sk_ref, sv_ref, sl_ref, mu_ref, w0_ref, w2_ref, a0_ref, a2_ref, g2_ref, kk_ref, ka_ref, ones_ref,
         vf_ref, v0_ref, v1_ref, v2_ref,
         r_out, lw_out, k_out, v_out, a_out, b_out, g_out, carry_ref) = refs
    else:
        (sr_ref, sk_ref, sv_ref, sl_ref, mu_ref, w0_ref, w2_ref, a0_ref, a2_ref, g2_ref, kk_ref, ka_ref, ones_ref,
         r_out, lw_out, k_out, v_out, a_out, b_out, g_out, carry_ref) = refs

    @pl.when(pl.program_id(1) == 0)
    def _():
        carry_ref[...] = jnp.zeros_like(carry_ref)

    row0 = lax.broadcasted_iota(jnp.int32, (ts, GROUP), 0) == 0

    def shift_mix(ref, idx):
        s = ref[0]
        prev = jnp.where(row0, carry_ref[idx:idx + 1, :], pltpu.roll(s, 1, axis=0))
        carry_ref[idx:idx + 1, :] = s[ts - 1:ts, :]
        return s + (prev - s) * mu_ref[idx:idx + 1, :]

    r = shift_mix(sr_ref, 0)
    k = shift_mix(sk_ref, 1)
    v = shift_mix(sv_ref, 2)
    low = shift_mix(sl_ref, 3)

    zw = w0_ref[...] + _dot(jnp.tanh(low).astype(BF16), w2_ref[...])
    lw = -jnp.exp(_log_sigmoid(zw) - 0.5)
    a = _sigmoid(a0_ref[...] + _dot(low.astype(BF16), a2_ref[...]))
    g = _dot(_sigmoid(low).astype(BF16), g2_ref[...])
    if has_vres:
        mix = _dot(_dot(v.astype(BF16), v1_ref[...]).astype(BF16), v2_ref[...])
        v = v + (vf_ref[0] - v) * _sigmoid(v0_ref[...] + mix)
    kk = k * kk_ref[...]
    ss = _dot(kk * kk, ones_ref[...], precision=HIGHEST)
    kk = kk / jnp.maximum(jnp.sqrt(ss), 1e-12)
    k = k * (1.0 + (a - 1.0) * ka_ref[...])

    r_out[0] = r
    lw_out[0] = lw
    k_out[0] = k
    v_out[0] = v
    a_out[0] = -kk
    b_out[0] = kk * a
    g_out[0] = g


def rwkv_prep(proj3, mu4, w0, w2p, a0, a2p, g2p, k_k, k_a, head_ones, vres, *, ts):
    b, s, _ = proj3.shape
    slab = lambda off: pl.BlockSpec((1, ts, GROUP), lambda i, j: (i, j, off // GROUP))
    vec = lambda: pl.BlockSpec((1, GROUP), lambda i, j: (0, 0))
    full = lambda shape: pl.BlockSpec(shape, lambda i, j: tuple(0 for _ in shape))
    tok = pl.BlockSpec((1, ts, GROUP), lambda i, j: (i, j, 0))
    in_specs = [slab(RWKV_R), slab(RWKV_K), slab(RWKV_V), slab(RWKV_LR), full((4, GROUP)),
                vec(), full((GROUP, GROUP)), vec(), full((GROUP, GROUP)), full((GROUP, GROUP)), vec(), vec(),
                full((GROUP, GROUP))]
    args = [proj3, proj3, proj3, proj3, mu4, w0.reshape(1, GROUP), w2p, a0.reshape(1, GROUP), a2p, g2p,
            k_k.reshape(1, GROUP), k_a.reshape(1, GROUP), head_ones]
    if vres is not None:
        v_first, v0, v1p, v2p = vres
        in_specs += [tok, vec(), full((GROUP, 128)), full((128, GROUP))]
        args += [v_first, v0.reshape(1, GROUP), v1p, v2p]
    return pl.pallas_call(
        functools.partial(_rwkv_prep_kernel, ts=ts, has_vres=vres is not None),
        grid=(b, s // ts),
        in_specs=in_specs,
        out_specs=[tok] * 7,
        out_shape=[jax.ShapeDtypeStruct((b, s, GROUP), F32)] * 7,
        scratch_shapes=[pltpu.VMEM((4, GROUP), F32)],
        compiler_params=_cparams(("parallel", "arbitrary")),
        name="rwkv_prep",
    )(*args)


def _unit_lower_inverse(a_strict, n):
    r = lax.broadcasted_iota(jnp.int32, (n, n), 0)
    c = lax.broadcasted_iota(jnp.int32, (n, n), 1)
    t = (r == c).astype(F32)
    m = 1
    while m < n:
        off = (r // (2 * m) == c // (2 * m)) & (r % (2 * m) >= m) & (c % (2 * m) < m)
        a_off = jnp.where(off, a_strict, 0.0)
        t = t + _dot(t, _dot(a_off, t, precision=HIGHEST), precision=HIGHEST)
        m *= 2
    return t


def _rwkv_scan_kernel(r_ref, lw_ref, k_ref, v_ref, a_ref, b_ref, g_ref, rk_ref, gw_ref, gb_ref, o_ref, state_ref,
                      *, chunk):
    @pl.when(pl.program_id(1) == 0)
    def _():
        state_ref[...] = jnp.zeros_like(state_ref)

    c = chunk
    r, lw, k, v, a, b = r_ref[0], lw_ref[0], k_ref[0], v_ref[0], a_ref[0], b_ref[0]
    cum = _dot(_tri(c).astype(F32), lw, precision=HIGHEST)
    cum_ex = cum - lw
    mid = cum[c // 2 - 1:c // 2, :]
    tot = cum[c - 1:c, :]
    e_fwd = jnp.exp(cum - mid)
    e_bwd = jnp.exp(mid - cum)
    r_rel, k_rel = r * e_fwd, k * e_bwd
    a_rel, b_rel = a * jnp.exp(cum_ex - mid), b * e_bwd
    a_abs, r_abs = a * jnp.exp(cum_ex), r * jnp.exp(cum)
    e_end = jnp.exp(tot - cum)
    k_end, b_end = k * e_end, b * e_end
    gam = jnp.exp(tot)
    bonus_rk = r * k * rk_ref[...]

    strict, incl = _tri(c, strict=True), _tri(c)
    hp = dict(precision=HIGHEST)
    outs = []
    for h in range(HEADS):
        sl = slice(h * HEAD_DIM, (h + 1) * HEAD_DIM)
        vh = v[:, sl]
        a_ab = jnp.where(strict, _dot_nt(a_rel[:, sl], b_rel[:, sl], **hp), 0.0)
        a_ak = jnp.where(strict, _dot_nt(a_rel[:, sl], k_rel[:, sl], **hp), 0.0)
        a_rb = jnp.where(incl, _dot_nt(r_rel[:, sl], b_rel[:, sl], **hp), 0.0)
        a_rk = jnp.where(incl, _dot_nt(r_rel[:, sl], k_rel[:, sl], **hp), 0.0)
        t_inv = _unit_lower_inverse(a_ab, c)
        s0 = state_ref[h]
        u = _dot(t_inv, _dot_nt(a_abs[:, sl], s0, **hp) + _dot(a_ak, vh, **hp), **hp)
        y = _dot_nt(r_abs[:, sl], s0, **hp) + _dot(a_rb, u, **hp) + _dot(a_rk, vh, **hp)
        state_ref[h] = s0 * gam[:, sl] + _dot_tn(u, b_end[:, sl], **hp) + _dot_tn(vh, k_end[:, sl], **hp)

        mu = jnp.mean(y, axis=-1, keepdims=True)
        var = jnp.mean(jnp.square(y - mu), axis=-1, keepdims=True)
        yn = (y - mu) * lax.rsqrt(var + RWKV_GN_EPS) * gw_ref[:, sl] + gb_ref[:, sl]
        bonus = jnp.sum(bonus_rk[:, sl], axis=-1, keepdims=True) * vh
        outs.append(yn + bonus)
    o_ref[0] = (jnp.concatenate(outs, axis=-1) * g_ref[0]).astype(o_ref.dtype)


def rwkv_scan(r, lw, k, v, a, b, g, r_k, gn_w, gn_b, *, chunk):
    bsz, s, _ = r.shape
    tok = pl.BlockSpec((1, chunk, GROUP), lambda i, j: (i, j, 0))
    vec = pl.BlockSpec((1, GROUP), lambda i, j: (0, 0))
    return pl.pallas_call(
        functools.partial(_rwkv_scan_kernel, chunk=chunk),
        grid=(bsz, s // chunk),
        in_specs=[tok] * 7 + [vec] * 3,
        out_specs=tok,
        out_shape=jax.ShapeDtypeStruct((bsz, s, GROUP), BF16),
        scratch_shapes=[pltpu.VMEM((HEADS, HEAD_DIM, HEAD_DIM), F32)],
        compiler_params=_cparams(("parallel", "arbitrary")),
        name="rwkv_scan",
    )(r, lw, k, v, a, b, g, r_k.reshape(1, GROUP), gn_w.reshape(1, GROUP), gn_b.reshape(1, GROUP))


def _ret_kernel(q_ref, k_ref, v_ref, g_ref, cos_ref, sin_ref, dmat_ref, xi_ref, zeta_ref, cd_ref, gw_ref, o_ref,
                state_ref, *, chunk):
    @pl.when(pl.program_id(1) == 0)
    def _():
        state_ref[...] = jnp.zeros_like(state_ref)

    lane = lax.broadcasted_iota(jnp.int32, (chunk, GROUP), 1)
    first_half = (lane % HEAD_DIM) < (HEAD_DIM // 2)
    cos, sin = cos_ref[...], sin_ref[...]

    def rotary(t):
        partner = jnp.where(first_half, pltpu.roll(t, GROUP - HEAD_DIM // 2, axis=1),
                            pltpu.roll(t, HEAD_DIM // 2, axis=1))
        return t * cos + partner * sin

    q = rotary(q_ref[0])
    k = rotary(k_ref[0]) * (HEAD_DIM ** -0.5)
    v = v_ref[0]
    q_cross = (q * xi_ref[...]).astype(BF16)
    k_decay = (k * zeta_ref[...]).astype(BF16)
    qb, kb, vb = q.astype(BF16), k.astype(BF16), v.astype(BF16)
    outs = []
    for h in range(HEADS):
        sl = slice(h * HEAD_DIM, (h + 1) * HEAD_DIM)
        inner = _dot_nt(qb[:, sl], kb[:, sl]) * dmat_ref[h]
        state = state_ref[h]
        o = _dot(inner.astype(BF16), vb[:, sl]) + _dot(q_cross[:, sl], state.astype(BF16))
        state_ref[h] = state * cd_ref[:, sl] + _dot_tn(k_decay[:, sl], vb[:, sl])
        outs.append(o * lax.rsqrt(jnp.mean(o * o, axis=-1, keepdims=True) + NORM_EPS))
    g = g_ref[0]
    o_ref[0] = (jnp.concatenate(outs, axis=-1) * gw_ref[...] * (g * _sigmoid(g))).astype(o_ref.dtype)


def retention(proj3, cos_t, sin_t, dmat, xi, zeta, cd, gn_w, *, chunk):
    b, s, _ = proj3.shape
    slab = lambda off: pl.BlockSpec((1, chunk, GROUP), lambda i, j: (i, j, off // GROUP))
    full = lambda shape: pl.BlockSpec(shape, lambda i, j: tuple(0 for _ in shape))
    return pl.pallas_call(
        functools.partial(_ret_kernel, chunk=chunk),
        grid=(b, s // chunk),
        in_specs=[slab(RET_Q), slab(RET_K), slab(RET_V), slab(RET_G),
                  pl.BlockSpec((chunk, GROUP), lambda i, j: (j, 0)),
                  pl.BlockSpec((chunk, GROUP), lambda i, j: (j, 0)),
                  full((HEADS, chunk, chunk)), full((chunk, GROUP)), full((chunk, GROUP)),
                  full((1, GROUP)), full((1, GROUP))],
        out_specs=pl.BlockSpec((1, chunk, GROUP), lambda i, j: (i, j, 0)),
        out_shape=jax.ShapeDtypeStruct((b, s, GROUP), BF16),
        scratch_shapes=[pltpu.VMEM((HEADS, HEAD_DIM, HEAD_DIM), F32)],
        compiler_params=_cparams(("parallel", "arbitrary")),
        name="retention",
    )(proj3, proj3, proj3, proj3, cos_t, sin_t, dmat, xi, zeta, cd, gn_w.reshape(1, GROUP))


def _retention_tables(s, chunk):
    half = HEAD_DIM // 2
    inv = 1.0 / (RET_THETA ** jnp.linspace(0.0, 1.0, half, dtype=F32))
    ang = jnp.arange(s, dtype=F32)[:, None] * inv[None, :]
    cos, sin = jnp.cos(ang), jnp.sin(ang)
    cos_t = jnp.tile(jnp.concatenate([cos, cos], axis=-1), (1, HEADS))
    sin_t = jnp.tile(jnp.concatenate([-sin, sin], axis=-1), (1, HEADS))
    lg = jnp.log(1.0 - 2.0 ** (-5.0 - jnp.arange(HEADS, dtype=F32)))
    n = jnp.arange(chunk, dtype=F32)
    diff = n[:, None] - n[None, :]
    dmat = jnp.where(diff >= 0, jnp.exp(lg[:, None, None] * jnp.maximum(diff, 0.0)), 0.0)
    zeta = jnp.exp(lg[:, None] * (chunk - 1.0 - n)[None, :])
    xi = jnp.exp(lg[:, None] * (n + 1.0)[None, :])
    per_lane = lambda t: jnp.repeat(t.T, HEAD_DIM, axis=1)
    cd = jnp.repeat(jnp.exp(lg * chunk), HEAD_DIM)[None, :]
    return cos_t, sin_t, dmat, per_lane(xi), per_lane(zeta), cd


def _out_proj_kernel(m0_ref, m1_ref, m2_ref, m3_ref, w_ref, x_ref, g_ref, o_ref):
    acc = _dot(m0_ref[...], w_ref[0:GROUP, :])
    for idx, m_ref in enumerate((m1_ref, m2_ref, m3_ref), start=1):
        acc = acc + _dot(m_ref[...], w_ref[idx * GROUP:(idx + 1) * GROUP, :])
    o_ref[...] = x_ref[...] + _rms(acc, g_ref[...])


def out_proj(mixed, w, x, g, *, tm):
    n, d = x.shape
    mix_spec = pl.BlockSpec((tm, GROUP), lambda i: (i, 0))
    return pl.pallas_call(
        _out_proj_kernel,
        grid=(n // tm,),
        in_specs=[mix_spec] * 4 + [pl.BlockSpec((d, d), lambda i: (0, 0)),
                                   pl.BlockSpec((tm, d), lambda i: (i, 0)),
                                   pl.BlockSpec((1, d), lambda i: (0, 0))],
        out_specs=pl.BlockSpec((tm, d), lambda i: (i, 0)),
        out_shape=jax.ShapeDtypeStruct((n, d), F32),
        compiler_params=_cparams(("parallel",)),
        name="out_proj",
    )(*mixed, w, x, g.reshape(1, d))


def _xattn_kernel(x_ref, kv_ref, wq_ref, wo_ref, gpre_ref, gpost_ref, o_ref):
    x = x_ref[0]
    xn = _rms(x, gpre_ref[...]).astype(BF16)
    q = (_dot(xn, wq_ref[...]) * (XATTN_HEAD_DIM ** -0.5)).astype(BF16)
    outs = []
    for h in range(XATTN_HEADS):
        sl = slice(h * XATTN_HEAD_DIM, (h + 1) * XATTN_HEAD_DIM)
        s = _dot_nt(q[:, sl], kv_ref[0, :, sl])
        e = jnp.exp(s - jnp.max(s, axis=-1, keepdims=True))
        p = e / jnp.sum(e, axis=-1, keepdims=True)
        outs.append(_dot(p.astype(BF16), kv_ref[0, :, D_MODEL + h * XATTN_HEAD_DIM:D_MODEL + (h + 1) * XATTN_HEAD_DIM]))
    o = jnp.concatenate(outs, axis=-1).astype(BF16)
    o_ref[0] = x + _rms(_dot(o, wo_ref[...]), gpost_ref[...])


def cross_attention(x3, kv, wq, wo, g_pre, g_post, *, tm):
    b, s, d = x3.shape
    m = kv.shape[1]
    return pl.pallas_call(
        _xattn_kernel,
        grid=(b, s // tm),
        in_specs=[pl.BlockSpec((1, tm, d), lambda bi, i: (bi, i, 0)),
                  pl.BlockSpec((1, m, 2 * d), lambda bi, i: (bi, 0, 0)),
                  pl.BlockSpec((d, d), lambda bi, i: (0, 0)),
                  pl.BlockSpec((d, d), lambda bi, i: (0, 0)),
                  pl.BlockSpec((1, d), lambda bi, i: (0, 0)),
                  pl.BlockSpec((1, d), lambda bi, i: (0, 0))],
        out_specs=pl.BlockSpec((1, tm, d), lambda bi, i: (bi, i, 0)),
        out_shape=jax.ShapeDtypeStruct((b, s, d), F32),
        compiler_params=_cparams(("parallel", "arbitrary")),
        name="cross_attention",
    )(x3, kv, wq, wo, g_pre.reshape(1, d), g_post.reshape(1, d))


def _mlp_kernel(x_ref, w1_ref, w2_ref, gpre_ref, gpost_ref, o_ref, *, ff_tile):
    x = x_ref[...]
    xn = _rms(x, gpre_ref[...]).astype(BF16)
    d_ff = w1_ref.shape[1]
    acc = None
    for c in range(d_ff // ff_tile):
        hid = jnp.square(jnp.maximum(_dot(xn, w1_ref[:, c * ff_tile:(c + 1) * ff_tile]), 0.0)).astype(BF16)
        part = _dot(hid, w2_ref[c * ff_tile:(c + 1) * ff_tile, :])
        acc = part if acc is None else acc + part
    o_ref[...] = x + _rms(acc, gpost_ref[...])


def mlp(x, w1, w2, g_pre, g_post, *, tm, ff_tile):
    n, d = x.shape
    d_ff = w1.shape[1]
    return pl.pallas_call(
        functools.partial(_mlp_kernel, ff_tile=ff_tile),
        grid=(n // tm,),
        in_specs=[pl.BlockSpec((tm, d), lambda i: (i, 0)),
                  pl.BlockSpec((d, d_ff), lambda i: (0, 0), pipeline_mode=pl.Buffered(1)),
                  pl.BlockSpec((d_ff, d), lambda i: (0, 0), pipeline_mode=pl.Buffered(1)),
                  pl.BlockSpec((1, d), lambda i: (0, 0)),
                  pl.BlockSpec((1, d), lambda i: (0, 0))],
        out_specs=pl.BlockSpec((tm, d), lambda i: (i, 0)),
        out_shape=jax.ShapeDtypeStruct((n, d), F32),
        compiler_params=_cparams(("parallel",)),
        name="mlp",
    )(x, w1, w2, g_pre.reshape(1, d), g_post.reshape(1, d))


def _block_diag(w):
    h, n, _ = w.shape
    eye = jnp.eye(h, dtype=w.dtype)
    return (eye[:, None, :, None] * w[:, :, None, :]).reshape(h * n, h * n)


def _pad_rows(w, start, total):
    return jnp.zeros((total, w.shape[1]), w.dtype).at[start:start + w.shape[0]].set(w)


def _tile(n, pref):
    return pref if n % pref == 0 else n


def kernel(x, mem, norm_mix_pre, norm_mix_post, norm_xa_pre, norm_xa_post, norm_mem, norm_mlp_pre, norm_mlp_post, w_in, w_out, fox_f_bias, lru_conv_w, lru_conv_b, lru_ra_w, lru_ra_b, lru_ri_w, lru_ri_b, lru_lambda, rwkv_mu, rwkv_w0, rwkv_w2, rwkv_a0, rwkv_a2, rwkv_g2, rwkv_k_k, rwkv_k_a, rwkv_r_k, rwkv_gn_w, rwkv_gn_b, rwkv_v0, rwkv_v1, rwkv_v2, ret_gn_w, xa_wq, xa_wk, xa_wv, xa_wo, mlp_w1, mlp_w2):
    bsz, seq, d = x.shape
    depth = w_in.shape[0]
    n_tok = bsz * seq
    mem_len = mem.shape[1]
    tm = _tile(n_tok, 512)
    tq = _tile(seq, 256)
    ts = _tile(seq, 256)
    chunk = _tile(seq, RWKV_CHUNK)
    ret_tables = _retention_tables(seq, RET_CHUNK)
    head_ones = _block_diag(jnp.ones((HEADS, HEAD_DIM, HEAD_DIM), F32))

    x2 = x.reshape(n_tok, d)
    v_first = None
    for l in range(depth):
        w = w_in[l]
        w_pad = jnp.concatenate([w[:, :FOX_REAL], jnp.zeros((d, LRU_X - FOX_REAL), w.dtype), w[:, FOX_REAL:]], axis=1)
        proj = norm_matmul(x2, norm_mix_pre[l], w_pad.astype(BF16), tm=tm, tn=IN_PAD // 4, out_dtype=F32)
        proj3 = proj.reshape(bsz, seq, IN_PAD)

        f_bias = jnp.zeros((1, GROUP), F32).at[0, :HEADS].set(fox_f_bias[l])
        cum = fox_cum(proj3, f_bias, tc=ts)
        heads = lambda t: t.reshape(bsz, seq, HEADS, HEAD_DIM).transpose(0, 2, 1, 3)
        fq = heads((proj3[..., FOX_Q:FOX_Q + GROUP] * (HEAD_DIM ** -0.5)).astype(BF16))
        fk = heads(proj3[..., FOX_K:FOX_K + GROUP].astype(BF16))
        fv = heads(proj3[..., FOX_V:FOX_V + GROUP].astype(BF16))
        ck = cum[..., :HEADS].transpose(0, 2, 1).reshape(bsz, HEADS, seq // tq, tq)
        fox_out = fox_attention(fq, fk, fv, cum, ck, tq=tq)

        lru_out = rglru(proj3, lru_conv_w[l], lru_conv_b[l], _block_diag(lru_ra_w[l]).astype(BF16), lru_ra_b[l],
                        _block_diag(lru_ri_w[l]).astype(BF16), lru_ri_b[l], lru_lambda[l], ts=ts)

        vres = None
        if l > 0:
            vres = (v_first, rwkv_v0[l - 1],
                    jnp.pad(rwkv_v1[l - 1], ((0, 0), (0, 128 - RWKV_V_RANK))).astype(BF16),
                    _pad_rows(rwkv_v2[l - 1], 0, 128).astype(BF16))
        r_, lw_, k_, v_, a_, b_, g_ = rwkv_prep(
            proj3, rwkv_mu[l].reshape(4, GROUP), rwkv_w0[l],
            _pad_rows(rwkv_w2[l], 0, GROUP).astype(BF16), rwkv_a0[l],
            _pad_rows(rwkv_a2[l], RWKV_W_RANK, GROUP).astype(BF16),
            _pad_rows(rwkv_g2[l], RWKV_W_RANK + RWKV_A_RANK, GROUP).astype(BF16),
            rwkv_k_k[l], rwkv_k_a[l], head_ones, vres, ts=ts)
        if l == 0:
            v_first = v_
        rwkv_out = rwkv_scan(r_, lw_, k_, v_, a_, b_, g_, rwkv_r_k[l], rwkv_gn_w[l], rwkv_gn_b[l], chunk=chunk)

        ret_out = retention(proj3, *ret_tables, ret_gn_w[l], chunk=RET_CHUNK)

        x2 = out_proj([m.reshape(n_tok, GROUP) for m in (fox_out, lru_out, rwkv_out, ret_out)],
                      w_out[l].astype(BF16), x2, norm_mix_post[l], tm=tm)

        wkv = jnp.concatenate([xa_wk[l], xa_wv[l]], axis=1).astype(BF16)
        kv = norm_matmul(mem.reshape(bsz * mem_len, d), norm_mem[l], wkv, tm=_tile(bsz * mem_len, 512),
                         tn=1024, out_dtype=BF16).reshape(bsz, mem_len, 2 * d)
        x2 = cross_attention(x2.reshape(bsz, seq, d), kv, xa_wq[l].astype(BF16), xa_wo[l].astype(BF16),
                             norm_xa_pre[l], norm_xa_post[l], tm=_tile(seq, 512)).reshape(n_tok, d)

        x2 = mlp(x2, mlp_w1[l].astype(BF16), mlp_w2[l].astype(BF16), norm_mlp_pre[l], norm_mlp_post[l],
                 tm=tm, ff_tile=1024)
    return x2.reshape(bsz, seq, d)
```

```python
import functools
import math

import jax
import jax.numpy as jnp
import numpy as np
from jax import lax
from jax.experimental import pallas as pl
from jax.experimental.pallas import tpu as pltpu

F32 = jnp.float32
BF16 = jnp.bfloat16
HIGHEST = lax.Precision.HIGHEST

D_MODEL = 1024
GROUP = 256
HEADS = 4
HEAD_DIM = 64
CONV_WIDTH = 4
LRU_C = 8.0
RET_THETA = 10000.0
RET_CHUNK = 128
RWKV_W_RANK, RWKV_A_RANK, RWKV_G_RANK, RWKV_V_RANK = 64, 64, 128, 32
RWKV_GN_EPS = 64e-5
XATTN_HEADS = 4
XATTN_HEAD_DIM = D_MODEL // XATTN_HEADS
NORM_EPS = 1e-6
NEG_BIG = -1e30

FOX_Q, FOX_K, FOX_V, FOX_F = 0, 256, 512, 768
LRU_X, LRU_Y = 1024, 1280
RWKV_R, RWKV_K, RWKV_V, RWKV_LR = 1536, 1792, 2048, 2304
RET_Q, RET_K, RET_V, RET_G = 2560, 2816, 3072, 3328
IN_PAD = 3584
FOX_REAL = 3 * GROUP + HEADS

VMEM_LIMIT = 56 * 1024 * 1024
RWKV_CHUNK = 64


def _cparams(sem):
    return pltpu.CompilerParams(dimension_semantics=sem, vmem_limit_bytes=VMEM_LIMIT)


def _rms(x, g):
    return x * lax.rsqrt(jnp.mean(x * x, axis=-1, keepdims=True) + NORM_EPS) * g


def _log_sigmoid(x):
    return jnp.minimum(x, 0.0) - jnp.log1p(jnp.exp(-jnp.abs(x)))


def _sigmoid(x):
    return 1.0 / (1.0 + jnp.exp(-x))


def _dot(a, b, **kw):
    return jnp.dot(a, b, preferred_element_type=F32, **kw)


def _dot_nt(a, b, **kw):
    return lax.dot_general(a, b, (((1,), (1,)), ((), ())), preferred_element_type=F32, **kw)


def _dot_tn(a, b, **kw):
    return lax.dot_general(a, b, (((0,), (0,)), ((), ())), preferred_element_type=F32, **kw)


def _split_bf16(x):
    hi = x.astype(BF16)
    return hi, (x - hi.astype(F32)).astype(BF16)


def _mm1(dot, a, b):
    return dot(a.astype(BF16), b.astype(BF16))


def _mm3(dot, a, b):
    ah, al = _split_bf16(a)
    bh, bl = _split_bf16(b)
    return dot(ah, bh) + (dot(ah, bl) + dot(al, bh))


def _tri(n, strict=False):
    r = lax.broadcasted_iota(jnp.int32, (n, n), 0)
    c = lax.broadcasted_iota(jnp.int32, (n, n), 1)
    return (r > c) if strict else (r >= c)


def _norm_matmul_kernel(x_ref, g_ref, w_ref, o_ref, xn_ref):
    @pl.when(pl.program_id(1) == 0)
    def _():
        xn_ref[...] = _rms(x_ref[...], g_ref[...]).astype(BF16)

    o_ref[...] = _dot(xn_ref[...], w_ref[...]).astype(o_ref.dtype)


def norm_matmul(x, g, w, *, tm, tn, out_dtype):
    n, d = x.shape
    width = w.shape[1]
    return pl.pallas_call(
        _norm_matmul_kernel,
        grid=(n // tm, width // tn),
        in_specs=[pl.BlockSpec((tm, d), lambda i, j: (i, 0)),
                  pl.BlockSpec((1, d), lambda i, j: (0, 0)),
                  pl.BlockSpec((d, tn), lambda i, j: (0, j))],
        out_specs=pl.BlockSpec((tm, tn), lambda i, j: (i, j)),
        out_shape=jax.ShapeDtypeStruct((n, width), out_dtype),
        scratch_shapes=[pltpu.VMEM((tm, d), BF16)],
        compiler_params=_cparams(("parallel", "arbitrary")),
        name="norm_matmul",
    )(x, g.reshape(1, d), w)


def _fox_cum_kernel(f_ref, b_ref, o_ref, carry_ref):
    @pl.when(pl.program_id(1) == 0)
    def _():
        carry_ref[...] = jnp.zeros_like(carry_ref)

    lf = _log_sigmoid(f_ref[0] + b_ref[...])
    tc = lf.shape[0]
    cum = _dot(_tri(tc).astype(F32), lf, precision=HIGHEST) + carry_ref[...]
    o_ref[0] = cum
    carry_ref[...] = cum[tc - 1:tc, :]


def fox_cum(proj3, f_bias_pad, *, tc):
    b, s, _ = proj3.shape
    return pl.pallas_call(
        _fox_cum_kernel,
        grid=(b, s // tc),
        in_specs=[pl.BlockSpec((1, tc, GROUP), lambda i, j: (i, j, FOX_F // GROUP)),
                  pl.BlockSpec((1, GROUP), lambda i, j: (0, 0))],
        out_specs=pl.BlockSpec((1, tc, GROUP), lambda i, j: (i, j, 0)),
        out_shape=jax.ShapeDtypeStruct((b, s, GROUP), F32),
        scratch_shapes=[pltpu.VMEM((1, GROUP), F32)],
        compiler_params=_cparams(("parallel", "arbitrary")),
        name="fox_cum",
    )(proj3, f_bias_pad)


def _fox_attn_kernel(q_ref, k_ref, v_ref, cq_ref, ck_ref, o_ref, *, tq):
    i = pl.program_id(1)
    cq = cq_ref[0]
    causal = _tri(tq)
    outs = []
    for h in range(HEADS):
        q = q_ref[0, h]
        cqh = cq[:, h:h + 1]

        def scores(j):
            ks = k_ref[0, h, pl.ds(pl.multiple_of(j * tq, tq), tq), :]
            vs = v_ref[0, h, pl.ds(pl.multiple_of(j * tq, tq), tq), :]
            ck = ck_ref[0, h, pl.ds(j, 1), :]
            return _dot_nt(q, ks) + (cqh - ck), vs

        s, vs = scores(i)
        s = jnp.where(causal, s, NEG_BIG)
        m = jnp.max(s, axis=-1, keepdims=True)
        p = jnp.exp(s - m)
        l = jnp.sum(p, axis=-1, keepdims=True)
        acc = _dot(p.astype(BF16), vs)

        def body(j, carry):
            m, l, acc = carry
            s, vs = scores(j)
            m_new = jnp.maximum(m, jnp.max(s, axis=-1, keepdims=True))
            alpha = jnp.exp(m - m_new)
            p = jnp.exp(s - m_new)
            l = alpha * l + jnp.sum(p, axis=-1, keepdims=True)
            acc = alpha * acc + _dot(p.astype(BF16), vs)
            return m_new, l, acc

        m, l, acc = lax.fori_loop(0, i, body, (m, l, acc))
        outs.append(acc / l)
    o_ref[0] = jnp.concatenate(outs, axis=-1).astype(o_ref.dtype)


def fox_attention(q, k, v, cum, ck, *, tq):
    b, h, s, dh = q.shape
    nk = s // tq
    return pl.pallas_call(
        functools.partial(_fox_attn_kernel, tq=tq),
        grid=(b, s // tq),
        in_specs=[pl.BlockSpec((1, h, tq, dh), lambda bi, i: (bi, 0, i, 0)),
                  pl.BlockSpec((1, h, s, dh), lambda bi, i: (bi, 0, 0, 0)),
                  pl.BlockSpec((1, h, s, dh), lambda bi, i: (bi, 0, 0, 0)),
                  pl.BlockSpec((1, tq, GROUP), lambda bi, i: (bi, i, 0)),
                  pl.BlockSpec((1, h, nk, tq), lambda bi, i: (bi, 0, 0, 0))],
        out_specs=pl.BlockSpec((1, tq, GROUP), lambda bi, i: (bi, i, 0)),
        out_shape=jax.ShapeDtypeStruct((b, s, GROUP), BF16),
        compiler_params=_cparams(("parallel", "arbitrary")),
        name="fox_attention",
    )(q, k, v, cum, ck)


def _lru_kernel(x_ref, y_ref, cw_ref, cb_ref, wra_ref, bra_ref, wri_ref, bri_ref, lam_ref, o_ref,
                buf_ref, h_ref, *, ts):
    @pl.when(pl.program_id(1) == 0)
    def _():
        buf_ref[0:8, :] = jnp.zeros((8, GROUP), F32)
        h_ref[...] = jnp.zeros_like(h_ref)

    xb = x_ref[0]
    buf_ref[8:8 + ts, :] = xb
    xc = cb_ref[...] + buf_ref[5:5 + ts, :] * cw_ref[0:1, :]
    for j in range(1, CONV_WIDTH):
        xc = xc + buf_ref[5 + j:5 + j + ts, :] * cw_ref[j:j + 1, :]
    buf_ref[0:8, :] = xb[ts - 8:ts, :]

    xcb = xc.astype(BF16)
    r = _sigmoid(_dot(xcb, wra_ref[...]) + bra_ref[...])
    gate_i = _sigmoid(_dot(xcb, wri_ref[...]) + bri_ref[...])
    log_a = LRU_C * r * _log_sigmoid(lam_ref[...])
    a = jnp.exp(log_a)
    z = 2.0 * log_a
    mult = jnp.sqrt(jnp.maximum(-jnp.tanh(0.5 * z) * (jnp.exp(z) + 1.0), 0.0))
    u = mult * (gate_i * xc)

    row = lax.broadcasted_iota(jnp.int32, (ts, GROUP), 0)
    pa, pb = a, u
    d = 1
    while d < ts:
        sa = pltpu.roll(pa, d, axis=0)
        sb = pltpu.roll(pb, d, axis=0)
        valid = row >= d
        pb = jnp.where(valid, pa * sb + pb, pb)
        pa = jnp.where(valid, pa * sa, pa)
        d *= 2
    hseq = pa * h_ref[...] + pb
    h_ref[...] = hseq[ts - 1:ts, :]

    y = y_ref[0]
    gelu = 0.5 * y * (1.0 + jnp.tanh(math.sqrt(2.0 / math.pi) * (y + 0.044715 * (y * y * y))))
    o_ref[0] = (hseq * gelu).astype(o_ref.dtype)


def rglru(proj3, conv_w, conv_b, wra_bd, ra_b, wri_bd, ri_b, lam, *, ts):
    b, s, _ = proj3.shape
    vec = lambda: pl.BlockSpec((1, GROUP), lambda i, j: (0, 0))
    mat = lambda: pl.BlockSpec((GROUP, GROUP), lambda i, j: (0, 0))
    return pl.pallas_call(
        functools.partial(_lru_kernel, ts=ts),
        grid=(b, s // ts),
        in_specs=[pl.BlockSpec((1, ts, GROUP), lambda i, j: (i, j, LRU_X // GROUP)),
                  pl.BlockSpec((1, ts, GROUP), lambda i, j: (i, j, LRU_Y // GROUP)),
                  pl.BlockSpec((CONV_WIDTH, GROUP), lambda i, j: (0, 0)),
                  vec(), mat(), vec(), mat(), vec(), vec()],
        out_specs=pl.BlockSpec((1, ts, GROUP), lambda i, j: (i, j, 0)),
        out_shape=jax.ShapeDtypeStruct((b, s, GROUP), BF16),
        scratch_shapes=[pltpu.VMEM((ts + 8, GROUP), F32), pltpu.VMEM((1, GROUP), F32)],
        compiler_params=_cparams(("parallel", "arbitrary")),
        name="rglru",
    )(proj3, proj3, conv_w, conv_b.reshape(1, GROUP), wra_bd, ra_b.reshape(1, GROUP), wri_bd,
      ri_b.reshape(1, GROUP), lam.reshape(1, GROUP))


def _bdot(dims):
    return lambda a, b: lax.dot_general(a, b, (dims, ((0,), (0,))), preferred_element_type=F32)


_bnn = _bdot(((2,), (1,)))
_bnt = _bdot(((2,), (2,)))
_btn = _bdot(((1,), (1,)))


def _unit_lower_inverse(a_strict, n):
    r = lax.broadcasted_iota(jnp.int32, (1, n, n), 1)
    c = lax.broadcasted_iota(jnp.int32, (1, n, n), 2)
    t = jnp.where(r == c, 1.0, jnp.where((r // 2 == c // 2) & (r > c), a_strict, 0.0))
    m = 2
    while m < n:
        off = (r // (2 * m) == c // (2 * m)) & (r % (2 * m) >= m) & (c % (2 * m) < m)
        tb = t.astype(BF16)
        t = t + _bnn(tb, _bnn(jnp.where(off, a_strict, 0.0).astype(BF16), tb).astype(BF16))
        m *= 2
    return t


def _to_heads(x):
    n, rows, _ = x.shape
    parts = [x[:, :, h * HEAD_DIM:(h + 1) * HEAD_DIM] for h in range(HEADS)]
    return jnp.stack(parts, axis=1).reshape(n * HEADS, rows, HEAD_DIM)


def _from_heads(x):
    nh, rows, _ = x.shape
    x = x.reshape(nh // HEADS, HEADS, rows, HEAD_DIM)
    return jnp.concatenate([x[:, h] for h in range(HEADS)], axis=-1)


def _rwkv_chunk_kernel(*refs, ts, chunk, has_vres):
    (sr_ref, sk_ref, sv_ref, sl_ref, mu_ref, w0_ref, w2_ref, a0_ref, a2_ref, g2_ref, kk_ref, ka_ref, rk_ref,
     ones_ref) = refs[:14]
    refs = refs[14:]
    if has_vres:
        vf_ref, v0_ref, v1_ref, v2_ref = refs[:4]
        refs = refs[4:]
    mm_out, gm_out, qm_out, y0_out, bonus_out, g_out, v_out, carry_ref = refs

    @pl.when(pl.program_id(1) == 0)
    def _():
        carry_ref[...] = jnp.zeros_like(carry_ref)

    row0 = lax.broadcasted_iota(jnp.int32, (ts, GROUP), 0) == 0

    def shift_mix(ref, idx):
        s = ref[0]
        prev = jnp.where(row0, carry_ref[idx:idx + 1, :], pltpu.roll(s, 1, axis=0))
        carry_ref[idx:idx + 1, :] = s[ts - 1:ts, :]
        return s + (prev - s) * mu_ref[idx:idx + 1, :]

    r = shift_mix(sr_ref, 0)
    k = shift_mix(sk_ref, 1)
    v = shift_mix(sv_ref, 2)
    low = shift_mix(sl_ref, 3)

    zw = w0_ref[...] + _dot(jnp.tanh(low).astype(BF16), w2_ref[...])
    lw = -jnp.exp(_log_sigmoid(zw) - 0.5)
    a = _sigmoid(a0_ref[...] + _dot(low.astype(BF16), a2_ref[...]))
    g_out[0] = _dot(_sigmoid(low).astype(BF16), g2_ref[...])
    if has_vres:
        mix = _dot(_dot(v.astype(BF16), v1_ref[...]).astype(BF16), v2_ref[...])
        v = v + (vf_ref[0] - v) * _sigmoid(v0_ref[...] + mix)
    v_out[0] = v
    kk = k * kk_ref[...]
    ss = _dot(kk * kk, ones_ref[...], precision=HIGHEST)
    kk = kk / jnp.maximum(jnp.sqrt(ss), 1e-12)
    k = k * (1.0 + (a - 1.0) * ka_ref[...])
    bonus_out[0] = _dot(r * k * rk_ref[...], ones_ref[...], precision=HIGHEST) * v
    a_vec, b_vec = -kk, kk * a

    c = chunk
    rr = lax.broadcasted_iota(jnp.int32, (ts, ts), 0)
    cc = lax.broadcasted_iota(jnp.int32, (ts, ts), 1)
    chunk_tri = ((rr >= cc) & (rr // c == cc // c)).astype(F32)
    cum_all = _dot(chunk_tri, lw, precision=HIGHEST)
    strict = _tri(c, strict=True)[None]
    incl = _tri(c)[None]
    eye = (lax.broadcasted_iota(jnp.int32, (1, HEAD_DIM, HEAD_DIM), 1)
           == lax.broadcasted_iota(jnp.int32, (1, HEAD_DIM, HEAD_DIM), 2)).astype(F32)

    nq = ts // c
    per_chunk = lambda t: t.reshape(nq, c, GROUP)
    cum = per_chunk(cum_all)
    rq, kq, vq, aq, bq = (per_chunk(t) for t in (r, k, v, a_vec, b_vec))
    cum_ex = cum - per_chunk(lw)
    mid = cum[:, c // 2 - 1:c // 2, :]
    tot = cum[:, c - 1:c, :]
    e_fwd = jnp.exp(cum - mid)
    e_bwd = jnp.exp(mid - cum)
    e_end = jnp.exp(tot - cum)
    mxu = lambda t: _to_heads(t).astype(BF16)
    r_rel, k_rel = mxu(rq * e_fwd), mxu(kq * e_bwd)
    a_rel, b_rel = mxu(aq * jnp.exp(cum_ex - mid)), mxu(bq * e_bwd)
    a_abs, r_abs = mxu(aq * jnp.exp(cum_ex)), _to_heads(rq * jnp.exp(cum))
    k_end, b_end = mxu(kq * e_end), mxu(bq * e_end)
    gam = _to_heads(jnp.exp(tot))
    vh = mxu(vq)

    a_ab = jnp.where(strict, _bnt(a_rel, b_rel), 0.0)
    a_ak = jnp.where(strict, _bnt(a_rel, k_rel), 0.0).astype(BF16)
    a_rb = jnp.where(incl, _bnt(r_rel, b_rel), 0.0).astype(BF16)
    a_rk = jnp.where(incl, _bnt(r_rel, k_rel), 0.0).astype(BF16)
    t_inv = _unit_lower_inverse(a_ab, c).astype(BF16)
    p_mat = _bnn(t_inv, a_abs)
    u0 = _bnn(t_inv, _bnn(a_ak, vh).astype(BF16))
    p_b, u_b = p_mat.astype(BF16), u0.astype(BF16)
    qm = r_abs + _bnn(a_rb, p_b)
    y0 = _bnn(a_rb, u_b) + _bnn(a_rk, vh)
    mm_mat = eye * gam + _btn(p_b, b_end)
    gm = _btn(u_b, b_end) + _btn(vh, k_end)
    mm_out[0] = _from_heads(mm_mat).reshape(ts, GROUP)
    gm_out[0] = _from_heads(gm).reshape(ts, GROUP)
    qm_out[0] = _from_heads(qm).reshape(ts, GROUP)
    y0_out[0] = _from_heads(y0).reshape(ts, GROUP)


def rwkv_chunk(proj3, mu4, w0, w2p, a0, a2p, g2p, k_k, k_a, r_k, head_ones, vres, *, ts, chunk):
    b, s, _ = proj3.shape
    slab = lambda off: pl.BlockSpec((1, ts, GROUP), lambda i, j: (i, j, off // GROUP))
    vec = lambda: pl.BlockSpec((1, GROUP), lambda i, j: (0, 0))
    full = lambda shape: pl.BlockSpec(shape, lambda i, j: tuple(0 for _ in shape))
    tok = pl.BlockSpec((1, ts, GROUP), lambda i, j: (i, j, 0))
    in_specs = [slab(RWKV_R), slab(RWKV_K), slab(RWKV_V), slab(RWKV_LR), full((4, GROUP)),
                vec(), full((GROUP, GROUP)), vec(), full((GROUP, GROUP)), full((GROUP, GROUP)), vec(), vec(), vec(),
                full((GROUP, GROUP))]
    args = [proj3, proj3, proj3, proj3, mu4, w0.reshape(1, GROUP), w2p, a0.reshape(1, GROUP), a2p, g2p,
            k_k.reshape(1, GROUP), k_a.reshape(1, GROUP), r_k.reshape(1, GROUP), head_ones]
    if vres is not None:
        v_first, v0, v1p, v2p = vres
        in_specs += [tok, vec(), full((GROUP, 128)), full((128, GROUP))]
        args += [v_first, v0.reshape(1, GROUP), v1p, v2p]
    return pl.pallas_call(
        functools.partial(_rwkv_chunk_kernel, ts=ts, chunk=chunk, has_vres=vres is not None),
        grid=(b, s // ts),
        in_specs=in_specs,
        out_specs=[tok] * 7,
        out_shape=[jax.ShapeDtypeStruct((b, s, GROUP), F32)] * 7,
        scratch_shapes=[pltpu.VMEM((4, GROUP), F32)],
        compiler_params=_cparams(("parallel", "arbitrary")),
        name="rwkv_chunk",
    )(*args)


def _rwkv_state_kernel(mm_ref, gm_ref, qm_ref, y0_ref, bonus_ref, g_ref, gw_ref, gb_ref, o_ref, state_ref):
    @pl.when(pl.program_id(0) == 0)
    def _():
        state_ref[...] = jnp.zeros_like(state_ref)

    s0 = state_ref[...]
    y = _mm3(_bnt, _to_heads(qm_ref[...]), s0) + _to_heads(y0_ref[...])
    state_ref[...] = _mm3(_bnn, s0, _to_heads(mm_ref[...])) + _to_heads(gm_ref[...])
    mu = jnp.mean(y, axis=-1, keepdims=True)
    var = jnp.mean(jnp.square(y - mu), axis=-1, keepdims=True)
    yn = _from_heads((y - mu) * lax.rsqrt(var + RWKV_GN_EPS)) * gw_ref[...] + gb_ref[...]
    o_ref[...] = ((yn + bonus_ref[...]) * g_ref[...]).astype(o_ref.dtype)


def rwkv_state(mm, gm, qm, y0, bonus, g, gn_w, gn_b, *, chunk):
    bsz, s, _ = mm.shape
    tok = pl.BlockSpec((bsz, chunk, GROUP), lambda j: (0, j, 0))
    vec = pl.BlockSpec((1, GROUP), lambda j: (0, 0))
    return pl.pallas_call(
        _rwkv_state_kernel,
        grid=(s // chunk,),
        in_specs=[tok] * 6 + [vec] * 2,
        out_specs=tok,
        out_shape=jax.ShapeDtypeStruct((bsz, s, GROUP), BF16),
        scratch_shapes=[pltpu.VMEM((bsz * HEADS, HEAD_DIM, HEAD_DIM), F32)],
        compiler_params=_cparams(("arbitrary",)),
        name="rwkv_state",
    )(mm, gm, qm, y0, bonus, g, gn_w.reshape(1, GROUP), gn_b.reshape(1, GROUP))


def _ret_kernel(q_ref, k_ref, v_ref, g_ref, cos_ref, sin_ref, dmat_ref, xi_ref, zeta_ref, cd_ref, gw_ref, o_ref,
                state_ref, *, chunk):
    @pl.when(pl.program_id(1) == 0)
    def _():
        state_ref[...] = jnp.zeros_like(state_ref)

    lane = lax.broadcasted_iota(jnp.int32, (chunk, GROUP), 1)
    first_half = (lane % HEAD_DIM) < (HEAD_DIM // 2)
    cos, sin = cos_ref[...], sin_ref[...]

    def rotary(t):
        partner = jnp.where(first_half, pltpu.roll(t, GROUP - HEAD_DIM // 2, axis=1),
                            pltpu.roll(t, HEAD_DIM // 2, axis=1))
        return t * cos + partner * sin

    q = rotary(q_ref[0])
    k = rotary(k_ref[0]) * (HEAD_DIM ** -0.5)
    v = v_ref[0]
    q_cross = (q * xi_ref[...]).astype(BF16)
    k_decay = (k * zeta_ref[...]).astype(BF16)
    qb, kb, vb = q.astype(BF16), k.astype(BF16), v.astype(BF16)
    outs = []
    for h in range(HEADS):
        sl = slice(h * HEAD_DIM, (h + 1) * HEAD_DIM)
        inner = _dot_nt(qb[:, sl], kb[:, sl]) * dmat_ref[h]
        state = state_ref[h]
        o = _dot(inner.astype(BF16), vb[:, sl]) + _dot(q_cross[:, sl], state.astype(BF16))
        state_ref[h] = state * cd_ref[:, sl] + _dot_tn(k_decay[:, sl], vb[:, sl])
        outs.append(o * lax.rsqrt(jnp.mean(o * o, axis=-1, keepdims=True) + NORM_EPS))
    g = g_ref[0]
    o_ref[0] = (jnp.concatenate(outs, axis=-1) * gw_ref[...] * (g * _sigmoid(g))).astype(o_ref.dtype)


def retention(proj3, cos_t, sin_t, dmat, xi, zeta, cd, gn_w, *, chunk):
    b, s, _ = proj3.shape
    slab = lambda off: pl.BlockSpec((1, chunk, GROUP), lambda i, j: (i, j, off // GROUP))
    full = lambda shape: pl.BlockSpec(shape, lambda i, j: tuple(0 for _ in shape))
    return pl.pallas_call(
        functools.partial(_ret_kernel, chunk=chunk),
        grid=(b, s // chunk),
        in_specs=[slab(RET_Q), slab(RET_K), slab(RET_V), slab(RET_G),
                  pl.BlockSpec((chunk, GROUP), lambda i, j: (j, 0)),
                  pl.BlockSpec((chunk, GROUP), lambda i, j: (j, 0)),
                  full((HEADS, chunk, chunk)), full((chunk, GROUP)), full((chunk, GROUP)),
                  full((1, GROUP)), full((1, GROUP))],
        out_specs=pl.BlockSpec((1, chunk, GROUP), lambda i, j: (i, j, 0)),
        out_shape=jax.ShapeDtypeStruct((b, s, GROUP), BF16),
        scratch_shapes=[pltpu.VMEM((HEADS, HEAD_DIM, HEAD_DIM), F32)],
        compiler_params=_cparams(("parallel", "arbitrary")),
        name="retention",
    )(proj3, proj3, proj3, proj3, cos_t, sin_t, dmat, xi, zeta, cd, gn_w.reshape(1, GROUP))


def _retention_tables(s, chunk):
    half = HEAD_DIM // 2
    inv = 1.0 / (RET_THETA ** jnp.linspace(0.0, 1.0, half, dtype=F32))
    ang = jnp.arange(s, dtype=F32)[:, None] * inv[None, :]
    cos, sin = jnp.cos(ang), jnp.sin(ang)
    cos_t = jnp.tile(jnp.concatenate([cos, cos], axis=-1), (1, HEADS))
    sin_t = jnp.tile(jnp.concatenate([-sin, sin], axis=-1), (1, HEADS))
    lg = jnp.log(1.0 - 2.0 ** (-5.0 - jnp.arange(HEADS, dtype=F32)))
    n = jnp.arange(chunk, dtype=F32)
    diff = n[:, None] - n[None, :]
    dmat = jnp.where(diff >= 0, jnp.exp(lg[:, None, None] * jnp.maximum(diff, 0.0)), 0.0)
    zeta = jnp.exp(lg[:, None] * (chunk - 1.0 - n)[None, :])
    xi = jnp.exp(lg[:, None] * (n + 1.0)[None, :])
    per_lane = lambda t: jnp.repeat(t.T, HEAD_DIM, axis=1)
    cd = jnp.repeat(jnp.exp(lg * chunk), HEAD_DIM)[None, :]
    return cos_t, sin_t, dmat, per_lane(xi), per_lane(zeta), cd


def _out_proj_kernel(m0_ref, m1_ref, m2_ref, m3_ref, w_ref, x_ref, g_ref, o_ref):
    acc = _dot(m0_ref[...], w_ref[0:GROUP, :])
    for idx, m_ref in enumerate((m1_ref, m2_ref, m3_ref), start=1):
        acc = acc + _dot(m_ref[...], w_ref[idx * GROUP:(idx + 1) * GROUP, :])
    o_ref[...] = x_ref[...] + _rms(acc, g_ref[...])


def out_proj(mixed, w, x, g, *, tm):
    n, d = x.shape
    mix_spec = pl.BlockSpec((tm, GROUP), lambda i: (i, 0))
    return pl.pallas_call(
        _out_proj_kernel,
        grid=(n // tm,),
        in_specs=[mix_spec] * 4 + [pl.BlockSpec((d, d), lambda i: (0, 0)),
                                   pl.BlockSpec((tm, d), lambda i: (i, 0)),
                                   pl.BlockSpec((1, d), lambda i: (0, 0))],
        out_specs=pl.BlockSpec((tm, d), lambda i: (i, 0)),
        out_shape=jax.ShapeDtypeStruct((n, d), F32),
        compiler_params=_cparams(("parallel",)),
        name="out_proj",
    )(*mixed, w, x, g.reshape(1, d))


def _xattn_kernel(x_ref, kv_ref, wq_ref, wo_ref, gpre_ref, gpost_ref, o_ref):
    x = x_ref[0]
    xn = _rms(x, gpre_ref[...]).astype(BF16)
    q = (_dot(xn, wq_ref[...]) * (XATTN_HEAD_DIM ** -0.5)).astype(BF16)
    outs = []
    for h in range(XATTN_HEADS):
        sl = slice(h * XATTN_HEAD_DIM, (h + 1) * XATTN_HEAD_DIM)
        s = _dot_nt(q[:, sl], kv_ref[0, :, sl])
        e = jnp.exp(s - jnp.max(s, axis=-1, keepdims=True))
        p = e / jnp.sum(e, axis=-1, keepdims=True)
        outs.append(_dot(p.astype(BF16), kv_ref[0, :, D_MODEL + h * XATTN_HEAD_DIM:D_MODEL + (h + 1) * XATTN_HEAD_DIM]))
    o = jnp.concatenate(outs, axis=-1).astype(BF16)
    o_ref[0] = x + _rms(_dot(o, wo_ref[...]), gpost_ref[...])


def cross_attention(x3, kv, wq, wo, g_pre, g_post, *, tm):
    b, s, d = x3.shape
    m = kv.shape[1]
    return pl.pallas_call(
        _xattn_kernel,
        grid=(b, s // tm),
        in_specs=[pl.BlockSpec((1, tm, d), lambda bi, i: (bi, i, 0)),
                  pl.BlockSpec((1, m, 2 * d), lambda bi, i: (bi, 0, 0)),
                  pl.BlockSpec((d, d), lambda bi, i: (0, 0)),
                  pl.BlockSpec((d, d), lambda bi, i: (0, 0)),
                  pl.BlockSpec((1, d), lambda bi, i: (0, 0)),
                  pl.BlockSpec((1, d), lambda bi, i: (0, 0))],
        out_specs=pl.BlockSpec((1, tm, d), lambda bi, i: (bi, i, 0)),
        out_shape=jax.ShapeDtypeStruct((b, s, d), F32),
        compiler_params=_cparams(("parallel", "arbitrary")),
        name="cross_attention",
    )(x3, kv, wq, wo, g_pre.reshape(1, d), g_post.reshape(1, d))


def _mlp_kernel(x_ref, w1_ref, w2_ref, gpre_ref, gpost_ref, o_ref, *, ff_tile):
    x = x_ref[...]
    xn = _rms(x, gpre_ref[...]).astype(BF16)
    d_ff = w1_ref.shape[1]
    acc = None
    for c in range(d_ff // ff_tile):
        hid = jnp.square(jnp.maximum(_dot(xn, w1_ref[:, c * ff_tile:(c + 1) * ff_tile]), 0.0)).astype(BF16)
        part = _dot(hid, w2_ref[c * ff_tile:(c + 1) * ff_tile, :])
        acc = part if acc is None else acc + part
    o_ref[...] = x + _rms(acc, gpost_ref[...])


def mlp(x, w1, w2, g_pre, g_post, *, tm, ff_tile):
    n, d = x.shape
    d_ff = w1.shape[1]
    return pl.pallas_call(
        functools.partial(_mlp_kernel, ff_tile=ff_tile),
        grid=(n // tm,),
        in_specs=[pl.BlockSpec((tm, d), lambda i: (i, 0)),
                  pl.BlockSpec((d, d_ff), lambda i: (0, 0), pipeline_mode=pl.Buffered(1)),
                  pl.BlockSpec((d_ff, d), lambda i: (0, 0), pipeline_mode=pl.Buffered(1)),
                  pl.BlockSpec((1, d), lambda i: (0, 0)),
                  pl.BlockSpec((1, d), lambda i: (0, 0))],
        out_specs=pl.BlockSpec((tm, d), lambda i: (i, 0)),
        out_shape=jax.ShapeDtypeStruct((n, d), F32),
        compiler_params=_cparams(("parallel",)),
        name="mlp",
    )(x, w1, w2, g_pre.reshape(1, d), g_post.reshape(1, d))


def _block_diag(w):
    h, n, _ = w.shape
    eye = jnp.eye(h, dtype=w.dtype)
    return (eye[:, None, :, None] * w[:, :, None, :]).reshape(h * n, h * n)


def _pad_rows(w, start, total):
    return jnp.zeros((total, w.shape[1]), w.dtype).at[start:start + w.shape[0]].set(w)


def _tile(n, pref):
    return pref if n % pref == 0 else n


def kernel(x, mem, norm_mix_pre, norm_mix_post, norm_xa_pre, norm_xa_post, norm_mem, norm_mlp_pre, norm_mlp_post, w_in, w_out, fox_f_bias, lru_conv_w, lru_conv_b, lru_ra_w, lru_ra_b, lru_ri_w, lru_ri_b, lru_lambda, rwkv_mu, rwkv_w0, rwkv_w2, rwkv_a0, rwkv_a2, rwkv_g2, rwkv_k_k, rwkv_k_a, rwkv_r_k, rwkv_gn_w, rwkv_gn_b, rwkv_v0, rwkv_v1, rwkv_v2, ret_gn_w, xa_wq, xa_wk, xa_wv, xa_wo, mlp_w1, mlp_w2):
    bsz, seq, d = x.shape
    depth = w_in.shape[0]
    n_tok = bsz * seq
    mem_len = mem.shape[1]
    tm = _tile(n_tok, 512)
    tq = _tile(seq, 256)
    ts = _tile(seq, 256)
    ret_tables = _retention_tables(seq, RET_CHUNK)
    head_ones = _block_diag(jnp.ones((HEADS, HEAD_DIM, HEAD_DIM), F32))

    x2 = x.reshape(n_tok, d)
    v_first = None
    for l in range(depth):
        w = w_in[l]
        w_pad = jnp.concatenate([w[:, :FOX_REAL], jnp.zeros((d, LRU_X - FOX_REAL), w.dtype), w[:, FOX_REAL:]], axis=1)
        proj = norm_matmul(x2, norm_mix_pre[l], w_pad.astype(BF16), tm=tm, tn=IN_PAD // 4, out_dtype=F32)
        proj3 = proj.reshape(bsz, seq, IN_PAD)

        f_bias = jnp.zeros((1, GROUP), F32).at[0, :HEADS].set(fox_f_bias[l])
        cum = fox_cum(proj3, f_bias, tc=ts)
        heads = lambda t: t.reshape(bsz, seq, HEADS, HEAD_DIM).transpose(0, 2, 1, 3)
        fq = heads((proj3[..., FOX_Q:FOX_Q + GROUP] * (HEAD_DIM ** -0.5)).astype(BF16))
        fk = heads(proj3[..., FOX_K:FOX_K + GROUP].astype(BF16))
        fv = heads(proj3[..., FOX_V:FOX_V + GROUP].astype(BF16))
        ck = cum[..., :HEADS].transpose(0, 2, 1).reshape(bsz, HEADS, seq // tq, tq)
        fox_out = fox_attention(fq, fk, fv, cum, ck, tq=tq)

        lru_out = rglru(proj3, lru_conv_w[l], lru_conv_b[l], _block_diag(lru_ra_w[l]).astype(BF16), lru_ra_b[l],
                        _block_diag(lru_ri_w[l]).astype(BF16), lru_ri_b[l], lru_lambda[l], ts=ts)

        vres = None
        if l > 0:
            vres = (v_first, rwkv_v0[l - 1],
                    jnp.pad(rwkv_v1[l - 1], ((0, 0), (0, 128 - RWKV_V_RANK))).astype(BF16),
                    _pad_rows(rwkv_v2[l - 1], 0, 128).astype(BF16))
        mm_, gm_, qm_, y0_, bonus_, g_, v_ = rwkv_chunk(
            proj3, rwkv_mu[l].reshape(4, GROUP), rwkv_w0[l],
            _pad_rows(rwkv_w2[l], 0, GROUP).astype(BF16), rwkv_a0[l],
            _pad_rows(rwkv_a2[l], RWKV_W_RANK, GROUP).astype(BF16),
            _pad_rows(rwkv_g2[l], RWKV_W_RANK + RWKV_A_RANK, GROUP).astype(BF16),
            rwkv_k_k[l], rwkv_k_a[l], rwkv_r_k[l], head_ones, vres, ts=_tile(seq, 4 * RWKV_CHUNK), chunk=RWKV_CHUNK)
        if l == 0:
            v_first = v_
        rwkv_out = rwkv_state(mm_, gm_, qm_, y0_, bonus_, g_, rwkv_gn_w[l], rwkv_gn_b[l], chunk=RWKV_CHUNK)

        ret_out = retention(proj3, *ret_tables, ret_gn_w[l], chunk=RET_CHUNK)

        x2 = out_proj([m.reshape(n_tok, GROUP) for m in (fox_out, lru_out, rwkv_out, ret_out)],
                      w_out[l].astype(BF16), x2, norm_mix_post[l], tm=tm)

        wkv = jnp.concatenate([xa_wk[l], xa_wv[l]], axis=1).astype(BF16)
        kv = norm_matmul(mem.reshape(bsz * mem_len, d), norm_mem[l], wkv, tm=_tile(bsz * mem_len, 512),
                         tn=1024, out_dtype=BF16).reshape(bsz, mem_len, 2 * d)
        x2 = cross_attention(x2.reshape(bsz, seq, d), kv, xa_wq[l].astype(BF16), xa_wo[l].astype(BF16),
                             norm_xa_pre[l], norm_xa_post[l], tm=_tile(seq, 512)).reshape(n_tok, d)

        x2 = mlp(x2, mlp_w1[l].astype(BF16), mlp_w2[l].astype(BF16), norm_mlp_pre[l], norm_mlp_post[l],
                 tm=tm, ff_tile=1024)
    return x2.reshape(bsz, seq, d)
```

```python
import functools
import math

import jax
import jax.numpy as jnp
import numpy as np
from jax import lax
from jax.experimental import pallas as pl
from jax.experimental.pallas import tpu as pltpu

F32 = jnp.float32
BF16 = jnp.bfloat16
HIGHEST = lax.Precision.HIGHEST

D_MODEL = 1024
GROUP = 256
HEADS = 4
HEAD_DIM = 64
CONV_WIDTH = 4
LRU_C = 8.0
RET_THETA = 10000.0
RET_CHUNK = 128
RWKV_W_RANK, RWKV_A_RANK, RWKV_G_RANK, RWKV_V_RANK = 64, 64, 128, 32
RWKV_GN_EPS = 64e-5
XATTN_HEADS = 4
XATTN_HEAD_DIM = D_MODEL // XATTN_HEADS
NORM_EPS = 1e-6
NEG_BIG = -1e30

FOX_Q, FOX_K, FOX_V, FOX_F = 0, 256, 512, 768
LRU_X, LRU_Y = 1024, 1280
RWKV_R, RWKV_K, RWKV_V, RWKV_LR = 1536, 1792, 2048, 2304
RET_Q, RET_K, RET_V, RET_G = 2560, 2816, 3072, 3328
IN_PAD = 3584
FOX_REAL = 3 * GROUP + HEADS

VMEM_LIMIT = 56 * 1024 * 1024
RWKV_CHUNK = 64


def _cparams(sem):
    return pltpu.CompilerParams(dimension_semantics=sem, vmem_limit_bytes=VMEM_LIMIT)


def _rms(x, g):
    return x * lax.rsqrt(jnp.mean(x * x, axis=-1, keepdims=True) + NORM_EPS) * g


def _log_sigmoid(x):
    return jnp.minimum(x, 0.0) - jnp.log1p(jnp.exp(-jnp.abs(x)))


def _sigmoid(x):
    return 1.0 / (1.0 + jnp.exp(-x))


def _dot(a, b, **kw):
    return jnp.dot(a, b, preferred_element_type=F32, **kw)


def _dot_nt(a, b, **kw):
    return lax.dot_general(a, b, (((1,), (1,)), ((), ())), preferred_element_type=F32, **kw)


def _dot_tn(a, b, **kw):
    return lax.dot_general(a, b, (((0,), (0,)), ((), ())), preferred_element_type=F32, **kw)


def _split_bf16(x):
    hi = x.astype(BF16)
    return hi, (x - hi.astype(F32)).astype(BF16)


def _mm1(dot, a, b):
    return dot(a.astype(BF16), b.astype(BF16))


def _mm3(dot, a, b):
    ah, al = _split_bf16(a)
    bh, bl = _split_bf16(b)
    return dot(ah, bh) + (dot(ah, bl) + dot(al, bh))


def _tri(n, strict=False):
    r = lax.broadcasted_iota(jnp.int32, (n, n), 0)
    c = lax.broadcasted_iota(jnp.int32, (n, n), 1)
    return (r > c) if strict else (r >= c)


def _norm_matmul_kernel(x_ref, g_ref, w_ref, o_ref, xn_ref):
    @pl.when(pl.program_id(1) == 0)
    def _():
        xn_ref[...] = _rms(x_ref[...], g_ref[...]).astype(BF16)

    o_ref[...] = _dot(xn_ref[...], w_ref[...]).astype(o_ref.dtype)


def norm_matmul(x, g, w, *, tm, tn, out_dtype):
    n, d = x.shape
    width = w.shape[1]
    return pl.pallas_call(
        _norm_matmul_kernel,
        grid=(n // tm, width // tn),
        in_specs=[pl.BlockSpec((tm, d), lambda i, j: (i, 0)),
                  pl.BlockSpec((1, d), lambda i, j: (0, 0)),
                  pl.BlockSpec((d, tn), lambda i, j: (0, j))],
        out_specs=pl.BlockSpec((tm, tn), lambda i, j: (i, j)),
        out_shape=jax.ShapeDtypeStruct((n, width), out_dtype),
        scratch_shapes=[pltpu.VMEM((tm, d), BF16)],
        compiler_params=_cparams(("parallel", "arbitrary")),
        name="norm_matmul",
    )(x, g.reshape(1, d), w)


def _in_proj_kernel(x_ref, g_ref, w_ref, o_ref, *, tn):
    xn = _rms(x_ref[...], g_ref[...]).astype(BF16)
    for c in range(w_ref.shape[1] // tn):
        o_ref[:, c * tn:(c + 1) * tn] = _dot(xn, w_ref[:, c * tn:(c + 1) * tn])


def in_proj(x, g, w, *, tm, tn):
    n, d = x.shape
    width = w.shape[1]
    return pl.pallas_call(
        functools.partial(_in_proj_kernel, tn=tn),
        grid=(n // tm,),
        in_specs=[pl.BlockSpec((tm, d), lambda i: (i, 0)),
                  pl.BlockSpec((1, d), lambda i: (0, 0)),
                  pl.BlockSpec((d, width), lambda i: (0, 0), pipeline_mode=pl.Buffered(1))],
        out_specs=pl.BlockSpec((tm, width), lambda i: (i, 0)),
        out_shape=jax.ShapeDtypeStruct((n, width), F32),
        compiler_params=_cparams(("parallel",)),
        name="in_proj",
    )(x, g.reshape(1, d), w)


FOX_AUG = 128


def _fox_prep_kernel(q_ref, k_ref, v_ref, f_ref, b_ref, qa_ref, ka_ref, vt_ref, carry_ref):
    @pl.when(pl.program_id(1) == 0)
    def _():
        carry_ref[...] = jnp.zeros_like(carry_ref)

    lf = _log_sigmoid(f_ref[0] + b_ref[...])
    tc = lf.shape[0]
    cum = _dot(_tri(tc).astype(F32), lf, precision=HIGHEST) + carry_ref[...]
    carry_ref[...] = cum[tc - 1:tc, :]

    lane = lax.broadcasted_iota(jnp.int32, (tc, FOX_AUG), 1)
    feat = lane < HEAD_DIM
    ones = jnp.where((lane >= HEAD_DIM) & (lane < HEAD_DIM + 3), 1.0, 0.0)
    q = q_ref[0] * (HEAD_DIM ** -0.5)
    k = k_ref[0]
    for h in range(HEADS):
        pair = slice((h // 2) * FOX_AUG, (h // 2 + 1) * FOX_AUG)
        qt, kt = q[:, pair], k[:, pair]
        if h % 2:
            qt, kt = pltpu.roll(qt, HEAD_DIM, axis=1), pltpu.roll(kt, HEAD_DIM, axis=1)
        neg_c = jnp.broadcast_to(-cum[:, h:h + 1], (tc, FOX_AUG))
        c_hi = neg_c.astype(BF16).astype(F32)
        rest = neg_c - c_hi
        c_mid = rest.astype(BF16).astype(F32)
        bias = jnp.where(lane == HEAD_DIM, c_hi, jnp.where(lane == HEAD_DIM + 1, c_mid,
                                                           jnp.where(lane == HEAD_DIM + 2, rest - c_mid, 0.0)))
        qa_ref[0, h] = jnp.where(feat, qt, ones).astype(BF16)
        ka_ref[0, h] = jnp.where(feat, kt, bias).astype(BF16)
    vt_ref[0, :, 0] = v_ref[0].T.reshape(HEADS, HEAD_DIM, tc).astype(BF16)


def fox_prep(proj3, f_bias_pad, *, tc):
    b, s, _ = proj3.shape
    slab = lambda off: pl.BlockSpec((1, tc, GROUP), lambda i, j: (i, j, off // GROUP))
    aug = pl.BlockSpec((1, HEADS, tc, FOX_AUG), lambda i, j: (i, 0, j, 0))
    return pl.pallas_call(
        _fox_prep_kernel,
        grid=(b, s // tc),
        in_specs=[slab(FOX_Q), slab(FOX_K), slab(FOX_V), slab(FOX_F), pl.BlockSpec((1, GROUP), lambda i, j: (0, 0))],
        out_specs=[aug, aug, pl.BlockSpec((1, HEADS, 1, HEAD_DIM, tc), lambda i, j: (i, 0, j, 0, 0))],
        out_shape=[jax.ShapeDtypeStruct((b, HEADS, s, FOX_AUG), BF16),
                   jax.ShapeDtypeStruct((b, HEADS, s, FOX_AUG), BF16),
                   jax.ShapeDtypeStruct((b, HEADS, s // tc, HEAD_DIM, tc), BF16)],
        scratch_shapes=[pltpu.VMEM((1, GROUP), F32)],
        compiler_params=_cparams(("parallel", "arbitrary")),
        name="fox_prep",
    )(proj3, proj3, proj3, proj3, f_bias_pad)


def _fox_attn_kernel(q_ref, k_ref, v_ref, o_ref, *, tq, tk):
    i = pl.program_id(1)
    ratio = tq // tk
    key = lax.broadcasted_iota(jnp.int32, (1, tk, tq), 1)
    qry = lax.broadcasted_iota(jnp.int32, (1, tk, tq), 2)
    qa = q_ref[0]

    def tile(j, diag=None):
        ka = k_ref[0, :, pl.ds(pl.multiple_of(j * tk, tk), tk), :]
        s = _bnt(ka, qa)
        if diag is not None:
            s = jnp.where(key + diag * tk <= qry, s, NEG_BIG)
        return s, v_ref[0, :, j]

    def update(carry, s, vt):
        m, l, acc = carry
        m_new = jnp.maximum(m, jnp.max(s, axis=1, keepdims=True))
        alpha = jnp.exp(m - m_new)
        p = jnp.exp(s - m_new)
        return (m_new, alpha * l + jnp.sum(p, axis=1, keepdims=True),
                alpha * acc + _bnn(vt, p.astype(BF16)))

    s, vt = tile(i * ratio, 0)
    m = jnp.max(s, axis=1, keepdims=True)
    p = jnp.exp(s - m)
    carry = (m, jnp.sum(p, axis=1, keepdims=True), _bnn(vt, p.astype(BF16)))
    for d in range(1, ratio):
        carry = update(carry, *tile(i * ratio + d, d))
    m, l, acc = lax.fori_loop(0, i * ratio, lambda j, c: update(c, *tile(j)), carry)
    o_ref[0] = (acc / l).reshape(GROUP, tq).T.astype(o_ref.dtype)


def fox_attention(qa, ka, vt, *, tq):
    b, h, s, _ = qa.shape
    nk, tk = vt.shape[2], vt.shape[4]
    return pl.pallas_call(
        functools.partial(_fox_attn_kernel, tq=tq, tk=tk),
        grid=(b, s // tq),
        in_specs=[pl.BlockSpec((1, h, tq, FOX_AUG), lambda bi, i: (bi, 0, i, 0)),
                  pl.BlockSpec((1, h, s, FOX_AUG), lambda bi, i: (bi, 0, 0, 0)),
                  pl.BlockSpec((1, h, nk, HEAD_DIM, tk), lambda bi, i: (bi, 0, 0, 0, 0))],
        out_specs=pl.BlockSpec((1, tq, GROUP), lambda bi, i: (bi, i, 0)),
        out_shape=jax.ShapeDtypeStruct((b, s, GROUP), BF16),
        compiler_params=_cparams(("parallel", "arbitrary")),
        name="fox_attention",
    )(qa, ka, vt)


def _lru_kernel(x_ref, y_ref, cw_ref, cb_ref, wra_ref, bra_ref, wri_ref, bri_ref, lam_ref, o_ref,
                buf_ref, h_ref, *, ts):
    @pl.when(pl.program_id(1) == 0)
    def _():
        buf_ref[0:8, :] = jnp.zeros((8, GROUP), F32)
        h_ref[...] = jnp.zeros_like(h_ref)

    xb = x_ref[0]
    buf_ref[8:8 + ts, :] = xb
    xc = cb_ref[...] + buf_ref[5:5 + ts, :] * cw_ref[0:1, :]
    for j in range(1, CONV_WIDTH):
        xc = xc + buf_ref[5 + j:5 + j + ts, :] * cw_ref[j:j + 1, :]
    buf_ref[0:8, :] = xb[ts - 8:ts, :]

    xcb = xc.astype(BF16)
    r = _sigmoid(_dot(xcb, wra_ref[...]) + bra_ref[...])
    gate_i = _sigmoid(_dot(xcb, wri_ref[...]) + bri_ref[...])
    log_a = LRU_C * r * _log_sigmoid(lam_ref[...])
    a = jnp.exp(log_a)
    z = 2.0 * log_a
    mult = jnp.sqrt(jnp.maximum(-jnp.tanh(0.5 * z) * (jnp.exp(z) + 1.0), 0.0))
    u = mult * (gate_i * xc)

    row = lax.broadcasted_iota(jnp.int32, (ts, GROUP), 0)
    pa, pb = a, u
    d = 1
    while d < ts:
        sa = pltpu.roll(pa, d, axis=0)
        sb = pltpu.roll(pb, d, axis=0)
        valid = row >= d
        pb = jnp.where(valid, pa * sb + pb, pb)
        pa = jnp.where(valid, pa * sa, pa)
        d *= 2
    hseq = pa * h_ref[...] + pb
    h_ref[...] = hseq[ts - 1:ts, :]

    y = y_ref[0]
    gelu = 0.5 * y * (1.0 + jnp.tanh(math.sqrt(2.0 / math.pi) * (y + 0.044715 * (y * y * y))))
    o_ref[0] = (hseq * gelu).astype(o_ref.dtype)


def rglru(proj3, conv_w, conv_b, wra_bd, ra_b, wri_bd, ri_b, lam, *, ts):
    b, s, _ = proj3.shape
    vec = lambda: pl.BlockSpec((1, GROUP), lambda i, j: (0, 0))
    mat = lambda: pl.BlockSpec((GROUP, GROUP), lambda i, j: (0, 0))
    return pl.pallas_call(
        functools.partial(_lru_kernel, ts=ts),
        grid=(b, s // ts),
        in_specs=[pl.BlockSpec((1, ts, GROUP), lambda i, j: (i, j, LRU_X // GROUP)),
                  pl.BlockSpec((1, ts, GROUP), lambda i, j: (i, j, LRU_Y // GROUP)),
                  pl.BlockSpec((CONV_WIDTH, GROUP), lambda i, j: (0, 0)),
                  vec(), mat(), vec(), mat(), vec(), vec()],
        out_specs=pl.BlockSpec((1, ts, GROUP), lambda i, j: (i, j, 0)),
        out_shape=jax.ShapeDtypeStruct((b, s, GROUP), BF16),
        scratch_shapes=[pltpu.VMEM((ts + 8, GROUP), F32), pltpu.VMEM((1, GROUP), F32)],
        compiler_params=_cparams(("parallel", "arbitrary")),
        name="rglru",
    )(proj3, proj3, conv_w, conv_b.reshape(1, GROUP), wra_bd, ra_b.reshape(1, GROUP), wri_bd,
      ri_b.reshape(1, GROUP), lam.reshape(1, GROUP))


def _bdot(dims):
    return lambda a, b: lax.dot_general(a, b, (dims, ((0,), (0,))), preferred_element_type=F32)


_bnn = _bdot(((2,), (1,)))
_bnt = _bdot(((2,), (2,)))
_btn = _bdot(((1,), (1,)))


def _unit_lower_inverse(a_strict, n):
    r = lax.broadcasted_iota(jnp.int32, (1, n, n), 1)
    c = lax.broadcasted_iota(jnp.int32, (1, n, n), 2)
    t = jnp.where(r == c, 1.0, jnp.where((r // 2 == c // 2) & (r > c), a_strict, 0.0))
    m = 2
    while m < n:
        off = (r // (2 * m) == c // (2 * m)) & (r % (2 * m) >= m) & (c % (2 * m) < m)
        tb = t.astype(BF16)
        t = t + _bnn(tb, _bnn(jnp.where(off, a_strict, 0.0).astype(BF16), tb).astype(BF16))
        m *= 2
    return t


def _to_heads(x):
    n, rows, _ = x.shape
    parts = [x[:, :, h * HEAD_DIM:(h + 1) * HEAD_DIM] for h in range(HEADS)]
    return jnp.stack(parts, axis=1).reshape(n * HEADS, rows, HEAD_DIM)


def _from_heads(x):
    nh, rows, _ = x.shape
    x = x.reshape(nh // HEADS, HEADS, rows, HEAD_DIM)
    return jnp.concatenate([x[:, h] for h in range(HEADS)], axis=-1)


def _rwkv_chunk_kernel(*refs, ts, chunk, has_vres):
    (sr_ref, sk_ref, sv_ref, sl_ref, mu_ref, w0_ref, w2_ref, a0_ref, a2_ref, g2_ref, kk_ref, ka_ref, rk_ref,
     ones_ref) = refs[:14]
    refs = refs[14:]
    if has_vres:
        vf_ref, v0_ref, v1_ref, v2_ref = refs[:4]
        refs = refs[4:]
    mm_out, gm_out, qm_out, y0_out, bonus_out, g_out, v_out, carry_ref = refs

    @pl.when(pl.program_id(1) == 0)
    def _():
        carry_ref[...] = jnp.zeros_like(carry_ref)

    row0 = lax.broadcasted_iota(jnp.int32, (ts, GROUP), 0) == 0

    def shift_mix(ref, idx):
        s = ref[0]
        prev = jnp.where(row0, carry_ref[idx:idx + 1, :], pltpu.roll(s, 1, axis=0))
        carry_ref[idx:idx + 1, :] = s[ts - 1:ts, :]
        return s + (prev - s) * mu_ref[idx:idx + 1, :]

    r = shift_mix(sr_ref, 0)
    k = shift_mix(sk_ref, 1)
    v = shift_mix(sv_ref, 2)
    low = shift_mix(sl_ref, 3)

    zw = w0_ref[...] + _dot(jnp.tanh(low).astype(BF16), w2_ref[...])
    lw = -jnp.exp(_log_sigmoid(zw) - 0.5)
    a = _sigmoid(a0_ref[...] + _dot(low.astype(BF16), a2_ref[...]))
    g_out[0] = _dot(_sigmoid(low).astype(BF16), g2_ref[...])
    if has_vres:
        mix = _dot(_dot(v.astype(BF16), v1_ref[...]).astype(BF16), v2_ref[...])
        v = v + (vf_ref[0] - v) * _sigmoid(v0_ref[...] + mix)
    v_out[0] = v
    kk = k * kk_ref[...]
    ss = _dot(kk * kk, ones_ref[...], precision=HIGHEST)
    kk = kk / jnp.maximum(jnp.sqrt(ss), 1e-12)
    k = k * (1.0 + (a - 1.0) * ka_ref[...])
    bonus_out[0] = _dot(r * k * rk_ref[...], ones_ref[...], precision=HIGHEST) * v
    a_vec, b_vec = -kk, kk * a

    c = chunk
    rr = lax.broadcasted_iota(jnp.int32, (ts, ts), 0)
    cc = lax.broadcasted_iota(jnp.int32, (ts, ts), 1)
    chunk_tri = ((rr >= cc) & (rr // c == cc // c)).astype(F32)
    cum_all = _dot(chunk_tri, lw, precision=HIGHEST)
    strict = _tri(c, strict=True)[None]
    incl = _tri(c)[None]
    eye = (lax.broadcasted_iota(jnp.int32, (1, HEAD_DIM, HEAD_DIM), 1)
           == lax.broadcasted_iota(jnp.int32, (1, HEAD_DIM, HEAD_DIM), 2)).astype(F32)

    nq = ts // c
    per_chunk = lambda t: t.reshape(nq, c, GROUP)
    cum = per_chunk(cum_all)
    rq, kq, vq, aq, bq = (per_chunk(t) for t in (r, k, v, a_vec, b_vec))
    cum_ex = cum - per_chunk(lw)
    mid = cum[:, c // 2 - 1:c // 2, :]
    tot = cum[:, c - 1:c, :]
    e_fwd = jnp.exp(cum - mid)
    e_bwd = jnp.exp(mid - cum)
    e_end = jnp.exp(tot - cum)
    mxu = lambda t: _to_heads(t).astype(BF16)
    r_rel, k_rel = mxu(rq * e_fwd), mxu(kq * e_bwd)
    a_rel, b_rel = mxu(aq * jnp.exp(cum_ex - mid)), mxu(bq * e_bwd)
    a_abs, r_abs = mxu(aq * jnp.exp(cum_ex)), _to_heads(rq * jnp.exp(cum))
    k_end, b_end = mxu(kq * e_end), mxu(bq * e_end)
    gam = _to_heads(jnp.exp(tot))
    vh = mxu(vq)

    a_ab = jnp.where(strict, _bnt(a_rel, b_rel), 0.0)
    a_ak = jnp.where(strict, _bnt(a_rel, k_rel), 0.0).astype(BF16)
    a_rb = jnp.where(incl, _bnt(r_rel, b_rel), 0.0).astype(BF16)
    a_rk = jnp.where(incl, _bnt(r_rel, k_rel), 0.0).astype(BF16)
    t_inv = _unit_lower_inverse(a_ab, c).astype(BF16)
    p_mat = _bnn(t_inv, a_abs)
    u0 = _bnn(t_inv, _bnn(a_ak, vh).astype(BF16))
    p_b, u_b = p_mat.astype(BF16), u0.astype(BF16)
    qm = r_abs + _bnn(a_rb, p_b)
    y0 = _bnn(a_rb, u_b) + _bnn(a_rk, vh)
    mm_mat = eye * gam + _btn(p_b, b_end)
    gm = _btn(u_b, b_end) + _btn(vh, k_end)
    mm_out[0] = _from_heads(mm_mat).reshape(ts, GROUP)
    gm_out[0] = _from_heads(gm).reshape(ts, GROUP)
    qm_out[0] = _from_heads(qm).reshape(ts, GROUP)
    y0_out[0] = _from_heads(y0).reshape(ts, GROUP)


def rwkv_chunk(proj3, mu4, w0, w2p, a0, a2p, g2p, k_k, k_a, r_k, head_ones, vres, *, ts, chunk):
    b, s, _ = proj3.shape
    slab = lambda off: pl.BlockSpec((1, ts, GROUP), lambda i, j: (i, j, off // GROUP))
    vec = lambda: pl.BlockSpec((1, GROUP), lambda i, j: (0, 0))
    full = lambda shape: pl.BlockSpec(shape, lambda i, j: tuple(0 for _ in shape))
    tok = pl.BlockSpec((1, ts, GROUP), lambda i, j: (i, j, 0))
    in_specs = [slab(RWKV_R), slab(RWKV_K), slab(RWKV_V), slab(RWKV_LR), full((4, GROUP)),
                vec(), full((GROUP, GROUP)), vec(), full((GROUP, GROUP)), full((GROUP, GROUP)), vec(), vec(), vec(),
                full((GROUP, GROUP))]
    args = [proj3, proj3, proj3, proj3, mu4, w0.reshape(1, GROUP), w2p, a0.reshape(1, GROUP), a2p, g2p,
            k_k.reshape(1, GROUP), k_a.reshape(1, GROUP), r_k.reshape(1, GROUP), head_ones]
    if vres is not None:
        v_first, v0, v1p, v2p = vres
        in_specs += [tok, vec(), full((GROUP, 128)), full((128, GROUP))]
        args += [v_first, v0.reshape(1, GROUP), v1p, v2p]
    return pl.pallas_call(
        functools.partial(_rwkv_chunk_kernel, ts=ts, chunk=chunk, has_vres=vres is not None),
        grid=(b, s // ts),
        in_specs=in_specs,
        out_specs=[tok] * 7,
        out_shape=[jax.ShapeDtypeStruct((b, s, GROUP), F32)] * 7,
        scratch_shapes=[pltpu.VMEM((4, GROUP), F32)],
        compiler_params=_cparams(("parallel", "arbitrary")),
        name="rwkv_chunk",
    )(*args)


def _rwkv_state_kernel(mm_ref, gm_ref, qm_ref, y0_ref, bonus_ref, g_ref, gw_ref, gb_ref, o_ref, state_ref):
    @pl.when(pl.program_id(0) == 0)
    def _():
        state_ref[...] = jnp.zeros_like(state_ref)

    s0 = state_ref[...]
    y = _mm3(_bnt, _to_heads(qm_ref[...]), s0) + _to_heads(y0_ref[...])
    state_ref[...] = _mm3(_bnn, s0, _to_heads(mm_ref[...])) + _to_heads(gm_ref[...])
    mu = jnp.mean(y, axis=-1, keepdims=True)
    var = jnp.mean(jnp.square(y - mu), axis=-1, keepdims=True)
    yn = _from_heads((y - mu) * lax.rsqrt(var + RWKV_GN_EPS)) * gw_ref[...] + gb_ref[...]
    o_ref[...] = ((yn + bonus_ref[...]) * g_ref[...]).astype(o_ref.dtype)


def rwkv_state(mm, gm, qm, y0, bonus, g, gn_w, gn_b, *, chunk):
    bsz, s, _ = mm.shape
    tok = pl.BlockSpec((bsz, chunk, GROUP), lambda j: (0, j, 0))
    vec = pl.BlockSpec((1, GROUP), lambda j: (0, 0))
    return pl.pallas_call(
        _rwkv_state_kernel,
        grid=(s // chunk,),
        in_specs=[tok] * 6 + [vec] * 2,
        out_specs=tok,
        out_shape=jax.ShapeDtypeStruct((bsz, s, GROUP), BF16),
        scratch_shapes=[pltpu.VMEM((bsz * HEADS, HEAD_DIM, HEAD_DIM), F32)],
        compiler_params=_cparams(("arbitrary",)),
        name="rwkv_state",
    )(mm, gm, qm, y0, bonus, g, gn_w.reshape(1, GROUP), gn_b.reshape(1, GROUP))


def _ret_kernel(q_ref, k_ref, v_ref, g_ref, cos_ref, sin_ref, dmat_ref, xi_ref, zeta_ref, cd_ref, gw_ref, o_ref,
                state_ref, *, chunk):
    @pl.when(pl.program_id(1) == 0)
    def _():
        state_ref[...] = jnp.zeros_like(state_ref)

    lane = lax.broadcasted_iota(jnp.int32, (chunk, GROUP), 1)
    first_half = (lane % HEAD_DIM) < (HEAD_DIM // 2)
    cos, sin = cos_ref[...], sin_ref[...]

    def rotary(t):
        partner = jnp.where(first_half, pltpu.roll(t, GROUP - HEAD_DIM // 2, axis=1),
                            pltpu.roll(t, HEAD_DIM // 2, axis=1))
        return t * cos + partner * sin

    q = rotary(q_ref[0])
    k = rotary(k_ref[0]) * (HEAD_DIM ** -0.5)
    v = v_ref[0]
    mxu = lambda t: _to_heads(t[None]).astype(BF16)
    q_cross, k_decay = mxu(q * xi_ref[...]), mxu(k * zeta_ref[...])
    qb, kb, vb = mxu(q), mxu(k), mxu(v)
    inner = _bnt(qb, kb) * dmat_ref[...]
    state = state_ref[...]
    o = _bnn(inner.astype(BF16), vb) + _bnn(q_cross, state.astype(BF16))
    state_ref[...] = state * _to_heads(cd_ref[...][None]) + _btn(k_decay, vb)
    o = _from_heads(o * lax.rsqrt(jnp.mean(o * o, axis=-1, keepdims=True) + NORM_EPS))[0]
    g = g_ref[0]
    o_ref[0] = (o * gw_ref[...] * (g * _sigmoid(g))).astype(o_ref.dtype)


def retention(proj3, cos_t, sin_t, dmat, xi, zeta, cd, gn_w, *, chunk):
    b, s, _ = proj3.shape
    slab = lambda off: pl.BlockSpec((1, chunk, GROUP), lambda i, j: (i, j, off // GROUP))
    full = lambda shape: pl.BlockSpec(shape, lambda i, j: tuple(0 for _ in shape))
    return pl.pallas_call(
        functools.partial(_ret_kernel, chunk=chunk),
        grid=(b, s // chunk),
        in_specs=[slab(RET_Q), slab(RET_K), slab(RET_V), slab(RET_G),
                  pl.BlockSpec((chunk, GROUP), lambda i, j: (j, 0)),
                  pl.BlockSpec((chunk, GROUP), lambda i, j: (j, 0)),
                  full((HEADS, chunk, chunk)), full((chunk, GROUP)), full((chunk, GROUP)),
                  full((1, GROUP)), full((1, GROUP))],
        out_specs=pl.BlockSpec((1, chunk, GROUP), lambda i, j: (i, j, 0)),
        out_shape=jax.ShapeDtypeStruct((b, s, GROUP), BF16),
        scratch_shapes=[pltpu.VMEM((HEADS, HEAD_DIM, HEAD_DIM), F32)],
        compiler_params=_cparams(("parallel", "arbitrary")),
        name="retention",
    )(proj3, proj3, proj3, proj3, cos_t, sin_t, dmat, xi, zeta, cd, gn_w.reshape(1, GROUP))


def _retention_tables(s, chunk):
    half = HEAD_DIM // 2
    inv = 1.0 / (RET_THETA ** jnp.linspace(0.0, 1.0, half, dtype=F32))
    ang = jnp.arange(s, dtype=F32)[:, None] * inv[None, :]
    cos, sin = jnp.cos(ang), jnp.sin(ang)
    cos_t = jnp.tile(jnp.concatenate([cos, cos], axis=-1), (1, HEADS))
    sin_t = jnp.tile(jnp.concatenate([-sin, sin], axis=-1), (1, HEADS))
    lg = jnp.log(1.0 - 2.0 ** (-5.0 - jnp.arange(HEADS, dtype=F32)))
    n = jnp.arange(chunk, dtype=F32)
    diff = n[:, None] - n[None, :]
    dmat = jnp.where(diff >= 0, jnp.exp(lg[:, None, None] * jnp.maximum(diff, 0.0)), 0.0)
    zeta = jnp.exp(lg[:, None] * (chunk - 1.0 - n)[None, :])
    xi = jnp.exp(lg[:, None] * (n + 1.0)[None, :])
    per_lane = lambda t: jnp.repeat(t.T, HEAD_DIM, axis=1)
    cd = jnp.repeat(jnp.exp(lg * chunk), HEAD_DIM)[None, :]
    return cos_t, sin_t, dmat, per_lane(xi), per_lane(zeta), cd


def _out_proj_kernel(m0_ref, m1_ref, m2_ref, m3_ref, w_ref, x_ref, g_ref, o_ref):
    acc = _dot(m0_ref[...], w_ref[0:GROUP, :])
    for idx, m_ref in enumerate((m1_ref, m2_ref, m3_ref), start=1):
        acc = acc + _dot(m_ref[...], w_ref[idx * GROUP:(idx + 1) * GROUP, :])
    o_ref[...] = x_ref[...] + _rms(acc, g_ref[...])


def out_proj(mixed, w, x, g, *, tm):
    n, d = x.shape
    mix_spec = pl.BlockSpec((tm, GROUP), lambda i: (i, 0))
    return pl.pallas_call(
        _out_proj_kernel,
        grid=(n // tm,),
        in_specs=[mix_spec] * 4 + [pl.BlockSpec((d, d), lambda i: (0, 0)),
                                   pl.BlockSpec((tm, d), lambda i: (i, 0)),
                                   pl.BlockSpec((1, d), lambda i: (0, 0))],
        out_specs=pl.BlockSpec((tm, d), lambda i: (i, 0)),
        out_shape=jax.ShapeDtypeStruct((n, d), F32),
        compiler_params=_cparams(("parallel",)),
        name="out_proj",
    )(*mixed, w, x, g.reshape(1, d))


def _xattn_kernel(x_ref, kv_ref, wq_ref, wo_ref, gpre_ref, gpost_ref, o_ref):
    x = x_ref[0]
    xn = _rms(x, gpre_ref[...]).astype(BF16)
    q = (_dot(xn, wq_ref[...]) * (XATTN_HEAD_DIM ** -0.5)).astype(BF16)
    outs = []
    for h in range(XATTN_HEADS):
        sl = slice(h * XATTN_HEAD_DIM, (h + 1) * XATTN_HEAD_DIM)
        s = _dot_nt(q[:, sl], kv_ref[0, :, sl])
        e = jnp.exp(s - jnp.max(s, axis=-1, keepdims=True))
        p = e / jnp.sum(e, axis=-1, keepdims=True)
        outs.append(_dot(p.astype(BF16), kv_ref[0, :, D_MODEL + h * XATTN_HEAD_DIM:D_MODEL + (h + 1) * XATTN_HEAD_DIM]))
    o = jnp.concatenate(outs, axis=-1).astype(BF16)
    o_ref[0] = x + _rms(_dot(o, wo_ref[...]), gpost_ref[...])


def cross_attention(x3, kv, wq, wo, g_pre, g_post, *, tm):
    b, s, d = x3.shape
    m = kv.shape[1]
    return pl.pallas_call(
        _xattn_kernel,
        grid=(b, s // tm),
        in_specs=[pl.BlockSpec((1, tm, d), lambda bi, i: (bi, i, 0)),
                  pl.BlockSpec((1, m, 2 * d), lambda bi, i: (bi, 0, 0)),
                  pl.BlockSpec((d, d), lambda bi, i: (0, 0)),
                  pl.BlockSpec((d, d), lambda bi, i: (0, 0)),
                  pl.BlockSpec((1, d), lambda bi, i: (0, 0)),
                  pl.BlockSpec((1, d), lambda bi, i: (0, 0))],
        out_specs=pl.BlockSpec((1, tm, d), lambda bi, i: (bi, i, 0)),
        out_shape=jax.ShapeDtypeStruct((b, s, d), F32),
        compiler_params=_cparams(("parallel", "arbitrary")),
        name="cross_attention",
    )(x3, kv, wq, wo, g_pre.reshape(1, d), g_post.reshape(1, d))


def _mlp_kernel(x_ref, w1_ref, w2_ref, gpre_ref, gpost_ref, o_ref, *, ff_tile):
    x = x_ref[...]
    xn = _rms(x, gpre_ref[...]).astype(BF16)
    d_ff = w1_ref.shape[1]
    acc = None
    for c in range(d_ff // ff_tile):
        hid = jnp.square(jnp.maximum(_dot(xn, w1_ref[:, c * ff_tile:(c + 1) * ff_tile]), 0.0)).astype(BF16)
        part = _dot(hid, w2_ref[c * ff_tile:(c + 1) * ff_tile, :])
        acc = part if acc is None else acc + part
    o_ref[...] = x + _rms(acc, gpost_ref[...])


def mlp(x, w1, w2, g_pre, g_post, *, tm, ff_tile):
    n, d = x.shape
    d_ff = w1.shape[1]
    return pl.pallas_call(
        functools.partial(_mlp_kernel, ff_tile=ff_tile),
        grid=(n // tm,),
        in_specs=[pl.BlockSpec((tm, d), lambda i: (i, 0)),
                  pl.BlockSpec((d, d_ff), lambda i: (0, 0), pipeline_mode=pl.Buffered(1)),
                  pl.BlockSpec((d_ff, d), lambda i: (0, 0), pipeline_mode=pl.Buffered(1)),
                  pl.BlockSpec((1, d), lambda i: (0, 0)),
                  pl.BlockSpec((1, d), lambda i: (0, 0))],
        out_specs=pl.BlockSpec((tm, d), lambda i: (i, 0)),
        out_shape=jax.ShapeDtypeStruct((n, d), F32),
        compiler_params=_cparams(("parallel",)),
        name="mlp",
    )(x, w1, w2, g_pre.reshape(1, d), g_post.reshape(1, d))


def _block_diag(w):
    h, n, _ = w.shape
    eye = jnp.eye(h, dtype=w.dtype)
    return (eye[:, None, :, None] * w[:, :, None, :]).reshape(h * n, h * n)


def _pad_rows(w, start, total):
    return jnp.zeros((total, w.shape[1]), w.dtype).at[start:start + w.shape[0]].set(w)


def _tile(n, pref):
    return pref if n % pref == 0 else n


def kernel(x, mem, norm_mix_pre, norm_mix_post, norm_xa_pre, norm_xa_post, norm_mem, norm_mlp_pre, norm_mlp_post, w_in, w_out, fox_f_bias, lru_conv_w, lru_conv_b, lru_ra_w, lru_ra_b, lru_ri_w, lru_ri_b, lru_lambda, rwkv_mu, rwkv_w0, rwkv_w2, rwkv_a0, rwkv_a2, rwkv_g2, rwkv_k_k, rwkv_k_a, rwkv_r_k, rwkv_gn_w, rwkv_gn_b, rwkv_v0, rwkv_v1, rwkv_v2, ret_gn_w, xa_wq, xa_wk, xa_wv, xa_wo, mlp_w1, mlp_w2):
    bsz, seq, d = x.shape
    depth = w_in.shape[0]
    n_tok = bsz * seq
    mem_len = mem.shape[1]
    tm = _tile(n_tok, 512)
    tq = _tile(seq, 256)
    ts = _tile(seq, 256)
    ret_tables = _retention_tables(seq, RET_CHUNK)
    head_ones = _block_diag(jnp.ones((HEADS, HEAD_DIM, HEAD_DIM), F32))

    x2 = x.reshape(n_tok, d)
    v_first = None
    for l in range(depth):
        w = w_in[l]
        w_pad = jnp.concatenate([w[:, :FOX_REAL], jnp.zeros((d, LRU_X - FOX_REAL), w.dtype), w[:, FOX_REAL:]], axis=1)
        proj = in_proj(x2, norm_mix_pre[l], w_pad.astype(BF16), tm=tm, tn=IN_PAD // 7)
        proj3 = proj.reshape(bsz, seq, IN_PAD)

        f_bias = jnp.zeros((1, GROUP), F32).at[0, :HEADS].set(fox_f_bias[l])
        qa, ka, vt = fox_prep(proj3, f_bias, tc=_tile(seq, 256))
        fox_out = fox_attention(qa, ka, vt, tq=tq)

        lru_out = rglru(proj3, lru_conv_w[l], lru_conv_b[l], _block_diag(lru_ra_w[l]).astype(BF16), lru_ra_b[l],
                        _block_diag(lru_ri_w[l]).astype(BF16), lru_ri_b[l], lru_lambda[l], ts=ts)

        vres = None
        if l > 0:
            vres = (v_first, rwkv_v0[l - 1],
                    jnp.pad(rwkv_v1[l - 1], ((0, 0), (0, 128 - RWKV_V_RANK))).astype(BF16),
                    _pad_rows(rwkv_v2[l - 1], 0, 128).astype(BF16))
        mm_, gm_, qm_, y0_, bonus_, g_, v_ = rwkv_chunk(
            proj3, rwkv_mu[l].reshape(4, GROUP), rwkv_w0[l],
            _pad_rows(rwkv_w2[l], 0, GROUP).astype(BF16), rwkv_a0[l],
            _pad_rows(rwkv_a2[l], RWKV_W_RANK, GROUP).astype(BF16),
            _pad_rows(rwkv_g2[l], RWKV_W_RANK + RWKV_A_RANK, GROUP).astype(BF16),
            rwkv_k_k[l], rwkv_k_a[l], rwkv_r_k[l], head_ones, vres, ts=_tile(seq, 4 * RWKV_CHUNK), chunk=RWKV_CHUNK)
        if l == 0:
            v_first = v_
        rwkv_out = rwkv_state(mm_, gm_, qm_, y0_, bonus_, g_, rwkv_gn_w[l], rwkv_gn_b[l], chunk=RWKV_CHUNK)

        ret_out = retention(proj3, *ret_tables, ret_gn_w[l], chunk=RET_CHUNK)

        x2 = out_proj([m.reshape(n_tok, GROUP) for m in (fox_out, lru_out, rwkv_out, ret_out)],
                      w_out[l].astype(BF16), x2, norm_mix_post[l], tm=tm)

        wkv = jnp.concatenate([xa_wk[l], xa_wv[l]], axis=1).astype(BF16)
        kv = norm_matmul(mem.reshape(bsz * mem_len, d), norm_mem[l], wkv, tm=_tile(bsz * mem_len, 512),
                         tn=1024, out_dtype=BF16).reshape(bsz, mem_len, 2 * d)
        x2 = cross_attention(x2.reshape(bsz, seq, d), kv, xa_wq[l].astype(BF16), xa_wo[l].astype(BF16),
                             norm_xa_pre[l], norm_xa_post[l], tm=_tile(seq, 512)).reshape(n_tok, d)

        x2 = mlp(x2, mlp_w1[l].astype(BF16), mlp_w2[l].astype(BF16), norm_mlp_pre[l], norm_mlp_post[l],
                 tm=tm, ff_tile=1024)
    return x2.reshape(bsz, seq, d)
```

```python
import functools
import math

import jax
import jax.numpy as jnp
import numpy as np
from jax import lax
from jax.experimental import pallas as pl
from jax.experimental.pallas import tpu as pltpu

F32 = jnp.float32
BF16 = jnp.bfloat16
HIGHEST = lax.Precision.HIGHEST

D_MODEL = 1024
GROUP = 256
HEADS = 4
HEAD_DIM = 64
CONV_WIDTH = 4
LRU_C = 8.0
RET_THETA = 10000.0
RET_CHUNK = 128
RWKV_W_RANK, RWKV_A_RANK, RWKV_G_RANK, RWKV_V_RANK = 64, 64, 128, 32
RWKV_GN_EPS = 64e-5
XATTN_HEADS = 4
XATTN_HEAD_DIM = D_MODEL // XATTN_HEADS
NORM_EPS = 1e-6
NEG_BIG = -1e30

FOX_Q, FOX_K, FOX_V, FOX_F = 0, 256, 512, 768
LRU_X, LRU_Y = 1024, 1280
RWKV_R, RWKV_K, RWKV_V, RWKV_LR = 1536, 1792, 2048, 2304
RET_Q, RET_K, RET_V, RET_G = 2560, 2816, 3072, 3328
IN_PAD = 3584
FOX_REAL = 3 * GROUP + HEADS

VMEM_LIMIT = 56 * 1024 * 1024
RWKV_CHUNK = 64


def _cparams(sem):
    return pltpu.CompilerParams(dimension_semantics=sem, vmem_limit_bytes=VMEM_LIMIT)


def _rms(x, g):
    return x * lax.rsqrt(jnp.mean(x * x, axis=-1, keepdims=True) + NORM_EPS) * g


def _log_sigmoid(x):
    return jnp.minimum(x, 0.0) - jnp.log1p(jnp.exp(-jnp.abs(x)))


def _sigmoid(x):
    return 1.0 / (1.0 + jnp.exp(-x))


def _dot(a, b, **kw):
    return jnp.dot(a, b, preferred_element_type=F32, **kw)


def _dot_nt(a, b, **kw):
    return lax.dot_general(a, b, (((1,), (1,)), ((), ())), preferred_element_type=F32, **kw)


def _dot_tn(a, b, **kw):
    return lax.dot_general(a, b, (((0,), (0,)), ((), ())), preferred_element_type=F32, **kw)


def _split_bf16(x):
    hi = x.astype(BF16)
    return hi, (x - hi.astype(F32)).astype(BF16)


def _mm1(dot, a, b):
    return dot(a.astype(BF16), b.astype(BF16))


def _mm3(dot, a, b):
    ah, al = _split_bf16(a)
    bh, bl = _split_bf16(b)
    return dot(ah, bh) + (dot(ah, bl) + dot(al, bh))


def _tri(n, strict=False):
    r = lax.broadcasted_iota(jnp.int32, (n, n), 0)
    c = lax.broadcasted_iota(jnp.int32, (n, n), 1)
    return (r > c) if strict else (r >= c)


def _norm_matmul_kernel(x_ref, g_ref, w_ref, o_ref, xn_ref):
    @pl.when(pl.program_id(1) == 0)
    def _():
        xn_ref[...] = _rms(x_ref[...], g_ref[...]).astype(BF16)

    o_ref[...] = _dot(xn_ref[...], w_ref[...]).astype(o_ref.dtype)


def norm_matmul(x, g, w, layer, *, tm, tn, out_dtype):
    n, d = x.shape
    width = w.shape[2]
    return pl.pallas_call(
        _norm_matmul_kernel,
        grid=(n // tm, width // tn),
        in_specs=[pl.BlockSpec((tm, d), lambda i, j: (i, 0)),
                  pl.BlockSpec((1, d), lambda i, j: (0, 0)),
                  pl.BlockSpec((None, d, tn), lambda i, j: (layer, 0, j))],
        out_specs=pl.BlockSpec((tm, tn), lambda i, j: (i, j)),
        out_shape=jax.ShapeDtypeStruct((n, width), out_dtype),
        scratch_shapes=[pltpu.VMEM((tm, d), BF16)],
        compiler_params=_cparams(("parallel", "arbitrary")),
        name="norm_matmul",
    )(x, g.reshape(1, d), w)


def _in_proj_kernel(x_ref, g_ref, w_ref, o_ref, *, tn):
    xn = _rms(x_ref[...], g_ref[...]).astype(BF16)
    for c in range(w_ref.shape[0] // tn):
        o_ref[:, c * tn:(c + 1) * tn] = _dot_nt(xn, w_ref[c * tn:(c + 1) * tn, :])


def in_proj(x, g, w, layer, *, tm, tn):
    n, d = x.shape
    width = w.shape[1]
    return pl.pallas_call(
        functools.partial(_in_proj_kernel, tn=tn),
        grid=(n // tm,),
        in_specs=[pl.BlockSpec((tm, d), lambda i: (i, 0)),
                  pl.BlockSpec((1, d), lambda i: (0, 0)),
                  pl.BlockSpec((None, width, d), lambda i: (layer, 0, 0), pipeline_mode=pl.Buffered(1))],
        out_specs=pl.BlockSpec((tm, width), lambda i: (i, 0)),
        out_shape=jax.ShapeDtypeStruct((n, width), F32),
        compiler_params=_cparams(("parallel",)),
        name="in_proj",
    )(x, g.reshape(1, d), w)


FOX_AUG = 128


def _fox_prep_kernel(q_ref, k_ref, v_ref, f_ref, b_ref, qa_ref, ka_ref, vt_ref, carry_ref):
    @pl.when(pl.program_id(1) == 0)
    def _():
        carry_ref[...] = jnp.zeros_like(carry_ref)

    lf = _log_sigmoid(f_ref[0] + b_ref[...])
    tc = lf.shape[0]
    cum = _dot(_tri(tc).astype(F32), lf, precision=HIGHEST) + carry_ref[...]
    carry_ref[...] = cum[tc - 1:tc, :]

    lane = lax.broadcasted_iota(jnp.int32, (tc, FOX_AUG), 1)
    feat = lane < HEAD_DIM
    ones = jnp.where((lane >= HEAD_DIM) & (lane < HEAD_DIM + 3), 1.0, 0.0)
    q = q_ref[0] * (HEAD_DIM ** -0.5)
    k = k_ref[0]
    for h in range(HEADS):
        pair = slice((h // 2) * FOX_AUG, (h // 2 + 1) * FOX_AUG)
        qt, kt = q[:, pair], k[:, pair]
        if h % 2:
            qt, kt = pltpu.roll(qt, HEAD_DIM, axis=1), pltpu.roll(kt, HEAD_DIM, axis=1)
        neg_c = jnp.broadcast_to(-cum[:, h:h + 1], (tc, FOX_AUG))
        c_hi = neg_c.astype(BF16).astype(F32)
        rest = neg_c - c_hi
        c_mid = rest.astype(BF16).astype(F32)
        bias = jnp.where(lane == HEAD_DIM, c_hi, jnp.where(lane == HEAD_DIM + 1, c_mid,
                                                           jnp.where(lane == HEAD_DIM + 2, rest - c_mid, 0.0)))
        qa_ref[0, h] = jnp.where(feat, qt, ones).astype(BF16)
        ka_ref[0, h] = jnp.where(feat, kt, bias).astype(BF16)
    vt_ref[0, :, 0] = v_ref[0].T.reshape(HEADS, HEAD_DIM, tc).astype(BF16)


def fox_prep(proj3, f_bias_pad, *, tc):
    b, s, _ = proj3.shape
    slab = lambda off: pl.BlockSpec((1, tc, GROUP), lambda i, j: (i, j, off // GROUP))
    aug = pl.BlockSpec((1, HEADS, tc, FOX_AUG), lambda i, j: (i, 0, j, 0))
    return pl.pallas_call(
        _fox_prep_kernel,
        grid=(b, s // tc),
        in_specs=[slab(FOX_Q), slab(FOX_K), slab(FOX_V), slab(FOX_F), pl.BlockSpec((1, GROUP), lambda i, j: (0, 0))],
        out_specs=[aug, aug, pl.BlockSpec((1, HEADS, 1, HEAD_DIM, tc), lambda i, j: (i, 0, j, 0, 0))],
        out_shape=[jax.ShapeDtypeStruct((b, HEADS, s, FOX_AUG), BF16),
                   jax.ShapeDtypeStruct((b, HEADS, s, FOX_AUG), BF16),
                   jax.ShapeDtypeStruct((b, HEADS, s // tc, HEAD_DIM, tc), BF16)],
        scratch_shapes=[pltpu.VMEM((1, GROUP), F32)],
        compiler_params=_cparams(("parallel", "arbitrary")),
        name="fox_prep",
    )(proj3, proj3, proj3, proj3, f_bias_pad)


def _fox_attn_kernel(q_ref, k_ref, v_ref, o_ref, *, tq, tk):
    i = pl.program_id(1)
    ratio = tq // tk
    key = lax.broadcasted_iota(jnp.int32, (1, tk, tq), 1)
    qry = lax.broadcasted_iota(jnp.int32, (1, tk, tq), 2)
    qa = q_ref[0]

    def tile(j, diag=None):
        ka = k_ref[0, :, pl.ds(pl.multiple_of(j * tk, tk), tk), :]
        s = _bnt(ka, qa)
        if diag is not None:
            s = jnp.where(key + diag * tk <= qry, s, NEG_BIG)
        return s, v_ref[0, :, j]

    def update(carry, s, vt):
        m, l, acc = carry
        m_new = jnp.maximum(m, jnp.max(s, axis=1, keepdims=True))
        alpha = jnp.exp(m - m_new)
        p = jnp.exp(s - m_new)
        return (m_new, alpha * l + jnp.sum(p, axis=1, keepdims=True),
                alpha * acc + _bnn(vt, p.astype(BF16)))

    s, vt = tile(i * ratio, 0)
    m = jnp.max(s, axis=1, keepdims=True)
    p = jnp.exp(s - m)
    carry = (m, jnp.sum(p, axis=1, keepdims=True), _bnn(vt, p.astype(BF16)))
    for d in range(1, ratio):
        carry = update(carry, *tile(i * ratio + d, d))
    m, l, acc = lax.fori_loop(0, i * ratio, lambda j, c: update(c, *tile(j)), carry)
    o_ref[0] = (acc / l).reshape(GROUP, tq).T.astype(o_ref.dtype)


def fox_attention(qa, ka, vt, *, tq):
    b, h, s, _ = qa.shape
    nk, tk = vt.shape[2], vt.shape[4]
    return pl.pallas_call(
        functools.partial(_fox_attn_kernel, tq=tq, tk=tk),
        grid=(b, s // tq),
        in_specs=[pl.BlockSpec((1, h, tq, FOX_AUG), lambda bi, i: (bi, 0, i, 0)),
                  pl.BlockSpec((1, h, s, FOX_AUG), lambda bi, i: (bi, 0, 0, 0)),
                  pl.BlockSpec((1, h, nk, HEAD_DIM, tk), lambda bi, i: (bi, 0, 0, 0, 0))],
        out_specs=pl.BlockSpec((1, tq, GROUP), lambda bi, i: (bi, i, 0)),
        out_shape=jax.ShapeDtypeStruct((b, s, GROUP), BF16),
        compiler_params=_cparams(("parallel", "arbitrary")),
        name="fox_attention",
    )(qa, ka, vt)


def _lru_kernel(x_ref, y_ref, cw_ref, cb_ref, wra_ref, bra_ref, wri_ref, bri_ref, lam_ref, o_ref,
                buf_ref, h_ref, *, ts):
    @pl.when(pl.program_id(1) == 0)
    def _():
        buf_ref[0:8, :] = jnp.zeros((8, GROUP), F32)
        h_ref[...] = jnp.zeros_like(h_ref)

    xb = x_ref[0]
    buf_ref[8:8 + ts, :] = xb
    xc = cb_ref[...] + buf_ref[5:5 + ts, :] * cw_ref[0:1, :]
    for j in range(1, CONV_WIDTH):
        xc = xc + buf_ref[5 + j:5 + j + ts, :] * cw_ref[j:j + 1, :]
    buf_ref[0:8, :] = xb[ts - 8:ts, :]

    xcb = xc.astype(BF16)
    r = _sigmoid(_dot(xcb, wra_ref[...]) + bra_ref[...])
    gate_i = _sigmoid(_dot(xcb, wri_ref[...]) + bri_ref[...])
    log_a = LRU_C * r * _log_sigmoid(lam_ref[...])
    a = jnp.exp(log_a)
    z = 2.0 * log_a
    mult = jnp.sqrt(jnp.maximum(-jnp.tanh(0.5 * z) * (jnp.exp(z) + 1.0), 0.0))
    u = mult * (gate_i * xc)

    row = lax.broadcasted_iota(jnp.int32, (ts, GROUP), 0)
    pa, pb = a, u
    d = 1
    while d < ts:
        sa = pltpu.roll(pa, d, axis=0)
        sb = pltpu.roll(pb, d, axis=0)
        valid = row >= d
        pb = jnp.where(valid, pa * sb + pb, pb)
        pa = jnp.where(valid, pa * sa, pa)
        d *= 2
    hseq = pa * h_ref[...] + pb
    h_ref[...] = hseq[ts - 1:ts, :]

    y = y_ref[0]
    gelu = 0.5 * y * (1.0 + jnp.tanh(math.sqrt(2.0 / math.pi) * (y + 0.044715 * (y * y * y))))
    o_ref[0] = (hseq * gelu).astype(o_ref.dtype)


def rglru(proj3, conv_w, conv_b, wra_bd, ra_b, wri_bd, ri_b, lam, *, ts):
    b, s, _ = proj3.shape
    vec = lambda: pl.BlockSpec((1, GROUP), lambda i, j: (0, 0))
    mat = lambda: pl.BlockSpec((GROUP, GROUP), lambda i, j: (0, 0))
    return pl.pallas_call(
        functools.partial(_lru_kernel, ts=ts),
        grid=(b, s // ts),
        in_specs=[pl.BlockSpec((1, ts, GROUP), lambda i, j: (i, j, LRU_X // GROUP)),
                  pl.BlockSpec((1, ts, GROUP), lambda i, j: (i, j, LRU_Y // GROUP)),
                  pl.BlockSpec((CONV_WIDTH, GROUP), lambda i, j: (0, 0)),
                  vec(), mat(), vec(), mat(), vec(), vec()],
        out_specs=pl.BlockSpec((1, ts, GROUP), lambda i, j: (i, j, 0)),
        out_shape=jax.ShapeDtypeStruct((b, s, GROUP), BF16),
        scratch_shapes=[pltpu.VMEM((ts + 8, GROUP), F32), pltpu.VMEM((1, GROUP), F32)],
        compiler_params=_cparams(("parallel", "arbitrary")),
        name="rglru",
    )(proj3, proj3, conv_w, conv_b.reshape(1, GROUP), wra_bd, ra_b.reshape(1, GROUP), wri_bd,
      ri_b.reshape(1, GROUP), lam.reshape(1, GROUP))


def _bdot(dims):
    return lambda a, b: lax.dot_general(a, b, (dims, ((0,), (0,))), preferred_element_type=F32)


_bnn = _bdot(((2,), (1,)))
_bnt = _bdot(((2,), (2,)))
_btn = _bdot(((1,), (1,)))


def _unit_lower_inverse(a_strict, n):
    r = lax.broadcasted_iota(jnp.int32, (1, n, n), 1)
    c = lax.broadcasted_iota(jnp.int32, (1, n, n), 2)
    t = jnp.where(r == c, 1.0, jnp.where((r // 2 == c // 2) & (r > c), a_strict, 0.0))
    m = 2
    while m < n:
        off = (r // (2 * m) == c // (2 * m)) & (r % (2 * m) >= m) & (c % (2 * m) < m)
        tb = t.astype(BF16)
        t = t + _bnn(tb, _bnn(jnp.where(off, a_strict, 0.0).astype(BF16), tb).astype(BF16))
        m *= 2
    return t


def _to_heads(x):
    n, rows, _ = x.shape
    parts = [x[:, :, h * HEAD_DIM:(h + 1) * HEAD_DIM] for h in range(HEADS)]
    return jnp.stack(parts, axis=1).reshape(n * HEADS, rows, HEAD_DIM)


def _from_heads(x):
    nh, rows, _ = x.shape
    x = x.reshape(nh // HEADS, HEADS, rows, HEAD_DIM)
    return jnp.concatenate([x[:, h] for h in range(HEADS)], axis=-1)


def _rwkv_chunk_kernel(*refs, ts, chunk, has_vres):
    (sr_ref, sk_ref, sv_ref, sl_ref, mu_ref, w0_ref, w2_ref, a0_ref, a2_ref, g2_ref, kk_ref, ka_ref, rk_ref,
     ones_ref) = refs[:14]
    refs = refs[14:]
    if has_vres:
        vf_ref, v0_ref, v1_ref, v2_ref = refs[:4]
        refs = refs[4:]
    mm_out, gm_out, qm_out, y0_out, bonus_out, g_out, v_out, carry_ref = refs

    @pl.when(pl.program_id(1) == 0)
    def _():
        carry_ref[...] = jnp.zeros_like(carry_ref)

    row0 = lax.broadcasted_iota(jnp.int32, (ts, GROUP), 0) == 0

    def shift_mix(ref, idx):
        s = ref[0]
        prev = jnp.where(row0, carry_ref[idx:idx + 1, :], pltpu.roll(s, 1, axis=0))
        carry_ref[idx:idx + 1, :] = s[ts - 1:ts, :]
        return s + (prev - s) * mu_ref[idx:idx + 1, :]

    r = shift_mix(sr_ref, 0)
    k = shift_mix(sk_ref, 1)
    v = shift_mix(sv_ref, 2)
    low = shift_mix(sl_ref, 3)

    zw = w0_ref[...] + _dot(jnp.tanh(low).astype(BF16), w2_ref[...])
    lw = -jnp.exp(_log_sigmoid(zw) - 0.5)
    a = _sigmoid(a0_ref[...] + _dot(low.astype(BF16), a2_ref[...]))
    g_out[0] = _dot(_sigmoid(low).astype(BF16), g2_ref[...])
    if has_vres:
        mix = _dot(_dot(v.astype(BF16), v1_ref[...]).astype(BF16), v2_ref[...])
        v = v + (vf_ref[0] - v) * _sigmoid(v0_ref[...] + mix)
    v_out[0] = v
    kk = k * kk_ref[...]
    ss = _dot(kk * kk, ones_ref[...], precision=HIGHEST)
    kk = kk / jnp.maximum(jnp.sqrt(ss), 1e-12)
    k = k * (1.0 + (a - 1.0) * ka_ref[...])
    bonus_out[0] = _dot(r * k * rk_ref[...], ones_ref[...], precision=HIGHEST) * v
    a_vec, b_vec = -kk, kk * a

    c = chunk
    rr = lax.broadcasted_iota(jnp.int32, (ts, ts), 0)
    cc = lax.broadcasted_iota(jnp.int32, (ts, ts), 1)
    chunk_tri = ((rr >= cc) & (rr // c == cc // c)).astype(F32)
    cum_all = _dot(chunk_tri, lw, precision=HIGHEST)
    strict = _tri(c, strict=True)[None]
    incl = _tri(c)[None]
    eye = (lax.broadcasted_iota(jnp.int32, (1, HEAD_DIM, HEAD_DIM), 1)
           == lax.broadcasted_iota(jnp.int32, (1, HEAD_DIM, HEAD_DIM), 2)).astype(F32)

    nq = ts // c
    per_chunk = lambda t: t.reshape(nq, c, GROUP)
    cum = per_chunk(cum_all)
    rq, kq, vq, aq, bq = (per_chunk(t) for t in (r, k, v, a_vec, b_vec))
    cum_ex = cum - per_chunk(lw)
    mid = cum[:, c // 2 - 1:c // 2, :]
    tot = cum[:, c - 1:c, :]
    e_fwd = jnp.exp(cum - mid)
    e_bwd = jnp.exp(mid - cum)
    e_end = jnp.exp(tot - cum)
    mxu = lambda t: _to_heads(t).astype(BF16)
    r_rel, k_rel = mxu(rq * e_fwd), mxu(kq * e_bwd)
    a_rel, b_rel = mxu(aq * jnp.exp(cum_ex - mid)), mxu(bq * e_bwd)
    a_abs, r_abs = mxu(aq * jnp.exp(cum_ex)), _to_heads(rq * jnp.exp(cum))
    k_end, b_end = mxu(kq * e_end), mxu(bq * e_end)
    gam = _to_heads(jnp.exp(tot))
    vh = mxu(vq)

    a_ab = jnp.where(strict, _bnt(a_rel, b_rel), 0.0)
    a_ak = jnp.where(strict, _bnt(a_rel, k_rel), 0.0).astype(BF16)
    a_rb = jnp.where(incl, _bnt(r_rel, b_rel), 0.0).astype(BF16)
    a_rk = jnp.where(incl, _bnt(r_rel, k_rel), 0.0).astype(BF16)
    t_inv = _unit_lower_inverse(a_ab, c).astype(BF16)
    p_mat = _bnn(t_inv, a_abs)
    u0 = _bnn(t_inv, _bnn(a_ak, vh).astype(BF16))
    p_b, u_b = p_mat.astype(BF16), u0.astype(BF16)
    qm = r_abs + _bnn(a_rb, p_b)
    y0 = _bnn(a_rb, u_b) + _bnn(a_rk, vh)
    mm_mat = eye * gam + _btn(p_b, b_end)
    gm = _btn(u_b, b_end) + _btn(vh, k_end)
    mm_out[0] = _from_heads(mm_mat).reshape(ts, GROUP)
    gm_out[0] = _from_heads(gm).reshape(ts, GROUP)
    qm_out[0] = _from_heads(qm).reshape(ts, GROUP)
    y0_out[0] = _from_heads(y0).reshape(ts, GROUP)


def rwkv_chunk(proj3, mu4, w0, w2p, a0, a2p, g2p, k_k, k_a, r_k, head_ones, vres, *, ts, chunk):
    b, s, _ = proj3.shape
    slab = lambda off: pl.BlockSpec((1, ts, GROUP), lambda i, j: (i, j, off // GROUP))
    vec = lambda: pl.BlockSpec((1, GROUP), lambda i, j: (0, 0))
    full = lambda shape: pl.BlockSpec(shape, lambda i, j: tuple(0 for _ in shape))
    tok = pl.BlockSpec((1, ts, GROUP), lambda i, j: (i, j, 0))
    in_specs = [slab(RWKV_R), slab(RWKV_K), slab(RWKV_V), slab(RWKV_LR), full((4, GROUP)),
                vec(), full((GROUP, GROUP)), vec(), full((GROUP, GROUP)), full((GROUP, GROUP)), vec(), vec(), vec(),
                full((GROUP, GROUP))]
    args = [proj3, proj3, proj3, proj3, mu4, w0.reshape(1, GROUP), w2p, a0.reshape(1, GROUP), a2p, g2p,
            k_k.reshape(1, GROUP), k_a.reshape(1, GROUP), r_k.reshape(1, GROUP), head_ones]
    if vres is not None:
        v_first, v0, v1p, v2p = vres
        in_specs += [tok, vec(), full((GROUP, 128)), full((128, GROUP))]
        args += [v_first, v0.reshape(1, GROUP), v1p, v2p]
    return pl.pallas_call(
        functools.partial(_rwkv_chunk_kernel, ts=ts, chunk=chunk, has_vres=vres is not None),
        grid=(b, s // ts),
        in_specs=in_specs,
        out_specs=[tok] * 7,
        out_shape=[jax.ShapeDtypeStruct((b, s, GROUP), F32)] * 7,
        scratch_shapes=[pltpu.VMEM((4, GROUP), F32)],
        compiler_params=_cparams(("parallel", "arbitrary")),
        name="rwkv_chunk",
    )(*args)


def _rwkv_state_kernel(mm_ref, gm_ref, qm_ref, y0_ref, bonus_ref, g_ref, gw_ref, gb_ref, o_ref, state_ref):
    @pl.when(pl.program_id(0) == 0)
    def _():
        state_ref[...] = jnp.zeros_like(state_ref)

    s0 = state_ref[...]
    y = _mm3(_bnt, _to_heads(qm_ref[...]), s0) + _to_heads(y0_ref[...])
    state_ref[...] = _mm3(_bnn, s0, _to_heads(mm_ref[...])) + _to_heads(gm_ref[...])
    mu = jnp.mean(y, axis=-1, keepdims=True)
    var = jnp.mean(jnp.square(y - mu), axis=-1, keepdims=True)
    yn = _from_heads((y - mu) * lax.rsqrt(var + RWKV_GN_EPS)) * gw_ref[...] + gb_ref[...]
    o_ref[...] = ((yn + bonus_ref[...]) * g_ref[...]).astype(o_ref.dtype)


def rwkv_state(mm, gm, qm, y0, bonus, g, gn_w, gn_b, *, chunk):
    bsz, s, _ = mm.shape
    tok = pl.BlockSpec((bsz, chunk, GROUP), lambda j: (0, j, 0))
    vec = pl.BlockSpec((1, GROUP), lambda j: (0, 0))
    return pl.pallas_call(
        _rwkv_state_kernel,
        grid=(s // chunk,),
        in_specs=[tok] * 6 + [vec] * 2,
        out_specs=tok,
        out_shape=jax.ShapeDtypeStruct((bsz, s, GROUP), BF16),
        scratch_shapes=[pltpu.VMEM((bsz * HEADS, HEAD_DIM, HEAD_DIM), F32)],
        compiler_params=_cparams(("arbitrary",)),
        name="rwkv_state",
    )(mm, gm, qm, y0, bonus, g, gn_w.reshape(1, GROUP), gn_b.reshape(1, GROUP))


def _ret_kernel(q_ref, k_ref, v_ref, g_ref, cos_ref, sin_ref, dmat_ref, xi_ref, zeta_ref, cd_ref, gw_ref, o_ref,
                state_ref, *, chunk):
    @pl.when(pl.program_id(1) == 0)
    def _():
        state_ref[...] = jnp.zeros_like(state_ref)

    lane = lax.broadcasted_iota(jnp.int32, (chunk, GROUP), 1)
    first_half = (lane % HEAD_DIM) < (HEAD_DIM // 2)
    cos, sin = cos_ref[...], sin_ref[...]

    def rotary(t):
        partner = jnp.where(first_half, pltpu.roll(t, GROUP - HEAD_DIM // 2, axis=1),
                            pltpu.roll(t, HEAD_DIM // 2, axis=1))
        return t * cos + partner * sin

    q = rotary(q_ref[0])
    k = rotary(k_ref[0]) * (HEAD_DIM ** -0.5)
    v = v_ref[0]
    mxu = lambda t: _to_heads(t[None]).astype(BF16)
    q_cross, k_decay = mxu(q * xi_ref[...]), mxu(k * zeta_ref[...])
    qb, kb, vb = mxu(q), mxu(k), mxu(v)
    inner = _bnt(qb, kb) * dmat_ref[...]
    state = state_ref[...]
    o = _bnn(inner.astype(BF16), vb) + _bnn(q_cross, state.astype(BF16))
    state_ref[...] = state * _to_heads(cd_ref[...][None]) + _btn(k_decay, vb)
    o = _from_heads(o * lax.rsqrt(jnp.mean(o * o, axis=-1, keepdims=True) + NORM_EPS))[0]
    g = g_ref[0]
    o_ref[0] = (o * gw_ref[...] * (g * _sigmoid(g))).astype(o_ref.dtype)


def retention(proj3, cos_t, sin_t, dmat, xi, zeta, cd, gn_w, *, chunk):
    b, s, _ = proj3.shape
    slab = lambda off: pl.BlockSpec((1, chunk, GROUP), lambda i, j: (i, j, off // GROUP))
    full = lambda shape: pl.BlockSpec(shape, lambda i, j: tuple(0 for _ in shape))
    return pl.pallas_call(
        functools.partial(_ret_kernel, chunk=chunk),
        grid=(b, s // chunk),
        in_specs=[slab(RET_Q), slab(RET_K), slab(RET_V), slab(RET_G),
                  pl.BlockSpec((chunk, GROUP), lambda i, j: (j, 0)),
                  pl.BlockSpec((chunk, GROUP), lambda i, j: (j, 0)),
                  full((HEADS, chunk, chunk)), full((chunk, GROUP)), full((chunk, GROUP)),
                  full((1, GROUP)), full((1, GROUP))],
        out_specs=pl.BlockSpec((1, chunk, GROUP), lambda i, j: (i, j, 0)),
        out_shape=jax.ShapeDtypeStruct((b, s, GROUP), BF16),
        scratch_shapes=[pltpu.VMEM((HEADS, HEAD_DIM, HEAD_DIM), F32)],
        compiler_params=_cparams(("parallel", "arbitrary")),
        name="retention",
    )(proj3, proj3, proj3, proj3, cos_t, sin_t, dmat, xi, zeta, cd, gn_w.reshape(1, GROUP))


def _retention_tables(s, chunk):
    half = HEAD_DIM // 2
    inv = 1.0 / (RET_THETA ** jnp.linspace(0.0, 1.0, half, dtype=F32))
    ang = jnp.arange(s, dtype=F32)[:, None] * inv[None, :]
    cos, sin = jnp.cos(ang), jnp.sin(ang)
    cos_t = jnp.tile(jnp.concatenate([cos, cos], axis=-1), (1, HEADS))
    sin_t = jnp.tile(jnp.concatenate([-sin, sin], axis=-1), (1, HEADS))
    lg = jnp.log(1.0 - 2.0 ** (-5.0 - jnp.arange(HEADS, dtype=F32)))
    n = jnp.arange(chunk, dtype=F32)
    diff = n[:, None] - n[None, :]
    dmat = jnp.where(diff >= 0, jnp.exp(lg[:, None, None] * jnp.maximum(diff, 0.0)), 0.0)
    zeta = jnp.exp(lg[:, None] * (chunk - 1.0 - n)[None, :])
    xi = jnp.exp(lg[:, None] * (n + 1.0)[None, :])
    per_lane = lambda t: jnp.repeat(t.T, HEAD_DIM, axis=1)
    cd = jnp.repeat(jnp.exp(lg * chunk), HEAD_DIM)[None, :]
    return cos_t, sin_t, dmat, per_lane(xi), per_lane(zeta), cd


def _out_proj_kernel(m0_ref, m1_ref, m2_ref, m3_ref, w_ref, x_ref, g_ref, o_ref):
    acc = _dot(m0_ref[...], w_ref[0:GROUP, :])
    for idx, m_ref in enumerate((m1_ref, m2_ref, m3_ref), start=1):
        acc = acc + _dot(m_ref[...], w_ref[idx * GROUP:(idx + 1) * GROUP, :])
    o_ref[...] = x_ref[...] + _rms(acc, g_ref[...])


def out_proj(mixed, w, layer, x, g, *, tm):
    n, d = x.shape
    mix_spec = pl.BlockSpec((tm, GROUP), lambda i: (i, 0))
    return pl.pallas_call(
        _out_proj_kernel,
        grid=(n // tm,),
        in_specs=[mix_spec] * 4 + [pl.BlockSpec((None, d, d), lambda i: (layer, 0, 0)),
                                   pl.BlockSpec((tm, d), lambda i: (i, 0)),
                                   pl.BlockSpec((1, d), lambda i: (0, 0))],
        out_specs=pl.BlockSpec((tm, d), lambda i: (i, 0)),
        out_shape=jax.ShapeDtypeStruct((n, d), F32),
        compiler_params=_cparams(("parallel",)),
        name="out_proj",
    )(*mixed, w, x, g.reshape(1, d))


def _xattn_kernel(x_ref, kv_ref, wq_ref, wo_ref, gpre_ref, gpost_ref, o_ref):
    x = x_ref[0]
    xn = _rms(x, gpre_ref[...]).astype(BF16)
    q = (_dot(xn, wq_ref[...]) * (XATTN_HEAD_DIM ** -0.5)).astype(BF16)
    outs = []
    for h in range(XATTN_HEADS):
        sl = slice(h * XATTN_HEAD_DIM, (h + 1) * XATTN_HEAD_DIM)
        s = _dot_nt(q[:, sl], kv_ref[0, :, sl])
        e = jnp.exp(s - jnp.max(s, axis=-1, keepdims=True))
        p = e / jnp.sum(e, axis=-1, keepdims=True)
        outs.append(_dot(p.astype(BF16), kv_ref[0, :, D_MODEL + h * XATTN_HEAD_DIM:D_MODEL + (h + 1) * XATTN_HEAD_DIM]))
    o = jnp.concatenate(outs, axis=-1).astype(BF16)
    o_ref[0] = x + _rms(_dot(o, wo_ref[...]), gpost_ref[...])


def cross_attention(x3, kv, wq, wo, layer, g_pre, g_post, *, tm):
    b, s, d = x3.shape
    m = kv.shape[1]
    return pl.pallas_call(
        _xattn_kernel,
        grid=(b, s // tm),
        in_specs=[pl.BlockSpec((1, tm, d), lambda bi, i: (bi, i, 0)),
                  pl.BlockSpec((1, m, 2 * d), lambda bi, i: (bi, 0, 0)),
                  pl.BlockSpec((None, d, d), lambda bi, i: (layer, 0, 0)),
                  pl.BlockSpec((None, d, d), lambda bi, i: (layer, 0, 0)),
                  pl.BlockSpec((1, d), lambda bi, i: (0, 0)),
                  pl.BlockSpec((1, d), lambda bi, i: (0, 0))],
        out_specs=pl.BlockSpec((1, tm, d), lambda bi, i: (bi, i, 0)),
        out_shape=jax.ShapeDtypeStruct((b, s, d), F32),
        compiler_params=_cparams(("parallel", "arbitrary")),
        name="cross_attention",
    )(x3, kv, wq, wo, g_pre.reshape(1, d), g_post.reshape(1, d))


def _mlp_kernel(x_ref, w1_ref, w2_ref, gpre_ref, gpost_ref, o_ref, *, ff_tile):
    x = x_ref[...]
    xn = _rms(x, gpre_ref[...]).astype(BF16)
    d_ff = w1_ref.shape[1]
    acc = None
    for c in range(d_ff // ff_tile):
        hid = jnp.square(jnp.maximum(_dot(xn, w1_ref[:, c * ff_tile:(c + 1) * ff_tile]), 0.0)).astype(BF16)
        part = _dot(hid, w2_ref[c * ff_tile:(c + 1) * ff_tile, :])
        acc = part if acc is None else acc + part
    o_ref[...] = x + _rms(acc, gpost_ref[...])


def mlp(x, w1, w2, layer, g_pre, g_post, *, tm, ff_tile):
    n, d = x.shape
    d_ff = w1.shape[2]
    return pl.pallas_call(
        functools.partial(_mlp_kernel, ff_tile=ff_tile),
        grid=(n // tm,),
        in_specs=[pl.BlockSpec((tm, d), lambda i: (i, 0)),
                  pl.BlockSpec((None, d, d_ff), lambda i: (layer, 0, 0), pipeline_mode=pl.Buffered(1)),
                  pl.BlockSpec((None, d_ff, d), lambda i: (layer, 0, 0), pipeline_mode=pl.Buffered(1)),
                  pl.BlockSpec((1, d), lambda i: (0, 0)),
                  pl.BlockSpec((1, d), lambda i: (0, 0))],
        out_specs=pl.BlockSpec((tm, d), lambda i: (i, 0)),
        out_shape=jax.ShapeDtypeStruct((n, d), F32),
        compiler_params=_cparams(("parallel",)),
        name="mlp",
    )(x, w1, w2, g_pre.reshape(1, d), g_post.reshape(1, d))


def _block_diag(w):
    h, n, _ = w.shape
    eye = jnp.eye(h, dtype=w.dtype)
    return (eye[:, None, :, None] * w[:, :, None, :]).reshape(h * n, h * n)


def _pad_rows(w, start, total):
    return jnp.zeros((total, w.shape[1]), w.dtype).at[start:start + w.shape[0]].set(w)


def _tile(n, pref):
    return pref if n % pref == 0 else n


def kernel(x, mem, norm_mix_pre, norm_mix_post, norm_xa_pre, norm_xa_post, norm_mem, norm_mlp_pre, norm_mlp_post, w_in, w_out, fox_f_bias, lru_conv_w, lru_conv_b, lru_ra_w, lru_ra_b, lru_ri_w, lru_ri_b, lru_lambda, rwkv_mu, rwkv_w0, rwkv_w2, rwkv_a0, rwkv_a2, rwkv_g2, rwkv_k_k, rwkv_k_a, rwkv_r_k, rwkv_gn_w, rwkv_gn_b, rwkv_v0, rwkv_v1, rwkv_v2, ret_gn_w, xa_wq, xa_wk, xa_wv, xa_wo, mlp_w1, mlp_w2):
    bsz, seq, d = x.shape
    depth = w_in.shape[0]
    n_tok = bsz * seq
    mem_len = mem.shape[1]
    tm = _tile(n_tok, 512)
    tq = _tile(seq, 512)
    ts = _tile(seq, 256)
    ret_tables = _retention_tables(seq, RET_CHUNK)
    head_ones = _block_diag(jnp.ones((HEADS, HEAD_DIM, HEAD_DIM), F32))

    w_in_t = w_in.swapaxes(1, 2)
    w_in_b = jnp.concatenate([w_in_t[:, :FOX_REAL], jnp.zeros((depth, LRU_X - FOX_REAL, d), w_in.dtype),
                              w_in_t[:, FOX_REAL:]], axis=1).astype(BF16)
    w_out_b = w_out.astype(BF16)
    wq_b, wo_b = xa_wq.astype(BF16), xa_wo.astype(BF16)
    wkv_b = jnp.concatenate([xa_wk, xa_wv], axis=-1).astype(BF16)
    w1_b, w2_b = mlp_w1.astype(BF16), mlp_w2.astype(BF16)

    x2 = x.reshape(n_tok, d)
    v_first = None
    for l in range(depth):
        proj = in_proj(x2, norm_mix_pre[l], w_in_b, l, tm=tm, tn=IN_PAD // 7)
        proj3 = proj.reshape(bsz, seq, IN_PAD)

        f_bias = jnp.zeros((1, GROUP), F32).at[0, :HEADS].set(fox_f_bias[l])
        qa, ka, vt = fox_prep(proj3, f_bias, tc=tq)
        fox_out = fox_attention(qa, ka, vt, tq=tq)

        lru_out = rglru(proj3, lru_conv_w[l], lru_conv_b[l], _block_diag(lru_ra_w[l]).astype(BF16), lru_ra_b[l],
                        _block_diag(lru_ri_w[l]).astype(BF16), lru_ri_b[l], lru_lambda[l], ts=ts)

        vres = None
        if l > 0:
            vres = (v_first, rwkv_v0[l - 1],
                    jnp.pad(rwkv_v1[l - 1], ((0, 0), (0, 128 - RWKV_V_RANK))).astype(BF16),
                    _pad_rows(rwkv_v2[l - 1], 0, 128).astype(BF16))
        mm_, gm_, qm_, y0_, bonus_, g_, v_ = rwkv_chunk(
            proj3, rwkv_mu[l].reshape(4, GROUP), rwkv_w0[l],
            _pad_rows(rwkv_w2[l], 0, GROUP).astype(BF16), rwkv_a0[l],
            _pad_rows(rwkv_a2[l], RWKV_W_RANK, GROUP).astype(BF16),
            _pad_rows(rwkv_g2[l], RWKV_W_RANK + RWKV_A_RANK, GROUP).astype(BF16),
            rwkv_k_k[l], rwkv_k_a[l], rwkv_r_k[l], head_ones, vres, ts=_tile(seq, 4 * RWKV_CHUNK), chunk=RWKV_CHUNK)
        if l == 0:
            v_first = v_
        rwkv_out = rwkv_state(mm_, gm_, qm_, y0_, bonus_, g_, rwkv_gn_w[l], rwkv_gn_b[l], chunk=RWKV_CHUNK)

        ret_out = retention(proj3, *ret_tables, ret_gn_w[l], chunk=RET_CHUNK)

        x2 = out_proj([m.reshape(n_tok, GROUP) for m in (fox_out, lru_out, rwkv_out, ret_out)],
                      w_out_b, l, x2, norm_mix_post[l], tm=tm)

        kv = norm_matmul(mem.reshape(bsz * mem_len, d), norm_mem[l], wkv_b, l, tm=_tile(bsz * mem_len, 512),
                         tn=1024, out_dtype=BF16).reshape(bsz, mem_len, 2 * d)
        x2 = cross_attention(x2.reshape(bsz, seq, d), kv, wq_b, wo_b, l,
                             norm_xa_pre[l], norm_xa_post[l], tm=_tile(seq, 512)).reshape(n_tok, d)

        x2 = mlp(x2, w1_b, w2_b, l, norm_mlp_pre[l], norm_mlp_post[l], tm=tm, ff_tile=1024)
    return x2.reshape(bsz, seq, d)
```

```python
import functools
import math

import jax
import jax.numpy as jnp
import numpy as np
from jax import lax
from jax.experimental import pallas as pl
from jax.experimental.pallas import tpu as pltpu

F32 = jnp.float32
BF16 = jnp.bfloat16
HIGHEST = lax.Precision.HIGHEST

D_MODEL = 1024
GROUP = 256
HEADS = 4
HEAD_DIM = 64
CONV_WIDTH = 4
LRU_C = 8.0
RET_THETA = 10000.0
RET_CHUNK = 128
RWKV_W_RANK, RWKV_A_RANK, RWKV_G_RANK, RWKV_V_RANK = 64, 64, 128, 32
RWKV_GN_EPS = 64e-5
XATTN_HEADS = 4
XATTN_HEAD_DIM = D_MODEL // XATTN_HEADS
NORM_EPS = 1e-6
NEG_BIG = -1e30

FOX_Q, FOX_K, FOX_V, FOX_F = 0, 256, 512, 768
LRU_X, LRU_Y = 1024, 1280
RWKV_R, RWKV_K, RWKV_V, RWKV_LR = 1536, 1792, 2048, 2304
RET_Q, RET_K, RET_V, RET_G = 2560, 2816, 3072, 3328
IN_PAD = 3584
FOX_REAL = 3 * GROUP + HEADS

VMEM_LIMIT = 56 * 1024 * 1024
RWKV_CHUNK = 64


def _cparams(sem):
    return pltpu.CompilerParams(dimension_semantics=sem, vmem_limit_bytes=VMEM_LIMIT)


def _rms(x, g):
    return x * lax.rsqrt(jnp.mean(x * x, axis=-1, keepdims=True) + NORM_EPS) * g


def _log_sigmoid(x):
    return jnp.minimum(x, 0.0) - jnp.log1p(jnp.exp(-jnp.abs(x)))


def _sigmoid(x):
    return 1.0 / (1.0 + jnp.exp(-x))


def _dot(a, b, **kw):
    return jnp.dot(a, b, preferred_element_type=F32, **kw)


def _dot_nt(a, b, **kw):
    return lax.dot_general(a, b, (((1,), (1,)), ((), ())), preferred_element_type=F32, **kw)


def _dot_tn(a, b, **kw):
    return lax.dot_general(a, b, (((0,), (0,)), ((), ())), preferred_element_type=F32, **kw)


def _split_bf16(x):
    hi = x.astype(BF16)
    return hi, (x - hi.astype(F32)).astype(BF16)


def _mm1(dot, a, b):
    return dot(a.astype(BF16), b.astype(BF16))


def _mm3(dot, a, b):
    ah, al = _split_bf16(a)
    bh, bl = _split_bf16(b)
    return dot(ah, bh) + (dot(ah, bl) + dot(al, bh))


def _bf16_terms(x, n):
    terms = []
    for _ in range(n - 1):
        t = x.astype(BF16)
        terms.append(t)
        x = x - t.astype(F32)
    return terms + [x.astype(BF16)]


def _dot_sel(a, b, *, split, n=3):
    if split == "a":
        parts = [_dot(t, b.astype(BF16)) for t in _bf16_terms(a, n)]
    else:
        parts = [_dot(a.astype(BF16), t) for t in _bf16_terms(b, n)]
    out = parts[-1]
    for p in reversed(parts[:-1]):
        out = out + p
    return out


def _tri(n, strict=False):
    r = lax.broadcasted_iota(jnp.int32, (n, n), 0)
    c = lax.broadcasted_iota(jnp.int32, (n, n), 1)
    return (r > c) if strict else (r >= c)


def _norm_matmul_kernel(x_ref, g_ref, w_ref, o_ref, xn_ref):
    @pl.when(pl.program_id(1) == 0)
    def _():
        xn_ref[...] = _rms(x_ref[...], g_ref[...]).astype(BF16)

    o_ref[...] = _dot(xn_ref[...], w_ref[...]).astype(o_ref.dtype)


def norm_matmul(x, g, w, layer, *, tm, tn, out_dtype):
    n, d = x.shape
    width = w.shape[2]
    return pl.pallas_call(
        _norm_matmul_kernel,
        grid=(n // tm, width // tn),
        in_specs=[pl.BlockSpec((tm, d), lambda i, j: (i, 0)),
                  pl.BlockSpec((1, d), lambda i, j: (0, 0)),
                  pl.BlockSpec((None, d, tn), lambda i, j: (layer, 0, j))],
        out_specs=pl.BlockSpec((tm, tn), lambda i, j: (i, j)),
        out_shape=jax.ShapeDtypeStruct((n, width), out_dtype),
        scratch_shapes=[pltpu.VMEM((tm, d), BF16)],
        compiler_params=_cparams(("parallel", "arbitrary")),
        name="norm_matmul",
    )(x, g.reshape(1, d), w)


def _in_proj_kernel(x_ref, g_ref, w_ref, o_ref, *, tn):
    xn = _rms(x_ref[...], g_ref[...]).astype(BF16)
    for c in range(w_ref.shape[0] // tn):
        o_ref[:, c * tn:(c + 1) * tn] = _dot_nt(xn, w_ref[c * tn:(c + 1) * tn, :])


def in_proj(x, g, w, layer, *, tm, tn):
    n, d = x.shape
    width = w.shape[1]
    return pl.pallas_call(
        functools.partial(_in_proj_kernel, tn=tn),
        grid=(n // tm,),
        in_specs=[pl.BlockSpec((tm, d), lambda i: (i, 0)),
                  pl.BlockSpec((1, d), lambda i: (0, 0)),
                  pl.BlockSpec((None, width, d), lambda i: (layer, 0, 0), pipeline_mode=pl.Buffered(1))],
        out_specs=pl.BlockSpec((tm, width), lambda i: (i, 0)),
        out_shape=jax.ShapeDtypeStruct((n, width), F32),
        compiler_params=_cparams(("parallel",)),
        name="in_proj",
    )(x, g.reshape(1, d), w)


FOX_AUG = 128
LOG2E = 1.4426950408889634


def _fox_prep_kernel(q_ref, k_ref, v_ref, f_ref, b_ref, qa_ref, ka_ref, vt_ref, carry_ref):
    @pl.when(pl.program_id(1) == 0)
    def _():
        carry_ref[...] = jnp.zeros_like(carry_ref)

    lf = _log_sigmoid(f_ref[0] + b_ref[...])
    tc = lf.shape[0]
    cum = _dot_sel(_tri(tc), lf, split="b") + carry_ref[...]
    carry_ref[...] = cum[tc - 1:tc, :]

    lane = lax.broadcasted_iota(jnp.int32, (tc, FOX_AUG), 1)
    feat = lane < HEAD_DIM
    ones = jnp.where((lane >= HEAD_DIM) & (lane < HEAD_DIM + 3), 1.0, 0.0)
    q = q_ref[0] * (HEAD_DIM ** -0.5 * LOG2E)
    k = k_ref[0]
    for h in range(HEADS):
        pair = slice((h // 2) * FOX_AUG, (h // 2 + 1) * FOX_AUG)
        qt, kt = q[:, pair], k[:, pair]
        if h % 2:
            qt, kt = pltpu.roll(qt, HEAD_DIM, axis=1), pltpu.roll(kt, HEAD_DIM, axis=1)
        neg_c = jnp.broadcast_to(cum[:, h:h + 1] * -LOG2E, (tc, FOX_AUG))
        c_hi = neg_c.astype(BF16).astype(F32)
        rest = neg_c - c_hi
        c_mid = rest.astype(BF16).astype(F32)
        bias = jnp.where(lane == HEAD_DIM, c_hi, jnp.where(lane == HEAD_DIM + 1, c_mid,
                                                           jnp.where(lane == HEAD_DIM + 2, rest - c_mid, 0.0)))
        qa_ref[0, h] = jnp.where(feat, qt, ones).astype(BF16)
        ka_ref[0, h] = jnp.where(feat, kt, bias).astype(BF16)
    vt_ref[0, :, 0] = v_ref[0].T.reshape(HEADS, HEAD_DIM, tc).astype(BF16)


def fox_prep(proj3, f_bias_pad, *, tc):
    b, s, _ = proj3.shape
    slab = lambda off: pl.BlockSpec((1, tc, GROUP), lambda i, j: (i, j, off // GROUP))
    aug = pl.BlockSpec((1, HEADS, tc, FOX_AUG), lambda i, j: (i, 0, j, 0))
    return pl.pallas_call(
        _fox_prep_kernel,
        grid=(b, s // tc),
        in_specs=[slab(FOX_Q), slab(FOX_K), slab(FOX_V), slab(FOX_F), pl.BlockSpec((1, GROUP), lambda i, j: (0, 0))],
        out_specs=[aug, aug, pl.BlockSpec((1, HEADS, 1, HEAD_DIM, tc), lambda i, j: (i, 0, j, 0, 0))],
        out_shape=[jax.ShapeDtypeStruct((b, HEADS, s, FOX_AUG), BF16),
                   jax.ShapeDtypeStruct((b, HEADS, s, FOX_AUG), BF16),
                   jax.ShapeDtypeStruct((b, HEADS, s // tc, HEAD_DIM, tc), BF16)],
        scratch_shapes=[pltpu.VMEM((1, GROUP), F32)],
        compiler_params=_cparams(("parallel", "arbitrary")),
        name="fox_prep",
    )(proj3, proj3, proj3, proj3, f_bias_pad)


def _fox_attn_kernel(q_ref, k_ref, v_ref, o_ref, s_ref, *, t):
    i = pl.program_id(1)
    qa = q_ref[0]

    def scores(j):
        return _bnt(k_ref[0, :, pl.ds(pl.multiple_of(j * t, t), t), :], qa)

    def update(carry, slot, j):
        m, l, acc = carry
        s = s_ref[slot]
        m_new = jnp.maximum(m, jnp.max(s, axis=1, keepdims=True))
        alpha = jnp.exp2(m - m_new)
        p = jnp.exp2(s - m_new)
        return (m_new, alpha * l + jnp.sum(p, axis=1, keepdims=True),
                alpha * acc + _bnn(v_ref[0, :, j], p.astype(BF16)))

    key = lax.broadcasted_iota(jnp.int32, (1, t, t), 1)
    qry = lax.broadcasted_iota(jnp.int32, (1, t, t), 2)
    s_ref[0] = jnp.where(key <= qry, scores(i), NEG_BIG)
    carry = (jnp.full((HEADS, 1, t), NEG_BIG, F32), jnp.zeros((HEADS, 1, t), F32),
             jnp.zeros((HEADS, HEAD_DIM, t), F32))

    def pair(jj, carry):
        s_ref[1] = scores(2 * jj)
        carry = update(carry, 0, jnp.where(jj == 0, i, 2 * jj - 1))
        s_ref[0] = scores(2 * jj + 1)
        return update(carry, 1, 2 * jj)

    carry = lax.fori_loop(0, i // 2, pair, carry)
    pending = jnp.where(i < 2, i, 2 * (i // 2) - 1)

    def odd_tail(carry):
        s_ref[1] = scores(i - 1)
        return update(update(carry, 0, pending), 1, i - 1)

    m, l, acc = lax.cond(i % 2 == 1, odd_tail, lambda c: update(c, 0, pending), carry)
    o_ref[0] = (acc / l).reshape(GROUP, t).T.astype(o_ref.dtype)


def fox_attention(qa, ka, vt):
    b, h, s, _ = qa.shape
    nk, t = vt.shape[2], vt.shape[4]
    return pl.pallas_call(
        functools.partial(_fox_attn_kernel, t=t),
        grid=(b, nk),
        in_specs=[pl.BlockSpec((1, h, t, FOX_AUG), lambda bi, i: (bi, 0, i, 0)),
                  pl.BlockSpec((1, h, s, FOX_AUG), lambda bi, i: (bi, 0, 0, 0)),
                  pl.BlockSpec((1, h, nk, HEAD_DIM, t), lambda bi, i: (bi, 0, 0, 0, 0))],
        out_specs=pl.BlockSpec((1, t, GROUP), lambda bi, i: (bi, i, 0)),
        out_shape=jax.ShapeDtypeStruct((b, s, GROUP), BF16),
        scratch_shapes=[pltpu.VMEM((2, h, t, t), F32)],
        compiler_params=_cparams(("parallel", "arbitrary")),
        name="fox_attention",
    )(qa, ka, vt)


def _lru_kernel(x_ref, y_ref, cw_ref, cb_ref, wra_ref, bra_ref, wri_ref, bri_ref, lam_ref, o_ref,
                buf_ref, h_ref, *, ts):
    @pl.when(pl.program_id(1) == 0)
    def _():
        buf_ref[0:8, :] = jnp.zeros((8, GROUP), F32)
        h_ref[...] = jnp.zeros_like(h_ref)

    xb = x_ref[0]
    buf_ref[8:8 + ts, :] = xb
    xc = cb_ref[...] + buf_ref[5:5 + ts, :] * cw_ref[0:1, :]
    for j in range(1, CONV_WIDTH):
        xc = xc + buf_ref[5 + j:5 + j + ts, :] * cw_ref[j:j + 1, :]
    buf_ref[0:8, :] = xb[ts - 8:ts, :]

    xcb = xc.astype(BF16)
    r = _sigmoid(_dot(xcb, wra_ref[...]) + bra_ref[...])
    gate_i = _sigmoid(_dot(xcb, wri_ref[...]) + bri_ref[...])
    log_a = LRU_C * r * _log_sigmoid(lam_ref[...])
    a = jnp.exp(log_a)
    z = 2.0 * log_a
    mult = jnp.sqrt(jnp.maximum(-jnp.tanh(0.5 * z) * (jnp.exp(z) + 1.0), 0.0))
    u = mult * (gate_i * xc)

    row = lax.broadcasted_iota(jnp.int32, (ts, GROUP), 0)
    pa, pb = a, u
    d = 1
    while d < ts:
        sa = pltpu.roll(pa, d, axis=0)
        sb = pltpu.roll(pb, d, axis=0)
        valid = row >= d
        pb = jnp.where(valid, pa * sb + pb, pb)
        pa = jnp.where(valid, pa * sa, pa)
        d *= 2
    hseq = pa * h_ref[...] + pb
    h_ref[...] = hseq[ts - 1:ts, :]

    y = y_ref[0]
    gelu = 0.5 * y * (1.0 + jnp.tanh(math.sqrt(2.0 / math.pi) * (y + 0.044715 * (y * y * y))))
    o_ref[0] = (hseq * gelu).astype(o_ref.dtype)


def rglru(proj3, conv_w, conv_b, wra_bd, ra_b, wri_bd, ri_b, lam, *, ts):
    b, s, _ = proj3.shape
    vec = lambda: pl.BlockSpec((1, GROUP), lambda i, j: (0, 0))
    mat = lambda: pl.BlockSpec((GROUP, GROUP), lambda i, j: (0, 0))
    return pl.pallas_call(
        functools.partial(_lru_kernel, ts=ts),
        grid=(b, s // ts),
        in_specs=[pl.BlockSpec((1, ts, GROUP), lambda i, j: (i, j, LRU_X // GROUP)),
                  pl.BlockSpec((1, ts, GROUP), lambda i, j: (i, j, LRU_Y // GROUP)),
                  pl.BlockSpec((CONV_WIDTH, GROUP), lambda i, j: (0, 0)),
                  vec(), mat(), vec(), mat(), vec(), vec()],
        out_specs=pl.BlockSpec((1, ts, GROUP), lambda i, j: (i, j, 0)),
        out_shape=jax.ShapeDtypeStruct((b, s, GROUP), BF16),
        scratch_shapes=[pltpu.VMEM((ts + 8, GROUP), F32), pltpu.VMEM((1, GROUP), F32)],
        compiler_params=_cparams(("parallel", "arbitrary")),
        name="rglru",
    )(proj3, proj3, conv_w, conv_b.reshape(1, GROUP), wra_bd, ra_b.reshape(1, GROUP), wri_bd,
      ri_b.reshape(1, GROUP), lam.reshape(1, GROUP))


def _bdot(dims):
    return lambda a, b: lax.dot_general(a, b, (dims, ((0,), (0,))), preferred_element_type=F32)


_bnn = _bdot(((2,), (1,)))
_bnt = _bdot(((2,), (2,)))
_btn = _bdot(((1,), (1,)))


def _unit_lower_inverse(a_strict, n):
    r = lax.broadcasted_iota(jnp.int32, (1, n, n), 1)
    c = lax.broadcasted_iota(jnp.int32, (1, n, n), 2)
    t = jnp.where(r == c, 1.0, jnp.where((r // 2 == c // 2) & (r > c), a_strict, 0.0))
    m = 2
    while m < n:
        off = (r // (2 * m) == c // (2 * m)) & (r % (2 * m) >= m) & (c % (2 * m) < m)
        tb = t.astype(BF16)
        t = t + _bnn(tb, _bnn(jnp.where(off, a_strict, 0.0).astype(BF16), tb).astype(BF16))
        m *= 2
    return t


def _to_heads(x):
    n, rows, _ = x.shape
    parts = [x[:, :, h * HEAD_DIM:(h + 1) * HEAD_DIM] for h in range(HEADS)]
    return jnp.stack(parts, axis=1).reshape(n * HEADS, rows, HEAD_DIM)


def _from_heads(x):
    nh, rows, _ = x.shape
    x = x.reshape(nh // HEADS, HEADS, rows, HEAD_DIM)
    return jnp.concatenate([x[:, h] for h in range(HEADS)], axis=-1)


def _rwkv_chunk_kernel(*refs, ts, chunk, has_vres):
    (sr_ref, sk_ref, sv_ref, sl_ref, mu_ref, w0_ref, w2_ref, a0_ref, a2_ref, g2_ref, kk_ref, ka_ref, rk_ref,
     ones_ref) = refs[:14]
    refs = refs[14:]
    if has_vres:
        vf_ref, v0_ref, v1_ref, v2_ref = refs[:4]
        refs = refs[4:]
    mm_out, gm_out, qm_out, y0_out, bonus_out, g_out, v_out, carry_ref = refs

    @pl.when(pl.program_id(1) == 0)
    def _():
        carry_ref[...] = jnp.zeros_like(carry_ref)

    row0 = lax.broadcasted_iota(jnp.int32, (ts, GROUP), 0) == 0

    def shift_mix(ref, idx):
        s = ref[0]
        prev = jnp.where(row0, carry_ref[idx:idx + 1, :], pltpu.roll(s, 1, axis=0))
        carry_ref[idx:idx + 1, :] = s[ts - 1:ts, :]
        return s + (prev - s) * mu_ref[idx:idx + 1, :]

    r = shift_mix(sr_ref, 0)
    k = shift_mix(sk_ref, 1)
    v = shift_mix(sv_ref, 2)
    low = shift_mix(sl_ref, 3)

    zw = w0_ref[...] + _dot(jnp.tanh(low).astype(BF16), w2_ref[...])
    lw = -jnp.exp(_log_sigmoid(zw) - 0.5)
    a = _sigmoid(a0_ref[...] + _dot(low.astype(BF16), a2_ref[...]))
    g_out[0] = _dot(_sigmoid(low).astype(BF16), g2_ref[...])
    if has_vres:
        mix = _dot(_dot(v.astype(BF16), v1_ref[...]).astype(BF16), v2_ref[...])
        v = v + (vf_ref[0] - v) * _sigmoid(v0_ref[...] + mix)
    v_out[0] = v
    kk = k * kk_ref[...]
    ss = _dot_sel(kk * kk, ones_ref[...], split="a")
    kk = kk / jnp.maximum(jnp.sqrt(ss), 1e-12)
    k = k * (1.0 + (a - 1.0) * ka_ref[...])
    bonus_out[0] = _dot_sel(r * k * rk_ref[...], ones_ref[...], split="a") * v
    a_vec, b_vec = -kk, kk * a

    c = chunk
    rr = lax.broadcasted_iota(jnp.int32, (ts, ts), 0)
    cc = lax.broadcasted_iota(jnp.int32, (ts, ts), 1)
    chunk_tri = ((rr >= cc) & (rr // c == cc // c)).astype(F32)
    cum_all = _dot_sel(chunk_tri, lw, split="b")
    strict = _tri(c, strict=True)[None]
    incl = _tri(c)[None]
    eye = (lax.broadcasted_iota(jnp.int32, (1, HEAD_DIM, HEAD_DIM), 1)
           == lax.broadcasted_iota(jnp.int32, (1, HEAD_DIM, HEAD_DIM), 2)).astype(F32)

    nq = ts // c
    per_chunk = lambda t: t.reshape(nq, c, GROUP)
    cum = per_chunk(cum_all)
    rq, kq, vq, aq, bq = (per_chunk(t) for t in (r, k, v, a_vec, b_vec))
    cum_ex = cum - per_chunk(lw)
    mid = cum[:, c // 2 - 1:c // 2, :]
    tot = cum[:, c - 1:c, :]
    e_fwd = jnp.exp(cum - mid)
    e_bwd = jnp.exp(mid - cum)
    e_end = jnp.exp(tot - cum)
    mxu = lambda t: _to_heads(t).astype(BF16)
    r_rel, k_rel = mxu(rq * e_fwd), mxu(kq * e_bwd)
    a_rel, b_rel = mxu(aq * jnp.exp(cum_ex - mid)), mxu(bq * e_bwd)
    a_abs, r_abs = _to_heads(aq * jnp.exp(cum_ex)), _to_heads(rq * jnp.exp(cum))
    k_end, b_end = mxu(kq * e_end), mxu(bq * e_end)
    gam = _to_heads(jnp.exp(tot))
    vh = mxu(vq)

    ar_rel = jnp.concatenate([a_rel, r_rel], axis=1)
    s_b, s_k = _bnt(ar_rel, b_rel), _bnt(ar_rel, k_rel)
    a_ab = jnp.where(strict, s_b[:, :c], 0.0)
    a_ak = jnp.where(strict, s_k[:, :c], 0.0).astype(BF16)
    a_rb = jnp.where(incl, s_b[:, c:], 0.0).astype(BF16)
    a_rk = jnp.where(incl, s_k[:, c:], 0.0).astype(BF16)
    t_inv = _unit_lower_inverse(a_ab, c).astype(BF16)
    pu = _bnn(t_inv, jnp.concatenate([a_abs, _bnn(a_ak, vh)], axis=-1).astype(BF16)).astype(BF16)
    z = _bnn(a_rb, pu)
    qm = r_abs + z[..., :HEAD_DIM]
    y0 = z[..., HEAD_DIM:] + _bnn(a_rk, vh)
    xtb = _btn(pu, b_end)
    mm_mat = eye * gam + xtb[:, :HEAD_DIM]
    gm = xtb[:, HEAD_DIM:] + _btn(vh, k_end)
    mm_out[0] = _from_heads(mm_mat).reshape(ts, GROUP)
    gm_out[0] = _from_heads(gm).reshape(ts, GROUP)
    qm_out[0] = _from_heads(qm).reshape(ts, GROUP)
    y0_out[0] = _from_heads(y0).reshape(ts, GROUP)


def rwkv_chunk(proj3, mu4, w0, w2p, a0, a2p, g2p, k_k, k_a, r_k, head_ones, vres, *, ts, chunk):
    b, s, _ = proj3.shape
    slab = lambda off: pl.BlockSpec((1, ts, GROUP), lambda i, j: (i, j, off // GROUP))
    vec = lambda: pl.BlockSpec((1, GROUP), lambda i, j: (0, 0))
    full = lambda shape: pl.BlockSpec(shape, lambda i, j: tuple(0 for _ in shape))
    tok = pl.BlockSpec((1, ts, GROUP), lambda i, j: (i, j, 0))
    in_specs = [slab(RWKV_R), slab(RWKV_K), slab(RWKV_V), slab(RWKV_LR), full((4, GROUP)),
                vec(), full((GROUP, GROUP)), vec(), full((GROUP, GROUP)), full((GROUP, GROUP)), vec(), vec(), vec(),
                full((GROUP, GROUP))]
    args = [proj3, proj3, proj3, proj3, mu4, w0.reshape(1, GROUP), w2p, a0.reshape(1, GROUP), a2p, g2p,
            k_k.reshape(1, GROUP), k_a.reshape(1, GROUP), r_k.reshape(1, GROUP), head_ones]
    if vres is not None:
        v_first, v0, v1p, v2p = vres
        in_specs += [tok, vec(), full((GROUP, 128)), full((128, GROUP))]
        args += [v_first, v0.reshape(1, GROUP), v1p, v2p]
    return pl.pallas_call(
        functools.partial(_rwkv_chunk_kernel, ts=ts, chunk=chunk, has_vres=vres is not None),
        grid=(b, s // ts),
        in_specs=in_specs,
        out_specs=[tok] * 7,
        out_shape=[jax.ShapeDtypeStruct((b, s, GROUP), F32)] * 7,
        scratch_shapes=[pltpu.VMEM((4, GROUP), F32)],
        compiler_params=_cparams(("parallel", "arbitrary")),
        name="rwkv_chunk",
    )(*args)


def _rwkv_state_kernel(mm_ref, gm_ref, qm_ref, y0_ref, bonus_ref, g_ref, gw_ref, gb_ref, o_ref, state_ref):
    @pl.when(pl.program_id(0) == 0)
    def _():
        state_ref[...] = jnp.zeros_like(state_ref)

    s0 = state_ref[...]
    y = _mm3(_bnt, _to_heads(qm_ref[...]), s0) + _to_heads(y0_ref[...])
    state_ref[...] = _mm3(_bnn, s0, _to_heads(mm_ref[...])) + _to_heads(gm_ref[...])
    mu = jnp.mean(y, axis=-1, keepdims=True)
    var = jnp.mean(jnp.square(y - mu), axis=-1, keepdims=True)
    yn = _from_heads((y - mu) * lax.rsqrt(var + RWKV_GN_EPS)) * gw_ref[...] + gb_ref[...]
    o_ref[...] = ((yn + bonus_ref[...]) * g_ref[...]).astype(o_ref.dtype)


def rwkv_state(mm, gm, qm, y0, bonus, g, gn_w, gn_b, *, chunk):
    bsz, s, _ = mm.shape
    tok = pl.BlockSpec((bsz, chunk, GROUP), lambda j: (0, j, 0))
    vec = pl.BlockSpec((1, GROUP), lambda j: (0, 0))
    return pl.pallas_call(
        _rwkv_state_kernel,
        grid=(s // chunk,),
        in_specs=[tok] * 6 + [vec] * 2,
        out_specs=tok,
        out_shape=jax.ShapeDtypeStruct((bsz, s, GROUP), BF16),
        scratch_shapes=[pltpu.VMEM((bsz * HEADS, HEAD_DIM, HEAD_DIM), F32)],
        compiler_params=_cparams(("arbitrary",)),
        name="rwkv_state",
    )(mm, gm, qm, y0, bonus, g, gn_w.reshape(1, GROUP), gn_b.reshape(1, GROUP))


def _ret_kernel(q_ref, k_ref, v_ref, g_ref, cos_ref, sin_ref, dmat_ref, xi_ref, zeta_ref, cd_ref, gw_ref, o_ref,
                state_ref, *, ts, chunk):
    @pl.when(pl.program_id(1) == 0)
    def _():
        state_ref[...] = jnp.zeros_like(state_ref)

    lane = lax.broadcasted_iota(jnp.int32, (ts, GROUP), 1)
    first_half = (lane % HEAD_DIM) < (HEAD_DIM // 2)
    cos, sin = cos_ref[...], sin_ref[...]

    def rotary(t):
        partner = jnp.where(first_half, pltpu.roll(t, GROUP - HEAD_DIM // 2, axis=1),
                            pltpu.roll(t, HEAD_DIM // 2, axis=1))
        return t * cos + partner * sin

    nq = ts // chunk
    per_chunk = lambda t: t.reshape(nq, chunk, GROUP)
    mxu = lambda t: _to_heads(t).astype(BF16)
    q = per_chunk(rotary(q_ref[0]))
    k = per_chunk(rotary(k_ref[0]) * (HEAD_DIM ** -0.5))
    qb, kb, vb = mxu(q), mxu(k), mxu(per_chunk(v_ref[0]))
    q_cross, k_decay = mxu(q * xi_ref[...]), mxu(k * zeta_ref[...])
    inner = _bnt(qb, kb).reshape(nq, HEADS, chunk, chunk) * dmat_ref[...]
    intra = _bnn(inner.reshape(nq * HEADS, chunk, chunk).astype(BF16), vb)
    kv = _btn(k_decay, vb)

    decay = _to_heads(cd_ref[...][None])
    state = state_ref[...]
    incoming = []
    for c in range(nq):
        incoming.append(state)
        state = state * decay + kv[c * HEADS:(c + 1) * HEADS]
    state_ref[...] = state
    o = intra + _bnn(q_cross, jnp.concatenate(incoming, axis=0).astype(BF16))
    o = _from_heads(o * lax.rsqrt(jnp.mean(o * o, axis=-1, keepdims=True) + NORM_EPS)).reshape(ts, GROUP)
    g = g_ref[0]
    o_ref[0] = (o * gw_ref[...] * (g * _sigmoid(g))).astype(o_ref.dtype)


def retention(proj3, cos_t, sin_t, dmat, xi, zeta, cd, gn_w, *, ts, chunk):
    b, s, _ = proj3.shape
    slab = lambda off: pl.BlockSpec((1, ts, GROUP), lambda i, j: (i, j, off // GROUP))
    full = lambda shape: pl.BlockSpec(shape, lambda i, j: tuple(0 for _ in shape))
    return pl.pallas_call(
        functools.partial(_ret_kernel, ts=ts, chunk=chunk),
        grid=(b, s // ts),
        in_specs=[slab(RET_Q), slab(RET_K), slab(RET_V), slab(RET_G),
                  pl.BlockSpec((ts, GROUP), lambda i, j: (j, 0)),
                  pl.BlockSpec((ts, GROUP), lambda i, j: (j, 0)),
                  full((HEADS, chunk, chunk)), full((chunk, GROUP)), full((chunk, GROUP)),
                  full((1, GROUP)), full((1, GROUP))],
        out_specs=pl.BlockSpec((1, ts, GROUP), lambda i, j: (i, j, 0)),
        out_shape=jax.ShapeDtypeStruct((b, s, GROUP), BF16),
        scratch_shapes=[pltpu.VMEM((HEADS, HEAD_DIM, HEAD_DIM), F32)],
        compiler_params=_cparams(("parallel", "arbitrary")),
        name="retention",
    )(proj3, proj3, proj3, proj3, cos_t, sin_t, dmat, xi, zeta, cd, gn_w.reshape(1, GROUP))


def _retention_tables(s, chunk):
    half = HEAD_DIM // 2
    inv = 1.0 / (RET_THETA ** jnp.linspace(0.0, 1.0, half, dtype=F32))
    ang = jnp.arange(s, dtype=F32)[:, None] * inv[None, :]
    cos, sin = jnp.cos(ang), jnp.sin(ang)
    cos_t = jnp.tile(jnp.concatenate([cos, cos], axis=-1), (1, HEADS))
    sin_t = jnp.tile(jnp.concatenate([-sin, sin], axis=-1), (1, HEADS))
    lg = jnp.log(1.0 - 2.0 ** (-5.0 - jnp.arange(HEADS, dtype=F32)))
    n = jnp.arange(chunk, dtype=F32)
    diff = n[:, None] - n[None, :]
    dmat = jnp.where(diff >= 0, jnp.exp(lg[:, None, None] * jnp.maximum(diff, 0.0)), 0.0)
    zeta = jnp.exp(lg[:, None] * (chunk - 1.0 - n)[None, :])
    xi = jnp.exp(lg[:, None] * (n + 1.0)[None, :])
    per_lane = lambda t: jnp.repeat(t.T, HEAD_DIM, axis=1)
    cd = jnp.repeat(jnp.exp(lg * chunk), HEAD_DIM)[None, :]
    return cos_t, sin_t, dmat, per_lane(xi), per_lane(zeta), cd


def _mix_xattn_kernel(m0_ref, m1_ref, m2_ref, m3_ref, x_ref, kv_ref, wout_ref, wq_ref, wo_ref, gmix_ref, gpre_ref,
                      gpost_ref, o_ref):
    acc = _dot(m0_ref[0], wout_ref[0:GROUP, :])
    for idx, m_ref in enumerate((m1_ref, m2_ref, m3_ref), start=1):
        acc = acc + _dot(m_ref[0], wout_ref[idx * GROUP:(idx + 1) * GROUP, :])
    x = x_ref[0] + _rms(acc, gmix_ref[...])
    xn = _rms(x, gpre_ref[...]).astype(BF16)
    q = (_dot(xn, wq_ref[...]) * (XATTN_HEAD_DIM ** -0.5)).astype(BF16)
    outs = []
    for h in range(XATTN_HEADS):
        sl = slice(h * XATTN_HEAD_DIM, (h + 1) * XATTN_HEAD_DIM)
        s = _dot_nt(q[:, sl], kv_ref[0, :, sl])
        e = jnp.exp(s - jnp.max(s, axis=-1, keepdims=True))
        p = e / jnp.sum(e, axis=-1, keepdims=True)
        outs.append(_dot(p.astype(BF16), kv_ref[0, :, D_MODEL + h * XATTN_HEAD_DIM:D_MODEL + (h + 1) * XATTN_HEAD_DIM]))
    o = jnp.concatenate(outs, axis=-1).astype(BF16)
    o_ref[0] = x + _rms(_dot(o, wo_ref[...]), gpost_ref[...])


def mix_xattn(mixed, x3, kv, w_out, wq, wo, layer, g_mix, g_pre, g_post, *, tm):
    b, s, d = x3.shape
    m = kv.shape[1]
    tok = lambda width: pl.BlockSpec((1, tm, width), lambda bi, i: (bi, i, 0))
    weight = lambda: pl.BlockSpec((None, d, d), lambda bi, i: (layer, 0, 0))
    vec = lambda: pl.BlockSpec((1, d), lambda bi, i: (0, 0))
    return pl.pallas_call(
        _mix_xattn_kernel,
        grid=(b, s // tm),
        in_specs=[tok(GROUP)] * 4 + [tok(d), pl.BlockSpec((1, m, 2 * d), lambda bi, i: (bi, 0, 0)),
                                     weight(), weight(), weight(), vec(), vec(), vec()],
        out_specs=tok(d),
        out_shape=jax.ShapeDtypeStruct((b, s, d), F32),
        compiler_params=_cparams(("parallel", "arbitrary")),
        name="mix_xattn",
    )(*mixed, x3, kv, w_out, wq, wo, g_mix.reshape(1, d), g_pre.reshape(1, d), g_post.reshape(1, d))


def _mlp_kernel(x_ref, w1_ref, w2_ref, gpre_ref, gpost_ref, o_ref, *, ff_tile):
    x = x_ref[...]
    xn = _rms(x, gpre_ref[...]).astype(BF16)
    d_ff = w1_ref.shape[1]
    acc = None
    for c in range(d_ff // ff_tile):
        hid = jnp.square(jnp.maximum(_dot(xn, w1_ref[:, c * ff_tile:(c + 1) * ff_tile]), 0.0)).astype(BF16)
        part = _dot(hid, w2_ref[c * ff_tile:(c + 1) * ff_tile, :])
        acc = part if acc is None else acc + part
    o_ref[...] = x + _rms(acc, gpost_ref[...])


def mlp(x, w1, w2, layer, g_pre, g_post, *, tm, ff_tile):
    n, d = x.shape
    d_ff = w1.shape[2]
    return pl.pallas_call(
        functools.partial(_mlp_kernel, ff_tile=ff_tile),
        grid=(n // tm,),
        in_specs=[pl.BlockSpec((tm, d), lambda i: (i, 0)),
                  pl.BlockSpec((None, d, d_ff), lambda i: (layer, 0, 0), pipeline_mode=pl.Buffered(1)),
                  pl.BlockSpec((None, d_ff, d), lambda i: (layer, 0, 0), pipeline_mode=pl.Buffered(1)),
                  pl.BlockSpec((1, d), lambda i: (0, 0)),
                  pl.BlockSpec((1, d), lambda i: (0, 0))],
        out_specs=pl.BlockSpec((tm, d), lambda i: (i, 0)),
        out_shape=jax.ShapeDtypeStruct((n, d), F32),
        compiler_params=_cparams(("parallel",)),
        name="mlp",
    )(x, w1, w2, g_pre.reshape(1, d), g_post.reshape(1, d))


def _block_diag(w):
    h, n, _ = w.shape
    eye = jnp.eye(h, dtype=w.dtype)
    return (eye[:, None, :, None] * w[:, :, None, :]).reshape(h * n, h * n)


def _pad_rows(w, start, total):
    return jnp.zeros((total, w.shape[1]), w.dtype).at[start:start + w.shape[0]].set(w)


def _tile(n, pref):
    return pref if n % pref == 0 else n


def kernel(x, mem, norm_mix_pre, norm_mix_post, norm_xa_pre, norm_xa_post, norm_mem, norm_mlp_pre, norm_mlp_post, w_in, w_out, fox_f_bias, lru_conv_w, lru_conv_b, lru_ra_w, lru_ra_b, lru_ri_w, lru_ri_b, lru_lambda, rwkv_mu, rwkv_w0, rwkv_w2, rwkv_a0, rwkv_a2, rwkv_g2, rwkv_k_k, rwkv_k_a, rwkv_r_k, rwkv_gn_w, rwkv_gn_b, rwkv_v0, rwkv_v1, rwkv_v2, ret_gn_w, xa_wq, xa_wk, xa_wv, xa_wo, mlp_w1, mlp_w2):
    bsz, seq, d = x.shape
    depth = w_in.shape[0]
    n_tok = bsz * seq
    mem_len = mem.shape[1]
    tm = _tile(n_tok, 512)
    tq = _tile(seq, 512)
    ts = _tile(seq, 256)
    ret_tables = _retention_tables(seq, RET_CHUNK)
    head_ones = _block_diag(jnp.ones((HEADS, HEAD_DIM, HEAD_DIM), BF16))

    w_in_t = w_in.swapaxes(1, 2)
    w_in_b = jnp.concatenate([w_in_t[:, :FOX_REAL], jnp.zeros((depth, LRU_X - FOX_REAL, d), w_in.dtype),
                              w_in_t[:, FOX_REAL:]], axis=1).astype(BF16)
    w_out_b = w_out.astype(BF16)
    wq_b, wo_b = xa_wq.astype(BF16), xa_wo.astype(BF16)
    wkv_b = jnp.concatenate([xa_wk, xa_wv], axis=-1).astype(BF16)
    w1_b, w2_b = mlp_w1.astype(BF16), mlp_w2.astype(BF16)

    x2 = x.reshape(n_tok, d)
    v_first = None
    for l in range(depth):
        proj = in_proj(x2, norm_mix_pre[l], w_in_b, l, tm=tm, tn=IN_PAD // 7)
        proj3 = proj.reshape(bsz, seq, IN_PAD)

        f_bias = jnp.zeros((1, GROUP), F32).at[0, :HEADS].set(fox_f_bias[l])
        qa, ka, vt = fox_prep(proj3, f_bias, tc=tq)
        fox_out = fox_attention(qa, ka, vt)

        lru_out = rglru(proj3, lru_conv_w[l], lru_conv_b[l], _block_diag(lru_ra_w[l]).astype(BF16), lru_ra_b[l],
                        _block_diag(lru_ri_w[l]).astype(BF16), lru_ri_b[l], lru_lambda[l], ts=ts)

        vres = None
        if l > 0:
            vres = (v_first, rwkv_v0[l - 1],
                    jnp.pad(rwkv_v1[l - 1], ((0, 0), (0, 128 - RWKV_V_RANK))).astype(BF16),
                    _pad_rows(rwkv_v2[l - 1], 0, 128).astype(BF16))
        mm_, gm_, qm_, y0_, bonus_, g_, v_ = rwkv_chunk(
            proj3, rwkv_mu[l].reshape(4, GROUP), rwkv_w0[l],
            _pad_rows(rwkv_w2[l], 0, GROUP).astype(BF16), rwkv_a0[l],
            _pad_rows(rwkv_a2[l], RWKV_W_RANK, GROUP).astype(BF16),
            _pad_rows(rwkv_g2[l], RWKV_W_RANK + RWKV_A_RANK, GROUP).astype(BF16),
            rwkv_k_k[l], rwkv_k_a[l], rwkv_r_k[l], head_ones, vres, ts=_tile(seq, 4 * RWKV_CHUNK), chunk=RWKV_CHUNK)
        if l == 0:
            v_first = v_
        rwkv_out = rwkv_state(mm_, gm_, qm_, y0_, bonus_, g_, rwkv_gn_w[l], rwkv_gn_b[l], chunk=RWKV_CHUNK)

        ret_out = retention(proj3, *ret_tables, ret_gn_w[l], ts=_tile(seq, 4 * RET_CHUNK), chunk=RET_CHUNK)

        kv = norm_matmul(mem.reshape(bsz * mem_len, d), norm_mem[l], wkv_b, l, tm=_tile(bsz * mem_len, 512),
                         tn=1024, out_dtype=BF16).reshape(bsz, mem_len, 2 * d)
        x2 = mix_xattn((fox_out, lru_out, rwkv_out, ret_out), x2.reshape(bsz, seq, d), kv, w_out_b, wq_b, wo_b, l,
                       norm_mix_post[l], norm_xa_pre[l], norm_xa_post[l], tm=_tile(seq, 512)).reshape(n_tok, d)

        x2 = mlp(x2, w1_b, w2_b, l, norm_mlp_pre[l], norm_mlp_post[l], tm=tm, ff_tile=1024)
    return x2.reshape(bsz, seq, d)
```

```python
import functools
import math

import jax
import jax.numpy as jnp
import numpy as np
from jax import lax
from jax.experimental import pallas as pl
from jax.experimental.pallas import tpu as pltpu

F32 = jnp.float32
BF16 = jnp.bfloat16
HIGHEST = lax.Precision.HIGHEST

D_MODEL = 1024
GROUP = 256
HEADS = 4
HEAD_DIM = 64
CONV_WIDTH = 4
LRU_C = 8.0
RET_THETA = 10000.0
RET_CHUNK = 128
RWKV_W_RANK, RWKV_A_RANK, RWKV_G_RANK, RWKV_V_RANK = 64, 64, 128, 32
RWKV_GN_EPS = 64e-5
XATTN_HEADS = 4
XATTN_HEAD_DIM = D_MODEL // XATTN_HEADS
NORM_EPS = 1e-6
NEG_BIG = -1e30

FOX_Q, FOX_K, FOX_V, FOX_F = 0, 256, 512, 768
LRU_X, LRU_Y = 1024, 1280
RWKV_R, RWKV_K, RWKV_V, RWKV_LR = 1536, 1792, 2048, 2304
RET_Q, RET_K, RET_V, RET_G = 2560, 2816, 3072, 3328
IN_PAD = 3584
FOX_REAL = 3 * GROUP + HEADS

VMEM_LIMIT = 56 * 1024 * 1024
RWKV_CHUNK = 64


def _cparams(sem):
    return pltpu.CompilerParams(dimension_semantics=sem, vmem_limit_bytes=VMEM_LIMIT)


def _rms(x, g):
    return x * lax.rsqrt(jnp.mean(x * x, axis=-1, keepdims=True) + NORM_EPS) * g


def _log_sigmoid(x):
    return jnp.minimum(x, 0.0) - jnp.log1p(jnp.exp(-jnp.abs(x)))


def _sigmoid(x):
    return 1.0 / (1.0 + jnp.exp(-x))


def _dot(a, b, **kw):
    return jnp.dot(a, b, preferred_element_type=F32, **kw)


def _dot_nt(a, b, **kw):
    return lax.dot_general(a, b, (((1,), (1,)), ((), ())), preferred_element_type=F32, **kw)


def _dot_tn(a, b, **kw):
    return lax.dot_general(a, b, (((0,), (0,)), ((), ())), preferred_element_type=F32, **kw)


def _split_bf16(x):
    hi = x.astype(BF16)
    return hi, (x - hi.astype(F32)).astype(BF16)


def _mm1(dot, a, b):
    return dot(a.astype(BF16), b.astype(BF16))


def _mm3(dot, a, b):
    ah, al = _split_bf16(a)
    bh, bl = _split_bf16(b)
    return dot(ah, bh) + (dot(ah, bl) + dot(al, bh))


def _bf16_terms(x, n):
    terms = []
    for _ in range(n - 1):
        t = x.astype(BF16)
        terms.append(t)
        x = x - t.astype(F32)
    return terms + [x.astype(BF16)]


def _dot_sel(a, b, *, split, n=3):
    if split == "a":
        parts = [_dot(t, b.astype(BF16)) for t in _bf16_terms(a, n)]
    else:
        parts = [_dot(a.astype(BF16), t) for t in _bf16_terms(b, n)]
    out = parts[-1]
    for p in reversed(parts[:-1]):
        out = out + p
    return out


def _tri(n, strict=False):
    r = lax.broadcasted_iota(jnp.int32, (n, n), 0)
    c = lax.broadcasted_iota(jnp.int32, (n, n), 1)
    return (r > c) if strict else (r >= c)


def _norm_matmul_kernel(x_ref, g_ref, w_ref, o_ref, xn_ref):
    @pl.when(pl.program_id(1) == 0)
    def _():
        xn_ref[...] = _rms(x_ref[...], g_ref[...]).astype(BF16)

    o_ref[...] = _dot(xn_ref[...], w_ref[...]).astype(o_ref.dtype)


def norm_matmul(x, g, w, layer, *, tm, tn, out_dtype):
    n, d = x.shape
    width = w.shape[2]
    return pl.pallas_call(
        _norm_matmul_kernel,
        grid=(n // tm, width // tn),
        in_specs=[pl.BlockSpec((tm, d), lambda i, j: (i, 0)),
                  pl.BlockSpec((1, d), lambda i, j: (0, 0)),
                  pl.BlockSpec((None, d, tn), lambda i, j: (layer, 0, j))],
        out_specs=pl.BlockSpec((tm, tn), lambda i, j: (i, j)),
        out_shape=jax.ShapeDtypeStruct((n, width), out_dtype),
        scratch_shapes=[pltpu.VMEM((tm, d), BF16)],
        compiler_params=_cparams(("parallel", "arbitrary")),
        name="norm_matmul",
    )(x, g.reshape(1, d), w)


def _in_proj_kernel(x_ref, g_ref, w_ref, o_ref, *, tn):
    xn = _rms(x_ref[...], g_ref[...]).astype(BF16)
    for c in range(w_ref.shape[0] // tn):
        o_ref[:, c * tn:(c + 1) * tn] = _dot_nt(xn, w_ref[c * tn:(c + 1) * tn, :])


def in_proj(x, g, w, layer, *, tm, tn):
    n, d = x.shape
    width = w.shape[1]
    return pl.pallas_call(
        functools.partial(_in_proj_kernel, tn=tn),
        grid=(n // tm,),
        in_specs=[pl.BlockSpec((tm, d), lambda i: (i, 0)),
                  pl.BlockSpec((1, d), lambda i: (0, 0)),
                  pl.BlockSpec((None, width, d), lambda i: (layer, 0, 0), pipeline_mode=pl.Buffered(1))],
        out_specs=pl.BlockSpec((tm, width), lambda i: (i, 0)),
        out_shape=jax.ShapeDtypeStruct((n, width), F32),
        compiler_params=_cparams(("parallel",)),
        name="in_proj",
    )(x, g.reshape(1, d), w)


FOX_AUG = 128
FOX_VROWS = HEAD_DIM + 16
LOG2E = 1.4426950408889634


def _fox_prep_kernel(q_ref, k_ref, v_ref, f_ref, b_ref, qa_ref, ka_ref, vt_ref, carry_ref):
    @pl.when(pl.program_id(1) == 0)
    def _():
        carry_ref[...] = jnp.zeros_like(carry_ref)

    lf = _log_sigmoid(f_ref[0] + b_ref[...])
    tc = lf.shape[0]
    cum = _dot_sel(_tri(tc), lf, split="b") + carry_ref[...]
    carry_ref[...] = cum[tc - 1:tc, :]

    lane = lax.broadcasted_iota(jnp.int32, (tc, FOX_AUG), 1)
    feat = lane < HEAD_DIM
    ones = jnp.where((lane >= HEAD_DIM) & (lane < HEAD_DIM + 3), 1.0, 0.0)
    q = q_ref[0] * (HEAD_DIM ** -0.5 * LOG2E)
    k = k_ref[0]
    for h in range(HEADS):
        pair = slice((h // 2) * FOX_AUG, (h // 2 + 1) * FOX_AUG)
        qt, kt = q[:, pair], k[:, pair]
        if h % 2:
            qt, kt = pltpu.roll(qt, HEAD_DIM, axis=1), pltpu.roll(kt, HEAD_DIM, axis=1)
        neg_c = jnp.broadcast_to(cum[:, h:h + 1] * -LOG2E, (tc, FOX_AUG))
        c_hi = neg_c.astype(BF16).astype(F32)
        rest = neg_c - c_hi
        c_mid = rest.astype(BF16).astype(F32)
        bias = jnp.where(lane == HEAD_DIM, c_hi, jnp.where(lane == HEAD_DIM + 1, c_mid,
                                                           jnp.where(lane == HEAD_DIM + 2, rest - c_mid, 0.0)))
        qa_ref[0, h] = jnp.where(feat, qt, ones).astype(BF16)
        ka_ref[0, h] = jnp.where(feat, kt, bias).astype(BF16)
    ones_row = lax.broadcasted_iota(jnp.int32, (HEADS, FOX_VROWS - HEAD_DIM, tc), 1) == 0
    vt = jnp.concatenate([v_ref[0].T.reshape(HEADS, HEAD_DIM, tc), ones_row.astype(F32)], axis=1)
    vt_ref[0, :, 0] = vt.astype(BF16)


def fox_prep(proj3, f_bias_pad, *, tc):
    b, s, _ = proj3.shape
    slab = lambda off: pl.BlockSpec((1, tc, GROUP), lambda i, j: (i, j, off // GROUP))
    aug = pl.BlockSpec((1, HEADS, tc, FOX_AUG), lambda i, j: (i, 0, j, 0))
    return pl.pallas_call(
        _fox_prep_kernel,
        grid=(b, s // tc),
        in_specs=[slab(FOX_Q), slab(FOX_K), slab(FOX_V), slab(FOX_F), pl.BlockSpec((1, GROUP), lambda i, j: (0, 0))],
        out_specs=[aug, aug, pl.BlockSpec((1, HEADS, 1, FOX_VROWS, tc), lambda i, j: (i, 0, j, 0, 0))],
        out_shape=[jax.ShapeDtypeStruct((b, HEADS, s, FOX_AUG), BF16),
                   jax.ShapeDtypeStruct((b, HEADS, s, FOX_AUG), BF16),
                   jax.ShapeDtypeStruct((b, HEADS, s // tc, FOX_VROWS, tc), BF16)],
        scratch_shapes=[pltpu.VMEM((1, GROUP), F32)],
        compiler_params=_cparams(("parallel", "arbitrary")),
        name="fox_prep",
    )(proj3, proj3, proj3, proj3, f_bias_pad)


def _fox_attn_kernel(q_ref, k_ref, v_ref, o_ref, s_ref, *, t):
    i = pl.program_id(1)
    qa = q_ref[0]

    def scores(j):
        return _bnt(k_ref[0, :, pl.ds(pl.multiple_of(j * t, t), t), :], qa)

    def update(carry, slot, j):
        m, acc = carry
        s = s_ref[slot]
        m_new = jnp.maximum(m, jnp.max(s, axis=1, keepdims=True))
        p = jnp.exp2(s - m_new).astype(BF16)
        return m_new, jnp.exp2(m - m_new) * acc + _bnn(v_ref[0, :, j], p)

    key = lax.broadcasted_iota(jnp.int32, (1, t, t), 1)
    qry = lax.broadcasted_iota(jnp.int32, (1, t, t), 2)
    s_ref[0] = jnp.where(key <= qry, scores(i), NEG_BIG)
    carry = (jnp.full((HEADS, 1, t), NEG_BIG, F32), jnp.zeros((HEADS, FOX_VROWS, t), F32))

    def pair(jj, carry):
        s_ref[1] = scores(2 * jj)
        carry = update(carry, 0, jnp.where(jj == 0, i, 2 * jj - 1))
        s_ref[0] = scores(2 * jj + 1)
        return update(carry, 1, 2 * jj)

    carry = lax.fori_loop(0, i // 2, pair, carry)
    pending = jnp.where(i < 2, i, 2 * (i // 2) - 1)

    def odd_tail(carry):
        s_ref[1] = scores(i - 1)
        return update(update(carry, 0, pending), 1, i - 1)

    _, acc = lax.cond(i % 2 == 1, odd_tail, lambda c: update(c, 0, pending), carry)
    out = acc[:, :HEAD_DIM] / acc[:, HEAD_DIM:HEAD_DIM + 1]
    o_ref[0] = out.reshape(GROUP, t).T.astype(o_ref.dtype)


def fox_attention(qa, ka, vt):
    b, h, s, _ = qa.shape
    nk, t = vt.shape[2], vt.shape[4]
    return pl.pallas_call(
        functools.partial(_fox_attn_kernel, t=t),
        grid=(b, nk),
        in_specs=[pl.BlockSpec((1, h, t, FOX_AUG), lambda bi, i: (bi, 0, i, 0)),
                  pl.BlockSpec((1, h, s, FOX_AUG), lambda bi, i: (bi, 0, 0, 0)),
                  pl.BlockSpec((1, h, nk, FOX_VROWS, t), lambda bi, i: (bi, 0, 0, 0, 0))],
        out_specs=pl.BlockSpec((1, t, GROUP), lambda bi, i: (bi, i, 0)),
        out_shape=jax.ShapeDtypeStruct((b, s, GROUP), BF16),
        scratch_shapes=[pltpu.VMEM((2, h, t, t), F32)],
        compiler_params=_cparams(("parallel", "arbitrary")),
        name="fox_attention",
    )(qa, ka, vt)


def _lru_kernel(x_ref, y_ref, cw_ref, cb_ref, wra_ref, bra_ref, wri_ref, bri_ref, lam_ref, o_ref,
                buf_ref, h_ref, *, ts):
    @pl.when(pl.program_id(1) == 0)
    def _():
        buf_ref[0:8, :] = jnp.zeros((8, GROUP), F32)
        h_ref[...] = jnp.zeros_like(h_ref)

    xb = x_ref[0]
    buf_ref[8:8 + ts, :] = xb
    xc = cb_ref[...] + buf_ref[5:5 + ts, :] * cw_ref[0:1, :]
    for j in range(1, CONV_WIDTH):
        xc = xc + buf_ref[5 + j:5 + j + ts, :] * cw_ref[j:j + 1, :]
    buf_ref[0:8, :] = xb[ts - 8:ts, :]

    xcb = xc.astype(BF16)
    r = _sigmoid(_dot(xcb, wra_ref[...]) + bra_ref[...])
    gate_i = _sigmoid(_dot(xcb, wri_ref[...]) + bri_ref[...])
    log_a = LRU_C * r * _log_sigmoid(lam_ref[...])
    a = jnp.exp(log_a)
    z = 2.0 * log_a
    mult = jnp.sqrt(jnp.maximum(-jnp.tanh(0.5 * z) * (jnp.exp(z) + 1.0), 0.0))
    u = mult * (gate_i * xc)

    row = lax.broadcasted_iota(jnp.int32, (ts, GROUP), 0)
    pa, pb = a, u
    d = 1
    while d < ts:
        sa = pltpu.roll(pa, d, axis=0)
        sb = pltpu.roll(pb, d, axis=0)
        valid = row >= d
        pb = jnp.where(valid, pa * sb + pb, pb)
        pa = jnp.where(valid, pa * sa, pa)
        d *= 2
    hseq = pa * h_ref[...] + pb
    h_ref[...] = hseq[ts - 1:ts, :]

    y = y_ref[0]
    gelu = 0.5 * y * (1.0 + jnp.tanh(math.sqrt(2.0 / math.pi) * (y + 0.044715 * (y * y * y))))
    o_ref[0] = (hseq * gelu).astype(o_ref.dtype)


def rglru(proj3, conv_w, conv_b, wra_bd, ra_b, wri_bd, ri_b, lam, *, ts):
    b, s, _ = proj3.shape
    vec = lambda: pl.BlockSpec((1, GROUP), lambda i, j: (0, 0))
    mat = lambda: pl.BlockSpec((GROUP, GROUP), lambda i, j: (0, 0))
    return pl.pallas_call(
        functools.partial(_lru_kernel, ts=ts),
        grid=(b, s // ts),
        in_specs=[pl.BlockSpec((1, ts, GROUP), lambda i, j: (i, j, LRU_X // GROUP)),
                  pl.BlockSpec((1, ts, GROUP), lambda i, j: (i, j, LRU_Y // GROUP)),
                  pl.BlockSpec((CONV_WIDTH, GROUP), lambda i, j: (0, 0)),
                  vec(), mat(), vec(), mat(), vec(), vec()],
        out_specs=pl.BlockSpec((1, ts, GROUP), lambda i, j: (i, j, 0)),
        out_shape=jax.ShapeDtypeStruct((b, s, GROUP), BF16),
        scratch_shapes=[pltpu.VMEM((ts + 8, GROUP), F32), pltpu.VMEM((1, GROUP), F32)],
        compiler_params=_cparams(("parallel", "arbitrary")),
        name="rglru",
    )(proj3, proj3, conv_w, conv_b.reshape(1, GROUP), wra_bd, ra_b.reshape(1, GROUP), wri_bd,
      ri_b.reshape(1, GROUP), lam.reshape(1, GROUP))


def _bdot(dims):
    return lambda a, b: lax.dot_general(a, b, (dims, ((0,), (0,))), preferred_element_type=F32)


_bnn = _bdot(((2,), (1,)))
_bnt = _bdot(((2,), (2,)))
_btn = _bdot(((1,), (1,)))


def _unit_lower_inverse(a_strict, n):
    r = lax.broadcasted_iota(jnp.int32, (1, n, n), 1)
    c = lax.broadcasted_iota(jnp.int32, (1, n, n), 2)
    t = jnp.where(r == c, 1.0, jnp.where((r // 2 == c // 2) & (r > c), a_strict, 0.0))
    m = 2
    while m < n:
        off = (r // (2 * m) == c // (2 * m)) & (r % (2 * m) >= m) & (c % (2 * m) < m)
        tb = t.astype(BF16)
        t = t + _bnn(tb, _bnn(jnp.where(off, a_strict, 0.0).astype(BF16), tb).astype(BF16))
        m *= 2
    return t


def _to_heads(x):
    n, rows, _ = x.shape
    parts = [x[:, :, h * HEAD_DIM:(h + 1) * HEAD_DIM] for h in range(HEADS)]
    return jnp.stack(parts, axis=1).reshape(n * HEADS, rows, HEAD_DIM)


def _from_heads(x):
    nh, rows, _ = x.shape
    x = x.reshape(nh // HEADS, HEADS, rows, HEAD_DIM)
    return jnp.concatenate([x[:, h] for h in range(HEADS)], axis=-1)


def _rwkv_chunk_kernel(*refs, ts, chunk, has_vres):
    (sr_ref, sk_ref, sv_ref, sl_ref, mu_ref, w0_ref, w2_ref, a0_ref, a2_ref, g2_ref, kk_ref, ka_ref, rk_ref,
     ones_ref) = refs[:14]
    refs = refs[14:]
    if has_vres:
        vf_ref, v0_ref, v1_ref, v2_ref = refs[:4]
        refs = refs[4:]
    mm_out, gm_out, qm_out, y0_out, bonus_out, g_out, v_out, carry_ref = refs

    @pl.when(pl.program_id(1) == 0)
    def _():
        carry_ref[...] = jnp.zeros_like(carry_ref)

    row0 = lax.broadcasted_iota(jnp.int32, (ts, GROUP), 0) == 0

    def shift_mix(ref, idx):
        s = ref[0]
        prev = jnp.where(row0, carry_ref[idx:idx + 1, :], pltpu.roll(s, 1, axis=0))
        carry_ref[idx:idx + 1, :] = s[ts - 1:ts, :]
        return s + (prev - s) * mu_ref[idx:idx + 1, :]

    r = shift_mix(sr_ref, 0)
    k = shift_mix(sk_ref, 1)
    v = shift_mix(sv_ref, 2)
    low = shift_mix(sl_ref, 3)

    zw = w0_ref[...] + _dot(jnp.tanh(low).astype(BF16), w2_ref[...])
    lw = -jnp.exp(_log_sigmoid(zw) - 0.5)
    a = _sigmoid(a0_ref[...] + _dot(low.astype(BF16), a2_ref[...]))
    g_out[0] = _dot(_sigmoid(low).astype(BF16), g2_ref[...])
    if has_vres:
        mix = _dot(_dot(v.astype(BF16), v1_ref[...]).astype(BF16), v2_ref[...])
        v = v + (vf_ref[0] - v) * _sigmoid(v0_ref[...] + mix)
    v_out[0] = v
    kk = k * kk_ref[...]
    ss = _dot_sel(kk * kk, ones_ref[...], split="a")
    kk = kk / jnp.maximum(jnp.sqrt(ss), 1e-12)
    k = k * (1.0 + (a - 1.0) * ka_ref[...])
    bonus_out[0] = _dot_sel(r * k * rk_ref[...], ones_ref[...], split="a") * v
    a_vec, b_vec = -kk, kk * a

    c = chunk
    rr = lax.broadcasted_iota(jnp.int32, (ts, ts), 0)
    cc = lax.broadcasted_iota(jnp.int32, (ts, ts), 1)
    chunk_tri = ((rr >= cc) & (rr // c == cc // c)).astype(F32)
    cum_all = _dot_sel(chunk_tri, lw, split="b")
    strict = _tri(c, strict=True)[None]
    incl = _tri(c)[None]
    eye = (lax.broadcasted_iota(jnp.int32, (1, HEAD_DIM, HEAD_DIM), 1)
           == lax.broadcasted_iota(jnp.int32, (1, HEAD_DIM, HEAD_DIM), 2)).astype(F32)

    nq = ts // c
    per_chunk = lambda t: t.reshape(nq, c, GROUP)
    cum = per_chunk(cum_all)
    rq, kq, vq, aq, bq = (per_chunk(t) for t in (r, k, v, a_vec, b_vec))
    cum_ex = cum - per_chunk(lw)
    mid = cum[:, c // 2 - 1:c // 2, :]
    tot = cum[:, c - 1:c, :]
    e_fwd = jnp.exp(cum - mid)
    e_bwd = jnp.exp(mid - cum)
    e_end = jnp.exp(tot - cum)
    mxu = lambda t: _to_heads(t.astype(BF16))
    r_rel, k_rel = mxu(rq * e_fwd), mxu(kq * e_bwd)
    a_rel, b_rel = mxu(aq * jnp.exp(cum_ex - mid)), mxu(bq * e_bwd)
    a_abs, r_abs = mxu(aq * jnp.exp(cum_ex)), rq * jnp.exp(cum)
    k_end, b_end = mxu(kq * e_end), mxu(bq * e_end)
    gam = _to_heads(jnp.exp(tot))
    vh = mxu(vq)

    ar_rel = jnp.concatenate([a_rel, r_rel], axis=1)
    s_b, s_k = _bnt(ar_rel, b_rel), _bnt(ar_rel, k_rel)
    a_ab = jnp.where(strict, s_b[:, :c], 0.0)
    a_ak = jnp.where(strict, s_k[:, :c], 0.0).astype(BF16)
    a_rb = jnp.where(incl, s_b[:, c:], 0.0).astype(BF16)
    a_rk = jnp.where(incl, s_k[:, c:], 0.0).astype(BF16)
    t_inv = _unit_lower_inverse(a_ab, c).astype(BF16)
    pu = _bnn(t_inv, jnp.concatenate([a_abs, _bnn(a_ak, vh).astype(BF16)], axis=-1)).astype(BF16)
    z = _bnn(a_rb, pu)
    y0 = z[..., HEAD_DIM:] + _bnn(a_rk, vh)
    xtb = _btn(pu, b_end)
    mm_mat = eye * gam + xtb[:, :HEAD_DIM]
    gm = xtb[:, HEAD_DIM:] + _btn(vh, k_end)
    dense = lambda t: _from_heads(t).reshape(ts, GROUP)
    mm_out[0] = dense(mm_mat).astype(mm_out.dtype)
    gm_out[0] = dense(gm)
    qm_out[0] = (dense(z[..., :HEAD_DIM]) + r_abs.reshape(ts, GROUP)).astype(qm_out.dtype)
    y0_out[0] = dense(y0)


def rwkv_chunk(proj3, mu4, w0, w2p, a0, a2p, g2p, k_k, k_a, r_k, head_ones, vres, *, ts, chunk):
    b, s, _ = proj3.shape
    slab = lambda off: pl.BlockSpec((1, ts, GROUP), lambda i, j: (i, j, off // GROUP))
    vec = lambda: pl.BlockSpec((1, GROUP), lambda i, j: (0, 0))
    full = lambda shape: pl.BlockSpec(shape, lambda i, j: tuple(0 for _ in shape))
    tok = pl.BlockSpec((1, ts, GROUP), lambda i, j: (i, j, 0))
    in_specs = [slab(RWKV_R), slab(RWKV_K), slab(RWKV_V), slab(RWKV_LR), full((4, GROUP)),
                vec(), full((GROUP, GROUP)), vec(), full((GROUP, GROUP)), full((GROUP, GROUP)), vec(), vec(), vec(),
                full((GROUP, GROUP))]
    args = [proj3, proj3, proj3, proj3, mu4, w0.reshape(1, GROUP), w2p, a0.reshape(1, GROUP), a2p, g2p,
            k_k.reshape(1, GROUP), k_a.reshape(1, GROUP), r_k.reshape(1, GROUP), head_ones]
    if vres is not None:
        v_first, v0, v1p, v2p = vres
        in_specs += [tok, vec(), full((GROUP, 128)), full((128, GROUP))]
        args += [v_first, v0.reshape(1, GROUP), v1p, v2p]
    return pl.pallas_call(
        functools.partial(_rwkv_chunk_kernel, ts=ts, chunk=chunk, has_vres=vres is not None),
        grid=(b, s // ts),
        in_specs=in_specs,
        out_specs=[tok] * 7,
        out_shape=[jax.ShapeDtypeStruct((b, s, GROUP), dt) for dt in (BF16, F32, BF16, F32, F32, F32, F32)],
        scratch_shapes=[pltpu.VMEM((4, GROUP), F32)],
        compiler_params=_cparams(("parallel", "arbitrary")),
        name="rwkv_chunk",
    )(*args)


def _rwkv_state_kernel(mm_ref, gm_ref, qm_ref, y0_ref, bonus_ref, g_ref, gw_ref, gb_ref, o_ref, state_ref):
    @pl.when(pl.program_id(0) == 0)
    def _():
        state_ref[...] = jnp.zeros_like(state_ref)

    s_hi, s_lo = _split_bf16(state_ref[...])
    qm, mm = _to_heads(qm_ref[...]), _to_heads(mm_ref[...])
    y = _bnt(qm, s_hi) + _bnt(qm, s_lo) + _to_heads(y0_ref[...])
    state_ref[...] = _bnn(s_hi, mm) + _bnn(s_lo, mm) + _to_heads(gm_ref[...])
    mu = jnp.mean(y, axis=-1, keepdims=True)
    var = jnp.mean(jnp.square(y - mu), axis=-1, keepdims=True)
    yn = _from_heads((y - mu) * lax.rsqrt(var + RWKV_GN_EPS)) * gw_ref[...] + gb_ref[...]
    o_ref[...] = ((yn + bonus_ref[...]) * g_ref[...]).astype(o_ref.dtype)


def rwkv_state(mm, gm, qm, y0, bonus, g, gn_w, gn_b, *, chunk):
    bsz, s, _ = mm.shape
    tok = pl.BlockSpec((bsz, chunk, GROUP), lambda j: (0, j, 0))
    vec = pl.BlockSpec((1, GROUP), lambda j: (0, 0))
    return pl.pallas_call(
        _rwkv_state_kernel,
        grid=(s // chunk,),
        in_specs=[tok] * 6 + [vec] * 2,
        out_specs=tok,
        out_shape=jax.ShapeDtypeStruct((bsz, s, GROUP), BF16),
        scratch_shapes=[pltpu.VMEM((bsz * HEADS, HEAD_DIM, HEAD_DIM), F32)],
        compiler_params=_cparams(("arbitrary",)),
        name="rwkv_state",
    )(mm, gm, qm, y0, bonus, g, gn_w.reshape(1, GROUP), gn_b.reshape(1, GROUP))


def _ret_kernel(q_ref, k_ref, v_ref, g_ref, cos_ref, sin_ref, dmat_ref, xi_ref, zeta_ref, cd_ref, gw_ref, o_ref,
                state_ref, *, ts, chunk):
    @pl.when(pl.program_id(1) == 0)
    def _():
        state_ref[...] = jnp.zeros_like(state_ref)

    lane = lax.broadcasted_iota(jnp.int32, (ts, GROUP), 1)
    first_half = (lane % HEAD_DIM) < (HEAD_DIM // 2)
    cos, sin = cos_ref[...], sin_ref[...]

    def rotary(t):
        partner = jnp.where(first_half, pltpu.roll(t, GROUP - HEAD_DIM // 2, axis=1),
                            pltpu.roll(t, HEAD_DIM // 2, axis=1))
        return t * cos + partner * sin

    nq = ts // chunk
    per_chunk = lambda t: t.reshape(nq, chunk, GROUP)
    mxu = lambda t: _to_heads(t).astype(BF16)
    q = per_chunk(rotary(q_ref[0]))
    k = per_chunk(rotary(k_ref[0]) * (HEAD_DIM ** -0.5))
    qb, kb, vb = mxu(q), mxu(k), mxu(per_chunk(v_ref[0]))
    q_cross, k_decay = mxu(q * xi_ref[...]), mxu(k * zeta_ref[...])
    inner = _bnt(qb, kb).reshape(nq, HEADS, chunk, chunk) * dmat_ref[...]
    intra = _bnn(inner.reshape(nq * HEADS, chunk, chunk).astype(BF16), vb)
    kv = _btn(k_decay, vb)

    decay = _to_heads(cd_ref[...][None])
    state = state_ref[...]
    incoming = []
    for c in range(nq):
        incoming.append(state)
        state = state * decay + kv[c * HEADS:(c + 1) * HEADS]
    state_ref[...] = state
    o = intra + _bnn(q_cross, jnp.concatenate(incoming, axis=0).astype(BF16))
    o = _from_heads(o * lax.rsqrt(jnp.mean(o * o, axis=-1, keepdims=True) + NORM_EPS)).reshape(ts, GROUP)
    g = g_ref[0]
    o_ref[0] = (o * gw_ref[...] * (g * _sigmoid(g))).astype(o_ref.dtype)


def retention(proj3, cos_t, sin_t, dmat, xi, zeta, cd, gn_w, *, ts, chunk):
    b, s, _ = proj3.shape
    slab = lambda off: pl.BlockSpec((1, ts, GROUP), lambda i, j: (i, j, off // GROUP))
    full = lambda shape: pl.BlockSpec(shape, lambda i, j: tuple(0 for _ in shape))
    return pl.pallas_call(
        functools.partial(_ret_kernel, ts=ts, chunk=chunk),
        grid=(b, s // ts),
        in_specs=[slab(RET_Q), slab(RET_K), slab(RET_V), slab(RET_G),
                  pl.BlockSpec((ts, GROUP), lambda i, j: (j, 0)),
                  pl.BlockSpec((ts, GROUP), lambda i, j: (j, 0)),
                  full((HEADS, chunk, chunk)), full((chunk, GROUP)), full((chunk, GROUP)),
                  full((1, GROUP)), full((1, GROUP))],
        out_specs=pl.BlockSpec((1, ts, GROUP), lambda i, j: (i, j, 0)),
        out_shape=jax.ShapeDtypeStruct((b, s, GROUP), BF16),
        scratch_shapes=[pltpu.VMEM((HEADS, HEAD_DIM, HEAD_DIM), F32)],
        compiler_params=_cparams(("parallel", "arbitrary")),
        name="retention",
    )(proj3, proj3, proj3, proj3, cos_t, sin_t, dmat, xi, zeta, cd, gn_w.reshape(1, GROUP))


def _retention_tables(s, chunk):
    half = HEAD_DIM // 2
    inv = 1.0 / (RET_THETA ** jnp.linspace(0.0, 1.0, half, dtype=F32))
    ang = jnp.arange(s, dtype=F32)[:, None] * inv[None, :]
    cos, sin = jnp.cos(ang), jnp.sin(ang)
    cos_t = jnp.tile(jnp.concatenate([cos, cos], axis=-1), (1, HEADS))
    sin_t = jnp.tile(jnp.concatenate([-sin, sin], axis=-1), (1, HEADS))
    lg = jnp.log(1.0 - 2.0 ** (-5.0 - jnp.arange(HEADS, dtype=F32)))
    n = jnp.arange(chunk, dtype=F32)
    diff = n[:, None] - n[None, :]
    dmat = jnp.where(diff >= 0, jnp.exp(lg[:, None, None] * jnp.maximum(diff, 0.0)), 0.0)
    zeta = jnp.exp(lg[:, None] * (chunk - 1.0 - n)[None, :])
    xi = jnp.exp(lg[:, None] * (n + 1.0)[None, :])
    per_lane = lambda t: jnp.repeat(t.T, HEAD_DIM, axis=1)
    cd = jnp.repeat(jnp.exp(lg * chunk), HEAD_DIM)[None, :]
    return cos_t, sin_t, dmat, per_lane(xi), per_lane(zeta), cd


def _mix_xattn_kernel(m0_ref, m1_ref, m2_ref, m3_ref, x_ref, kv_ref, wout_ref, wq_ref, wo_ref, gmix_ref, gpre_ref,
                      gpost_ref, o_ref):
    acc = _dot(m0_ref[0], wout_ref[0:GROUP, :])
    for idx, m_ref in enumerate((m1_ref, m2_ref, m3_ref), start=1):
        acc = acc + _dot(m_ref[0], wout_ref[idx * GROUP:(idx + 1) * GROUP, :])
    x = x_ref[0] + _rms(acc, gmix_ref[...])
    xn = _rms(x, gpre_ref[...]).astype(BF16)
    q = (_dot(xn, wq_ref[...]) * (XATTN_HEAD_DIM ** -0.5)).astype(BF16)
    outs = []
    for h in range(XATTN_HEADS):
        sl = slice(h * XATTN_HEAD_DIM, (h + 1) * XATTN_HEAD_DIM)
        s = _dot_nt(q[:, sl], kv_ref[0, :, sl])
        e = jnp.exp(s - jnp.max(s, axis=-1, keepdims=True))
        p = e / jnp.sum(e, axis=-1, keepdims=True)
        outs.append(_dot(p.astype(BF16), kv_ref[0, :, D_MODEL + h * XATTN_HEAD_DIM:D_MODEL + (h + 1) * XATTN_HEAD_DIM]))
    o = jnp.concatenate(outs, axis=-1).astype(BF16)
    o_ref[0] = x + _rms(_dot(o, wo_ref[...]), gpost_ref[...])


def mix_xattn(mixed, x3, kv, w_out, wq, wo, layer, g_mix, g_pre, g_post, *, tm):
    b, s, d = x3.shape
    m = kv.shape[1]
    tok = lambda width: pl.BlockSpec((1, tm, width), lambda bi, i: (bi, i, 0))
    weight = lambda: pl.BlockSpec((None, d, d), lambda bi, i: (layer, 0, 0))
    vec = lambda: pl.BlockSpec((1, d), lambda bi, i: (0, 0))
    return pl.pallas_call(
        _mix_xattn_kernel,
        grid=(b, s // tm),
        in_specs=[tok(GROUP)] * 4 + [tok(d), pl.BlockSpec((1, m, 2 * d), lambda bi, i: (bi, 0, 0)),
                                     weight(), weight(), weight(), vec(), vec(), vec()],
        out_specs=tok(d),
        out_shape=jax.ShapeDtypeStruct((b, s, d), F32),
        compiler_params=_cparams(("parallel", "arbitrary")),
        name="mix_xattn",
    )(*mixed, x3, kv, w_out, wq, wo, g_mix.reshape(1, d), g_pre.reshape(1, d), g_post.reshape(1, d))


def _mlp_kernel(x_ref, w1_ref, w2_ref, gpre_ref, gpost_ref, o_ref, *, ff_tile):
    x = x_ref[...]
    xn = _rms(x, gpre_ref[...]).astype(BF16)
    d_ff = w1_ref.shape[1]
    acc = None
    for c in range(d_ff // ff_tile):
        hid = jnp.square(jnp.maximum(_dot(xn, w1_ref[:, c * ff_tile:(c + 1) * ff_tile]), 0.0)).astype(BF16)
        part = _dot(hid, w2_ref[c * ff_tile:(c + 1) * ff_tile, :])
        acc = part if acc is None else acc + part
    o_ref[...] = x + _rms(acc, gpost_ref[...])


def mlp(x, w1, w2, layer, g_pre, g_post, *, tm, ff_tile):
    n, d = x.shape
    d_ff = w1.shape[2]
    return pl.pallas_call(
        functools.partial(_mlp_kernel, ff_tile=ff_tile),
        grid=(n // tm,),
        in_specs=[pl.BlockSpec((tm, d), lambda i: (i, 0)),
                  pl.BlockSpec((None, d, d_ff), lambda i: (layer, 0, 0), pipeline_mode=pl.Buffered(1)),
                  pl.BlockSpec((None, d_ff, d), lambda i: (layer, 0, 0), pipeline_mode=pl.Buffered(1)),
                  pl.BlockSpec((1, d), lambda i: (0, 0)),
                  pl.BlockSpec((1, d), lambda i: (0, 0))],
        out_specs=pl.BlockSpec((tm, d), lambda i: (i, 0)),
        out_shape=jax.ShapeDtypeStruct((n, d), F32),
        compiler_params=_cparams(("parallel",)),
        name="mlp",
    )(x, w1, w2, g_pre.reshape(1, d), g_post.reshape(1, d))


def _block_diag(w):
    h, n, _ = w.shape
    eye = jnp.eye(h, dtype=w.dtype)
    return (eye[:, None, :, None] * w[:, :, None, :]).reshape(h * n, h * n)


def _pad_rows(w, start, total):
    return jnp.zeros((total, w.shape[1]), w.dtype).at[start:start + w.shape[0]].set(w)


def _tile(n, pref):
    return pref if n % pref == 0 else n


def kernel(x, mem, norm_mix_pre, norm_mix_post, norm_xa_pre, norm_xa_post, norm_mem, norm_mlp_pre, norm_mlp_post, w_in, w_out, fox_f_bias, lru_conv_w, lru_conv_b, lru_ra_w, lru_ra_b, lru_ri_w, lru_ri_b, lru_lambda, rwkv_mu, rwkv_w0, rwkv_w2, rwkv_a0, rwkv_a2, rwkv_g2, rwkv_k_k, rwkv_k_a, rwkv_r_k, rwkv_gn_w, rwkv_gn_b, rwkv_v0, rwkv_v1, rwkv_v2, ret_gn_w, xa_wq, xa_wk, xa_wv, xa_wo, mlp_w1, mlp_w2):
    bsz, seq, d = x.shape
    depth = w_in.shape[0]
    n_tok = bsz * seq
    mem_len = mem.shape[1]
    tm = _tile(n_tok, 512)
    tq = _tile(seq, 512)
    ts = _tile(seq, 256)
    ret_tables = _retention_tables(seq, RET_CHUNK)
    head_ones = _block_diag(jnp.ones((HEADS, HEAD_DIM, HEAD_DIM), BF16))

    w_in_t = w_in.swapaxes(1, 2)
    w_in_b = jnp.concatenate([w_in_t[:, :FOX_REAL], jnp.zeros((depth, LRU_X - FOX_REAL, d), w_in.dtype),
                              w_in_t[:, FOX_REAL:]], axis=1).astype(BF16)
    w_out_b = w_out.astype(BF16)
    wq_b, wo_b = xa_wq.astype(BF16), xa_wo.astype(BF16)
    wkv_b = jnp.concatenate([xa_wk, xa_wv], axis=-1).astype(BF16)
    w1_b, w2_b = mlp_w1.astype(BF16), mlp_w2.astype(BF16)

    x2 = x.reshape(n_tok, d)
    v_first = None
    for l in range(depth):
        proj = in_proj(x2, norm_mix_pre[l], w_in_b, l, tm=tm, tn=IN_PAD // 7)
        proj3 = proj.reshape(bsz, seq, IN_PAD)

        f_bias = jnp.zeros((1, GROUP), F32).at[0, :HEADS].set(fox_f_bias[l])
        qa, ka, vt = fox_prep(proj3, f_bias, tc=tq)
        fox_out = fox_attention(qa, ka, vt)

        lru_out = rglru(proj3, lru_conv_w[l], lru_conv_b[l], _block_diag(lru_ra_w[l]).astype(BF16), lru_ra_b[l],
                        _block_diag(lru_ri_w[l]).astype(BF16), lru_ri_b[l], lru_lambda[l], ts=ts)

        vres = None
        if l > 0:
            vres = (v_first, rwkv_v0[l - 1],
                    jnp.pad(rwkv_v1[l - 1], ((0, 0), (0, 128 - RWKV_V_RANK))).astype(BF16),
                    _pad_rows(rwkv_v2[l - 1], 0, 128).astype(BF16))
        mm_, gm_, qm_, y0_, bonus_, g_, v_ = rwkv_chunk(
            proj3, rwkv_mu[l].reshape(4, GROUP), rwkv_w0[l],
            _pad_rows(rwkv_w2[l], 0, GROUP).astype(BF16), rwkv_a0[l],
            _pad_rows(rwkv_a2[l], RWKV_W_RANK, GROUP).astype(BF16),
            _pad_rows(rwkv_g2[l], RWKV_W_RANK + RWKV_A_RANK, GROUP).astype(BF16),
            rwkv_k_k[l], rwkv_k_a[l], rwkv_r_k[l], head_ones, vres, ts=_tile(seq, 4 * RWKV_CHUNK), chunk=RWKV_CHUNK)
        if l == 0:
            v_first = v_
        rwkv_out = rwkv_state(mm_, gm_, qm_, y0_, bonus_, g_, rwkv_gn_w[l], rwkv_gn_b[l], chunk=RWKV_CHUNK)

        ret_out = retention(proj3, *ret_tables, ret_gn_w[l], ts=_tile(seq, 4 * RET_CHUNK), chunk=RET_CHUNK)

        kv = norm_matmul(mem.reshape(bsz * mem_len, d), norm_mem[l], wkv_b, l, tm=_tile(bsz * mem_len, 512),
                         tn=1024, out_dtype=BF16).reshape(bsz, mem_len, 2 * d)
        x2 = mix_xattn((fox_out, lru_out, rwkv_out, ret_out), x2.reshape(bsz, seq, d), kv, w_out_b, wq_b, wo_b, l,
                       norm_mix_post[l], norm_xa_pre[l], norm_xa_post[l], tm=_tile(seq, 512)).reshape(n_tok, d)

        x2 = mlp(x2, w1_b, w2_b, l, norm_mlp_pre[l], norm_mlp_post[l], tm=tm, ff_tile=1024)
    return x2.reshape(bsz, seq, d)
```

```python
import functools
import math

import jax
import jax.numpy as jnp
import numpy as np
from jax import lax
from jax.experimental import pallas as pl
from jax.experimental.pallas import tpu as pltpu

F32 = jnp.float32
BF16 = jnp.bfloat16
HIGHEST = lax.Precision.HIGHEST

D_MODEL = 1024
GROUP = 256
HEADS = 4
HEAD_DIM = 64
CONV_WIDTH = 4
LRU_C = 8.0
RET_THETA = 10000.0
RET_CHUNK = 128
RWKV_W_RANK, RWKV_A_RANK, RWKV_G_RANK, RWKV_V_RANK = 64, 64, 128, 32
RWKV_GN_EPS = 64e-5
XATTN_HEADS = 4
XATTN_HEAD_DIM = D_MODEL // XATTN_HEADS
NORM_EPS = 1e-6
NEG_BIG = -1e30

FOX_Q, FOX_K, FOX_V, FOX_F = 0, 256, 512, 768
LRU_X, LRU_Y = 1024, 1280
RWKV_R, RWKV_K, RWKV_V, RWKV_LR = 1536, 1792, 2048, 2304
RET_Q, RET_K, RET_V, RET_G = 2560, 2816, 3072, 3328
IN_PAD = 3584
FOX_REAL = 3 * GROUP + HEADS

VMEM_LIMIT = 56 * 1024 * 1024
RWKV_CHUNK = 64


def _cparams(sem):
    return pltpu.CompilerParams(dimension_semantics=sem, vmem_limit_bytes=VMEM_LIMIT)


def _rms(x, g):
    return x * lax.rsqrt(jnp.mean(x * x, axis=-1, keepdims=True) + NORM_EPS) * g


def _log_sigmoid(x):
    return jnp.minimum(x, 0.0) - jnp.log1p(jnp.exp(-jnp.abs(x)))


def _sigmoid(x):
    return 1.0 / (1.0 + jnp.exp(-x))


def _dot(a, b, **kw):
    return jnp.dot(a, b, preferred_element_type=F32, **kw)


def _dot_nt(a, b, **kw):
    return lax.dot_general(a, b, (((1,), (1,)), ((), ())), preferred_element_type=F32, **kw)


def _dot_tn(a, b, **kw):
    return lax.dot_general(a, b, (((0,), (0,)), ((), ())), preferred_element_type=F32, **kw)


def _split_bf16(x):
    hi = x.astype(BF16)
    return hi, (x - hi.astype(F32)).astype(BF16)


def _mm1(dot, a, b):
    return dot(a.astype(BF16), b.astype(BF16))


def _mm3(dot, a, b):
    ah, al = _split_bf16(a)
    bh, bl = _split_bf16(b)
    return dot(ah, bh) + (dot(ah, bl) + dot(al, bh))


def _bf16_terms(x, n):
    terms = []
    for _ in range(n - 1):
        t = x.astype(BF16)
        terms.append(t)
        x = x - t.astype(F32)
    return terms + [x.astype(BF16)]


def _dot_sel(a, b, *, split, n=3):
    if split == "a":
        parts = [_dot(t, b.astype(BF16)) for t in _bf16_terms(a, n)]
    else:
        parts = [_dot(a.astype(BF16), t) for t in _bf16_terms(b, n)]
    out = parts[-1]
    for p in reversed(parts[:-1]):
        out = out + p
    return out


def _tri(n, strict=False):
    r = lax.broadcasted_iota(jnp.int32, (n, n), 0)
    c = lax.broadcasted_iota(jnp.int32, (n, n), 1)
    return (r > c) if strict else (r >= c)


def _norm_matmul_kernel(x_ref, g_ref, w_ref, o_ref, xn_ref):
    @pl.when(pl.program_id(1) == 0)
    def _():
        xn_ref[...] = _rms(x_ref[...], g_ref[...]).astype(BF16)

    o_ref[...] = _dot(xn_ref[...], w_ref[...]).astype(o_ref.dtype)


def norm_matmul(x, g, w, layer, *, tm, tn, out_dtype):
    n, d = x.shape
    width = w.shape[2]
    return pl.pallas_call(
        _norm_matmul_kernel,
        grid=(n // tm, width // tn),
        in_specs=[pl.BlockSpec((tm, d), lambda i, j: (i, 0)),
                  pl.BlockSpec((1, d), lambda i, j: (0, 0)),
                  pl.BlockSpec((None, d, tn), lambda i, j: (layer, 0, j))],
        out_specs=pl.BlockSpec((tm, tn), lambda i, j: (i, j)),
        out_shape=jax.ShapeDtypeStruct((n, width), out_dtype),
        scratch_shapes=[pltpu.VMEM((tm, d), BF16)],
        compiler_params=_cparams(("parallel", "arbitrary")),
        name="norm_matmul",
    )(x, g.reshape(1, d), w)


def _in_proj_kernel(x_ref, g_ref, w_ref, o_ref, *, tn):
    xn = _rms(x_ref[...], g_ref[...]).astype(BF16)
    for c in range(w_ref.shape[0] // tn):
        o_ref[:, c * tn:(c + 1) * tn] = _dot_nt(xn, w_ref[c * tn:(c + 1) * tn, :])


def in_proj(x, g, w, layer, *, tm, tn):
    n, d = x.shape
    width = w.shape[1]
    return pl.pallas_call(
        functools.partial(_in_proj_kernel, tn=tn),
        grid=(n // tm,),
        in_specs=[pl.BlockSpec((tm, d), lambda i: (i, 0)),
                  pl.BlockSpec((1, d), lambda i: (0, 0)),
                  pl.BlockSpec((None, width, d), lambda i: (layer, 0, 0), pipeline_mode=pl.Buffered(1))],
        out_specs=pl.BlockSpec((tm, width), lambda i: (i, 0)),
        out_shape=jax.ShapeDtypeStruct((n, width), F32),
        compiler_params=_cparams(("parallel",)),
        name="in_proj",
    )(x, g.reshape(1, d), w)


FOX_AUG = 128
FOX_VROWS = HEAD_DIM + 16
LOG2E = 1.4426950408889634


def _fox_prep_kernel(q_ref, k_ref, v_ref, f_ref, b_ref, qa_ref, ka_ref, vt_ref, carry_ref):
    @pl.when(pl.program_id(1) == 0)
    def _():
        carry_ref[...] = jnp.zeros_like(carry_ref)

    lf = _log_sigmoid(f_ref[0] + b_ref[...])
    tc = lf.shape[0]
    cum = _dot_sel(_tri(tc), lf, split="b") + carry_ref[...]
    carry_ref[...] = cum[tc - 1:tc, :]

    lane = lax.broadcasted_iota(jnp.int32, (tc, FOX_AUG), 1)
    feat = lane < HEAD_DIM
    ones = jnp.where((lane >= HEAD_DIM) & (lane < HEAD_DIM + 3), 1.0, 0.0)
    q = q_ref[0] * (HEAD_DIM ** -0.5 * LOG2E)
    k = k_ref[0]
    for h in range(HEADS):
        pair = slice((h // 2) * FOX_AUG, (h // 2 + 1) * FOX_AUG)
        qt, kt = q[:, pair], k[:, pair]
        if h % 2:
            qt, kt = pltpu.roll(qt, HEAD_DIM, axis=1), pltpu.roll(kt, HEAD_DIM, axis=1)
        neg_c = jnp.broadcast_to(cum[:, h:h + 1] * -LOG2E, (tc, FOX_AUG))
        c_hi = neg_c.astype(BF16).astype(F32)
        rest = neg_c - c_hi
        c_mid = rest.astype(BF16).astype(F32)
        bias = jnp.where(lane == HEAD_DIM, c_hi, jnp.where(lane == HEAD_DIM + 1, c_mid,
                                                           jnp.where(lane == HEAD_DIM + 2, rest - c_mid, 0.0)))
        qa_ref[0, h] = jnp.where(feat, qt, ones).astype(BF16)
        ka_ref[0, h] = jnp.where(feat, kt, bias).astype(BF16)
    ones_row = lax.broadcasted_iota(jnp.int32, (HEADS, FOX_VROWS - HEAD_DIM, tc), 1) == 0
    vt = jnp.concatenate([v_ref[0].T.reshape(HEADS, HEAD_DIM, tc), ones_row.astype(F32)], axis=1)
    vt_ref[0, :, 0] = vt.astype(BF16)


def fox_prep(proj3, f_bias_pad, *, tc):
    b, s, _ = proj3.shape
    slab = lambda off: pl.BlockSpec((1, tc, GROUP), lambda i, j: (i, j, off // GROUP))
    aug = pl.BlockSpec((1, HEADS, tc, FOX_AUG), lambda i, j: (i, 0, j, 0))
    return pl.pallas_call(
        _fox_prep_kernel,
        grid=(b, s // tc),
        in_specs=[slab(FOX_Q), slab(FOX_K), slab(FOX_V), slab(FOX_F), pl.BlockSpec((1, GROUP), lambda i, j: (0, 0))],
        out_specs=[aug, aug, pl.BlockSpec((1, HEADS, 1, FOX_VROWS, tc), lambda i, j: (i, 0, j, 0, 0))],
        out_shape=[jax.ShapeDtypeStruct((b, HEADS, s, FOX_AUG), BF16),
                   jax.ShapeDtypeStruct((b, HEADS, s, FOX_AUG), BF16),
                   jax.ShapeDtypeStruct((b, HEADS, s // tc, FOX_VROWS, tc), BF16)],
        scratch_shapes=[pltpu.VMEM((1, GROUP), F32)],
        compiler_params=_cparams(("parallel", "arbitrary")),
        name="fox_prep",
    )(proj3, proj3, proj3, proj3, f_bias_pad)


def _fox_attn_kernel(q_ref, k_ref, v_ref, o_ref, s_ref, *, t):
    i = pl.program_id(1)
    qa = q_ref[0]

    def scores(j):
        return _bnt(k_ref[0, :, pl.ds(pl.multiple_of(j * t, t), t), :], qa)

    def update(carry, slot, j):
        m, acc = carry
        s = s_ref[slot]
        m_new = jnp.maximum(m, jnp.max(s, axis=1, keepdims=True))
        p = jnp.exp2(s - m_new).astype(BF16)
        return m_new, jnp.exp2(m - m_new) * acc + _bnn(v_ref[0, :, j], p)

    key = lax.broadcasted_iota(jnp.int32, (1, t, t), 1)
    qry = lax.broadcasted_iota(jnp.int32, (1, t, t), 2)
    s_ref[0] = jnp.where(key <= qry, scores(i), NEG_BIG)
    carry = (jnp.full((HEADS, 1, t), NEG_BIG, F32), jnp.zeros((HEADS, FOX_VROWS, t), F32))

    def pair(jj, carry):
        s_ref[1] = scores(2 * jj)
        carry = update(carry, 0, jnp.where(jj == 0, i, 2 * jj - 1))
        s_ref[0] = scores(2 * jj + 1)
        return update(carry, 1, 2 * jj)

    carry = lax.fori_loop(0, i // 2, pair, carry)
    pending = jnp.where(i < 2, i, 2 * (i // 2) - 1)

    def odd_tail(carry):
        s_ref[1] = scores(i - 1)
        return update(update(carry, 0, pending), 1, i - 1)

    _, acc = lax.cond(i % 2 == 1, odd_tail, lambda c: update(c, 0, pending), carry)
    out = acc[:, :HEAD_DIM] / acc[:, HEAD_DIM:HEAD_DIM + 1]
    o_ref[0] = out.reshape(GROUP, t).T.astype(o_ref.dtype)


def fox_attention(qa, ka, vt):
    b, h, s, _ = qa.shape
    nk, t = vt.shape[2], vt.shape[4]
    return pl.pallas_call(
        functools.partial(_fox_attn_kernel, t=t),
        grid=(b, nk),
        in_specs=[pl.BlockSpec((1, h, t, FOX_AUG), lambda bi, i: (bi, 0, i, 0)),
                  pl.BlockSpec((1, h, s, FOX_AUG), lambda bi, i: (bi, 0, 0, 0)),
                  pl.BlockSpec((1, h, nk, FOX_VROWS, t), lambda bi, i: (bi, 0, 0, 0, 0))],
        out_specs=pl.BlockSpec((1, t, GROUP), lambda bi, i: (bi, i, 0)),
        out_shape=jax.ShapeDtypeStruct((b, s, GROUP), BF16),
        scratch_shapes=[pltpu.VMEM((2, h, t, t), F32)],
        compiler_params=_cparams(("parallel", "arbitrary")),
        name="fox_attention",
    )(qa, ka, vt)


def _lru_kernel(x_ref, y_ref, cw_ref, cb_ref, wra_ref, bra_ref, wri_ref, bri_ref, lam_ref, o_ref,
                buf_ref, h_ref, *, ts):
    @pl.when(pl.program_id(1) == 0)
    def _():
        buf_ref[0:8, :] = jnp.zeros((8, GROUP), F32)
        h_ref[...] = jnp.zeros_like(h_ref)

    xb = x_ref[0]
    buf_ref[8:8 + ts, :] = xb
    xc = cb_ref[...] + buf_ref[5:5 + ts, :] * cw_ref[0:1, :]
    for j in range(1, CONV_WIDTH):
        xc = xc + buf_ref[5 + j:5 + j + ts, :] * cw_ref[j:j + 1, :]
    buf_ref[0:8, :] = xb[ts - 8:ts, :]

    xcb = xc.astype(BF16)
    r = _sigmoid(_dot(xcb, wra_ref[...]) + bra_ref[...])
    gate_i = _sigmoid(_dot(xcb, wri_ref[...]) + bri_ref[...])
    log_a = LRU_C * r * _log_sigmoid(lam_ref[...])
    a = jnp.exp(log_a)
    z = 2.0 * log_a
    mult = jnp.sqrt(jnp.maximum(-jnp.tanh(0.5 * z) * (jnp.exp(z) + 1.0), 0.0))
    u = mult * (gate_i * xc)

    row = lax.broadcasted_iota(jnp.int32, (ts, GROUP), 0)
    pa, pb = a, u
    d = 1
    while d < ts:
        sa = pltpu.roll(pa, d, axis=0)
        sb = pltpu.roll(pb, d, axis=0)
        valid = row >= d
        pb = jnp.where(valid, pa * sb + pb, pb)
        pa = jnp.where(valid, pa * sa, pa)
        d *= 2
    hseq = pa * h_ref[...] + pb
    h_ref[...] = hseq[ts - 1:ts, :]

    y = y_ref[0]
    gelu = 0.5 * y * (1.0 + jnp.tanh(math.sqrt(2.0 / math.pi) * (y + 0.044715 * (y * y * y))))
    o_ref[0] = (hseq * gelu).astype(o_ref.dtype)


def rglru(proj3, conv_w, conv_b, wra_bd, ra_b, wri_bd, ri_b, lam, *, ts):
    b, s, _ = proj3.shape
    vec = lambda: pl.BlockSpec((1, GROUP), lambda i, j: (0, 0))
    mat = lambda: pl.BlockSpec((GROUP, GROUP), lambda i, j: (0, 0))
    return pl.pallas_call(
        functools.partial(_lru_kernel, ts=ts),
        grid=(b, s // ts),
        in_specs=[pl.BlockSpec((1, ts, GROUP), lambda i, j: (i, j, LRU_X // GROUP)),
                  pl.BlockSpec((1, ts, GROUP), lambda i, j: (i, j, LRU_Y // GROUP)),
                  pl.BlockSpec((CONV_WIDTH, GROUP), lambda i, j: (0, 0)),
                  vec(), mat(), vec(), mat(), vec(), vec()],
        out_specs=pl.BlockSpec((1, ts, GROUP), lambda i, j: (i, j, 0)),
        out_shape=jax.ShapeDtypeStruct((b, s, GROUP), BF16),
        scratch_shapes=[pltpu.VMEM((ts + 8, GROUP), F32), pltpu.VMEM((1, GROUP), F32)],
        compiler_params=_cparams(("parallel", "arbitrary")),
        name="rglru",
    )(proj3, proj3, conv_w, conv_b.reshape(1, GROUP), wra_bd, ra_b.reshape(1, GROUP), wri_bd,
      ri_b.reshape(1, GROUP), lam.reshape(1, GROUP))


def _bdot(dims):
    return lambda a, b: lax.dot_general(a, b, (dims, ((0,), (0,))), preferred_element_type=F32)


_bnn = _bdot(((2,), (1,)))
_bnt = _bdot(((2,), (2,)))
_btn = _bdot(((1,), (1,)))


def _unit_lower_inverse(a_strict, n):
    r = lax.broadcasted_iota(jnp.int32, (1, n, n), 1)
    c = lax.broadcasted_iota(jnp.int32, (1, n, n), 2)
    t = jnp.where(r == c, 1.0, jnp.where((r // 2 == c // 2) & (r > c), a_strict, 0.0))
    m = 2
    while m < n:
        off = (r // (2 * m) == c // (2 * m)) & (r % (2 * m) >= m) & (c % (2 * m) < m)
        tb = t.astype(BF16)
        t = t + _bnn(tb, _bnn(jnp.where(off, a_strict, 0.0).astype(BF16), tb).astype(BF16))
        m *= 2
    return t


def _to_heads(x):
    n, rows, _ = x.shape
    parts = [x[:, :, h * HEAD_DIM:(h + 1) * HEAD_DIM] for h in range(HEADS)]
    return jnp.stack(parts, axis=1).reshape(n * HEADS, rows, HEAD_DIM)


def _from_heads(x):
    nh, rows, _ = x.shape
    x = x.reshape(nh // HEADS, HEADS, rows, HEAD_DIM)
    return jnp.concatenate([x[:, h] for h in range(HEADS)], axis=-1)


def _rwkv_chunk_kernel(*refs, ts, chunk, has_vres):
    (sr_ref, sk_ref, sv_ref, sl_ref, mu_ref, w0_ref, w2_ref, a0_ref, a2_ref, g2_ref, kk_ref, ka_ref, rk_ref,
     ones_ref) = refs[:14]
    refs = refs[14:]
    if has_vres:
        vf_ref, v0_ref, v1_ref, v2_ref = refs[:4]
        refs = refs[4:]
    mm_out, gm_out, qm_out, y0_out, bonus_out, g_out, v_out, carry_ref = refs

    @pl.when(pl.program_id(1) == 0)
    def _():
        carry_ref[...] = jnp.zeros_like(carry_ref)

    row0 = lax.broadcasted_iota(jnp.int32, (ts, GROUP), 0) == 0

    def shift_mix(ref, idx):
        s = ref[0]
        prev = jnp.where(row0, carry_ref[idx:idx + 1, :], pltpu.roll(s, 1, axis=0))
        carry_ref[idx:idx + 1, :] = s[ts - 1:ts, :]
        return s + (prev - s) * mu_ref[idx:idx + 1, :]

    r = shift_mix(sr_ref, 0)
    k = shift_mix(sk_ref, 1)
    v = shift_mix(sv_ref, 2)
    low = shift_mix(sl_ref, 3)

    zw = w0_ref[...] + _dot(jnp.tanh(low).astype(BF16), w2_ref[...])
    lw = -jnp.exp(_log_sigmoid(zw) - 0.5)
    a = _sigmoid(a0_ref[...] + _dot(low.astype(BF16), a2_ref[...]))
    g_out[0] = _dot(_sigmoid(low).astype(BF16), g2_ref[...])
    if has_vres:
        mix = _dot(_dot(v.astype(BF16), v1_ref[...]).astype(BF16), v2_ref[...])
        v = v + (vf_ref[0] - v) * _sigmoid(v0_ref[...] + mix)
    v_out[0] = v
    kk = k * kk_ref[...]
    ss = _dot_sel(kk * kk, ones_ref[...], split="a")
    kk = kk / jnp.maximum(jnp.sqrt(ss), 1e-12)
    k = k * (1.0 + (a - 1.0) * ka_ref[...])
    bonus_out[0] = _dot_sel(r * k * rk_ref[...], ones_ref[...], split="a") * v
    a_vec, b_vec = -kk, kk * a

    c = chunk
    rr = lax.broadcasted_iota(jnp.int32, (ts, ts), 0)
    cc = lax.broadcasted_iota(jnp.int32, (ts, ts), 1)
    chunk_tri = ((rr >= cc) & (rr // c == cc // c)).astype(F32)
    cum_all = _dot_sel(chunk_tri, lw, split="b")
    strict = _tri(c, strict=True)[None]
    incl = _tri(c)[None]
    eye = (lax.broadcasted_iota(jnp.int32, (1, HEAD_DIM, HEAD_DIM), 1)
           == lax.broadcasted_iota(jnp.int32, (1, HEAD_DIM, HEAD_DIM), 2)).astype(F32)

    nq = ts // c
    per_chunk = lambda t: t.reshape(nq, c, GROUP)
    cum = per_chunk(cum_all)
    rq, kq, vq, aq, bq = (per_chunk(t) for t in (r, k, v, a_vec, b_vec))
    cum_ex = cum - per_chunk(lw)
    mid = cum[:, c // 2 - 1:c // 2, :]
    tot = cum[:, c - 1:c, :]
    e_fwd = jnp.exp(cum - mid)
    e_bwd = jnp.exp(mid - cum)
    e_end = jnp.exp(tot - cum)
    mxu = lambda t: _to_heads(t.astype(BF16))
    r_rel, k_rel = mxu(rq * e_fwd), mxu(kq * e_bwd)
    a_rel, b_rel = mxu(aq * jnp.exp(cum_ex - mid)), mxu(bq * e_bwd)
    a_abs, r_abs = mxu(aq * jnp.exp(cum_ex)), rq * jnp.exp(cum)
    k_end, b_end = mxu(kq * e_end), mxu(bq * e_end)
    gam = _to_heads(jnp.exp(tot))
    vh = mxu(vq)

    ar_rel = jnp.concatenate([a_rel, r_rel], axis=1)
    s_b, s_k = _bnt(ar_rel, b_rel), _bnt(ar_rel, k_rel)
    a_ab = jnp.where(strict, s_b[:, :c], 0.0)
    a_ak = jnp.where(strict, s_k[:, :c], 0.0).astype(BF16)
    a_rb = jnp.where(incl, s_b[:, c:], 0.0).astype(BF16)
    a_rk = jnp.where(incl, s_k[:, c:], 0.0).astype(BF16)
    t_inv = _unit_lower_inverse(a_ab, c).astype(BF16)
    pu = _bnn(t_inv, jnp.concatenate([a_abs, _bnn(a_ak, vh).astype(BF16)], axis=-1)).astype(BF16)
    z = _bnn(a_rb, pu)
    y0 = z[..., HEAD_DIM:] + _bnn(a_rk, vh)
    xtb = _btn(pu, b_end)
    mm_mat = eye * gam + xtb[:, :HEAD_DIM]
    gm = xtb[:, HEAD_DIM:] + _btn(vh, k_end)
    dense = lambda t: _from_heads(t).reshape(ts, GROUP)
    mm_out[0] = dense(mm_mat).astype(mm_out.dtype)
    gm_out[0] = dense(gm)
    qm_out[0] = (dense(z[..., :HEAD_DIM]) + r_abs.reshape(ts, GROUP)).astype(qm_out.dtype)
    y0_out[0] = dense(y0)


def rwkv_chunk(proj3, mu4, w0, w2p, a0, a2p, g2p, k_k, k_a, r_k, head_ones, vres, *, ts, chunk):
    b, s, _ = proj3.shape
    slab = lambda off: pl.BlockSpec((1, ts, GROUP), lambda i, j: (i, j, off // GROUP))
    vec = lambda: pl.BlockSpec((1, GROUP), lambda i, j: (0, 0))
    full = lambda shape: pl.BlockSpec(shape, lambda i, j: tuple(0 for _ in shape))
    tok = pl.BlockSpec((1, ts, GROUP), lambda i, j: (i, j, 0))
    in_specs = [slab(RWKV_R), slab(RWKV_K), slab(RWKV_V), slab(RWKV_LR), full((4, GROUP)),
                vec(), full((GROUP, GROUP)), vec(), full((GROUP, GROUP)), full((GROUP, GROUP)), vec(), vec(), vec(),
                full((GROUP, GROUP))]
    args = [proj3, proj3, proj3, proj3, mu4, w0.reshape(1, GROUP), w2p, a0.reshape(1, GROUP), a2p, g2p,
            k_k.reshape(1, GROUP), k_a.reshape(1, GROUP), r_k.reshape(1, GROUP), head_ones]
    if vres is not None:
        v_first, v0, v1p, v2p = vres
        in_specs += [tok, vec(), full((GROUP, 128)), full((128, GROUP))]
        args += [v_first, v0.reshape(1, GROUP), v1p, v2p]
    return pl.pallas_call(
        functools.partial(_rwkv_chunk_kernel, ts=ts, chunk=chunk, has_vres=vres is not None),
        grid=(b, s // ts),
        in_specs=in_specs,
        out_specs=[tok] * 7,
        out_shape=[jax.ShapeDtypeStruct((b, s, GROUP), dt) for dt in (BF16, F32, BF16, F32, F32, F32, F32)],
        scratch_shapes=[pltpu.VMEM((4, GROUP), F32)],
        compiler_params=_cparams(("parallel", "arbitrary")),
        name="rwkv_chunk",
    )(*args)


def _rwkv_state_kernel(mm_ref, gm_ref, qm_ref, y0_ref, bonus_ref, g_ref, gw_ref, gb_ref, o_ref, state_ref, *, chunk):
    @pl.when(pl.program_id(0) == 0)
    def _():
        state_ref[...] = jnp.zeros_like(state_ref)

    state = state_ref[...]
    for c in range(mm_ref.shape[1] // chunk):
        rows = slice(c * chunk, (c + 1) * chunk)
        s_hi, s_lo = _split_bf16(state)
        qm, mm = _to_heads(qm_ref[:, rows, :]), _to_heads(mm_ref[:, rows, :])
        y = _bnt(qm, s_hi) + _bnt(qm, s_lo) + _to_heads(y0_ref[:, rows, :])
        state = _bnn(s_hi, mm) + _bnn(s_lo, mm) + _to_heads(gm_ref[:, rows, :])
        mu = jnp.mean(y, axis=-1, keepdims=True)
        var = jnp.mean(jnp.square(y - mu), axis=-1, keepdims=True)
        yn = _from_heads((y - mu) * lax.rsqrt(var + RWKV_GN_EPS)) * gw_ref[...] + gb_ref[...]
        o_ref[:, rows, :] = ((yn + bonus_ref[:, rows, :]) * g_ref[:, rows, :]).astype(o_ref.dtype)
    state_ref[...] = state


def rwkv_state(mm, gm, qm, y0, bonus, g, gn_w, gn_b, *, ts, chunk):
    bsz, s, _ = mm.shape
    tok = pl.BlockSpec((bsz, ts, GROUP), lambda j: (0, j, 0))
    vec = pl.BlockSpec((1, GROUP), lambda j: (0, 0))
    return pl.pallas_call(
        functools.partial(_rwkv_state_kernel, chunk=chunk),
        grid=(s // ts,),
        in_specs=[tok] * 6 + [vec] * 2,
        out_specs=tok,
        out_shape=jax.ShapeDtypeStruct((bsz, s, GROUP), BF16),
        scratch_shapes=[pltpu.VMEM((bsz * HEADS, HEAD_DIM, HEAD_DIM), F32)],
        compiler_params=_cparams(("arbitrary",)),
        name="rwkv_state",
    )(mm, gm, qm, y0, bonus, g, gn_w.reshape(1, GROUP), gn_b.reshape(1, GROUP))


def _ret_kernel(q_ref, k_ref, v_ref, g_ref, cos_ref, sin_ref, dmat_ref, xi_ref, zeta_ref, cd_ref, gw_ref, o_ref,
                state_ref, *, ts, chunk):
    @pl.when(pl.program_id(1) == 0)
    def _():
        state_ref[...] = jnp.zeros_like(state_ref)

    lane = lax.broadcasted_iota(jnp.int32, (ts, GROUP), 1)
    first_half = (lane % HEAD_DIM) < (HEAD_DIM // 2)
    cos, sin = cos_ref[...], sin_ref[...]

    def rotary(t):
        partner = jnp.where(first_half, pltpu.roll(t, GROUP - HEAD_DIM // 2, axis=1),
                            pltpu.roll(t, HEAD_DIM // 2, axis=1))
        return t * cos + partner * sin

    nq = ts // chunk
    per_chunk = lambda t: t.reshape(nq, chunk, GROUP)
    mxu = lambda t: _to_heads(t).astype(BF16)
    q = per_chunk(rotary(q_ref[0]))
    k = per_chunk(rotary(k_ref[0]) * (HEAD_DIM ** -0.5))
    qb, kb, vb = mxu(q), mxu(k), mxu(per_chunk(v_ref[0]))
    q_cross, k_decay = mxu(q * xi_ref[...]), mxu(k * zeta_ref[...])
    inner = _bnt(qb, kb).reshape(nq, HEADS, chunk, chunk) * dmat_ref[...]
    intra = _bnn(inner.reshape(nq * HEADS, chunk, chunk).astype(BF16), vb)
    kv = _btn(k_decay, vb)

    decay = _to_heads(cd_ref[...][None])
    state = state_ref[...]
    incoming = []
    for c in range(nq):
        incoming.append(state)
        state = state * decay + kv[c * HEADS:(c + 1) * HEADS]
    state_ref[...] = state
    o = intra + _bnn(q_cross, jnp.concatenate(incoming, axis=0).astype(BF16))
    o = _from_heads(o * lax.rsqrt(jnp.mean(o * o, axis=-1, keepdims=True) + NORM_EPS)).reshape(ts, GROUP)
    g = g_ref[0]
    o_ref[0] = (o * gw_ref[...] * (g * _sigmoid(g))).astype(o_ref.dtype)


def retention(proj3, cos_t, sin_t, dmat, xi, zeta, cd, gn_w, *, ts, chunk):
    b, s, _ = proj3.shape
    slab = lambda off: pl.BlockSpec((1, ts, GROUP), lambda i, j: (i, j, off // GROUP))
    full = lambda shape: pl.BlockSpec(shape, lambda i, j: tuple(0 for _ in shape))
    return pl.pallas_call(
        functools.partial(_ret_kernel, ts=ts, chunk=chunk),
        grid=(b, s // ts),
        in_specs=[slab(RET_Q), slab(RET_K), slab(RET_V), slab(RET_G),
                  pl.BlockSpec((ts, GROUP), lambda i, j: (j, 0)),
                  pl.BlockSpec((ts, GROUP), lambda i, j: (j, 0)),
                  full((HEADS, chunk, chunk)), full((chunk, GROUP)), full((chunk, GROUP)),
                  full((1, GROUP)), full((1, GROUP))],
        out_specs=pl.BlockSpec((1, ts, GROUP), lambda i, j: (i, j, 0)),
        out_shape=jax.ShapeDtypeStruct((b, s, GROUP), BF16),
        scratch_shapes=[pltpu.VMEM((HEADS, HEAD_DIM, HEAD_DIM), F32)],
        compiler_params=_cparams(("parallel", "arbitrary")),
        name="retention",
    )(proj3, proj3, proj3, proj3, cos_t, sin_t, dmat, xi, zeta, cd, gn_w.reshape(1, GROUP))


def _retention_tables(s, chunk):
    half = HEAD_DIM // 2
    inv = 1.0 / (RET_THETA ** jnp.linspace(0.0, 1.0, half, dtype=F32))
    ang = jnp.arange(s, dtype=F32)[:, None] * inv[None, :]
    cos, sin = jnp.cos(ang), jnp.sin(ang)
    cos_t = jnp.tile(jnp.concatenate([cos, cos], axis=-1), (1, HEADS))
    sin_t = jnp.tile(jnp.concatenate([-sin, sin], axis=-1), (1, HEADS))
    lg = jnp.log(1.0 - 2.0 ** (-5.0 - jnp.arange(HEADS, dtype=F32)))
    n = jnp.arange(chunk, dtype=F32)
    diff = n[:, None] - n[None, :]
    dmat = jnp.where(diff >= 0, jnp.exp(lg[:, None, None] * jnp.maximum(diff, 0.0)), 0.0)
    zeta = jnp.exp(lg[:, None] * (chunk - 1.0 - n)[None, :])
    xi = jnp.exp(lg[:, None] * (n + 1.0)[None, :])
    per_lane = lambda t: jnp.repeat(t.T, HEAD_DIM, axis=1)
    cd = jnp.repeat(jnp.exp(lg * chunk), HEAD_DIM)[None, :]
    return cos_t, sin_t, dmat, per_lane(xi), per_lane(zeta), cd


def _mix_xattn_kernel(m0_ref, m1_ref, m2_ref, m3_ref, x_ref, kv_ref, wout_ref, wq_ref, wo_ref, gmix_ref, gpre_ref,
                      gpost_ref, o_ref):
    acc = _dot(m0_ref[0], wout_ref[0:GROUP, :])
    for idx, m_ref in enumerate((m1_ref, m2_ref, m3_ref), start=1):
        acc = acc + _dot(m_ref[0], wout_ref[idx * GROUP:(idx + 1) * GROUP, :])
    x = x_ref[0] + _rms(acc, gmix_ref[...])
    xn = _rms(x, gpre_ref[...]).astype(BF16)
    q = (_dot(xn, wq_ref[...]) * (XATTN_HEAD_DIM ** -0.5)).astype(BF16)
    heads = lambda t, off: jnp.stack([t[:, off + h * XATTN_HEAD_DIM:off + (h + 1) * XATTN_HEAD_DIM]
                                      for h in range(XATTN_HEADS)])
    kv = kv_ref[0]
    s = _bnt(heads(q, 0), heads(kv, 0))
    e = jnp.exp(s - jnp.max(s, axis=-1, keepdims=True))
    p = e / jnp.sum(e, axis=-1, keepdims=True)
    o = _bnn(p.astype(BF16), heads(kv, D_MODEL)).astype(BF16)
    o = jnp.concatenate([o[h] for h in range(XATTN_HEADS)], axis=-1)
    o_ref[0] = x + _rms(_dot(o, wo_ref[...]), gpost_ref[...])


def mix_xattn(mixed, x3, kv, w_out, wq, wo, layer, g_mix, g_pre, g_post, *, tm):
    b, s, d = x3.shape
    m = kv.shape[1]
    tok = lambda width: pl.BlockSpec((1, tm, width), lambda bi, i: (bi, i, 0))
    weight = lambda: pl.BlockSpec((None, d, d), lambda bi, i: (layer, 0, 0))
    vec = lambda: pl.BlockSpec((1, d), lambda bi, i: (0, 0))
    return pl.pallas_call(
        _mix_xattn_kernel,
        grid=(b, s // tm),
        in_specs=[tok(GROUP)] * 4 + [tok(d), pl.BlockSpec((1, m, 2 * d), lambda bi, i: (bi, 0, 0)),
                                     weight(), weight(), weight(), vec(), vec(), vec()],
        out_specs=tok(d),
        out_shape=jax.ShapeDtypeStruct((b, s, d), F32),
        compiler_params=_cparams(("parallel", "arbitrary")),
        name="mix_xattn",
    )(*mixed, x3, kv, w_out, wq, wo, g_mix.reshape(1, d), g_pre.reshape(1, d), g_post.reshape(1, d))


def _mlp_kernel(x_ref, w1_ref, w2_ref, gpre_ref, gpost_ref, o_ref, *, ff_tile):
    x = x_ref[...]
    xn = _rms(x, gpre_ref[...]).astype(BF16)
    d_ff = w1_ref.shape[1]
    acc = None
    for c in range(d_ff // ff_tile):
        hid = jnp.square(jnp.maximum(_dot(xn, w1_ref[:, c * ff_tile:(c + 1) * ff_tile]), 0.0)).astype(BF16)
        part = _dot(hid, w2_ref[c * ff_tile:(c + 1) * ff_tile, :])
        acc = part if acc is None else acc + part
    o_ref[...] = x + _rms(acc, gpost_ref[...])


def mlp(x, w1, w2, layer, g_pre, g_post, *, tm, ff_tile):
    n, d = x.shape
    d_ff = w1.shape[2]
    return pl.pallas_call(
        functools.partial(_mlp_kernel, ff_tile=ff_tile),
        grid=(n // tm,),
        in_specs=[pl.BlockSpec((tm, d), lambda i: (i, 0)),
                  pl.BlockSpec((None, d, d_ff), lambda i: (layer, 0, 0), pipeline_mode=pl.Buffered(1)),
                  pl.BlockSpec((None, d_ff, d), lambda i: (layer, 0, 0), pipeline_mode=pl.Buffered(1)),
                  pl.BlockSpec((1, d), lambda i: (0, 0)),
                  pl.BlockSpec((1, d), lambda i: (0, 0))],
        out_specs=pl.BlockSpec((tm, d), lambda i: (i, 0)),
        out_shape=jax.ShapeDtypeStruct((n, d), F32),
        compiler_params=_cparams(("parallel",)),
        name="mlp",
    )(x, w1, w2, g_pre.reshape(1, d), g_post.reshape(1, d))


def _block_diag(w):
    h, n, _ = w.shape
    eye = jnp.eye(h, dtype=w.dtype)
    return (eye[:, None, :, None] * w[:, :, None, :]).reshape(h * n, h * n)


def _pad_rows(w, start, total):
    return jnp.zeros((total, w.shape[1]), w.dtype).at[start:start + w.shape[0]].set(w)


def _tile(n, pref):
    return pref if n % pref == 0 else n


def kernel(x, mem, norm_mix_pre, norm_mix_post, norm_xa_pre, norm_xa_post, norm_mem, norm_mlp_pre, norm_mlp_post, w_in, w_out, fox_f_bias, lru_conv_w, lru_conv_b, lru_ra_w, lru_ra_b, lru_ri_w, lru_ri_b, lru_lambda, rwkv_mu, rwkv_w0, rwkv_w2, rwkv_a0, rwkv_a2, rwkv_g2, rwkv_k_k, rwkv_k_a, rwkv_r_k, rwkv_gn_w, rwkv_gn_b, rwkv_v0, rwkv_v1, rwkv_v2, ret_gn_w, xa_wq, xa_wk, xa_wv, xa_wo, mlp_w1, mlp_w2):
    bsz, seq, d = x.shape
    depth = w_in.shape[0]
    n_tok = bsz * seq
    mem_len = mem.shape[1]
    tm = _tile(n_tok, 512)
    tq = _tile(seq, 512)
    ts = _tile(seq, 512)
    ret_tables = _retention_tables(seq, RET_CHUNK)
    head_ones = _block_diag(jnp.ones((HEADS, HEAD_DIM, HEAD_DIM), BF16))

    w_in_t = w_in.swapaxes(1, 2)
    w_in_b = jnp.concatenate([w_in_t[:, :FOX_REAL], jnp.zeros((depth, LRU_X - FOX_REAL, d), w_in.dtype),
                              w_in_t[:, FOX_REAL:]], axis=1).astype(BF16)
    w_out_b = w_out.astype(BF16)
    wq_b, wo_b = xa_wq.astype(BF16), xa_wo.astype(BF16)
    wkv_b = jnp.concatenate([xa_wk, xa_wv], axis=-1).astype(BF16)
    w1_b, w2_b = mlp_w1.astype(BF16), mlp_w2.astype(BF16)

    x2 = x.reshape(n_tok, d)
    v_first = None
    for l in range(depth):
        proj = in_proj(x2, norm_mix_pre[l], w_in_b, l, tm=tm, tn=IN_PAD // 7)
        proj3 = proj.reshape(bsz, seq, IN_PAD)

        f_bias = jnp.zeros((1, GROUP), F32).at[0, :HEADS].set(fox_f_bias[l])
        qa, ka, vt = fox_prep(proj3, f_bias, tc=tq)
        fox_out = fox_attention(qa, ka, vt)

        lru_out = rglru(proj3, lru_conv_w[l], lru_conv_b[l], _block_diag(lru_ra_w[l]).astype(BF16), lru_ra_b[l],
                        _block_diag(lru_ri_w[l]).astype(BF16), lru_ri_b[l], lru_lambda[l], ts=ts)

        vres = None
        if l > 0:
            vres = (v_first, rwkv_v0[l - 1],
                    jnp.pad(rwkv_v1[l - 1], ((0, 0), (0, 128 - RWKV_V_RANK))).astype(BF16),
                    _pad_rows(rwkv_v2[l - 1], 0, 128).astype(BF16))
        mm_, gm_, qm_, y0_, bonus_, g_, v_ = rwkv_chunk(
            proj3, rwkv_mu[l].reshape(4, GROUP), rwkv_w0[l],
            _pad_rows(rwkv_w2[l], 0, GROUP).astype(BF16), rwkv_a0[l],
            _pad_rows(rwkv_a2[l], RWKV_W_RANK, GROUP).astype(BF16),
            _pad_rows(rwkv_g2[l], RWKV_W_RANK + RWKV_A_RANK, GROUP).astype(BF16),
            rwkv_k_k[l], rwkv_k_a[l], rwkv_r_k[l], head_ones, vres, ts=_tile(seq, 8 * RWKV_CHUNK), chunk=RWKV_CHUNK)
        if l == 0:
            v_first = v_
        rwkv_out = rwkv_state(mm_, gm_, qm_, y0_, bonus_, g_, rwkv_gn_w[l], rwkv_gn_b[l],
                              ts=_tile(seq, 4 * RWKV_CHUNK), chunk=RWKV_CHUNK)

        ret_out = retention(proj3, *ret_tables, ret_gn_w[l], ts=_tile(seq, 8 * RET_CHUNK), chunk=RET_CHUNK)

        kv = norm_matmul(mem.reshape(bsz * mem_len, d), norm_mem[l], wkv_b, l, tm=_tile(bsz * mem_len, 512),
                         tn=1024, out_dtype=BF16).reshape(bsz, mem_len, 2 * d)
        x2 = mix_xattn((fox_out, lru_out, rwkv_out, ret_out), x2.reshape(bsz, seq, d), kv, w_out_b, wq_b, wo_b, l,
                       norm_mix_post[l], norm_xa_pre[l], norm_xa_post[l], tm=_tile(seq, 512)).reshape(n_tok, d)

        x2 = mlp(x2, w1_b, w2_b, l, norm_mlp_pre[l], norm_mlp_post[l], tm=tm, ff_tile=1024)
    return x2.reshape(bsz, seq, d)
```

```python
import functools
import math

import jax
import jax.numpy as jnp
import numpy as np
from jax import lax
from jax.experimental import pallas as pl
from jax.experimental.pallas import tpu as pltpu

F32 = jnp.float32
BF16 = jnp.bfloat16
HIGHEST = lax.Precision.HIGHEST

D_MODEL = 1024
GROUP = 256
HEADS = 4
HEAD_DIM = 64
CONV_WIDTH = 4
LRU_C = 8.0
RET_THETA = 10000.0
RET_CHUNK = 128
RWKV_W_RANK, RWKV_A_RANK, RWKV_G_RANK, RWKV_V_RANK = 64, 64, 128, 32
RWKV_GN_EPS = 64e-5
XATTN_HEADS = 4
XATTN_HEAD_DIM = D_MODEL // XATTN_HEADS
NORM_EPS = 1e-6
NEG_BIG = -1e30

FOX_Q, FOX_K, FOX_V, FOX_F = 0, 256, 512, 768
LRU_X, LRU_Y = 1024, 1280
RWKV_R, RWKV_K, RWKV_V, RWKV_LR = 1536, 1792, 2048, 2304
RET_Q, RET_K, RET_V, RET_G = 2560, 2816, 3072, 3328
IN_PAD = 3584
FOX_REAL = 3 * GROUP + HEADS

VMEM_LIMIT = 56 * 1024 * 1024
RWKV_CHUNK = 64


def _cparams(sem):
    return pltpu.CompilerParams(dimension_semantics=sem, vmem_limit_bytes=VMEM_LIMIT)


def _rms(x, g):
    return x * lax.rsqrt(jnp.mean(x * x, axis=-1, keepdims=True) + NORM_EPS) * g


def _log_sigmoid(x):
    return jnp.minimum(x, 0.0) - jnp.log1p(jnp.exp(-jnp.abs(x)))


def _sigmoid(x):
    return 1.0 / (1.0 + jnp.exp(-x))


def _dot(a, b, **kw):
    return jnp.dot(a, b, preferred_element_type=F32, **kw)


def _dot_nt(a, b, **kw):
    return lax.dot_general(a, b, (((1,), (1,)), ((), ())), preferred_element_type=F32, **kw)


def _dot_tn(a, b, **kw):
    return lax.dot_general(a, b, (((0,), (0,)), ((), ())), preferred_element_type=F32, **kw)


def _split_bf16(x):
    hi = x.astype(BF16)
    return hi, (x - hi.astype(F32)).astype(BF16)


def _mm1(dot, a, b):
    return dot(a.astype(BF16), b.astype(BF16))


def _mm3(dot, a, b):
    ah, al = _split_bf16(a)
    bh, bl = _split_bf16(b)
    return dot(ah, bh) + (dot(ah, bl) + dot(al, bh))


def _bf16_terms(x, n):
    terms = []
    for _ in range(n - 1):
        t = x.astype(BF16)
        terms.append(t)
        x = x - t.astype(F32)
    return terms + [x.astype(BF16)]


def _dot_sel(a, b, *, split, n=3):
    if split == "a":
        parts = [_dot(t, b.astype(BF16)) for t in _bf16_terms(a, n)]
    else:
        parts = [_dot(a.astype(BF16), t) for t in _bf16_terms(b, n)]
    out = parts[-1]
    for p in reversed(parts[:-1]):
        out = out + p
    return out


def _tri(n, strict=False):
    r = lax.broadcasted_iota(jnp.int32, (n, n), 0)
    c = lax.broadcasted_iota(jnp.int32, (n, n), 1)
    return (r > c) if strict else (r >= c)


def _norm_matmul_kernel(x_ref, g_ref, w_ref, o_ref, xn_ref):
    @pl.when(pl.program_id(1) == 0)
    def _():
        xn_ref[...] = _rms(x_ref[...], g_ref[...]).astype(BF16)

    o_ref[...] = _dot(xn_ref[...], w_ref[...]).astype(o_ref.dtype)


def norm_matmul(x, g, w, layer, *, tm, tn, out_dtype):
    n, d = x.shape
    width = w.shape[2]
    return pl.pallas_call(
        _norm_matmul_kernel,
        grid=(n // tm, width // tn),
        in_specs=[pl.BlockSpec((tm, d), lambda i, j: (i, 0)),
                  pl.BlockSpec((1, d), lambda i, j: (0, 0)),
                  pl.BlockSpec((None, d, tn), lambda i, j: (layer, 0, j))],
        out_specs=pl.BlockSpec((tm, tn), lambda i, j: (i, j)),
        out_shape=jax.ShapeDtypeStruct((n, width), out_dtype),
        scratch_shapes=[pltpu.VMEM((tm, d), BF16)],
        compiler_params=_cparams(("parallel", "arbitrary")),
        name="norm_matmul",
    )(x, g.reshape(1, d), w)


def _in_proj_kernel(x_ref, g_ref, w_ref, o_ref, *, tn):
    xn = _rms(x_ref[...], g_ref[...]).astype(BF16)
    for c in range(w_ref.shape[0] // tn):
        o_ref[:, c * tn:(c + 1) * tn] = _dot_nt(xn, w_ref[c * tn:(c + 1) * tn, :])


def in_proj(x, g, w, layer, *, tm, tn):
    n, d = x.shape
    width = w.shape[1]
    return pl.pallas_call(
        functools.partial(_in_proj_kernel, tn=tn),
        grid=(n // tm,),
        in_specs=[pl.BlockSpec((tm, d), lambda i: (i, 0)),
                  pl.BlockSpec((1, d), lambda i: (0, 0)),
                  pl.BlockSpec((None, width, d), lambda i: (layer, 0, 0), pipeline_mode=pl.Buffered(1))],
        out_specs=pl.BlockSpec((tm, width), lambda i: (i, 0)),
        out_shape=jax.ShapeDtypeStruct((n, width), F32),
        compiler_params=_cparams(("parallel",)),
        name="in_proj",
    )(x, g.reshape(1, d), w)


FOX_AUG = 128
FOX_VROWS = HEAD_DIM + 16
LOG2E = 1.4426950408889634


def _fox_prep_kernel(q_ref, k_ref, v_ref, f_ref, b_ref, qa_ref, ka_ref, vt_ref, carry_ref):
    @pl.when(pl.program_id(1) == 0)
    def _():
        carry_ref[...] = jnp.zeros_like(carry_ref)

    lf = _log_sigmoid(f_ref[0] + b_ref[...])
    tc = lf.shape[0]
    cum = _dot_sel(_tri(tc), lf, split="b") + carry_ref[...]
    carry_ref[...] = cum[tc - 1:tc, :]

    lane = lax.broadcasted_iota(jnp.int32, (tc, FOX_AUG), 1)
    feat = lane < HEAD_DIM
    ones = jnp.where((lane >= HEAD_DIM) & (lane < HEAD_DIM + 3), 1.0, 0.0)
    q = q_ref[0] * (HEAD_DIM ** -0.5 * LOG2E)
    k = k_ref[0]
    for h in range(HEADS):
        pair = slice((h // 2) * FOX_AUG, (h // 2 + 1) * FOX_AUG)
        qt, kt = q[:, pair], k[:, pair]
        if h % 2:
            qt, kt = pltpu.roll(qt, HEAD_DIM, axis=1), pltpu.roll(kt, HEAD_DIM, axis=1)
        neg_c = jnp.broadcast_to(cum[:, h:h + 1] * -LOG2E, (tc, FOX_AUG))
        c_hi = neg_c.astype(BF16).astype(F32)
        rest = neg_c - c_hi
        c_mid = rest.astype(BF16).astype(F32)
        bias = jnp.where(lane == HEAD_DIM, c_hi, jnp.where(lane == HEAD_DIM + 1, c_mid,
                                                           jnp.where(lane == HEAD_DIM + 2, rest - c_mid, 0.0)))
        qa_ref[0, h] = jnp.where(feat, qt, ones).astype(BF16)
        ka_ref[0, h] = jnp.where(feat, kt, bias).astype(BF16)
    ones_row = lax.broadcasted_iota(jnp.int32, (HEADS, FOX_VROWS - HEAD_DIM, tc), 1) == 0
    vt = jnp.concatenate([v_ref[0].T.reshape(HEADS, HEAD_DIM, tc), ones_row.astype(F32)], axis=1)
    vt_ref[0, :, 0] = vt.astype(BF16)


def fox_prep(proj3, f_bias_pad, *, tc):
    b, s, _ = proj3.shape
    slab = lambda off: pl.BlockSpec((1, tc, GROUP), lambda i, j: (i, j, off // GROUP))
    aug = pl.BlockSpec((1, HEADS, tc, FOX_AUG), lambda i, j: (i, 0, j, 0))
    return pl.pallas_call(
        _fox_prep_kernel,
        grid=(b, s // tc),
        in_specs=[slab(FOX_Q), slab(FOX_K), slab(FOX_V), slab(FOX_F), pl.BlockSpec((1, GROUP), lambda i, j: (0, 0))],
        out_specs=[aug, aug, pl.BlockSpec((1, HEADS, 1, FOX_VROWS, tc), lambda i, j: (i, 0, j, 0, 0))],
        out_shape=[jax.ShapeDtypeStruct((b, HEADS, s, FOX_AUG), BF16),
                   jax.ShapeDtypeStruct((b, HEADS, s, FOX_AUG), BF16),
                   jax.ShapeDtypeStruct((b, HEADS, s // tc, FOX_VROWS, tc), BF16)],
        scratch_shapes=[pltpu.VMEM((1, GROUP), F32)],
        compiler_params=_cparams(("parallel", "arbitrary")),
        name="fox_prep",
    )(proj3, proj3, proj3, proj3, f_bias_pad)


def _fox_attn_kernel(q_ref, k_ref, v_ref, o_ref, s_ref, *, t):
    i = pl.program_id(1)
    qa = q_ref[0]

    def scores(j):
        return _bnt(k_ref[0, :, pl.ds(pl.multiple_of(j * t, t), t), :], qa)

    def update(carry, slot, j):
        m, acc = carry
        s = s_ref[slot]
        m_new = jnp.maximum(m, jnp.max(s, axis=1, keepdims=True))
        p = jnp.exp2(s - m_new).astype(BF16)
        return m_new, jnp.exp2(m - m_new) * acc + _bnn(v_ref[0, :, j], p)

    key = lax.broadcasted_iota(jnp.int32, (1, t, t), 1)
    qry = lax.broadcasted_iota(jnp.int32, (1, t, t), 2)
    s_ref[0] = jnp.where(key <= qry, scores(i), NEG_BIG)
    carry = (jnp.full((HEADS, 1, t), NEG_BIG, F32), jnp.zeros((HEADS, FOX_VROWS, t), F32))

    def pair(jj, carry):
        s_ref[1] = scores(2 * jj)
        carry = update(carry, 0, jnp.where(jj == 0, i, 2 * jj - 1))
        s_ref[0] = scores(2 * jj + 1)
        return update(carry, 1, 2 * jj)

    carry = lax.fori_loop(0, i // 2, pair, carry)
    pending = jnp.where(i < 2, i, 2 * (i // 2) - 1)

    def odd_tail(carry):
        s_ref[1] = scores(i - 1)
        return update(update(carry, 0, pending), 1, i - 1)

    _, acc = lax.cond(i % 2 == 1, odd_tail, lambda c: update(c, 0, pending), carry)
    out = acc[:, :HEAD_DIM] / acc[:, HEAD_DIM:HEAD_DIM + 1]
    o_ref[0] = out.reshape(GROUP, t).T.astype(o_ref.dtype)


def fox_attention(qa, ka, vt):
    b, h, s, _ = qa.shape
    nk, t = vt.shape[2], vt.shape[4]
    return pl.pallas_call(
        functools.partial(_fox_attn_kernel, t=t),
        grid=(b, nk),
        in_specs=[pl.BlockSpec((1, h, t, FOX_AUG), lambda bi, i: (bi, 0, i, 0)),
                  pl.BlockSpec((1, h, s, FOX_AUG), lambda bi, i: (bi, 0, 0, 0)),
                  pl.BlockSpec((1, h, nk, FOX_VROWS, t), lambda bi, i: (bi, 0, 0, 0, 0))],
        out_specs=pl.BlockSpec((1, t, GROUP), lambda bi, i: (bi, i, 0)),
        out_shape=jax.ShapeDtypeStruct((b, s, GROUP), BF16),
        scratch_shapes=[pltpu.VMEM((2, h, t, t), F32)],
        compiler_params=_cparams(("parallel", "arbitrary")),
        name="fox_attention",
    )(qa, ka, vt)


def _lru_kernel(x_ref, y_ref, cw_ref, cb_ref, wra_ref, bra_ref, wri_ref, bri_ref, lam_ref, o_ref,
                buf_ref, h_ref, *, ts):
    @pl.when(pl.program_id(1) == 0)
    def _():
        buf_ref[0:8, :] = jnp.zeros((8, GROUP), F32)
        h_ref[...] = jnp.zeros_like(h_ref)

    xb = x_ref[0]
    buf_ref[8:8 + ts, :] = xb
    xc = cb_ref[...] + buf_ref[5:5 + ts, :] * cw_ref[0:1, :]
    for j in range(1, CONV_WIDTH):
        xc = xc + buf_ref[5 + j:5 + j + ts, :] * cw_ref[j:j + 1, :]
    buf_ref[0:8, :] = xb[ts - 8:ts, :]

    xcb = xc.astype(BF16)
    r = _sigmoid(_dot(xcb, wra_ref[...]) + bra_ref[...])
    gate_i = _sigmoid(_dot(xcb, wri_ref[...]) + bri_ref[...])
    log_a = LRU_C * r * _log_sigmoid(lam_ref[...])
    a = jnp.exp(log_a)
    z = 2.0 * log_a
    mult = jnp.sqrt(jnp.maximum(-jnp.tanh(0.5 * z) * (jnp.exp(z) + 1.0), 0.0))
    u = mult * (gate_i * xc)

    row = lax.broadcasted_iota(jnp.int32, (ts, GROUP), 0)
    pa, pb = a, u
    d = 1
    while d < ts:
        sa = pltpu.roll(pa, d, axis=0)
        sb = pltpu.roll(pb, d, axis=0)
        valid = row >= d
        pb = jnp.where(valid, pa * sb + pb, pb)
        pa = jnp.where(valid, pa * sa, pa)
        d *= 2
    hseq = pa * h_ref[...] + pb
    h_ref[...] = hseq[ts - 1:ts, :]

    y = y_ref[0]
    gelu = 0.5 * y * (1.0 + jnp.tanh(math.sqrt(2.0 / math.pi) * (y + 0.044715 * (y * y * y))))
    o_ref[0] = (hseq * gelu).astype(o_ref.dtype)


def rglru(proj3, conv_w, conv_b, wra_bd, ra_b, wri_bd, ri_b, lam, *, ts):
    b, s, _ = proj3.shape
    vec = lambda: pl.BlockSpec((1, GROUP), lambda i, j: (0, 0))
    mat = lambda: pl.BlockSpec((GROUP, GROUP), lambda i, j: (0, 0))
    return pl.pallas_call(
        functools.partial(_lru_kernel, ts=ts),
        grid=(b, s // ts),
        in_specs=[pl.BlockSpec((1, ts, GROUP), lambda i, j: (i, j, LRU_X // GROUP)),
                  pl.BlockSpec((1, ts, GROUP), lambda i, j: (i, j, LRU_Y // GROUP)),
                  pl.BlockSpec((CONV_WIDTH, GROUP), lambda i, j: (0, 0)),
                  vec(), mat(), vec(), mat(), vec(), vec()],
        out_specs=pl.BlockSpec((1, ts, GROUP), lambda i, j: (i, j, 0)),
        out_shape=jax.ShapeDtypeStruct((b, s, GROUP), BF16),
        scratch_shapes=[pltpu.VMEM((ts + 8, GROUP), F32), pltpu.VMEM((1, GROUP), F32)],
        compiler_params=_cparams(("parallel", "arbitrary")),
        name="rglru",
    )(proj3, proj3, conv_w, conv_b.reshape(1, GROUP), wra_bd, ra_b.reshape(1, GROUP), wri_bd,
      ri_b.reshape(1, GROUP), lam.reshape(1, GROUP))


def _bdot(dims):
    return lambda a, b: lax.dot_general(a, b, (dims, ((0,), (0,))), preferred_element_type=F32)


_bnn = _bdot(((2,), (1,)))
_bnt = _bdot(((2,), (2,)))
_btn = _bdot(((1,), (1,)))


def _unit_lower_inverse(a_strict, n):
    r = lax.broadcasted_iota(jnp.int32, (1, n, n), 1)
    c = lax.broadcasted_iota(jnp.int32, (1, n, n), 2)
    t = jnp.where(r == c, 1.0, jnp.where((r // 2 == c // 2) & (r > c), a_strict, 0.0))
    a_b = a_strict.astype(BF16)
    zero = jnp.zeros((), BF16)
    m = 2
    while m < n:
        off = (r // (2 * m) == c // (2 * m)) & (r % (2 * m) >= m) & (c % (2 * m) < m)
        tb = t.astype(BF16)
        t = t + _bnn(tb, _bnn(jnp.where(off, a_b, zero), tb).astype(BF16))
        m *= 2
    return t


def _to_heads(x):
    n, rows, _ = x.shape
    parts = [x[:, :, h * HEAD_DIM:(h + 1) * HEAD_DIM] for h in range(HEADS)]
    return jnp.stack(parts, axis=1).reshape(n * HEADS, rows, HEAD_DIM)


def _from_heads(x):
    nh, rows, _ = x.shape
    x = x.reshape(nh // HEADS, HEADS, rows, HEAD_DIM)
    return jnp.concatenate([x[:, h] for h in range(HEADS)], axis=-1)


def _rwkv_chunk_kernel(*refs, ts, chunk, has_vres):
    (sr_ref, sk_ref, sv_ref, sl_ref, mu_ref, w0_ref, w2_ref, a0_ref, a2_ref, g2_ref, kk_ref, ka_ref, rk_ref,
     ones_ref) = refs[:14]
    refs = refs[14:]
    if has_vres:
        vf_ref, v0_ref, v1_ref, v2_ref = refs[:4]
        refs = refs[4:]
    mm_out, gm_out, qm_out, y0_out, bonus_out, g_out, v_out, carry_ref = refs

    @pl.when(pl.program_id(1) == 0)
    def _():
        carry_ref[...] = jnp.zeros_like(carry_ref)

    row0 = lax.broadcasted_iota(jnp.int32, (ts, GROUP), 0) == 0

    def shift_mix(ref, idx):
        s = ref[0]
        prev = jnp.where(row0, carry_ref[idx:idx + 1, :], pltpu.roll(s, 1, axis=0))
        carry_ref[idx:idx + 1, :] = s[ts - 1:ts, :]
        return s + (prev - s) * mu_ref[idx:idx + 1, :]

    r = shift_mix(sr_ref, 0)
    k = shift_mix(sk_ref, 1)
    v = shift_mix(sv_ref, 2)
    low = shift_mix(sl_ref, 3)

    zw = w0_ref[...] + _dot(jnp.tanh(low).astype(BF16), w2_ref[...])
    lw = -math.exp(-0.5) * _sigmoid(zw)
    a = _sigmoid(a0_ref[...] + _dot(low.astype(BF16), a2_ref[...]))
    g_out[0] = _dot(_sigmoid(low).astype(BF16), g2_ref[...])
    if has_vres:
        mix = _dot(_dot(v.astype(BF16), v1_ref[...]).astype(BF16), v2_ref[...])
        v = v + (vf_ref[0] - v) * _sigmoid(v0_ref[...] + mix)
    v_out[0] = v
    kk = k * kk_ref[...]
    ss = _dot_sel(kk * kk, ones_ref[...], split="a")
    kk = kk / jnp.maximum(jnp.sqrt(ss), 1e-12)
    k = k * (1.0 + (a - 1.0) * ka_ref[...])
    bonus_out[0] = _dot_sel(r * k * rk_ref[...], ones_ref[...], split="a") * v
    a_vec, b_vec = -kk, kk * a

    c = chunk
    rr = lax.broadcasted_iota(jnp.int32, (ts, ts), 0)
    cc = lax.broadcasted_iota(jnp.int32, (ts, ts), 1)
    chunk_tri = ((rr >= cc) & (rr // c == cc // c)).astype(F32)
    cum_all = _dot_sel(chunk_tri, lw, split="b")
    strict = _tri(c, strict=True)[None]
    incl = _tri(c)[None]
    eye = (lax.broadcasted_iota(jnp.int32, (1, HEAD_DIM, HEAD_DIM), 1)
           == lax.broadcasted_iota(jnp.int32, (1, HEAD_DIM, HEAD_DIM), 2)).astype(F32)

    nq = ts // c
    per_chunk = lambda t: t.reshape(nq, c, GROUP)
    cum = per_chunk(cum_all)
    rq, kq, vq, aq, bq = (per_chunk(t) for t in (r, k, v, a_vec, b_vec))
    cum_ex = cum - per_chunk(lw)
    mid = cum[:, c // 2 - 1:c // 2, :]
    tot = cum[:, c - 1:c, :]
    e_fwd = jnp.exp(cum - mid)
    e_bwd = jnp.exp(mid - cum)
    e_end = jnp.exp(tot - cum)
    mxu = lambda t: _to_heads(t.astype(BF16))
    r_rel, k_rel = mxu(rq * e_fwd), mxu(kq * e_bwd)
    a_rel, b_rel = mxu(aq * jnp.exp(cum_ex - mid)), mxu(bq * e_bwd)
    a_abs, r_abs = mxu(aq * jnp.exp(cum_ex)), rq * jnp.exp(cum)
    k_end, b_end = mxu(kq * e_end), mxu(bq * e_end)
    gam = _to_heads(jnp.exp(tot))
    vh = mxu(vq)

    ar_rel = jnp.concatenate([a_rel, r_rel], axis=1)
    s_b, s_k = _bnt(ar_rel, b_rel), _bnt(ar_rel, k_rel)
    a_ab = jnp.where(strict, s_b[:, :c], 0.0)
    zero = jnp.zeros((), BF16)
    a_ak = jnp.where(strict, s_k[:, :c].astype(BF16), zero)
    a_rb = jnp.where(incl, s_b[:, c:].astype(BF16), zero)
    a_rk = jnp.where(incl, s_k[:, c:].astype(BF16), zero)
    t_inv = _unit_lower_inverse(a_ab, c).astype(BF16)
    pu = _bnn(t_inv, jnp.concatenate([a_abs, _bnn(a_ak, vh).astype(BF16)], axis=-1)).astype(BF16)
    z = _bnn(a_rb, pu)
    y0 = z[..., HEAD_DIM:] + _bnn(a_rk, vh)
    xtb = _btn(pu, b_end)
    mm_mat = eye * gam + xtb[:, :HEAD_DIM]
    gm = xtb[:, HEAD_DIM:] + _btn(vh, k_end)
    dense = lambda t: _from_heads(t).reshape(ts, GROUP)
    mm_out[0] = dense(mm_mat).astype(mm_out.dtype)
    gm_out[0] = dense(gm)
    qm_out[0] = (dense(z[..., :HEAD_DIM]) + r_abs.reshape(ts, GROUP)).astype(qm_out.dtype)
    y0_out[0] = dense(y0)


def rwkv_chunk(proj3, mu4, w0, w2p, a0, a2p, g2p, k_k, k_a, r_k, head_ones, vres, *, ts, chunk):
    b, s, _ = proj3.shape
    slab = lambda off: pl.BlockSpec((1, ts, GROUP), lambda i, j: (i, j, off // GROUP))
    vec = lambda: pl.BlockSpec((1, GROUP), lambda i, j: (0, 0))
    full = lambda shape: pl.BlockSpec(shape, lambda i, j: tuple(0 for _ in shape))
    tok = pl.BlockSpec((1, ts, GROUP), lambda i, j: (i, j, 0))
    in_specs = [slab(RWKV_R), slab(RWKV_K), slab(RWKV_V), slab(RWKV_LR), full((4, GROUP)),
                vec(), full((GROUP, GROUP)), vec(), full((GROUP, GROUP)), full((GROUP, GROUP)), vec(), vec(), vec(),
                full((GROUP, GROUP))]
    args = [proj3, proj3, proj3, proj3, mu4, w0.reshape(1, GROUP), w2p, a0.reshape(1, GROUP), a2p, g2p,
            k_k.reshape(1, GROUP), k_a.reshape(1, GROUP), r_k.reshape(1, GROUP), head_ones]
    if vres is not None:
        v_first, v0, v1p, v2p = vres
        in_specs += [tok, vec(), full((GROUP, 128)), full((128, GROUP))]
        args += [v_first, v0.reshape(1, GROUP), v1p, v2p]
    return pl.pallas_call(
        functools.partial(_rwkv_chunk_kernel, ts=ts, chunk=chunk, has_vres=vres is not None),
        grid=(b, s // ts),
        in_specs=in_specs,
        out_specs=[tok] * 7,
        out_shape=[jax.ShapeDtypeStruct((b, s, GROUP), dt) for dt in (BF16, F32, BF16, F32, F32, F32, F32)],
        scratch_shapes=[pltpu.VMEM((4, GROUP), F32)],
        compiler_params=_cparams(("parallel", "arbitrary")),
        name="rwkv_chunk",
    )(*args)


def _rwkv_state_kernel(mm_ref, gm_ref, qm_ref, y0_ref, bonus_ref, g_ref, gw_ref, gb_ref, o_ref, state_ref, *, chunk):
    @pl.when(pl.program_id(0) == 0)
    def _():
        state_ref[...] = jnp.zeros_like(state_ref)

    state = state_ref[...]
    for c in range(mm_ref.shape[1] // chunk):
        rows = slice(c * chunk, (c + 1) * chunk)
        s_hi, s_lo = _split_bf16(state)
        qm, mm = _to_heads(qm_ref[:, rows, :]), _to_heads(mm_ref[:, rows, :])
        y = _bnt(qm, s_hi) + _bnt(qm, s_lo) + _to_heads(y0_ref[:, rows, :])
        state = _bnn(s_hi, mm) + _bnn(s_lo, mm) + _to_heads(gm_ref[:, rows, :])
        mu = jnp.mean(y, axis=-1, keepdims=True)
        var = jnp.mean(jnp.square(y - mu), axis=-1, keepdims=True)
        yn = _from_heads((y - mu) * lax.rsqrt(var + RWKV_GN_EPS)) * gw_ref[...] + gb_ref[...]
        o_ref[:, rows, :] = ((yn + bonus_ref[:, rows, :]) * g_ref[:, rows, :]).astype(o_ref.dtype)
    state_ref[...] = state


def rwkv_state(mm, gm, qm, y0, bonus, g, gn_w, gn_b, *, ts, chunk):
    bsz, s, _ = mm.shape
    tok = pl.BlockSpec((bsz, ts, GROUP), lambda j: (0, j, 0))
    vec = pl.BlockSpec((1, GROUP), lambda j: (0, 0))
    return pl.pallas_call(
        functools.partial(_rwkv_state_kernel, chunk=chunk),
        grid=(s // ts,),
        in_specs=[tok] * 6 + [vec] * 2,
        out_specs=tok,
        out_shape=jax.ShapeDtypeStruct((bsz, s, GROUP), BF16),
        scratch_shapes=[pltpu.VMEM((bsz * HEADS, HEAD_DIM, HEAD_DIM), F32)],
        compiler_params=_cparams(("arbitrary",)),
        name="rwkv_state",
    )(mm, gm, qm, y0, bonus, g, gn_w.reshape(1, GROUP), gn_b.reshape(1, GROUP))


def _ret_kernel(q_ref, k_ref, v_ref, g_ref, cos_ref, sin_ref, dmat_ref, xi_ref, zeta_ref, cd_ref, gw_ref, o_ref,
                state_ref, *, ts, chunk):
    @pl.when(pl.program_id(1) == 0)
    def _():
        state_ref[...] = jnp.zeros_like(state_ref)

    lane = lax.broadcasted_iota(jnp.int32, (ts, GROUP), 1)
    first_half = (lane % HEAD_DIM) < (HEAD_DIM // 2)
    cos, sin = cos_ref[...], sin_ref[...]

    def rotary(t):
        partner = jnp.where(first_half, pltpu.roll(t, GROUP - HEAD_DIM // 2, axis=1),
                            pltpu.roll(t, HEAD_DIM // 2, axis=1))
        return t * cos + partner * sin

    nq = ts // chunk
    per_chunk = lambda t: t.reshape(nq, chunk, GROUP)
    mxu = lambda t: _to_heads(t).astype(BF16)
    q = per_chunk(rotary(q_ref[0]))
    k = per_chunk(rotary(k_ref[0]) * (HEAD_DIM ** -0.5))
    qb, kb, vb = mxu(q), mxu(k), mxu(per_chunk(v_ref[0]))
    q_cross, k_decay = mxu(q * xi_ref[...]), mxu(k * zeta_ref[...])
    inner = _bnt(qb, kb).reshape(nq, HEADS, chunk, chunk) * dmat_ref[...]
    intra = _bnn(inner.reshape(nq * HEADS, chunk, chunk).astype(BF16), vb)
    kv = _btn(k_decay, vb)

    decay = _to_heads(cd_ref[...][None])
    state = state_ref[...]
    incoming = []
    for c in range(nq):
        incoming.append(state)
        state = state * decay + kv[c * HEADS:(c + 1) * HEADS]
    state_ref[...] = state
    o = intra + _bnn(q_cross, jnp.concatenate(incoming, axis=0).astype(BF16))
    o = _from_heads(o * lax.rsqrt(jnp.mean(o * o, axis=-1, keepdims=True) + NORM_EPS)).reshape(ts, GROUP)
    g = g_ref[0]
    o_ref[0] = (o * gw_ref[...] * (g * _sigmoid(g))).astype(o_ref.dtype)


def retention(proj3, cos_t, sin_t, dmat, xi, zeta, cd, gn_w, *, ts, chunk):
    b, s, _ = proj3.shape
    slab = lambda off: pl.BlockSpec((1, ts, GROUP), lambda i, j: (i, j, off // GROUP))
    full = lambda shape: pl.BlockSpec(shape, lambda i, j: tuple(0 for _ in shape))
    return pl.pallas_call(
        functools.partial(_ret_kernel, ts=ts, chunk=chunk),
        grid=(b, s // ts),
        in_specs=[slab(RET_Q), slab(RET_K), slab(RET_V), slab(RET_G),
                  pl.BlockSpec((ts, GROUP), lambda i, j: (j, 0)),
                  pl.BlockSpec((ts, GROUP), lambda i, j: (j, 0)),
                  full((HEADS, chunk, chunk)), full((chunk, GROUP)), full((chunk, GROUP)),
                  full((1, GROUP)), full((1, GROUP))],
        out_specs=pl.BlockSpec((1, ts, GROUP), lambda i, j: (i, j, 0)),
        out_shape=jax.ShapeDtypeStruct((b, s, GROUP), BF16),
        scratch_shapes=[pltpu.VMEM((HEADS, HEAD_DIM, HEAD_DIM), F32)],
        compiler_params=_cparams(("parallel", "arbitrary")),
        name="retention",
    )(proj3, proj3, proj3, proj3, cos_t, sin_t, dmat, xi, zeta, cd, gn_w.reshape(1, GROUP))


def _retention_tables(s, chunk):
    half = HEAD_DIM // 2
    inv = 1.0 / (RET_THETA ** jnp.linspace(0.0, 1.0, half, dtype=F32))
    ang = jnp.arange(s, dtype=F32)[:, None] * inv[None, :]
    cos, sin = jnp.cos(ang), jnp.sin(ang)
    cos_t = jnp.tile(jnp.concatenate([cos, cos], axis=-1), (1, HEADS))
    sin_t = jnp.tile(jnp.concatenate([-sin, sin], axis=-1), (1, HEADS))
    lg = jnp.log(1.0 - 2.0 ** (-5.0 - jnp.arange(HEADS, dtype=F32)))
    n = jnp.arange(chunk, dtype=F32)
    diff = n[:, None] - n[None, :]
    dmat = jnp.where(diff >= 0, jnp.exp(lg[:, None, None] * jnp.maximum(diff, 0.0)), 0.0)
    zeta = jnp.exp(lg[:, None] * (chunk - 1.0 - n)[None, :])
    xi = jnp.exp(lg[:, None] * (n + 1.0)[None, :])
    per_lane = lambda t: jnp.repeat(t.T, HEAD_DIM, axis=1)
    cd = jnp.repeat(jnp.exp(lg * chunk), HEAD_DIM)[None, :]
    return cos_t, sin_t, dmat, per_lane(xi), per_lane(zeta), cd


MIX_XATTN_PARTS = 4


def _interleave(*step_generators):
    live = list(step_generators)
    while live:
        for gen in list(live):
            try:
                next(gen)
            except StopIteration:
                live.remove(gen)


def _mix_xattn_steps(rows, m_refs, x_ref, kv_ref, wout_ref, wq_ref, wo_ref, gmix_ref, gpre_ref, gpost_ref, o_ref):
    acc = _dot(m_refs[0][0, rows, :], wout_ref[0:GROUP, :])
    for idx, m_ref in enumerate(m_refs[1:], start=1):
        acc = acc + _dot(m_ref[0, rows, :], wout_ref[idx * GROUP:(idx + 1) * GROUP, :])
    yield
    x = x_ref[0, rows, :] + _rms(acc, gmix_ref[...])
    xn = _rms(x, gpre_ref[...]).astype(BF16)
    q = (_dot(xn, wq_ref[...]) * (XATTN_HEAD_DIM ** -0.5)).astype(BF16)
    yield
    heads = lambda t, off: jnp.stack([t[:, off + h * XATTN_HEAD_DIM:off + (h + 1) * XATTN_HEAD_DIM]
                                      for h in range(XATTN_HEADS)])
    kv = kv_ref[0]
    s = _bnt(heads(q, 0), heads(kv, 0))
    yield
    e = jnp.exp(s - jnp.max(s, axis=-1, keepdims=True))
    p = e / jnp.sum(e, axis=-1, keepdims=True)
    o = _bnn(p.astype(BF16), heads(kv, D_MODEL)).astype(BF16)
    yield
    o = jnp.concatenate([o[h] for h in range(XATTN_HEADS)], axis=-1)
    o_ref[0, rows, :] = x + _rms(_dot(o, wo_ref[...]), gpost_ref[...])


def _mix_xattn_kernel(m0_ref, m1_ref, m2_ref, m3_ref, x_ref, kv_ref, wout_ref, wq_ref, wo_ref, gmix_ref, gpre_ref,
                      gpost_ref, o_ref):
    n = x_ref.shape[1] // MIX_XATTN_PARTS
    _interleave(*[_mix_xattn_steps(slice(part * n, (part + 1) * n), (m0_ref, m1_ref, m2_ref, m3_ref), x_ref,
                                   kv_ref, wout_ref, wq_ref, wo_ref, gmix_ref, gpre_ref, gpost_ref, o_ref)
                  for part in range(MIX_XATTN_PARTS)])


def mix_xattn(mixed, x3, kv, w_out, wq, wo, layer, g_mix, g_pre, g_post, *, tm):
    b, s, d = x3.shape
    m = kv.shape[1]
    tok = lambda width: pl.BlockSpec((1, tm, width), lambda bi, i: (bi, i, 0))
    weight = lambda: pl.BlockSpec((None, d, d), lambda bi, i: (layer, 0, 0))
    vec = lambda: pl.BlockSpec((1, d), lambda bi, i: (0, 0))
    return pl.pallas_call(
        _mix_xattn_kernel,
        grid=(b, s // tm),
        in_specs=[tok(GROUP)] * 4 + [tok(d), pl.BlockSpec((1, m, 2 * d), lambda bi, i: (bi, 0, 0)),
                                     weight(), weight(), weight(), vec(), vec(), vec()],
        out_specs=tok(d),
        out_shape=jax.ShapeDtypeStruct((b, s, d), F32),
        compiler_params=_cparams(("parallel", "arbitrary")),
        name="mix_xattn",
    )(*mixed, x3, kv, w_out, wq, wo, g_mix.reshape(1, d), g_pre.reshape(1, d), g_post.reshape(1, d))


def _mlp_kernel(x_ref, w1_ref, w2_ref, gpre_ref, gpost_ref, o_ref, *, ff_tile):
    x = x_ref[...]
    xn = _rms(x, gpre_ref[...]).astype(BF16)
    d_ff = w1_ref.shape[1]
    acc = None
    for c in range(d_ff // ff_tile):
        hid = jnp.square(jnp.maximum(_dot(xn, w1_ref[:, c * ff_tile:(c + 1) * ff_tile]), 0.0)).astype(BF16)
        part = _dot(hid, w2_ref[c * ff_tile:(c + 1) * ff_tile, :])
        acc = part if acc is None else acc + part
    o_ref[...] = x + _rms(acc, gpost_ref[...])


def mlp(x, w1, w2, layer, g_pre, g_post, *, tm, ff_tile):
    n, d = x.shape
    d_ff = w1.shape[2]
    return pl.pallas_call(
        functools.partial(_mlp_kernel, ff_tile=ff_tile),
        grid=(n // tm,),
        in_specs=[pl.BlockSpec((tm, d), lambda i: (i, 0)),
                  pl.BlockSpec((None, d, d_ff), lambda i: (layer, 0, 0), pipeline_mode=pl.Buffered(1)),
                  pl.BlockSpec((None, d_ff, d), lambda i: (layer, 0, 0), pipeline_mode=pl.Buffered(1)),
                  pl.BlockSpec((1, d), lambda i: (0, 0)),
                  pl.BlockSpec((1, d), lambda i: (0, 0))],
        out_specs=pl.BlockSpec((tm, d), lambda i: (i, 0)),
        out_shape=jax.ShapeDtypeStruct((n, d), F32),
        compiler_params=_cparams(("parallel",)),
        name="mlp",
    )(x, w1, w2, g_pre.reshape(1, d), g_post.reshape(1, d))


def _block_diag(w):
    h, n, _ = w.shape
    eye = jnp.eye(h, dtype=w.dtype)
    return (eye[:, None, :, None] * w[:, :, None, :]).reshape(h * n, h * n)


def _pad_rows(w, start, total):
    return jnp.zeros((total, w.shape[1]), w.dtype).at[start:start + w.shape[0]].set(w)


def _tile(n, pref):
    return pref if n % pref == 0 else n


def kernel(x, mem, norm_mix_pre, norm_mix_post, norm_xa_pre, norm_xa_post, norm_mem, norm_mlp_pre, norm_mlp_post, w_in, w_out, fox_f_bias, lru_conv_w, lru_conv_b, lru_ra_w, lru_ra_b, lru_ri_w, lru_ri_b, lru_lambda, rwkv_mu, rwkv_w0, rwkv_w2, rwkv_a0, rwkv_a2, rwkv_g2, rwkv_k_k, rwkv_k_a, rwkv_r_k, rwkv_gn_w, rwkv_gn_b, rwkv_v0, rwkv_v1, rwkv_v2, ret_gn_w, xa_wq, xa_wk, xa_wv, xa_wo, mlp_w1, mlp_w2):
    bsz, seq, d = x.shape
    depth = w_in.shape[0]
    n_tok = bsz * seq
    mem_len = mem.shape[1]
    tm = _tile(n_tok, 512)
    tq = _tile(seq, 512)
    ts = _tile(seq, 512)
    ret_tables = _retention_tables(seq, RET_CHUNK)
    head_ones = _block_diag(jnp.ones((HEADS, HEAD_DIM, HEAD_DIM), BF16))

    w_in_t = w_in.astype(BF16).swapaxes(1, 2)
    w_in_b = jnp.concatenate([w_in_t[:, :FOX_REAL], jnp.zeros((depth, LRU_X - FOX_REAL, d), BF16),
                              w_in_t[:, FOX_REAL:]], axis=1)
    w_out_b = w_out.astype(BF16)
    wq_b, wo_b = xa_wq.astype(BF16), xa_wo.astype(BF16)
    wkv_b = jnp.concatenate([xa_wk, xa_wv], axis=-1).astype(BF16)
    w1_b, w2_b = mlp_w1.astype(BF16), mlp_w2.astype(BF16)

    x2 = x.reshape(n_tok, d)
    v_first = None
    for l in range(depth):
        proj = in_proj(x2, norm_mix_pre[l], w_in_b, l, tm=tm, tn=IN_PAD // 7)
        proj3 = proj.reshape(bsz, seq, IN_PAD)

        f_bias = jnp.zeros((1, GROUP), F32).at[0, :HEADS].set(fox_f_bias[l])
        qa, ka, vt = fox_prep(proj3, f_bias, tc=tq)
        fox_out = fox_attention(qa, ka, vt)

        lru_out = rglru(proj3, lru_conv_w[l], lru_conv_b[l], _block_diag(lru_ra_w[l]).astype(BF16), lru_ra_b[l],
                        _block_diag(lru_ri_w[l]).astype(BF16), lru_ri_b[l], lru_lambda[l], ts=ts)

        vres = None
        if l > 0:
            vres = (v_first, rwkv_v0[l - 1],
                    jnp.pad(rwkv_v1[l - 1], ((0, 0), (0, 128 - RWKV_V_RANK))).astype(BF16),
                    _pad_rows(rwkv_v2[l - 1], 0, 128).astype(BF16))
        mm_, gm_, qm_, y0_, bonus_, g_, v_ = rwkv_chunk(
            proj3, rwkv_mu[l].reshape(4, GROUP), rwkv_w0[l],
            _pad_rows(rwkv_w2[l], 0, GROUP).astype(BF16), rwkv_a0[l],
            _pad_rows(rwkv_a2[l], RWKV_W_RANK, GROUP).astype(BF16),
            _pad_rows(rwkv_g2[l], RWKV_W_RANK + RWKV_A_RANK, GROUP).astype(BF16),
            rwkv_k_k[l], rwkv_k_a[l], rwkv_r_k[l], head_ones, vres, ts=_tile(seq, 8 * RWKV_CHUNK), chunk=RWKV_CHUNK)
        if l == 0:
            v_first = v_
        rwkv_out = rwkv_state(mm_, gm_, qm_, y0_, bonus_, g_, rwkv_gn_w[l], rwkv_gn_b[l],
                              ts=_tile(seq, 4 * RWKV_CHUNK), chunk=RWKV_CHUNK)

        ret_out = retention(proj3, *ret_tables, ret_gn_w[l], ts=_tile(seq, 8 * RET_CHUNK), chunk=RET_CHUNK)

        kv = norm_matmul(mem.reshape(bsz * mem_len, d), norm_mem[l], wkv_b, l, tm=_tile(bsz * mem_len, 512),
                         tn=1024, out_dtype=BF16).reshape(bsz, mem_len, 2 * d)
        x2 = mix_xattn((fox_out, lru_out, rwkv_out, ret_out), x2.reshape(bsz, seq, d), kv, w_out_b, wq_b, wo_b, l,
                       norm_mix_post[l], norm_xa_pre[l], norm_xa_post[l], tm=_tile(seq, 1024)).reshape(n_tok, d)

        x2 = mlp(x2, w1_b, w2_b, l, norm_mlp_pre[l], norm_mlp_post[l], tm=tm, ff_tile=1024)
    return x2.reshape(bsz, seq, d)
```

```python
import functools
import math

import jax
import jax.numpy as jnp
from jax import lax
from jax.experimental import pallas as pl
from jax.experimental.pallas import tpu as pltpu

F32 = jnp.float32
BF16 = jnp.bfloat16

D_MODEL = 1024
GROUP = 256
HEADS = 4
HEAD_DIM = 64
CONV_WIDTH = 4
LRU_C = 8.0
RET_THETA = 10000.0
RET_CHUNK = 128
RWKV_W_RANK, RWKV_A_RANK, RWKV_G_RANK, RWKV_V_RANK = 64, 64, 128, 32
RWKV_GN_EPS = 64e-5
XATTN_HEADS = 4
XATTN_HEAD_DIM = D_MODEL // XATTN_HEADS
NORM_EPS = 1e-6
NEG_BIG = -1e30

FOX_Q, FOX_K, FOX_V, FOX_F = 0, 256, 512, 768
FOX_REAL = 3 * GROUP + HEADS
FOX_WIDTH = 896
LRU_X, LRU_Y = 0, 256
RWKV_R, RWKV_K, RWKV_V, RWKV_LR = 512, 768, 1024, 1280
RET_Q, RET_K, RET_V, RET_G = 1536, 1792, 2048, 2304
MIX_WIDTH = 2560

VMEM_LIMIT = 56 * 1024 * 1024
RWKV_CHUNK = 64


def _cparams(sem):
    return pltpu.CompilerParams(dimension_semantics=sem, vmem_limit_bytes=VMEM_LIMIT)


def _rms(x, g):
    return x * lax.rsqrt(jnp.mean(x * x, axis=-1, keepdims=True) + NORM_EPS) * g


def _log_sigmoid(x):
    return jnp.minimum(x, 0.0) - jnp.log1p(jnp.exp(-jnp.abs(x)))


def _sigmoid(x):
    return 1.0 / (1.0 + jnp.exp(-x))


def _dot(a, b, **kw):
    return jnp.dot(a, b, preferred_element_type=F32, **kw)


def _dot_nt(a, b, **kw):
    return lax.dot_general(a, b, (((1,), (1,)), ((), ())), preferred_element_type=F32, **kw)


def _split_bf16(x):
    hi = x.astype(BF16)
    return hi, (x - hi.astype(F32)).astype(BF16)


def _interleave(*step_generators):
    live = list(step_generators)
    while live:
        for gen in list(live):
            try:
                next(gen)
            except StopIteration:
                live.remove(gen)


def _bf16_terms(x, n):
    terms = []
    for _ in range(n - 1):
        t = x.astype(BF16)
        terms.append(t)
        x = x - t.astype(F32)
    return terms + [x.astype(BF16)]


def _dot_sel(a, b, *, split, n=3):
    if split == "a":
        parts = [_dot(t, b.astype(BF16)) for t in _bf16_terms(a, n)]
    else:
        parts = [_dot(a.astype(BF16), t) for t in _bf16_terms(b, n)]
    out = parts[-1]
    for p in reversed(parts[:-1]):
        out = out + p
    return out


def _tri(n, strict=False):
    r = lax.broadcasted_iota(jnp.int32, (n, n), 0)
    c = lax.broadcasted_iota(jnp.int32, (n, n), 1)
    return (r > c) if strict else (r >= c)


def _norm_matmul_kernel(x_ref, g_ref, w_ref, o_ref, xn_ref):
    @pl.when(pl.program_id(1) == 0)
    def _():
        xn_ref[...] = _rms(x_ref[...], g_ref[...]).astype(BF16)

    o_ref[...] = _dot(xn_ref[...], w_ref[...]).astype(o_ref.dtype)


def norm_matmul(x, g, w, layer, *, tm, tn, out_dtype):
    n, d = x.shape
    width = w.shape[2]
    return pl.pallas_call(
        _norm_matmul_kernel,
        grid=(n // tm, width // tn),
        in_specs=[pl.BlockSpec((tm, d), lambda i, j: (i, 0)),
                  pl.BlockSpec((1, d), lambda i, j: (0, 0)),
                  pl.BlockSpec((None, d, tn), lambda i, j: (layer, 0, j))],
        out_specs=pl.BlockSpec((tm, tn), lambda i, j: (i, j)),
        out_shape=jax.ShapeDtypeStruct((n, width), out_dtype),
        scratch_shapes=[pltpu.VMEM((tm, d), BF16)],
        compiler_params=_cparams(("parallel", "arbitrary")),
        name="norm_matmul",
    )(x, g.reshape(1, d), w)


def _in_proj_kernel(x_ref, g_ref, wf_ref, wm_ref, of_ref, om_ref, *, tn):
    xn = _rms(x_ref[...], g_ref[...]).astype(BF16)
    of_ref[...] = _dot_nt(xn, wf_ref[...])
    for c in range(wm_ref.shape[0] // tn):
        om_ref[:, c * tn:(c + 1) * tn] = _dot_nt(xn, wm_ref[c * tn:(c + 1) * tn, :])


def in_proj(x, g, w_fox, w_mix, layer, *, tm, tn):
    n, d = x.shape
    weight = lambda w: pl.BlockSpec((None, w.shape[1], d), lambda i: (layer, 0, 0), pipeline_mode=pl.Buffered(1))
    out = lambda w: pl.BlockSpec((tm, w.shape[1]), lambda i: (i, 0))
    return pl.pallas_call(
        functools.partial(_in_proj_kernel, tn=tn),
        grid=(n // tm,),
        in_specs=[pl.BlockSpec((tm, d), lambda i: (i, 0)), pl.BlockSpec((1, d), lambda i: (0, 0)),
                  weight(w_fox), weight(w_mix)],
        out_specs=[out(w_fox), out(w_mix)],
        out_shape=[jax.ShapeDtypeStruct((n, w.shape[1]), F32) for w in (w_fox, w_mix)],
        compiler_params=_cparams(("parallel",)),
        name="in_proj",
    )(x, g.reshape(1, d), w_fox, w_mix)


FOX_AUG = 128
FOX_VROWS = HEAD_DIM + 16
LOG2E = 1.4426950408889634


def _fox_prep_kernel(q_ref, k_ref, v_ref, f_ref, b_ref, qa_ref, ka_ref, vt_ref, carry_ref):
    @pl.when(pl.program_id(1) == 0)
    def _():
        carry_ref[...] = jnp.zeros_like(carry_ref)

    lf = _log_sigmoid(f_ref[0] + b_ref[...])
    tc = lf.shape[0]
    cum = _dot_sel(_tri(tc), lf, split="b") + carry_ref[...]
    carry_ref[...] = cum[tc - 1:tc, :]

    lane = lax.broadcasted_iota(jnp.int32, (tc, FOX_AUG), 1)
    feat = lane < HEAD_DIM
    ones = jnp.where((lane >= HEAD_DIM) & (lane < HEAD_DIM + 3), 1.0, 0.0)
    q = q_ref[0] * (HEAD_DIM ** -0.5 * LOG2E)
    k = k_ref[0]
    for h in range(HEADS):
        pair = slice((h // 2) * FOX_AUG, (h // 2 + 1) * FOX_AUG)
        qt, kt = q[:, pair], k[:, pair]
        if h % 2:
            qt, kt = pltpu.roll(qt, HEAD_DIM, axis=1), pltpu.roll(kt, HEAD_DIM, axis=1)
        neg_c = jnp.broadcast_to(cum[:, h:h + 1] * -LOG2E, (tc, FOX_AUG))
        c_hi = neg_c.astype(BF16).astype(F32)
        rest = neg_c - c_hi
        c_mid = rest.astype(BF16).astype(F32)
        bias = jnp.where(lane == HEAD_DIM, c_hi, jnp.where(lane == HEAD_DIM + 1, c_mid,
                                                           jnp.where(lane == HEAD_DIM + 2, rest - c_mid, 0.0)))
        qa_ref[0, h] = jnp.where(feat, qt, ones).astype(BF16)
        ka_ref[0, h] = jnp.where(feat, kt, bias).astype(BF16)
    ones_row = lax.broadcasted_iota(jnp.int32, (HEADS, FOX_VROWS - HEAD_DIM, tc), 1) == 0
    vt = jnp.concatenate([v_ref[0].T.reshape(HEADS, HEAD_DIM, tc), ones_row.astype(F32)], axis=1)
    vt_ref[0, :, 0] = vt.astype(BF16)


def fox_prep(proj3, f_bias_pad, *, tc):
    b, s, _ = proj3.shape
    slab = lambda off: pl.BlockSpec((1, tc, GROUP), lambda i, j: (i, j, off // GROUP))
    aug = pl.BlockSpec((1, HEADS, tc, FOX_AUG), lambda i, j: (i, 0, j, 0))
    return pl.pallas_call(
        _fox_prep_kernel,
        grid=(b, s // tc),
        in_specs=[slab(FOX_Q), slab(FOX_K), slab(FOX_V),
                  pl.BlockSpec((1, tc, FOX_WIDTH - FOX_F), lambda i, j: (i, j, FOX_F // (FOX_WIDTH - FOX_F))),
                  pl.BlockSpec((1, FOX_WIDTH - FOX_F), lambda i, j: (0, 0))],
        out_specs=[aug, aug, pl.BlockSpec((1, HEADS, 1, FOX_VROWS, tc), lambda i, j: (i, 0, j, 0, 0))],
        out_shape=[jax.ShapeDtypeStruct((b, HEADS, s, FOX_AUG), BF16),
                   jax.ShapeDtypeStruct((b, HEADS, s, FOX_AUG), BF16),
                   jax.ShapeDtypeStruct((b, HEADS, s // tc, FOX_VROWS, tc), BF16)],
        scratch_shapes=[pltpu.VMEM((1, FOX_WIDTH - FOX_F), F32)],
        compiler_params=_cparams(("parallel", "arbitrary")),
        name="fox_prep",
    )(proj3, proj3, proj3, proj3, f_bias_pad)


def _fox_attn_kernel(q_ref, k_ref, v_ref, o_ref, s_ref, *, t):
    i = pl.program_id(1)
    qa = q_ref[0]

    def scores(j):
        return _bnt(k_ref[0, :, pl.ds(pl.multiple_of(j * t, t), t), :], qa)

    def update(carry, slot, j):
        m, acc = carry
        s = s_ref[slot]
        m_new = jnp.maximum(m, jnp.max(s, axis=1, keepdims=True))
        p = jnp.exp2(s - m_new).astype(BF16)
        return m_new, jnp.exp2(m - m_new) * acc + _bnn(v_ref[0, :, j], p)

    key = lax.broadcasted_iota(jnp.int32, (1, t, t), 1)
    qry = lax.broadcasted_iota(jnp.int32, (1, t, t), 2)
    s_ref[0] = jnp.where(key <= qry, scores(i), NEG_BIG)
    carry = (jnp.full((HEADS, 1, t), NEG_BIG, F32), jnp.zeros((HEADS, FOX_VROWS, t), F32))

    def pair(jj, carry):
        s_ref[1] = scores(2 * jj)
        carry = update(carry, 0, jnp.where(jj == 0, i, 2 * jj - 1))
        s_ref[0] = scores(2 * jj + 1)
        return update(carry, 1, 2 * jj)

    carry = lax.fori_loop(0, i // 2, pair, carry)
    pending = jnp.where(i < 2, i, 2 * (i // 2) - 1)

    def odd_tail(carry):
        s_ref[1] = scores(i - 1)
        return update(update(carry, 0, pending), 1, i - 1)

    _, acc = lax.cond(i % 2 == 1, odd_tail, lambda c: update(c, 0, pending), carry)
    out = acc[:, :HEAD_DIM] / acc[:, HEAD_DIM:HEAD_DIM + 1]
    o_ref[0] = out.reshape(GROUP, t).T.astype(o_ref.dtype)


def fox_attention(qa, ka, vt):
    b, h, s, _ = qa.shape
    nk, t = vt.shape[2], vt.shape[4]
    return pl.pallas_call(
        functools.partial(_fox_attn_kernel, t=t),
        grid=(b, nk),
        in_specs=[pl.BlockSpec((1, h, t, FOX_AUG), lambda bi, i: (bi, 0, i, 0)),
                  pl.BlockSpec((1, h, s, FOX_AUG), lambda bi, i: (bi, 0, 0, 0)),
                  pl.BlockSpec((1, h, nk, FOX_VROWS, t), lambda bi, i: (bi, 0, 0, 0, 0))],
        out_specs=pl.BlockSpec((1, t, GROUP), lambda bi, i: (bi, i, 0)),
        out_shape=jax.ShapeDtypeStruct((b, s, GROUP), BF16),
        scratch_shapes=[pltpu.VMEM((2, h, t, t), F32)],
        compiler_params=_cparams(("parallel", "arbitrary")),
        name="fox_attention",
    )(qa, ka, vt)


def _lru_steps(rows, x_ref, y_ref, cw_ref, cb_ref, wra_ref, bra_ref, wri_ref, bri_ref, lam_ref, o_ref, buf_ref,
               h_ref):
    ts = rows.stop - rows.start
    xb = x_ref[0, rows, :]
    buf_ref[8:8 + ts, :] = xb
    xc = cb_ref[...] + buf_ref[5:5 + ts, :] * cw_ref[0:1, :]
    for j in range(1, CONV_WIDTH):
        xc = xc + buf_ref[5 + j:5 + j + ts, :] * cw_ref[j:j + 1, :]
    buf_ref[0:8, :] = xb[ts - 8:ts, :]
    yield

    xcb = xc.astype(BF16)
    r = _sigmoid(_dot(xcb, wra_ref[...]) + bra_ref[...])
    yield
    gate_i = _sigmoid(_dot(xcb, wri_ref[...]) + bri_ref[...])
    yield
    log_a = LRU_C * r * _log_sigmoid(lam_ref[...])
    a = jnp.exp(log_a)
    z = 2.0 * log_a
    mult = jnp.sqrt(jnp.maximum(-jnp.tanh(0.5 * z) * (jnp.exp(z) + 1.0), 0.0))
    u = mult * (gate_i * xc)
    yield

    row = lax.broadcasted_iota(jnp.int32, (ts, GROUP), 0)
    pa, pb = a, u
    d = 1
    while d < ts:
        sa = pltpu.roll(pa, d, axis=0)
        sb = pltpu.roll(pb, d, axis=0)
        valid = row >= d
        pb = jnp.where(valid, pa * sb + pb, pb)
        pa = jnp.where(valid, pa * sa, pa)
        d *= 2
        yield
    hseq = pa * h_ref[...] + pb
    h_ref[...] = hseq[ts - 1:ts, :]

    y = y_ref[0, rows, :]
    gelu = 0.5 * y * (1.0 + jnp.tanh(math.sqrt(2.0 / math.pi) * (y + 0.044715 * (y * y * y))))
    o_ref[0, rows, :] = (hseq * gelu).astype(o_ref.dtype)


def _bdot(dims):
    return lambda a, b: lax.dot_general(a, b, (dims, ((0,), (0,))), preferred_element_type=F32)


_bnn = _bdot(((2,), (1,)))
_bnt = _bdot(((2,), (2,)))
_btn = _bdot(((1,), (1,)))


def _unit_lower_inverse(a_strict, n):
    r = lax.broadcasted_iota(jnp.int32, (1, n, n), 1)
    c = lax.broadcasted_iota(jnp.int32, (1, n, n), 2)
    t = jnp.where(r == c, 1.0, jnp.where((r // 2 == c // 2) & (r > c), a_strict, 0.0))
    a_b = a_strict.astype(BF16)
    zero = jnp.zeros((), BF16)
    m = 2
    while m < n:
        off = (r // (2 * m) == c // (2 * m)) & (r % (2 * m) >= m) & (c % (2 * m) < m)
        tb = t.astype(BF16)
        t = t + _bnn(tb, _bnn(jnp.where(off, a_b, zero), tb).astype(BF16))
        m *= 2
    return t


def _to_heads(x):
    n, rows, _ = x.shape
    parts = [x[:, :, h * HEAD_DIM:(h + 1) * HEAD_DIM] for h in range(HEADS)]
    return jnp.stack(parts, axis=1).reshape(n * HEADS, rows, HEAD_DIM)


def _from_heads(x):
    nh, rows, _ = x.shape
    x = x.reshape(nh // HEADS, HEADS, rows, HEAD_DIM)
    return jnp.concatenate([x[:, h] for h in range(HEADS)], axis=-1)


def _rwkv_chunk_kernel(*refs, ts, chunk, has_vres):
    (sr_ref, sk_ref, sv_ref, sl_ref, mu_ref, w0_ref, w2_ref, a0_ref, a2_ref, g2_ref, kk_ref, ka_ref, rk_ref,
     ones_ref) = refs[:14]
    refs = refs[14:]
    if has_vres:
        vf_ref, v0_ref, v1_ref, v2_ref = refs[:4]
        refs = refs[4:]
    mm_out, gm_out, qm_out, y0_out, bonus_out, g_out, v_out, carry_ref = refs

    @pl.when(pl.program_id(1) == 0)
    def _():
        carry_ref[...] = jnp.zeros_like(carry_ref)

    row0 = lax.broadcasted_iota(jnp.int32, (ts, GROUP), 0) == 0

    def shift_mix(ref, idx):
        s = ref[0]
        prev = jnp.where(row0, carry_ref[idx:idx + 1, :], pltpu.roll(s, 1, axis=0))
        carry_ref[idx:idx + 1, :] = s[ts - 1:ts, :]
        return s + (prev - s) * mu_ref[idx:idx + 1, :]

    r = shift_mix(sr_ref, 0)
    k = shift_mix(sk_ref, 1)
    v = shift_mix(sv_ref, 2)
    low = shift_mix(sl_ref, 3)

    zw = w0_ref[...] + _dot(jnp.tanh(low).astype(BF16), w2_ref[...])
    lw = -math.exp(-0.5) * _sigmoid(zw)
    a = _sigmoid(a0_ref[...] + _dot(low.astype(BF16), a2_ref[...]))
    g_out[0] = _dot(_sigmoid(low).astype(BF16), g2_ref[...])
    if has_vres:
        mix = _dot(_dot(v.astype(BF16), v1_ref[...]).astype(BF16), v2_ref[...])
        v = v + (vf_ref[0] - v) * _sigmoid(v0_ref[...] + mix)
    v_out[0] = v
    kk = k * kk_ref[...]
    ss = _dot_sel(kk * kk, ones_ref[...], split="a")
    kk = kk / jnp.maximum(jnp.sqrt(ss), 1e-12)
    k = k * (1.0 + (a - 1.0) * ka_ref[...])
    bonus_out[0] = _dot_sel(r * k * rk_ref[...], ones_ref[...], split="a") * v
    a_vec, b_vec = -kk, kk * a

    c = chunk
    rr = lax.broadcasted_iota(jnp.int32, (ts, ts), 0)
    cc = lax.broadcasted_iota(jnp.int32, (ts, ts), 1)
    chunk_tri = ((rr >= cc) & (rr // c == cc // c)).astype(F32)
    cum_all = _dot_sel(chunk_tri, lw, split="b")
    strict = _tri(c, strict=True)[None]
    incl = _tri(c)[None]
    eye = (lax.broadcasted_iota(jnp.int32, (1, HEAD_DIM, HEAD_DIM), 1)
           == lax.broadcasted_iota(jnp.int32, (1, HEAD_DIM, HEAD_DIM), 2)).astype(F32)

    nq = ts // c
    per_chunk = lambda t: t.reshape(nq, c, GROUP)
    cum = per_chunk(cum_all)
    rq, kq, vq, aq, bq = (per_chunk(t) for t in (r, k, v, a_vec, b_vec))
    cum_ex = cum - per_chunk(lw)
    mid = cum[:, c // 2 - 1:c // 2, :]
    tot = cum[:, c - 1:c, :]
    e_fwd = jnp.exp(cum - mid)
    e_bwd = jnp.exp(mid - cum)
    e_end = jnp.exp(tot - cum)
    mxu = lambda t: _to_heads(t.astype(BF16))
    r_rel, k_rel = mxu(rq * e_fwd), mxu(kq * e_bwd)
    a_rel, b_rel = mxu(aq * jnp.exp(cum_ex - mid)), mxu(bq * e_bwd)
    a_abs, r_abs = mxu(aq * jnp.exp(cum_ex)), rq * jnp.exp(cum)
    k_end, b_end = mxu(kq * e_end), mxu(bq * e_end)
    gam = _to_heads(jnp.exp(tot))
    vh = mxu(vq)

    ar_rel = jnp.concatenate([a_rel, r_rel], axis=1)
    s_b, s_k = _bnt(ar_rel, b_rel), _bnt(ar_rel, k_rel)
    a_ab = jnp.where(strict, s_b[:, :c], 0.0)
    zero = jnp.zeros((), BF16)
    a_ak = jnp.where(strict, s_k[:, :c].astype(BF16), zero)
    a_rb = jnp.where(incl, s_b[:, c:].astype(BF16), zero)
    a_rk = jnp.where(incl, s_k[:, c:].astype(BF16), zero)
    t_inv = _unit_lower_inverse(a_ab, c).astype(BF16)
    pu = _bnn(t_inv, jnp.concatenate([a_abs, _bnn(a_ak, vh).astype(BF16)], axis=-1)).astype(BF16)
    z = _bnn(a_rb, pu)
    y0 = z[..., HEAD_DIM:] + _bnn(a_rk, vh)
    xtb = _btn(pu, b_end)
    mm_mat = eye * gam + xtb[:, :HEAD_DIM]
    gm = xtb[:, HEAD_DIM:] + _btn(vh, k_end)
    dense = lambda t: _from_heads(t).reshape(ts, GROUP)
    mm_out[0] = dense(mm_mat).astype(mm_out.dtype)
    gm_out[0] = dense(gm)
    qm_out[0] = (dense(z[..., :HEAD_DIM]) + r_abs.reshape(ts, GROUP)).astype(qm_out.dtype)
    y0_out[0] = dense(y0)


def rwkv_chunk(proj3, mu4, w0, w2p, a0, a2p, g2p, k_k, k_a, r_k, head_ones, vres, *, ts, chunk):
    b, s, _ = proj3.shape
    slab = lambda off: pl.BlockSpec((1, ts, GROUP), lambda i, j: (i, j, off // GROUP))
    vec = lambda: pl.BlockSpec((1, GROUP), lambda i, j: (0, 0))
    full = lambda shape: pl.BlockSpec(shape, lambda i, j: tuple(0 for _ in shape))
    tok = pl.BlockSpec((1, ts, GROUP), lambda i, j: (i, j, 0))
    in_specs = [slab(RWKV_R), slab(RWKV_K), slab(RWKV_V), slab(RWKV_LR), full((4, GROUP)),
                vec(), full((GROUP, GROUP)), vec(), full((GROUP, GROUP)), full((GROUP, GROUP)), vec(), vec(), vec(),
                full((GROUP, GROUP))]
    args = [proj3, proj3, proj3, proj3, mu4, w0.reshape(1, GROUP), w2p, a0.reshape(1, GROUP), a2p, g2p,
            k_k.reshape(1, GROUP), k_a.reshape(1, GROUP), r_k.reshape(1, GROUP), head_ones]
    if vres is not None:
        v_first, v0, v1p, v2p = vres
        in_specs += [tok, vec(), full((GROUP, 128)), full((128, GROUP))]
        args += [v_first, v0.reshape(1, GROUP), v1p, v2p]
    return pl.pallas_call(
        functools.partial(_rwkv_chunk_kernel, ts=ts, chunk=chunk, has_vres=vres is not None),
        grid=(b, s // ts),
        in_specs=in_specs,
        out_specs=[tok] * 7,
        out_shape=[jax.ShapeDtypeStruct((b, s, GROUP), dt) for dt in (BF16, F32, BF16, F32, F32, F32, F32)],
        scratch_shapes=[pltpu.VMEM((4, GROUP), F32)],
        compiler_params=_cparams(("parallel", "arbitrary")),
        name="rwkv_chunk",
    )(*args)


def _rwkv_state_kernel(mm_ref, gm_ref, qm_ref, y0_ref, bonus_ref, g_ref, gw_ref, gb_ref, o_ref, state_ref, *, chunk):
    @pl.when(pl.program_id(0) == 0)
    def _():
        state_ref[...] = jnp.zeros_like(state_ref)

    state = state_ref[...]
    for c in range(mm_ref.shape[1] // chunk):
        rows = slice(c * chunk, (c + 1) * chunk)
        s_hi, s_lo = _split_bf16(state)
        qm, mm = _to_heads(qm_ref[:, rows, :]), _to_heads(mm_ref[:, rows, :])
        y = _bnt(qm, s_hi) + _bnt(qm, s_lo) + _to_heads(y0_ref[:, rows, :])
        state = _bnn(s_hi, mm) + _bnn(s_lo, mm) + _to_heads(gm_ref[:, rows, :])
        mu = jnp.mean(y, axis=-1, keepdims=True)
        var = jnp.mean(jnp.square(y - mu), axis=-1, keepdims=True)
        yn = _from_heads((y - mu) * lax.rsqrt(var + RWKV_GN_EPS)) * gw_ref[...] + gb_ref[...]
        o_ref[:, rows, :] = ((yn + bonus_ref[:, rows, :]) * g_ref[:, rows, :]).astype(o_ref.dtype)
    state_ref[...] = state


def rwkv_state(mm, gm, qm, y0, bonus, g, gn_w, gn_b, *, ts, chunk):
    bsz, s, _ = mm.shape
    tok = pl.BlockSpec((bsz, ts, GROUP), lambda j: (0, j, 0))
    vec = pl.BlockSpec((1, GROUP), lambda j: (0, 0))
    return pl.pallas_call(
        functools.partial(_rwkv_state_kernel, chunk=chunk),
        grid=(s // ts,),
        in_specs=[tok] * 6 + [vec] * 2,
        out_specs=tok,
        out_shape=jax.ShapeDtypeStruct((bsz, s, GROUP), BF16),
        scratch_shapes=[pltpu.VMEM((bsz * HEADS, HEAD_DIM, HEAD_DIM), F32)],
        compiler_params=_cparams(("arbitrary",)),
        name="rwkv_state",
    )(mm, gm, qm, y0, bonus, g, gn_w.reshape(1, GROUP), gn_b.reshape(1, GROUP))


def _ret_steps(q_ref, k_ref, v_ref, g_ref, cos_ref, sin_ref, dmat_ref, xi_ref, zeta_ref, cd_ref, gw_ref, o_ref,
               state_ref, *, ts, chunk):
    lane = lax.broadcasted_iota(jnp.int32, (ts, GROUP), 1)
    first_half = (lane % HEAD_DIM) < (HEAD_DIM // 2)
    cos, sin = cos_ref[...], sin_ref[...]

    def rotary(t):
        partner = jnp.where(first_half, pltpu.roll(t, GROUP - HEAD_DIM // 2, axis=1),
                            pltpu.roll(t, HEAD_DIM // 2, axis=1))
        return t * cos + partner * sin

    nq = ts // chunk
    per_chunk = lambda t: t.reshape(nq, chunk, GROUP)
    mxu = lambda t: _to_heads(t).astype(BF16)
    q = per_chunk(rotary(q_ref[0]))
    yield
    k = per_chunk(rotary(k_ref[0]) * (HEAD_DIM ** -0.5))
    yield
    qb, kb, vb = mxu(q), mxu(k), mxu(per_chunk(v_ref[0]))
    yield
    q_cross, k_decay = mxu(q * xi_ref[...]), mxu(k * zeta_ref[...])
    yield
    inner = _bnt(qb, kb).reshape(nq, HEADS, chunk, chunk) * dmat_ref[...]
    yield
    intra = _bnn(inner.reshape(nq * HEADS, chunk, chunk).astype(BF16), vb)
    yield
    kv = _btn(k_decay, vb)
    yield

    decay = _to_heads(cd_ref[...][None])
    state = state_ref[...]
    incoming = []
    for c in range(nq):
        incoming.append(state)
        state = state * decay + kv[c * HEADS:(c + 1) * HEADS]
    state_ref[...] = state
    o = intra + _bnn(q_cross, jnp.concatenate(incoming, axis=0).astype(BF16))
    yield
    o = _from_heads(o * lax.rsqrt(jnp.mean(o * o, axis=-1, keepdims=True) + NORM_EPS)).reshape(ts, GROUP)
    g = g_ref[0]
    o_ref[0] = (o * gw_ref[...] * (g * _sigmoid(g))).astype(o_ref.dtype)


def _lru_ret_kernel(*refs, ts, lru_ts, chunk):
    lru_in, ret_in, (lru_out, ret_out, buf_ref, h_ref, state_ref) = refs[:9], refs[9:20], refs[20:]

    @pl.when(pl.program_id(1) == 0)
    def _():
        buf_ref[0:8, :] = jnp.zeros((8, GROUP), F32)
        h_ref[...] = jnp.zeros_like(h_ref)
        state_ref[...] = jnp.zeros_like(state_ref)

    def lru_ranges():
        for start in range(0, ts, lru_ts):
            yield from _lru_steps(slice(start, start + lru_ts), *lru_in, lru_out, buf_ref, h_ref)

    _interleave(lru_ranges(), _ret_steps(*ret_in, ret_out, state_ref, ts=ts, chunk=chunk))


def lru_retention(proj3, conv_w, conv_b, wra_bd, ra_b, wri_bd, ri_b, lam, cos_t, sin_t, dmat, xi, zeta, cd, gn_w, *,
                  ts, lru_ts, chunk):
    b, s, _ = proj3.shape
    slab = lambda off: pl.BlockSpec((1, ts, GROUP), lambda i, j: (i, j, off // GROUP))
    full = lambda shape: pl.BlockSpec(shape, lambda i, j: tuple(0 for _ in shape))
    vec, mat = full((1, GROUP)), full((GROUP, GROUP))
    table = pl.BlockSpec((ts, GROUP), lambda i, j: (j, 0))
    out = pl.BlockSpec((1, ts, GROUP), lambda i, j: (i, j, 0))
    row = lambda t: t.reshape(1, GROUP)
    return pl.pallas_call(
        functools.partial(_lru_ret_kernel, ts=ts, lru_ts=lru_ts, chunk=chunk),
        grid=(b, s // ts),
        in_specs=[slab(LRU_X), slab(LRU_Y), full((CONV_WIDTH, GROUP)), vec, mat, vec, mat, vec, vec,
                  slab(RET_Q), slab(RET_K), slab(RET_V), slab(RET_G), table, table,
                  full((HEADS, chunk, chunk)), full((chunk, GROUP)), full((chunk, GROUP)), vec, vec],
        out_specs=[out, out],
        out_shape=[jax.ShapeDtypeStruct((b, s, GROUP), BF16)] * 2,
        scratch_shapes=[pltpu.VMEM((lru_ts + 8, GROUP), F32), pltpu.VMEM((1, GROUP), F32),
                        pltpu.VMEM((HEADS, HEAD_DIM, HEAD_DIM), F32)],
        compiler_params=_cparams(("parallel", "arbitrary")),
        name="lru_retention",
    )(proj3, proj3, conv_w, row(conv_b), wra_bd, row(ra_b), wri_bd, row(ri_b), row(lam),
      proj3, proj3, proj3, proj3, cos_t, sin_t, dmat, xi, zeta, cd, row(gn_w))


def _retention_tables(s, chunk):
    half = HEAD_DIM // 2
    inv = 1.0 / (RET_THETA ** jnp.linspace(0.0, 1.0, half, dtype=F32))
    ang = jnp.arange(s, dtype=F32)[:, None] * inv[None, :]
    cos, sin = jnp.cos(ang), jnp.sin(ang)
    cos_t = jnp.tile(jnp.concatenate([cos, cos], axis=-1), (1, HEADS))
    sin_t = jnp.tile(jnp.concatenate([-sin, sin], axis=-1), (1, HEADS))
    lg = jnp.log(1.0 - 2.0 ** (-5.0 - jnp.arange(HEADS, dtype=F32)))
    n = jnp.arange(chunk, dtype=F32)
    diff = n[:, None] - n[None, :]
    dmat = jnp.where(diff >= 0, jnp.exp(lg[:, None, None] * jnp.maximum(diff, 0.0)), 0.0)
    zeta = jnp.exp(lg[:, None] * (chunk - 1.0 - n)[None, :])
    xi = jnp.exp(lg[:, None] * (n + 1.0)[None, :])
    per_lane = lambda t: jnp.repeat(t.T, HEAD_DIM, axis=1)
    cd = jnp.repeat(jnp.exp(lg * chunk), HEAD_DIM)[None, :]
    return cos_t, sin_t, dmat, per_lane(xi), per_lane(zeta), cd


MIX_XATTN_PARTS = 4


def _mix_xattn_steps(rows, m_refs, x_ref, kv_ref, wout_ref, wq_ref, wo_ref, gmix_ref, gpre_ref, gpost_ref, o_ref):
    acc = _dot(m_refs[0][0, rows, :], wout_ref[0:GROUP, :])
    for idx, m_ref in enumerate(m_refs[1:], start=1):
        acc = acc + _dot(m_ref[0, rows, :], wout_ref[idx * GROUP:(idx + 1) * GROUP, :])
    yield
    x = x_ref[0, rows, :] + _rms(acc, gmix_ref[...])
    xn = _rms(x, gpre_ref[...]).astype(BF16)
    q = (_dot(xn, wq_ref[...]) * (XATTN_HEAD_DIM ** -0.5)).astype(BF16)
    yield
    heads = lambda t, off: jnp.stack([t[:, off + h * XATTN_HEAD_DIM:off + (h + 1) * XATTN_HEAD_DIM]
                                      for h in range(XATTN_HEADS)])
    kv = kv_ref[0]
    s = _bnt(heads(q, 0), heads(kv, 0))
    yield
    e = jnp.exp(s - jnp.max(s, axis=-1, keepdims=True))
    p = e / jnp.sum(e, axis=-1, keepdims=True)
    o = _bnn(p.astype(BF16), heads(kv, D_MODEL)).astype(BF16)
    yield
    o = jnp.concatenate([o[h] for h in range(XATTN_HEADS)], axis=-1)
    o_ref[0, rows, :] = x + _rms(_dot(o, wo_ref[...]), gpost_ref[...])


def _mix_xattn_kernel(m0_ref, m1_ref, m2_ref, m3_ref, x_ref, kv_ref, wout_ref, wq_ref, wo_ref, gmix_ref, gpre_ref,
                      gpost_ref, o_ref):
    n = x_ref.shape[1] // MIX_XATTN_PARTS
    _interleave(*[_mix_xattn_steps(slice(part * n, (part + 1) * n), (m0_ref, m1_ref, m2_ref, m3_ref), x_ref,
                                   kv_ref, wout_ref, wq_ref, wo_ref, gmix_ref, gpre_ref, gpost_ref, o_ref)
                  for part in range(MIX_XATTN_PARTS)])


def mix_xattn(mixed, x3, kv, w_out, wq, wo, layer, g_mix, g_pre, g_post, *, tm):
    b, s, d = x3.shape
    m = kv.shape[1]
    tok = lambda width: pl.BlockSpec((1, tm, width), lambda bi, i: (bi, i, 0))
    weight = lambda: pl.BlockSpec((None, d, d), lambda bi, i: (layer, 0, 0))
    vec = lambda: pl.BlockSpec((1, d), lambda bi, i: (0, 0))
    return pl.pallas_call(
        _mix_xattn_kernel,
        grid=(b, s // tm),
        in_specs=[tok(GROUP)] * 4 + [tok(d), pl.BlockSpec((1, m, 2 * d), lambda bi, i: (bi, 0, 0)),
                                     weight(), weight(), weight(), vec(), vec(), vec()],
        out_specs=tok(d),
        out_shape=jax.ShapeDtypeStruct((b, s, d), F32),
        compiler_params=_cparams(("parallel", "arbitrary")),
        name="mix_xattn",
    )(*mixed, x3, kv, w_out, wq, wo, g_mix.reshape(1, d), g_pre.reshape(1, d), g_post.reshape(1, d))


def _mlp_kernel(x_ref, w1_ref, w2_ref, gpre_ref, gpost_ref, o_ref, *, ff_tile):
    x = x_ref[...]
    xn = _rms(x, gpre_ref[...]).astype(BF16)
    d_ff = w1_ref.shape[1]
    acc = None
    for c in range(d_ff // ff_tile):
        hid = jnp.square(jnp.maximum(_dot(xn, w1_ref[:, c * ff_tile:(c + 1) * ff_tile]), 0.0)).astype(BF16)
        part = _dot(hid, w2_ref[c * ff_tile:(c + 1) * ff_tile, :])
        acc = part if acc is None else acc + part
    o_ref[...] = x + _rms(acc, gpost_ref[...])


def mlp(x, w1, w2, layer, g_pre, g_post, *, tm, ff_tile):
    n, d = x.shape
    d_ff = w1.shape[2]
    return pl.pallas_call(
        functools.partial(_mlp_kernel, ff_tile=ff_tile),
        grid=(n // tm,),
        in_specs=[pl.BlockSpec((tm, d), lambda i: (i, 0)),
                  pl.BlockSpec((None, d, d_ff), lambda i: (layer, 0, 0), pipeline_mode=pl.Buffered(1)),
                  pl.BlockSpec((None, d_ff, d), lambda i: (layer, 0, 0), pipeline_mode=pl.Buffered(1)),
                  pl.BlockSpec((1, d), lambda i: (0, 0)),
                  pl.BlockSpec((1, d), lambda i: (0, 0))],
        out_specs=pl.BlockSpec((tm, d), lambda i: (i, 0)),
        out_shape=jax.ShapeDtypeStruct((n, d), F32),
        compiler_params=_cparams(("parallel",)),
        name="mlp",
    )(x, w1, w2, g_pre.reshape(1, d), g_post.reshape(1, d))


def _block_diag(w):
    h, n, _ = w.shape
    eye = jnp.eye(h, dtype=w.dtype)
    return (eye[:, None, :, None] * w[:, :, None, :]).reshape(h * n, h * n)


def _pad_rows(w, start, total):
    return jnp.zeros((total, w.shape[1]), w.dtype).at[start:start + w.shape[0]].set(w)


def _tile(n, pref):
    return pref if n % pref == 0 else n


def kernel(x, mem, norm_mix_pre, norm_mix_post, norm_xa_pre, norm_xa_post, norm_mem, norm_mlp_pre, norm_mlp_post, w_in, w_out, fox_f_bias, lru_conv_w, lru_conv_b, lru_ra_w, lru_ra_b, lru_ri_w, lru_ri_b, lru_lambda, rwkv_mu, rwkv_w0, rwkv_w2, rwkv_a0, rwkv_a2, rwkv_g2, rwkv_k_k, rwkv_k_a, rwkv_r_k, rwkv_gn_w, rwkv_gn_b, rwkv_v0, rwkv_v1, rwkv_v2, ret_gn_w, xa_wq, xa_wk, xa_wv, xa_wo, mlp_w1, mlp_w2):
    bsz, seq, d = x.shape
    depth = w_in.shape[0]
    n_tok = bsz * seq
    mem_len = mem.shape[1]
    tm = _tile(n_tok, 512)
    tq = _tile(seq, 512)
    ret_tables = _retention_tables(seq, RET_CHUNK)
    head_ones = _block_diag(jnp.ones((HEADS, HEAD_DIM, HEAD_DIM), BF16))

    w_in_t = w_in.astype(BF16).swapaxes(1, 2)
    w_fox_b, w_mix_b = w_in_t[:, :FOX_WIDTH], w_in_t[:, FOX_REAL:]
    w_out_b = w_out.astype(BF16)
    wq_b, wo_b = xa_wq.astype(BF16), xa_wo.astype(BF16)
    wkv_b = jnp.concatenate([xa_wk, xa_wv], axis=-1).astype(BF16)
    w1_b, w2_b = mlp_w1.astype(BF16), mlp_w2.astype(BF16)

    x2 = x.reshape(n_tok, d)
    v_first = None
    for l in range(depth):
        proj_fox, proj_mix = in_proj(x2, norm_mix_pre[l], w_fox_b, w_mix_b, l, tm=tm, tn=MIX_WIDTH // 5)
        proj3 = proj_mix.reshape(bsz, seq, MIX_WIDTH)

        f_bias = jnp.zeros((1, FOX_WIDTH - FOX_F), F32).at[0, :HEADS].set(fox_f_bias[l])
        qa, ka, vt = fox_prep(proj_fox.reshape(bsz, seq, FOX_WIDTH), f_bias, tc=tq)
        fox_out = fox_attention(qa, ka, vt)

        ret_ts = _tile(seq, 8 * RET_CHUNK)
        lru_out, ret_out = lru_retention(
            proj3, lru_conv_w[l], lru_conv_b[l], _block_diag(lru_ra_w[l]).astype(BF16), lru_ra_b[l],
            _block_diag(lru_ri_w[l]).astype(BF16), lru_ri_b[l], lru_lambda[l], *ret_tables, ret_gn_w[l],
            ts=ret_ts, lru_ts=_tile(ret_ts, 512), chunk=RET_CHUNK)

        vres = None
        if l > 0:
            vres = (v_first, rwkv_v0[l - 1],
                    jnp.pad(rwkv_v1[l - 1], ((0, 0), (0, 128 - RWKV_V_RANK))).astype(BF16),
                    _pad_rows(rwkv_v2[l - 1], 0, 128).astype(BF16))
        mm_, gm_, qm_, y0_, bonus_, g_, v_ = rwkv_chunk(
            proj3, rwkv_mu[l].reshape(4, GROUP), rwkv_w0[l],
            _pad_rows(rwkv_w2[l], 0, GROUP).astype(BF16), rwkv_a0[l],
            _pad_rows(rwkv_a2[l], RWKV_W_RANK, GROUP).astype(BF16),
            _pad_rows(rwkv_g2[l], RWKV_W_RANK + RWKV_A_RANK, GROUP).astype(BF16),
            rwkv_k_k[l], rwkv_k_a[l], rwkv_r_k[l], head_ones, vres, ts=_tile(seq, 8 * RWKV_CHUNK), chunk=RWKV_CHUNK)
        if l == 0:
            v_first = v_
        rwkv_out = rwkv_state(mm_, gm_, qm_, y0_, bonus_, g_, rwkv_gn_w[l], rwkv_gn_b[l],
                              ts=_tile(seq, 4 * RWKV_CHUNK), chunk=RWKV_CHUNK)

        kv = norm_matmul(mem.reshape(bsz * mem_len, d), norm_mem[l], wkv_b, l, tm=_tile(bsz * mem_len, 512),
                         tn=1024, out_dtype=BF16).reshape(bsz, mem_len, 2 * d)
        x2 = mix_xattn((fox_out, lru_out, rwkv_out, ret_out), x2.reshape(bsz, seq, d), kv, w_out_b, wq_b, wo_b, l,
                       norm_mix_post[l], norm_xa_pre[l], norm_xa_post[l], tm=_tile(seq, 1024)).reshape(n_tok, d)

        x2 = mlp(x2, w1_b, w2_b, l, norm_mlp_pre[l], norm_mlp_post[l], tm=tm, ff_tile=1024)
    return x2.reshape(bsz, seq, d)
```

```python
import functools
import math

import jax
import jax.numpy as jnp
from jax import lax
from jax.experimental import pallas as pl
from jax.experimental.pallas import tpu as pltpu

F32 = jnp.float32
BF16 = jnp.bfloat16

D_MODEL = 1024
GROUP = 256
HEADS = 4
HEAD_DIM = 64
CONV_WIDTH = 4
LRU_C = 8.0
RET_THETA = 10000.0
RET_CHUNK = 128
RWKV_W_RANK, RWKV_A_RANK, RWKV_G_RANK, RWKV_V_RANK = 64, 64, 128, 32
RWKV_GN_EPS = 64e-5
XATTN_HEADS = 4
XATTN_HEAD_DIM = D_MODEL // XATTN_HEADS
NORM_EPS = 1e-6
NEG_BIG = -1e30

FOX_Q, FOX_K, FOX_V, FOX_F = 0, 256, 512, 768
FOX_REAL = 3 * GROUP + HEADS
FOX_WIDTH = 896
LRU_X, LRU_Y = 0, 256
RWKV_R, RWKV_K, RWKV_V, RWKV_LR = 512, 768, 1024, 1280
RET_Q, RET_K, RET_V, RET_G = 1536, 1792, 2048, 2304
MIX_WIDTH = 2560

VMEM_LIMIT = 56 * 1024 * 1024
RWKV_CHUNK = 64


def _cparams(sem):
    return pltpu.CompilerParams(dimension_semantics=sem, vmem_limit_bytes=VMEM_LIMIT)


def _rms(x, g):
    return x * lax.rsqrt(jnp.mean(x * x, axis=-1, keepdims=True) + NORM_EPS) * g


def _log_sigmoid(x):
    return jnp.minimum(x, 0.0) - jnp.log1p(jnp.exp(-jnp.abs(x)))


def _sigmoid(x):
    return 1.0 / (1.0 + jnp.exp(-x))


def _dot(a, b, **kw):
    return jnp.dot(a, b, preferred_element_type=F32, **kw)


def _dot_nt(a, b, **kw):
    return lax.dot_general(a, b, (((1,), (1,)), ((), ())), preferred_element_type=F32, **kw)


def _split_bf16(x):
    hi = x.astype(BF16)
    return hi, (x - hi.astype(F32)).astype(BF16)


def _interleave(*step_generators):
    live = list(step_generators)
    while live:
        for gen in list(live):
            try:
                next(gen)
            except StopIteration:
                live.remove(gen)


def _bf16_terms(x, n):
    terms = []
    for _ in range(n - 1):
        t = x.astype(BF16)
        terms.append(t)
        x = x - t.astype(F32)
    return terms + [x.astype(BF16)]


def _dot_sel(a, b, *, split, n=3):
    if split == "a":
        parts = [_dot(t, b.astype(BF16)) for t in _bf16_terms(a, n)]
    else:
        parts = [_dot(a.astype(BF16), t) for t in _bf16_terms(b, n)]
    out = parts[-1]
    for p in reversed(parts[:-1]):
        out = out + p
    return out


def _tri(n, strict=False):
    r = lax.broadcasted_iota(jnp.int32, (n, n), 0)
    c = lax.broadcasted_iota(jnp.int32, (n, n), 1)
    return (r > c) if strict else (r >= c)


def _norm_matmul_kernel(x_ref, g_ref, w_ref, o_ref, xn_ref):
    @pl.when(pl.program_id(1) == 0)
    def _():
        xn_ref[...] = _rms(x_ref[...], g_ref[...]).astype(BF16)

    o_ref[...] = _dot(xn_ref[...], w_ref[...]).astype(o_ref.dtype)


def norm_matmul(x, g, w, layer, *, tm, tn, out_dtype):
    n, d = x.shape
    width = w.shape[2]
    return pl.pallas_call(
        _norm_matmul_kernel,
        grid=(n // tm, width // tn),
        in_specs=[pl.BlockSpec((tm, d), lambda i, j: (i, 0)),
                  pl.BlockSpec((1, d), lambda i, j: (0, 0)),
                  pl.BlockSpec((None, d, tn), lambda i, j: (layer, 0, j))],
        out_specs=pl.BlockSpec((tm, tn), lambda i, j: (i, j)),
        out_shape=jax.ShapeDtypeStruct((n, width), out_dtype),
        scratch_shapes=[pltpu.VMEM((tm, d), BF16)],
        compiler_params=_cparams(("parallel", "arbitrary")),
        name="norm_matmul",
    )(x, g.reshape(1, d), w)


def _in_proj_kernel(x_ref, g_ref, wf_ref, wm_ref, of_ref, om_ref, *, tn):
    xn = _rms(x_ref[...], g_ref[...]).astype(BF16)
    of_ref[...] = _dot_nt(xn, wf_ref[...])
    for c in range(wm_ref.shape[0] // tn):
        om_ref[:, c * tn:(c + 1) * tn] = _dot_nt(xn, wm_ref[c * tn:(c + 1) * tn, :])


def in_proj(x, g, w_fox, w_mix, layer, *, tm, tn):
    n, d = x.shape
    weight = lambda w: pl.BlockSpec((None, w.shape[1], d), lambda i: (layer, 0, 0), pipeline_mode=pl.Buffered(1))
    out = lambda w: pl.BlockSpec((tm, w.shape[1]), lambda i: (i, 0))
    return pl.pallas_call(
        functools.partial(_in_proj_kernel, tn=tn),
        grid=(n // tm,),
        in_specs=[pl.BlockSpec((tm, d), lambda i: (i, 0)), pl.BlockSpec((1, d), lambda i: (0, 0)),
                  weight(w_fox), weight(w_mix)],
        out_specs=[out(w_fox), out(w_mix)],
        out_shape=[jax.ShapeDtypeStruct((n, w.shape[1]), F32) for w in (w_fox, w_mix)],
        compiler_params=_cparams(("parallel",)),
        name="in_proj",
    )(x, g.reshape(1, d), w_fox, w_mix)


FOX_AUG = 128
FOX_VROWS = HEAD_DIM + 16
LOG2E = 1.4426950408889634


def _fox_prep_steps(rows, tile, q_ref, k_ref, v_ref, f_ref, b_ref, qa_ref, ka_ref, vt_ref, carry_ref):
    lf = _log_sigmoid(f_ref[0, rows, :] + b_ref[...])
    tc = lf.shape[0]
    cum = _dot_sel(_tri(tc), lf, split="b") + carry_ref[...]
    carry_ref[...] = cum[tc - 1:tc, :]
    yield

    lane = lax.broadcasted_iota(jnp.int32, (tc, FOX_AUG), 1)
    feat = lane < HEAD_DIM
    ones = jnp.where((lane >= HEAD_DIM) & (lane < HEAD_DIM + 3), 1.0, 0.0)
    q = q_ref[0, rows, :] * (HEAD_DIM ** -0.5 * LOG2E)
    k = k_ref[0, rows, :]
    for h in range(HEADS):
        pair = slice((h // 2) * FOX_AUG, (h // 2 + 1) * FOX_AUG)
        qt, kt = q[:, pair], k[:, pair]
        if h % 2:
            qt, kt = pltpu.roll(qt, HEAD_DIM, axis=1), pltpu.roll(kt, HEAD_DIM, axis=1)
        neg_c = jnp.broadcast_to(cum[:, h:h + 1] * -LOG2E, (tc, FOX_AUG))
        c_hi = neg_c.astype(BF16).astype(F32)
        rest = neg_c - c_hi
        c_mid = rest.astype(BF16).astype(F32)
        bias = jnp.where(lane == HEAD_DIM, c_hi, jnp.where(lane == HEAD_DIM + 1, c_mid,
                                                           jnp.where(lane == HEAD_DIM + 2, rest - c_mid, 0.0)))
        qa_ref[0, h, rows, :] = jnp.where(feat, qt, ones).astype(BF16)
        ka_ref[0, h, rows, :] = jnp.where(feat, kt, bias).astype(BF16)
        yield
    ones_row = lax.broadcasted_iota(jnp.int32, (HEADS, FOX_VROWS - HEAD_DIM, tc), 1) == 0
    vt = jnp.concatenate([v_ref[0, rows, :].T.reshape(HEADS, HEAD_DIM, tc), ones_row.astype(F32)], axis=1)
    vt_ref[0, :, tile] = vt.astype(BF16)


def _fox_attn_kernel(q_ref, k_ref, v_ref, o_ref, s_ref, *, t):
    i = pl.program_id(1)
    qa = q_ref[0]

    def scores(j):
        return _bnt(k_ref[0, :, pl.ds(pl.multiple_of(j * t, t), t), :], qa)

    def update(carry, slot, j):
        m, acc = carry
        s = s_ref[slot]
        m_new = jnp.maximum(m, jnp.max(s, axis=1, keepdims=True))
        p = jnp.exp2(s - m_new).astype(BF16)
        return m_new, jnp.exp2(m - m_new) * acc + _bnn(v_ref[0, :, j], p)

    key = lax.broadcasted_iota(jnp.int32, (1, t, t), 1)
    qry = lax.broadcasted_iota(jnp.int32, (1, t, t), 2)
    s_ref[0] = jnp.where(key <= qry, scores(i), NEG_BIG)
    carry = (jnp.full((HEADS, 1, t), NEG_BIG, F32), jnp.zeros((HEADS, FOX_VROWS, t), F32))

    def pair(jj, carry):
        s_ref[1] = scores(2 * jj)
        carry = update(carry, 0, jnp.where(jj == 0, i, 2 * jj - 1))
        s_ref[0] = scores(2 * jj + 1)
        return update(carry, 1, 2 * jj)

    carry = lax.fori_loop(0, i // 2, pair, carry)
    pending = jnp.where(i < 2, i, 2 * (i // 2) - 1)

    def odd_tail(carry):
        s_ref[1] = scores(i - 1)
        return update(update(carry, 0, pending), 1, i - 1)

    _, acc = lax.cond(i % 2 == 1, odd_tail, lambda c: update(c, 0, pending), carry)
    out = acc[:, :HEAD_DIM] / acc[:, HEAD_DIM:HEAD_DIM + 1]
    o_ref[0] = out.reshape(GROUP, t).T.astype(o_ref.dtype)


def fox_attention(qa, ka, vt):
    b, h, s, _ = qa.shape
    nk, t = vt.shape[2], vt.shape[4]
    return pl.pallas_call(
        functools.partial(_fox_attn_kernel, t=t),
        grid=(b, nk),
        in_specs=[pl.BlockSpec((1, h, t, FOX_AUG), lambda bi, i: (bi, 0, i, 0)),
                  pl.BlockSpec((1, h, s, FOX_AUG), lambda bi, i: (bi, 0, 0, 0)),
                  pl.BlockSpec((1, h, nk, FOX_VROWS, t), lambda bi, i: (bi, 0, 0, 0, 0))],
        out_specs=pl.BlockSpec((1, t, GROUP), lambda bi, i: (bi, i, 0)),
        out_shape=jax.ShapeDtypeStruct((b, s, GROUP), BF16),
        scratch_shapes=[pltpu.VMEM((2, h, t, t), F32)],
        compiler_params=_cparams(("parallel", "arbitrary")),
        name="fox_attention",
    )(qa, ka, vt)


def _lru_steps(rows, x_ref, y_ref, cw_ref, cb_ref, wra_ref, bra_ref, wri_ref, bri_ref, lam_ref, o_ref, buf_ref,
               h_ref):
    ts = rows.stop - rows.start
    xb = x_ref[0, rows, :]
    buf_ref[8:8 + ts, :] = xb
    xc = cb_ref[...] + buf_ref[5:5 + ts, :] * cw_ref[0:1, :]
    for j in range(1, CONV_WIDTH):
        xc = xc + buf_ref[5 + j:5 + j + ts, :] * cw_ref[j:j + 1, :]
    buf_ref[0:8, :] = xb[ts - 8:ts, :]
    yield

    xcb = xc.astype(BF16)
    r = _sigmoid(_dot(xcb, wra_ref[...]) + bra_ref[...])
    yield
    gate_i = _sigmoid(_dot(xcb, wri_ref[...]) + bri_ref[...])
    yield
    log_a = LRU_C * r * _log_sigmoid(lam_ref[...])
    a = jnp.exp(log_a)
    z = 2.0 * log_a
    mult = jnp.sqrt(jnp.maximum(-jnp.tanh(0.5 * z) * (jnp.exp(z) + 1.0), 0.0))
    u = mult * (gate_i * xc)
    yield

    row = lax.broadcasted_iota(jnp.int32, (ts, GROUP), 0)
    pa, pb = a, u
    d = 1
    while d < ts:
        sa = pltpu.roll(pa, d, axis=0)
        sb = pltpu.roll(pb, d, axis=0)
        valid = row >= d
        pb = jnp.where(valid, pa * sb + pb, pb)
        pa = jnp.where(valid, pa * sa, pa)
        d *= 2
        yield
    hseq = pa * h_ref[...] + pb
    h_ref[...] = hseq[ts - 1:ts, :]

    y = y_ref[0, rows, :]
    gelu = 0.5 * y * (1.0 + jnp.tanh(math.sqrt(2.0 / math.pi) * (y + 0.044715 * (y * y * y))))
    o_ref[0, rows, :] = (hseq * gelu).astype(o_ref.dtype)


def _bdot(dims):
    return lambda a, b: lax.dot_general(a, b, (dims, ((0,), (0,))), preferred_element_type=F32)


_bnn = _bdot(((2,), (1,)))
_bnt = _bdot(((2,), (2,)))
_btn = _bdot(((1,), (1,)))


def _unit_lower_inverse(a_strict, n):
    r = lax.broadcasted_iota(jnp.int32, (1, n, n), 1)
    c = lax.broadcasted_iota(jnp.int32, (1, n, n), 2)
    t = jnp.where(r == c, 1.0, jnp.where((r // 2 == c // 2) & (r > c), a_strict, 0.0))
    a_b = a_strict.astype(BF16)
    zero = jnp.zeros((), BF16)
    m = 2
    while m < n:
        off = (r // (2 * m) == c // (2 * m)) & (r % (2 * m) >= m) & (c % (2 * m) < m)
        tb = t.astype(BF16)
        t = t + _bnn(tb, _bnn(jnp.where(off, a_b, zero), tb).astype(BF16))
        m *= 2
    return t


def _to_heads(x):
    n, rows, _ = x.shape
    parts = [x[:, :, h * HEAD_DIM:(h + 1) * HEAD_DIM] for h in range(HEADS)]
    return jnp.stack(parts, axis=1).reshape(n * HEADS, rows, HEAD_DIM)


def _from_heads(x):
    nh, rows, _ = x.shape
    x = x.reshape(nh // HEADS, HEADS, rows, HEAD_DIM)
    return jnp.concatenate([x[:, h] for h in range(HEADS)], axis=-1)


def _rwkv_chunk_kernel(*refs, ts, chunk, has_vres):
    (sr_ref, sk_ref, sv_ref, sl_ref, mu_ref, w0_ref, w2_ref, a0_ref, a2_ref, g2_ref, kk_ref, ka_ref, rk_ref,
     ones_ref) = refs[:14]
    refs = refs[14:]
    if has_vres:
        vf_ref, v0_ref, v1_ref, v2_ref = refs[:4]
        refs = refs[4:]
    mm_out, gm_out, qm_out, y0_out, bonus_out, g_out, v_out, carry_ref = refs

    @pl.when(pl.program_id(1) == 0)
    def _():
        carry_ref[...] = jnp.zeros_like(carry_ref)

    row0 = lax.broadcasted_iota(jnp.int32, (ts, GROUP), 0) == 0

    def shift_mix(ref, idx):
        s = ref[0]
        prev = jnp.where(row0, carry_ref[idx:idx + 1, :], pltpu.roll(s, 1, axis=0))
        carry_ref[idx:idx + 1, :] = s[ts - 1:ts, :]
        return s + (prev - s) * mu_ref[idx:idx + 1, :]

    r = shift_mix(sr_ref, 0)
    k = shift_mix(sk_ref, 1)
    v = shift_mix(sv_ref, 2)
    low = shift_mix(sl_ref, 3)

    zw = w0_ref[...] + _dot(jnp.tanh(low).astype(BF16), w2_ref[...])
    lw = -math.exp(-0.5) * _sigmoid(zw)
    a = _sigmoid(a0_ref[...] + _dot(low.astype(BF16), a2_ref[...]))
    g_out[0] = _dot(_sigmoid(low).astype(BF16), g2_ref[...])
    if has_vres:
        mix = _dot(_dot(v.astype(BF16), v1_ref[...]).astype(BF16), v2_ref[...])
        v = v + (vf_ref[0] - v) * _sigmoid(v0_ref[...] + mix)
    v_out[0] = v
    kk = k * kk_ref[...]
    ss = _dot_sel(kk * kk, ones_ref[...], split="a")
    kk = kk / jnp.maximum(jnp.sqrt(ss), 1e-12)
    k = k * (1.0 + (a - 1.0) * ka_ref[...])
    bonus_out[0] = _dot_sel(r * k * rk_ref[...], ones_ref[...], split="a") * v
    a_vec, b_vec = -kk, kk * a

    c = chunk
    rr = lax.broadcasted_iota(jnp.int32, (ts, ts), 0)
    cc = lax.broadcasted_iota(jnp.int32, (ts, ts), 1)
    chunk_tri = ((rr >= cc) & (rr // c == cc // c)).astype(F32)
    cum_all = _dot_sel(chunk_tri, lw, split="b")
    strict = _tri(c, strict=True)[None]
    incl = _tri(c)[None]
    eye = (lax.broadcasted_iota(jnp.int32, (1, HEAD_DIM, HEAD_DIM), 1)
           == lax.broadcasted_iota(jnp.int32, (1, HEAD_DIM, HEAD_DIM), 2)).astype(F32)

    nq = ts // c
    per_chunk = lambda t: t.reshape(nq, c, GROUP)
    cum = per_chunk(cum_all)
    rq, kq, vq, aq, bq = (per_chunk(t) for t in (r, k, v, a_vec, b_vec))
    cum_ex = cum - per_chunk(lw)
    mid = cum[:, c // 2 - 1:c // 2, :]
    tot = cum[:, c - 1:c, :]
    e_fwd = jnp.exp(cum - mid)
    e_bwd = jnp.exp(mid - cum)
    e_end = jnp.exp(tot - cum)
    mxu = lambda t: _to_heads(t.astype(BF16))
    r_rel, k_rel = mxu(rq * e_fwd), mxu(kq * e_bwd)
    a_rel, b_rel = mxu(aq * jnp.exp(cum_ex - mid)), mxu(bq * e_bwd)
    a_abs, r_abs = mxu(aq * jnp.exp(cum_ex)), rq * jnp.exp(cum)
    k_end, b_end = mxu(kq * e_end), mxu(bq * e_end)
    gam = _to_heads(jnp.exp(tot))
    vh = mxu(vq)

    ar_rel = jnp.concatenate([a_rel, r_rel], axis=1)
    s_b, s_k = _bnt(ar_rel, b_rel), _bnt(ar_rel, k_rel)
    a_ab = jnp.where(strict, s_b[:, :c], 0.0)
    zero = jnp.zeros((), BF16)
    a_ak = jnp.where(strict, s_k[:, :c].astype(BF16), zero)
    a_rb = jnp.where(incl, s_b[:, c:].astype(BF16), zero)
    a_rk = jnp.where(incl, s_k[:, c:].astype(BF16), zero)
    t_inv = _unit_lower_inverse(a_ab, c).astype(BF16)
    pu = _bnn(t_inv, jnp.concatenate([a_abs, _bnn(a_ak, vh).astype(BF16)], axis=-1)).astype(BF16)
    z = _bnn(a_rb, pu)
    y0 = z[..., HEAD_DIM:] + _bnn(a_rk, vh)
    xtb = _btn(pu, b_end)
    mm_mat = eye * gam + xtb[:, :HEAD_DIM]
    gm = xtb[:, HEAD_DIM:] + _btn(vh, k_end)
    dense = lambda t: _from_heads(t).reshape(ts, GROUP)
    mm_out[0] = dense(mm_mat).astype(mm_out.dtype)
    gm_out[0] = dense(gm)
    qm_out[0] = (dense(z[..., :HEAD_DIM]) + r_abs.reshape(ts, GROUP)).astype(qm_out.dtype)
    y0_out[0] = dense(y0)


def rwkv_chunk(proj3, mu4, w0, w2p, a0, a2p, g2p, k_k, k_a, r_k, head_ones, vres, *, ts, chunk):
    b, s, _ = proj3.shape
    slab = lambda off: pl.BlockSpec((1, ts, GROUP), lambda i, j: (i, j, off // GROUP))
    vec = lambda: pl.BlockSpec((1, GROUP), lambda i, j: (0, 0))
    full = lambda shape: pl.BlockSpec(shape, lambda i, j: tuple(0 for _ in shape))
    tok = pl.BlockSpec((1, ts, GROUP), lambda i, j: (i, j, 0))
    in_specs = [slab(RWKV_R), slab(RWKV_K), slab(RWKV_V), slab(RWKV_LR), full((4, GROUP)),
                vec(), full((GROUP, GROUP)), vec(), full((GROUP, GROUP)), full((GROUP, GROUP)), vec(), vec(), vec(),
                full((GROUP, GROUP))]
    args = [proj3, proj3, proj3, proj3, mu4, w0.reshape(1, GROUP), w2p, a0.reshape(1, GROUP), a2p, g2p,
            k_k.reshape(1, GROUP), k_a.reshape(1, GROUP), r_k.reshape(1, GROUP), head_ones]
    if vres is not None:
        v_first, v0, v1p, v2p = vres
        in_specs += [tok, vec(), full((GROUP, 128)), full((128, GROUP))]
        args += [v_first, v0.reshape(1, GROUP), v1p, v2p]
    return pl.pallas_call(
        functools.partial(_rwkv_chunk_kernel, ts=ts, chunk=chunk, has_vres=vres is not None),
        grid=(b, s // ts),
        in_specs=in_specs,
        out_specs=[tok] * 7,
        out_shape=[jax.ShapeDtypeStruct((b, s, GROUP), dt) for dt in (BF16, F32, BF16, F32, F32, F32, F32)],
        scratch_shapes=[pltpu.VMEM((4, GROUP), F32)],
        compiler_params=_cparams(("parallel", "arbitrary")),
        name="rwkv_chunk",
    )(*args)


def _rwkv_state_kernel(mm_ref, gm_ref, qm_ref, y0_ref, bonus_ref, g_ref, gw_ref, gb_ref, o_ref, state_ref, *, chunk):
    @pl.when(pl.program_id(0) == 0)
    def _():
        state_ref[...] = jnp.zeros_like(state_ref)

    state = state_ref[...]
    for c in range(mm_ref.shape[1] // chunk):
        rows = slice(c * chunk, (c + 1) * chunk)
        s_hi, s_lo = _split_bf16(state)
        qm, mm = _to_heads(qm_ref[:, rows, :]), _to_heads(mm_ref[:, rows, :])
        y = _bnt(qm, s_hi) + _bnt(qm, s_lo) + _to_heads(y0_ref[:, rows, :])
        state = _bnn(s_hi, mm) + _bnn(s_lo, mm) + _to_heads(gm_ref[:, rows, :])
        mu = jnp.mean(y, axis=-1, keepdims=True)
        var = jnp.mean(jnp.square(y - mu), axis=-1, keepdims=True)
        yn = _from_heads((y - mu) * lax.rsqrt(var + RWKV_GN_EPS)) * gw_ref[...] + gb_ref[...]
        o_ref[:, rows, :] = ((yn + bonus_ref[:, rows, :]) * g_ref[:, rows, :]).astype(o_ref.dtype)
    state_ref[...] = state


def rwkv_state(mm, gm, qm, y0, bonus, g, gn_w, gn_b, *, ts, chunk):
    bsz, s, _ = mm.shape
    tok = pl.BlockSpec((bsz, ts, GROUP), lambda j: (0, j, 0))
    vec = pl.BlockSpec((1, GROUP), lambda j: (0, 0))
    return pl.pallas_call(
        functools.partial(_rwkv_state_kernel, chunk=chunk),
        grid=(s // ts,),
        in_specs=[tok] * 6 + [vec] * 2,
        out_specs=tok,
        out_shape=jax.ShapeDtypeStruct((bsz, s, GROUP), BF16),
        scratch_shapes=[pltpu.VMEM((bsz * HEADS, HEAD_DIM, HEAD_DIM), F32)],
        compiler_params=_cparams(("arbitrary",)),
        name="rwkv_state",
    )(mm, gm, qm, y0, bonus, g, gn_w.reshape(1, GROUP), gn_b.reshape(1, GROUP))


def _ret_steps(q_ref, k_ref, v_ref, g_ref, cos_ref, sin_ref, dmat_ref, xi_ref, zeta_ref, cd_ref, gw_ref, o_ref,
               state_ref, *, ts, chunk):
    lane = lax.broadcasted_iota(jnp.int32, (ts, GROUP), 1)
    first_half = (lane % HEAD_DIM) < (HEAD_DIM // 2)
    cos, sin = cos_ref[...], sin_ref[...]

    def rotary(t):
        partner = jnp.where(first_half, pltpu.roll(t, GROUP - HEAD_DIM // 2, axis=1),
                            pltpu.roll(t, HEAD_DIM // 2, axis=1))
        return t * cos + partner * sin

    nq = ts // chunk
    per_chunk = lambda t: t.reshape(nq, chunk, GROUP)
    mxu = lambda t: _to_heads(t).astype(BF16)
    q = per_chunk(rotary(q_ref[0]))
    yield
    k = per_chunk(rotary(k_ref[0]) * (HEAD_DIM ** -0.5))
    yield
    qb, kb, vb = mxu(q), mxu(k), mxu(per_chunk(v_ref[0]))
    yield
    q_cross, k_decay = mxu(q * xi_ref[...]), mxu(k * zeta_ref[...])
    yield
    inner = _bnt(qb, kb).reshape(nq, HEADS, chunk, chunk) * dmat_ref[...]
    yield
    intra = _bnn(inner.reshape(nq * HEADS, chunk, chunk).astype(BF16), vb)
    yield
    kv = _btn(k_decay, vb)
    yield

    decay = _to_heads(cd_ref[...][None])
    state = state_ref[...]
    incoming = []
    for c in range(nq):
        incoming.append(state)
        state = state * decay + kv[c * HEADS:(c + 1) * HEADS]
    state_ref[...] = state
    o = intra + _bnn(q_cross, jnp.concatenate(incoming, axis=0).astype(BF16))
    yield
    o = _from_heads(o * lax.rsqrt(jnp.mean(o * o, axis=-1, keepdims=True) + NORM_EPS)).reshape(ts, GROUP)
    g = g_ref[0]
    o_ref[0] = (o * gw_ref[...] * (g * _sigmoid(g))).astype(o_ref.dtype)


def _side_mixers_kernel(*refs, ts, sub, chunk):
    lru_in, ret_in, fox_in = refs[:9], refs[9:20], refs[20:25]
    lru_out, ret_out, qa_ref, ka_ref, vt_ref, buf_ref, h_ref, state_ref, cum_ref = refs[25:]

    @pl.when(pl.program_id(1) == 0)
    def _():
        buf_ref[0:8, :] = jnp.zeros((8, GROUP), F32)
        h_ref[...] = jnp.zeros_like(h_ref)
        state_ref[...] = jnp.zeros_like(state_ref)
        cum_ref[...] = jnp.zeros_like(cum_ref)

    def lru_ranges():
        for start in range(0, ts, sub):
            yield from _lru_steps(slice(start, start + sub), *lru_in, lru_out, buf_ref, h_ref)

    def fox_ranges():
        for tile in range(ts // sub):
            yield from _fox_prep_steps(slice(tile * sub, (tile + 1) * sub), tile, *fox_in, qa_ref, ka_ref, vt_ref,
                                       cum_ref)

    _interleave(_ret_steps(*ret_in, ret_out, state_ref, ts=ts, chunk=chunk), lru_ranges(), fox_ranges())


def side_mixers(proj_fox3, proj3, f_bias_pad, conv_w, conv_b, wra_bd, ra_b, wri_bd, ri_b, lam, cos_t, sin_t, dmat, xi,
                zeta, cd, gn_w, *, ts, sub, chunk):
    b, s, _ = proj3.shape
    slab = lambda off: pl.BlockSpec((1, ts, GROUP), lambda i, j: (i, j, off // GROUP))
    full = lambda shape: pl.BlockSpec(shape, lambda i, j: tuple(0 for _ in shape))
    vec, mat = full((1, GROUP)), full((GROUP, GROUP))
    table = pl.BlockSpec((ts, GROUP), lambda i, j: (j, 0))
    out = pl.BlockSpec((1, ts, GROUP), lambda i, j: (i, j, 0))
    aug = pl.BlockSpec((1, HEADS, ts, FOX_AUG), lambda i, j: (i, 0, j, 0))
    f_width = FOX_WIDTH - FOX_F
    row = lambda t: t.reshape(1, GROUP)
    return pl.pallas_call(
        functools.partial(_side_mixers_kernel, ts=ts, sub=sub, chunk=chunk),
        grid=(b, s // ts),
        in_specs=[slab(LRU_X), slab(LRU_Y), full((CONV_WIDTH, GROUP)), vec, mat, vec, mat, vec, vec,
                  slab(RET_Q), slab(RET_K), slab(RET_V), slab(RET_G), table, table,
                  full((HEADS, chunk, chunk)), full((chunk, GROUP)), full((chunk, GROUP)), vec, vec,
                  slab(FOX_Q), slab(FOX_K), slab(FOX_V),
                  pl.BlockSpec((1, ts, f_width), lambda i, j: (i, j, FOX_F // f_width)), full((1, f_width))],
        out_specs=[out, out, aug, aug,
                   pl.BlockSpec((1, HEADS, ts // sub, FOX_VROWS, sub), lambda i, j: (i, 0, j, 0, 0))],
        out_shape=[jax.ShapeDtypeStruct((b, s, GROUP), BF16)] * 2
                  + [jax.ShapeDtypeStruct((b, HEADS, s, FOX_AUG), BF16)] * 2
                  + [jax.ShapeDtypeStruct((b, HEADS, s // sub, FOX_VROWS, sub), BF16)],
        scratch_shapes=[pltpu.VMEM((sub + 8, GROUP), F32), pltpu.VMEM((1, GROUP), F32),
                        pltpu.VMEM((HEADS, HEAD_DIM, HEAD_DIM), F32), pltpu.VMEM((1, f_width), F32)],
        compiler_params=_cparams(("parallel", "arbitrary")),
        name="side_mixers",
    )(proj3, proj3, conv_w, row(conv_b), wra_bd, row(ra_b), wri_bd, row(ri_b), row(lam),
      proj3, proj3, proj3, proj3, cos_t, sin_t, dmat, xi, zeta, cd, row(gn_w),
      proj_fox3, proj_fox3, proj_fox3, proj_fox3, f_bias_pad)


def _retention_tables(s, chunk):
    half = HEAD_DIM // 2
    inv = 1.0 / (RET_THETA ** jnp.linspace(0.0, 1.0, half, dtype=F32))
    ang = jnp.arange(s, dtype=F32)[:, None] * inv[None, :]
    cos, sin = jnp.cos(ang), jnp.sin(ang)
    cos_t = jnp.tile(jnp.concatenate([cos, cos], axis=-1), (1, HEADS))
    sin_t = jnp.tile(jnp.concatenate([-sin, sin], axis=-1), (1, HEADS))
    lg = jnp.log(1.0 - 2.0 ** (-5.0 - jnp.arange(HEADS, dtype=F32)))
    n = jnp.arange(chunk, dtype=F32)
    diff = n[:, None] - n[None, :]
    dmat = jnp.where(diff >= 0, jnp.exp(lg[:, None, None] * jnp.maximum(diff, 0.0)), 0.0)
    zeta = jnp.exp(lg[:, None] * (chunk - 1.0 - n)[None, :])
    xi = jnp.exp(lg[:, None] * (n + 1.0)[None, :])
    per_lane = lambda t: jnp.repeat(t.T, HEAD_DIM, axis=1)
    cd = jnp.repeat(jnp.exp(lg * chunk), HEAD_DIM)[None, :]
    return cos_t, sin_t, dmat, per_lane(xi), per_lane(zeta), cd


MIX_XATTN_PARTS = 4


def _mix_xattn_steps(rows, m_refs, x_ref, kv_ref, wout_ref, wq_ref, wo_ref, gmix_ref, gpre_ref, gpost_ref, o_ref):
    acc = _dot(m_refs[0][0, rows, :], wout_ref[0:GROUP, :])
    for idx, m_ref in enumerate(m_refs[1:], start=1):
        acc = acc + _dot(m_ref[0, rows, :], wout_ref[idx * GROUP:(idx + 1) * GROUP, :])
    yield
    x = x_ref[0, rows, :] + _rms(acc, gmix_ref[...])
    xn = _rms(x, gpre_ref[...]).astype(BF16)
    q = (_dot(xn, wq_ref[...]) * (XATTN_HEAD_DIM ** -0.5)).astype(BF16)
    yield
    heads = lambda t, off: jnp.stack([t[:, off + h * XATTN_HEAD_DIM:off + (h + 1) * XATTN_HEAD_DIM]
                                      for h in range(XATTN_HEADS)])
    kv = kv_ref[0]
    s = _bnt(heads(q, 0), heads(kv, 0))
    yield
    e = jnp.exp(s - jnp.max(s, axis=-1, keepdims=True))
    p = e / jnp.sum(e, axis=-1, keepdims=True)
    o = _bnn(p.astype(BF16), heads(kv, D_MODEL)).astype(BF16)
    yield
    o = jnp.concatenate([o[h] for h in range(XATTN_HEADS)], axis=-1)
    o_ref[0, rows, :] = x + _rms(_dot(o, wo_ref[...]), gpost_ref[...])


def _mix_xattn_kernel(m0_ref, m1_ref, m2_ref, m3_ref, x_ref, kv_ref, wout_ref, wq_ref, wo_ref, gmix_ref, gpre_ref,
                      gpost_ref, o_ref):
    n = x_ref.shape[1] // MIX_XATTN_PARTS
    _interleave(*[_mix_xattn_steps(slice(part * n, (part + 1) * n), (m0_ref, m1_ref, m2_ref, m3_ref), x_ref,
                                   kv_ref, wout_ref, wq_ref, wo_ref, gmix_ref, gpre_ref, gpost_ref, o_ref)
                  for part in range(MIX_XATTN_PARTS)])


def mix_xattn(mixed, x3, kv, w_out, wq, wo, layer, g_mix, g_pre, g_post, *, tm):
    b, s, d = x3.shape
    m = kv.shape[1]
    tok = lambda width: pl.BlockSpec((1, tm, width), lambda bi, i: (bi, i, 0))
    weight = lambda: pl.BlockSpec((None, d, d), lambda bi, i: (layer, 0, 0))
    vec = lambda: pl.BlockSpec((1, d), lambda bi, i: (0, 0))
    return pl.pallas_call(
        _mix_xattn_kernel,
        grid=(b, s // tm),
        in_specs=[tok(GROUP)] * 4 + [tok(d), pl.BlockSpec((1, m, 2 * d), lambda bi, i: (bi, 0, 0)),
                                     weight(), weight(), weight(), vec(), vec(), vec()],
        out_specs=tok(d),
        out_shape=jax.ShapeDtypeStruct((b, s, d), F32),
        compiler_params=_cparams(("parallel", "arbitrary")),
        name="mix_xattn",
    )(*mixed, x3, kv, w_out, wq, wo, g_mix.reshape(1, d), g_pre.reshape(1, d), g_post.reshape(1, d))


def _mlp_kernel(x_ref, w1_ref, w2_ref, gpre_ref, gpost_ref, o_ref, *, ff_tile):
    x = x_ref[...]
    xn = _rms(x, gpre_ref[...]).astype(BF16)
    d_ff = w1_ref.shape[1]
    acc = None
    for c in range(d_ff // ff_tile):
        hid = jnp.square(jnp.maximum(_dot(xn, w1_ref[:, c * ff_tile:(c + 1) * ff_tile]), 0.0)).astype(BF16)
        part = _dot(hid, w2_ref[c * ff_tile:(c + 1) * ff_tile, :])
        acc = part if acc is None else acc + part
    o_ref[...] = x + _rms(acc, gpost_ref[...])


def mlp(x, w1, w2, layer, g_pre, g_post, *, tm, ff_tile):
    n, d = x.shape
    d_ff = w1.shape[2]
    return pl.pallas_call(
        functools.partial(_mlp_kernel, ff_tile=ff_tile),
        grid=(n // tm,),
        in_specs=[pl.BlockSpec((tm, d), lambda i: (i, 0)),
                  pl.BlockSpec((None, d, d_ff), lambda i: (layer, 0, 0), pipeline_mode=pl.Buffered(1)),
                  pl.BlockSpec((None, d_ff, d), lambda i: (layer, 0, 0), pipeline_mode=pl.Buffered(1)),
                  pl.BlockSpec((1, d), lambda i: (0, 0)),
                  pl.BlockSpec((1, d), lambda i: (0, 0))],
        out_specs=pl.BlockSpec((tm, d), lambda i: (i, 0)),
        out_shape=jax.ShapeDtypeStruct((n, d), F32),
        compiler_params=_cparams(("parallel",)),
        name="mlp",
    )(x, w1, w2, g_pre.reshape(1, d), g_post.reshape(1, d))


def _block_diag(w):
    h, n, _ = w.shape
    eye = jnp.eye(h, dtype=w.dtype)
    return (eye[:, None, :, None] * w[:, :, None, :]).reshape(h * n, h * n)


def _pad_rows(w, start, total):
    return jnp.zeros((total, w.shape[1]), w.dtype).at[start:start + w.shape[0]].set(w)


def _tile(n, pref):
    return pref if n % pref == 0 else n


def kernel(x, mem, norm_mix_pre, norm_mix_post, norm_xa_pre, norm_xa_post, norm_mem, norm_mlp_pre, norm_mlp_post, w_in, w_out, fox_f_bias, lru_conv_w, lru_conv_b, lru_ra_w, lru_ra_b, lru_ri_w, lru_ri_b, lru_lambda, rwkv_mu, rwkv_w0, rwkv_w2, rwkv_a0, rwkv_a2, rwkv_g2, rwkv_k_k, rwkv_k_a, rwkv_r_k, rwkv_gn_w, rwkv_gn_b, rwkv_v0, rwkv_v1, rwkv_v2, ret_gn_w, xa_wq, xa_wk, xa_wv, xa_wo, mlp_w1, mlp_w2):
    bsz, seq, d = x.shape
    depth = w_in.shape[0]
    n_tok = bsz * seq
    mem_len = mem.shape[1]
    tm = _tile(n_tok, 512)
    tq = _tile(seq, 512)
    ret_tables = _retention_tables(seq, RET_CHUNK)
    head_ones = _block_diag(jnp.ones((HEADS, HEAD_DIM, HEAD_DIM), BF16))

    w_in_t = w_in.astype(BF16).swapaxes(1, 2)
    w_fox_b, w_mix_b = w_in_t[:, :FOX_WIDTH], w_in_t[:, FOX_REAL:]
    w_out_b = w_out.astype(BF16)
    wq_b, wo_b = xa_wq.astype(BF16), xa_wo.astype(BF16)
    wkv_b = jnp.concatenate([xa_wk, xa_wv], axis=-1).astype(BF16)
    w1_b, w2_b = mlp_w1.astype(BF16), mlp_w2.astype(BF16)

    x2 = x.reshape(n_tok, d)
    v_first = None
    for l in range(depth):
        proj_fox, proj_mix = in_proj(x2, norm_mix_pre[l], w_fox_b, w_mix_b, l, tm=tm, tn=MIX_WIDTH // 5)
        proj3 = proj_mix.reshape(bsz, seq, MIX_WIDTH)

        f_bias = jnp.zeros((1, FOX_WIDTH - FOX_F), F32).at[0, :HEADS].set(fox_f_bias[l])
        side_ts = _tile(seq, 8 * RET_CHUNK)
        lru_out, ret_out, qa, ka, vt = side_mixers(
            proj_fox.reshape(bsz, seq, FOX_WIDTH), proj3, f_bias, lru_conv_w[l], lru_conv_b[l],
            _block_diag(lru_ra_w[l]).astype(BF16), lru_ra_b[l], _block_diag(lru_ri_w[l]).astype(BF16), lru_ri_b[l],
            lru_lambda[l], *ret_tables, ret_gn_w[l], ts=side_ts, sub=_tile(side_ts, tq), chunk=RET_CHUNK)
        fox_out = fox_attention(qa, ka, vt)

        vres = None
        if l > 0:
            vres = (v_first, rwkv_v0[l - 1],
                    jnp.pad(rwkv_v1[l - 1], ((0, 0), (0, 128 - RWKV_V_RANK))).astype(BF16),
                    _pad_rows(rwkv_v2[l - 1], 0, 128).astype(BF16))
        mm_, gm_, qm_, y0_, bonus_, g_, v_ = rwkv_chunk(
            proj3, rwkv_mu[l].reshape(4, GROUP), rwkv_w0[l],
            _pad_rows(rwkv_w2[l], 0, GROUP).astype(BF16), rwkv_a0[l],
            _pad_rows(rwkv_a2[l], RWKV_W_RANK, GROUP).astype(BF16),
            _pad_rows(rwkv_g2[l], RWKV_W_RANK + RWKV_A_RANK, GROUP).astype(BF16),
            rwkv_k_k[l], rwkv_k_a[l], rwkv_r_k[l], head_ones, vres, ts=_tile(seq, 8 * RWKV_CHUNK), chunk=RWKV_CHUNK)
        if l == 0:
            v_first = v_
        rwkv_out = rwkv_state(mm_, gm_, qm_, y0_, bonus_, g_, rwkv_gn_w[l], rwkv_gn_b[l],
                              ts=_tile(seq, 4 * RWKV_CHUNK), chunk=RWKV_CHUNK)

        kv = norm_matmul(mem.reshape(bsz * mem_len, d), norm_mem[l], wkv_b, l, tm=_tile(bsz * mem_len, 512),
                         tn=1024, out_dtype=BF16).reshape(bsz, mem_len, 2 * d)
        x2 = mix_xattn((fox_out, lru_out, rwkv_out, ret_out), x2.reshape(bsz, seq, d), kv, w_out_b, wq_b, wo_b, l,
                       norm_mix_post[l], norm_xa_pre[l], norm_xa_post[l], tm=_tile(seq, 1024)).reshape(n_tok, d)

        x2 = mlp(x2, w1_b, w2_b, l, norm_mlp_pre[l], norm_mlp_post[l], tm=tm, ff_tile=1024)
    return x2.reshape(bsz, seq, d)
```

```python
import functools
import math

import jax
import jax.numpy as jnp
from jax import lax
from jax.experimental import pallas as pl
from jax.experimental.pallas import tpu as pltpu

F32 = jnp.float32
BF16 = jnp.bfloat16

D_MODEL = 1024
GROUP = 256
HEADS = 4
HEAD_DIM = 64
CONV_WIDTH = 4
LRU_C = 8.0
RET_THETA = 10000.0
RET_CHUNK = 128
RWKV_W_RANK, RWKV_A_RANK, RWKV_G_RANK, RWKV_V_RANK = 64, 64, 128, 32
RWKV_GN_EPS = 64e-5
XATTN_HEADS = 4
XATTN_HEAD_DIM = D_MODEL // XATTN_HEADS
NORM_EPS = 1e-6
NEG_BIG = -1e30

FOX_Q, FOX_K, FOX_V, FOX_F = 0, 256, 512, 768
FOX_REAL = 3 * GROUP + HEADS
FOX_WIDTH = 896
LRU_X, LRU_Y = 0, 256
RWKV_R, RWKV_K, RWKV_V, RWKV_LR = 512, 768, 1024, 1280
RET_Q, RET_K, RET_V, RET_G = 1536, 1792, 2048, 2304
MIX_WIDTH = 2560

VMEM_LIMIT = 56 * 1024 * 1024
RWKV_CHUNK = 64


def _cparams(sem):
    return pltpu.CompilerParams(dimension_semantics=sem, vmem_limit_bytes=VMEM_LIMIT)


def _rms(x, g):
    return x * lax.rsqrt(jnp.mean(x * x, axis=-1, keepdims=True) + NORM_EPS) * g


def _log_sigmoid(x):
    return jnp.minimum(x, 0.0) - jnp.log1p(jnp.exp(-jnp.abs(x)))


def _sigmoid(x):
    return 1.0 / (1.0 + jnp.exp(-x))


def _dot(a, b, **kw):
    return jnp.dot(a, b, preferred_element_type=F32, **kw)


def _dot_nt(a, b, **kw):
    return lax.dot_general(a, b, (((1,), (1,)), ((), ())), preferred_element_type=F32, **kw)


def _split_bf16(x):
    hi = x.astype(BF16)
    return hi, (x - hi.astype(F32)).astype(BF16)


def _interleave(*step_generators):
    live = list(step_generators)
    while live:
        for gen in list(live):
            try:
                next(gen)
            except StopIteration:
                live.remove(gen)


def _bf16_terms(x, n):
    terms = []
    for _ in range(n - 1):
        t = x.astype(BF16)
        terms.append(t)
        x = x - t.astype(F32)
    return terms + [x.astype(BF16)]


def _dot_sel(a, b, *, split, n=3):
    if split == "a":
        parts = [_dot(t, b.astype(BF16)) for t in _bf16_terms(a, n)]
    else:
        parts = [_dot(a.astype(BF16), t) for t in _bf16_terms(b, n)]
    out = parts[-1]
    for p in reversed(parts[:-1]):
        out = out + p
    return out


def _tri(n, strict=False):
    r = lax.broadcasted_iota(jnp.int32, (n, n), 0)
    c = lax.broadcasted_iota(jnp.int32, (n, n), 1)
    return (r > c) if strict else (r >= c)


def _norm_matmul_kernel(x_ref, g_ref, w_ref, o_ref, xn_ref):
    @pl.when(pl.program_id(1) == 0)
    def _():
        xn_ref[...] = _rms(x_ref[...], g_ref[...]).astype(BF16)

    o_ref[...] = _dot(xn_ref[...], w_ref[...]).astype(o_ref.dtype)


def norm_matmul(x, g, w, layer, *, tm, tn, out_dtype):
    n, d = x.shape
    width = w.shape[2]
    return pl.pallas_call(
        _norm_matmul_kernel,
        grid=(n // tm, width // tn),
        in_specs=[pl.BlockSpec((tm, d), lambda i, j: (i, 0)),
                  pl.BlockSpec((1, d), lambda i, j: (0, 0)),
                  pl.BlockSpec((None, d, tn), lambda i, j: (layer, 0, j))],
        out_specs=pl.BlockSpec((tm, tn), lambda i, j: (i, j)),
        out_shape=jax.ShapeDtypeStruct((n, width), out_dtype),
        scratch_shapes=[pltpu.VMEM((tm, d), BF16)],
        compiler_params=_cparams(("parallel", "arbitrary")),
        name="norm_matmul",
    )(x, g.reshape(1, d), w)


def _in_proj_steps(rows, x_ref, g_ref, wf_ref, wm_ref, of_ref, om_ref, tn):
    xn = _rms(x_ref[rows, :], g_ref[...]).astype(BF16)
    yield
    of_ref[rows, :] = _dot_nt(xn, wf_ref[...])
    for c in range(wm_ref.shape[0] // tn):
        yield
        om_ref[rows, c * tn:(c + 1) * tn] = _dot_nt(xn, wm_ref[c * tn:(c + 1) * tn, :])


def _in_proj_kernel(x_ref, g_ref, wf_ref, wm_ref, of_ref, om_ref, *, tn):
    half = x_ref.shape[0] // 2
    _interleave(*[_in_proj_steps(slice(p * half, (p + 1) * half), x_ref, g_ref, wf_ref, wm_ref, of_ref, om_ref, tn)
                  for p in range(2)])


def in_proj(x, g, w_fox, w_mix, layer, *, tm, tn):
    n, d = x.shape
    weight = lambda w: pl.BlockSpec((None, w.shape[1], d), lambda i: (layer, 0, 0), pipeline_mode=pl.Buffered(1))
    out = lambda w: pl.BlockSpec((tm, w.shape[1]), lambda i: (i, 0))
    return pl.pallas_call(
        functools.partial(_in_proj_kernel, tn=tn),
        grid=(n // tm,),
        in_specs=[pl.BlockSpec((tm, d), lambda i: (i, 0)), pl.BlockSpec((1, d), lambda i: (0, 0)),
                  weight(w_fox), weight(w_mix)],
        out_specs=[out(w_fox), out(w_mix)],
        out_shape=[jax.ShapeDtypeStruct((n, w.shape[1]), F32) for w in (w_fox, w_mix)],
        compiler_params=_cparams(("parallel",)),
        name="in_proj",
    )(x, g.reshape(1, d), w_fox, w_mix)


FOX_AUG = 128
FOX_VROWS = HEAD_DIM + 16
LOG2E = 1.4426950408889634


def _fox_prep_steps(rows, tile, q_ref, k_ref, v_ref, f_ref, b_ref, qa_ref, ka_ref, vt_ref, carry_ref):
    lf = _log_sigmoid(f_ref[0, rows, :] + b_ref[...])
    tc = lf.shape[0]
    cum = _dot_sel(_tri(tc), lf, split="b") + carry_ref[...]
    carry_ref[...] = cum[tc - 1:tc, :]
    yield

    lane = lax.broadcasted_iota(jnp.int32, (tc, FOX_AUG), 1)
    feat = lane < HEAD_DIM
    ones = jnp.where((lane >= HEAD_DIM) & (lane < HEAD_DIM + 3), 1.0, 0.0)
    q = q_ref[0, rows, :] * (HEAD_DIM ** -0.5 * LOG2E)
    k = k_ref[0, rows, :]
    for h in range(HEADS):
        pair = slice((h // 2) * FOX_AUG, (h // 2 + 1) * FOX_AUG)
        qt, kt = q[:, pair], k[:, pair]
        if h % 2:
            qt, kt = pltpu.roll(qt, HEAD_DIM, axis=1), pltpu.roll(kt, HEAD_DIM, axis=1)
        neg_c = jnp.broadcast_to(cum[:, h:h + 1] * -LOG2E, (tc, FOX_AUG))
        c_hi = neg_c.astype(BF16).astype(F32)
        rest = neg_c - c_hi
        c_mid = rest.astype(BF16).astype(F32)
        bias = jnp.where(lane == HEAD_DIM, c_hi, jnp.where(lane == HEAD_DIM + 1, c_mid,
                                                           jnp.where(lane == HEAD_DIM + 2, rest - c_mid, 0.0)))
        qa_ref[0, h, rows, :] = jnp.where(feat, qt, ones).astype(BF16)
        ka_ref[0, h, rows, :] = jnp.where(feat, kt, bias).astype(BF16)
        yield
    ones_row = lax.broadcasted_iota(jnp.int32, (HEADS, FOX_VROWS - HEAD_DIM, tc), 1) == 0
    vt = jnp.concatenate([v_ref[0, rows, :].T.reshape(HEADS, HEAD_DIM, tc), ones_row.astype(F32)], axis=1)
    vt_ref[0, :, tile] = vt.astype(BF16)


def _fox_attn_kernel(q_ref, k_ref, v_ref, o_ref, s_ref, *, t):
    i = pl.program_id(1)
    qa = q_ref[0]

    def scores(j):
        return _bnt(k_ref[0, :, pl.ds(pl.multiple_of(j * t, t), t), :], qa)

    def update(carry, slot, j):
        m, acc = carry
        s = s_ref[slot]
        m_new = jnp.maximum(m, jnp.max(s, axis=1, keepdims=True))
        p = jnp.exp2(s - m_new).astype(BF16)
        return m_new, jnp.exp2(m - m_new) * acc + _bnn(v_ref[0, :, j], p)

    key = lax.broadcasted_iota(jnp.int32, (1, t, t), 1)
    qry = lax.broadcasted_iota(jnp.int32, (1, t, t), 2)
    s_ref[0] = jnp.where(key <= qry, scores(i), NEG_BIG)
    carry = (jnp.full((HEADS, 1, t), NEG_BIG, F32), jnp.zeros((HEADS, FOX_VROWS, t), F32))

    def pair(jj, carry):
        s_ref[1] = scores(2 * jj)
        carry = update(carry, 0, jnp.where(jj == 0, i, 2 * jj - 1))
        s_ref[0] = scores(2 * jj + 1)
        return update(carry, 1, 2 * jj)

    carry = lax.fori_loop(0, i // 2, pair, carry)
    pending = jnp.where(i < 2, i, 2 * (i // 2) - 1)

    def odd_tail(carry):
        s_ref[1] = scores(i - 1)
        return update(update(carry, 0, pending), 1, i - 1)

    _, acc = lax.cond(i % 2 == 1, odd_tail, lambda c: update(c, 0, pending), carry)
    out = acc[:, :HEAD_DIM] / acc[:, HEAD_DIM:HEAD_DIM + 1]
    o_ref[0] = out.reshape(GROUP, t).T.astype(o_ref.dtype)


def fox_attention(qa, ka, vt):
    b, h, s, _ = qa.shape
    nk, t = vt.shape[2], vt.shape[4]
    return pl.pallas_call(
        functools.partial(_fox_attn_kernel, t=t),
        grid=(b, nk),
        in_specs=[pl.BlockSpec((1, h, t, FOX_AUG), lambda bi, i: (bi, 0, i, 0)),
                  pl.BlockSpec((1, h, s, FOX_AUG), lambda bi, i: (bi, 0, 0, 0)),
                  pl.BlockSpec((1, h, nk, FOX_VROWS, t), lambda bi, i: (bi, 0, 0, 0, 0))],
        out_specs=pl.BlockSpec((1, t, GROUP), lambda bi, i: (bi, i, 0)),
        out_shape=jax.ShapeDtypeStruct((b, s, GROUP), BF16),
        scratch_shapes=[pltpu.VMEM((2, h, t, t), F32)],
        compiler_params=_cparams(("parallel", "arbitrary")),
        name="fox_attention",
    )(qa, ka, vt)


def _lru_steps(rows, x_ref, y_ref, cw_ref, cb_ref, wra_ref, bra_ref, wri_ref, bri_ref, lam_ref, o_ref, buf_ref,
               h_ref):
    ts = rows.stop - rows.start
    xb = x_ref[0, rows, :]
    buf_ref[8:8 + ts, :] = xb
    xc = cb_ref[...] + buf_ref[5:5 + ts, :] * cw_ref[0:1, :]
    for j in range(1, CONV_WIDTH):
        xc = xc + buf_ref[5 + j:5 + j + ts, :] * cw_ref[j:j + 1, :]
    buf_ref[0:8, :] = xb[ts - 8:ts, :]
    yield

    xcb = xc.astype(BF16)
    r = _sigmoid(_dot(xcb, wra_ref[...]) + bra_ref[...])
    yield
    gate_i = _sigmoid(_dot(xcb, wri_ref[...]) + bri_ref[...])
    yield
    log_a = LRU_C * r * _log_sigmoid(lam_ref[...])
    a = jnp.exp(log_a)
    z = 2.0 * log_a
    mult = jnp.sqrt(jnp.maximum(-jnp.tanh(0.5 * z) * (jnp.exp(z) + 1.0), 0.0))
    u = mult * (gate_i * xc)
    yield

    row = lax.broadcasted_iota(jnp.int32, (ts, GROUP), 0)
    pa, pb = a, u
    d = 1
    while d < ts:
        sa = pltpu.roll(pa, d, axis=0)
        sb = pltpu.roll(pb, d, axis=0)
        valid = row >= d
        pb = jnp.where(valid, pa * sb + pb, pb)
        pa = jnp.where(valid, pa * sa, pa)
        d *= 2
        yield
    hseq = pa * h_ref[...] + pb
    h_ref[...] = hseq[ts - 1:ts, :]

    y = y_ref[0, rows, :]
    gelu = 0.5 * y * (1.0 + jnp.tanh(math.sqrt(2.0 / math.pi) * (y + 0.044715 * (y * y * y))))
    o_ref[0, rows, :] = (hseq * gelu).astype(o_ref.dtype)


def _bdot(dims):
    return lambda a, b: lax.dot_general(a, b, (dims, ((0,), (0,))), preferred_element_type=F32)


_bnn = _bdot(((2,), (1,)))
_bnt = _bdot(((2,), (2,)))
_btn = _bdot(((1,), (1,)))


def _unit_lower_inverse(a_strict, n):
    r = lax.broadcasted_iota(jnp.int32, (1, n, n), 1)
    c = lax.broadcasted_iota(jnp.int32, (1, n, n), 2)
    a_b = a_strict.astype(BF16)
    zero, one = jnp.zeros((), BF16), jnp.ones((), BF16)
    t = jnp.where(r == c, one, jnp.where((r // 2 == c // 2) & (r > c), a_b, zero))
    m = 2
    while m < n:
        off = (r // (2 * m) == c // (2 * m)) & (r % (2 * m) >= m) & (c % (2 * m) < m)
        t = t + _bnn(t, _bnn(jnp.where(off, a_b, zero), t).astype(BF16)).astype(BF16)
        m *= 2
    return t


def _to_heads(x):
    n, rows, _ = x.shape
    parts = [x[:, :, h * HEAD_DIM:(h + 1) * HEAD_DIM] for h in range(HEADS)]
    return jnp.stack(parts, axis=1).reshape(n * HEADS, rows, HEAD_DIM)


def _from_heads(x):
    nh, rows, _ = x.shape
    x = x.reshape(nh // HEADS, HEADS, rows, HEAD_DIM)
    return jnp.concatenate([x[:, h] for h in range(HEADS)], axis=-1)


def _rwkv_chunk_kernel(*refs, ts, chunk, has_vres):
    (sr_ref, sk_ref, sv_ref, sl_ref, mu_ref, w0_ref, w2_ref, a0_ref, a2_ref, g2_ref, kk_ref, ka_ref, rk_ref,
     ones_ref) = refs[:14]
    refs = refs[14:]
    if has_vres:
        vf_ref, v0_ref, v1_ref, v2_ref = refs[:4]
        refs = refs[4:]
    mm_out, gm_out, qm_out, y0_out, bonus_out, g_out, v_out, carry_ref = refs

    @pl.when(pl.program_id(1) == 0)
    def _():
        carry_ref[...] = jnp.zeros_like(carry_ref)

    row0 = lax.broadcasted_iota(jnp.int32, (ts, GROUP), 0) == 0

    def shift_mix(ref, idx):
        s = ref[0]
        prev = jnp.where(row0, carry_ref[idx:idx + 1, :], pltpu.roll(s, 1, axis=0))
        carry_ref[idx:idx + 1, :] = s[ts - 1:ts, :]
        return s + (prev - s) * mu_ref[idx:idx + 1, :]

    r = shift_mix(sr_ref, 0)
    k = shift_mix(sk_ref, 1)
    v = shift_mix(sv_ref, 2)
    low = shift_mix(sl_ref, 3)

    zw = w0_ref[...] + _dot(jnp.tanh(low).astype(BF16), w2_ref[...])
    lw = -math.exp(-0.5) * _sigmoid(zw)
    a = _sigmoid(a0_ref[...] + _dot(low.astype(BF16), a2_ref[...]))
    g_out[0] = _dot(_sigmoid(low).astype(BF16), g2_ref[...])
    if has_vres:
        mix = _dot(_dot(v.astype(BF16), v1_ref[...]).astype(BF16), v2_ref[...])
        v = v + (vf_ref[0] - v) * _sigmoid(v0_ref[...] + mix)
    v_out[0] = v
    kk = k * kk_ref[...]
    ss = _dot_sel(kk * kk, ones_ref[...], split="a")
    kk = kk / jnp.maximum(jnp.sqrt(ss), 1e-12)
    k = k * (1.0 + (a - 1.0) * ka_ref[...])
    bonus_out[0] = _dot_sel(r * k * rk_ref[...], ones_ref[...], split="a") * v
    a_vec, b_vec = -kk, kk * a

    c = chunk
    rr = lax.broadcasted_iota(jnp.int32, (ts, ts), 0)
    cc = lax.broadcasted_iota(jnp.int32, (ts, ts), 1)
    chunk_tri = ((rr >= cc) & (rr // c == cc // c)).astype(F32)
    cum_all = _dot_sel(chunk_tri, lw, split="b")
    strict = _tri(c, strict=True)[None]
    incl = _tri(c)[None]
    eye = (lax.broadcasted_iota(jnp.int32, (1, HEAD_DIM, HEAD_DIM), 1)
           == lax.broadcasted_iota(jnp.int32, (1, HEAD_DIM, HEAD_DIM), 2)).astype(F32)

    nq = ts // c
    per_chunk = lambda t: t.reshape(nq, c, GROUP)
    cum = per_chunk(cum_all)
    rq, kq, vq, aq, bq = (per_chunk(t) for t in (r, k, v, a_vec, b_vec))
    cum_ex = cum - per_chunk(lw)
    mid = cum[:, c // 2 - 1:c // 2, :]
    tot = cum[:, c - 1:c, :]
    e_fwd = jnp.exp(cum - mid)
    e_bwd = jnp.exp(mid - cum)
    e_end = jnp.exp(tot - cum)
    mxu = lambda t: _to_heads(t.astype(BF16))
    r_rel, k_rel = mxu(rq * e_fwd), mxu(kq * e_bwd)
    a_rel, b_rel = mxu(aq * jnp.exp(cum_ex - mid)), mxu(bq * e_bwd)
    a_abs, r_abs = mxu(aq * jnp.exp(cum_ex)), rq * jnp.exp(cum)
    k_end, b_end = mxu(kq * e_end), mxu(bq * e_end)
    gam = _to_heads(jnp.exp(tot))
    vh = mxu(vq)

    ar_rel = jnp.concatenate([a_rel, r_rel], axis=1)
    s_b, s_k = _bnt(ar_rel, b_rel), _bnt(ar_rel, k_rel)
    a_ab = jnp.where(strict, s_b[:, :c], 0.0)
    zero = jnp.zeros((), BF16)
    a_ak = jnp.where(strict, s_k[:, :c].astype(BF16), zero)
    a_rb = jnp.where(incl, s_b[:, c:].astype(BF16), zero)
    a_rk = jnp.where(incl, s_k[:, c:].astype(BF16), zero)
    t_inv = _unit_lower_inverse(a_ab, c)
    pu = _bnn(t_inv, jnp.concatenate([a_abs, _bnn(a_ak, vh).astype(BF16)], axis=-1)).astype(BF16)
    z = _bnn(a_rb, pu)
    y0 = z[..., HEAD_DIM:] + _bnn(a_rk, vh)
    xtb = _btn(pu, b_end)
    mm_mat = eye * gam + xtb[:, :HEAD_DIM]
    gm = xtb[:, HEAD_DIM:] + _btn(vh, k_end)
    dense = lambda t: _from_heads(t).reshape(ts, GROUP)
    mm_out[0] = dense(mm_mat).astype(mm_out.dtype)
    gm_out[0] = dense(gm)
    qm_out[0] = (dense(z[..., :HEAD_DIM]) + r_abs.reshape(ts, GROUP)).astype(qm_out.dtype)
    y0_out[0] = dense(y0)


def rwkv_chunk(proj3, mu4, w0, w2p, a0, a2p, g2p, k_k, k_a, r_k, head_ones, vres, *, ts, chunk):
    b, s, _ = proj3.shape
    slab = lambda off: pl.BlockSpec((1, ts, GROUP), lambda i, j: (i, j, off // GROUP))
    vec = lambda: pl.BlockSpec((1, GROUP), lambda i, j: (0, 0))
    full = lambda shape: pl.BlockSpec(shape, lambda i, j: tuple(0 for _ in shape))
    tok = pl.BlockSpec((1, ts, GROUP), lambda i, j: (i, j, 0))
    in_specs = [slab(RWKV_R), slab(RWKV_K), slab(RWKV_V), slab(RWKV_LR), full((4, GROUP)),
                vec(), full((GROUP, GROUP)), vec(), full((GROUP, GROUP)), full((GROUP, GROUP)), vec(), vec(), vec(),
                full((GROUP, GROUP))]
    args = [proj3, proj3, proj3, proj3, mu4, w0.reshape(1, GROUP), w2p, a0.reshape(1, GROUP), a2p, g2p,
            k_k.reshape(1, GROUP), k_a.reshape(1, GROUP), r_k.reshape(1, GROUP), head_ones]
    if vres is not None:
        v_first, v0, v1p, v2p = vres
        in_specs += [tok, vec(), full((GROUP, 128)), full((128, GROUP))]
        args += [v_first, v0.reshape(1, GROUP), v1p, v2p]
    return pl.pallas_call(
        functools.partial(_rwkv_chunk_kernel, ts=ts, chunk=chunk, has_vres=vres is not None),
        grid=(b, s // ts),
        in_specs=in_specs,
        out_specs=[tok] * 7,
        out_shape=[jax.ShapeDtypeStruct((b, s, GROUP), dt) for dt in (BF16, F32, BF16, F32, F32, F32, F32)],
        scratch_shapes=[pltpu.VMEM((4, GROUP), F32)],
        compiler_params=_cparams(("parallel", "arbitrary")),
        name="rwkv_chunk",
    )(*args)


def _rwkv_state_kernel(mm_ref, gm_ref, qm_ref, y0_ref, bonus_ref, g_ref, gw_ref, gb_ref, o_ref, state_ref, *, chunk):
    @pl.when(pl.program_id(0) == 0)
    def _():
        state_ref[...] = jnp.zeros_like(state_ref)

    state = state_ref[...]
    for c in range(mm_ref.shape[1] // chunk):
        rows = slice(c * chunk, (c + 1) * chunk)
        s_hi, s_lo = _split_bf16(state)
        qm, mm = _to_heads(qm_ref[:, rows, :]), _to_heads(mm_ref[:, rows, :])
        y = _bnt(qm, s_hi) + _bnt(qm, s_lo) + _to_heads(y0_ref[:, rows, :])
        state = _bnn(s_hi, mm) + _bnn(s_lo, mm) + _to_heads(gm_ref[:, rows, :])
        mu = jnp.mean(y, axis=-1, keepdims=True)
        var = jnp.mean(jnp.square(y - mu), axis=-1, keepdims=True)
        yn = _from_heads((y - mu) * lax.rsqrt(var + RWKV_GN_EPS)) * gw_ref[...] + gb_ref[...]
        o_ref[:, rows, :] = ((yn + bonus_ref[:, rows, :]) * g_ref[:, rows, :]).astype(o_ref.dtype)
    state_ref[...] = state


def rwkv_state(mm, gm, qm, y0, bonus, g, gn_w, gn_b, *, ts, chunk):
    bsz, s, _ = mm.shape
    tok = pl.BlockSpec((bsz, ts, GROUP), lambda j: (0, j, 0))
    vec = pl.BlockSpec((1, GROUP), lambda j: (0, 0))
    return pl.pallas_call(
        functools.partial(_rwkv_state_kernel, chunk=chunk),
        grid=(s // ts,),
        in_specs=[tok] * 6 + [vec] * 2,
        out_specs=tok,
        out_shape=jax.ShapeDtypeStruct((bsz, s, GROUP), BF16),
        scratch_shapes=[pltpu.VMEM((bsz * HEADS, HEAD_DIM, HEAD_DIM), F32)],
        compiler_params=_cparams(("arbitrary",)),
        name="rwkv_state",
    )(mm, gm, qm, y0, bonus, g, gn_w.reshape(1, GROUP), gn_b.reshape(1, GROUP))


def _ret_steps(q_ref, k_ref, v_ref, g_ref, cos_ref, sin_ref, dmat_ref, xi_ref, zeta_ref, cd_ref, gw_ref, o_ref,
               state_ref, *, ts, chunk):
    lane = lax.broadcasted_iota(jnp.int32, (ts, GROUP), 1)
    first_half = (lane % HEAD_DIM) < (HEAD_DIM // 2)
    cos, sin = cos_ref[...], sin_ref[...]

    def rotary(t):
        partner = jnp.where(first_half, pltpu.roll(t, GROUP - HEAD_DIM // 2, axis=1),
                            pltpu.roll(t, HEAD_DIM // 2, axis=1))
        return t * cos + partner * sin

    nq = ts // chunk
    per_chunk = lambda t: t.reshape(nq, chunk, GROUP)
    mxu = lambda t: _to_heads(t).astype(BF16)
    q = per_chunk(rotary(q_ref[0]))
    yield
    k = per_chunk(rotary(k_ref[0]) * (HEAD_DIM ** -0.5))
    yield
    qb, kb, vb = mxu(q), mxu(k), mxu(per_chunk(v_ref[0]))
    yield
    q_cross, k_decay = mxu(q * xi_ref[...]), mxu(k * zeta_ref[...])
    yield
    inner = _bnt(qb, kb).reshape(nq, HEADS, chunk, chunk) * dmat_ref[...]
    yield
    intra = _bnn(inner.reshape(nq * HEADS, chunk, chunk).astype(BF16), vb)
    yield
    kv = _btn(k_decay, vb)
    yield

    decay = _to_heads(cd_ref[...][None])
    state = state_ref[...]
    incoming = []
    for c in range(nq):
        incoming.append(state)
        state = state * decay + kv[c * HEADS:(c + 1) * HEADS]
    state_ref[...] = state
    o = intra + _bnn(q_cross, jnp.concatenate(incoming, axis=0).astype(BF16))
    yield
    o = _from_heads(o * lax.rsqrt(jnp.mean(o * o, axis=-1, keepdims=True) + NORM_EPS)).reshape(ts, GROUP)
    g = g_ref[0]
    o_ref[0] = (o * gw_ref[...] * (g * _sigmoid(g))).astype(o_ref.dtype)


def _side_mixers_kernel(*refs, ts, sub, chunk):
    lru_in, ret_in, fox_in = refs[:9], refs[9:20], refs[20:25]
    lru_out, ret_out, qa_ref, ka_ref, vt_ref, buf_ref, h_ref, state_ref, cum_ref = refs[25:]

    @pl.when(pl.program_id(1) == 0)
    def _():
        buf_ref[0:8, :] = jnp.zeros((8, GROUP), F32)
        h_ref[...] = jnp.zeros_like(h_ref)
        state_ref[...] = jnp.zeros_like(state_ref)
        cum_ref[...] = jnp.zeros_like(cum_ref)

    def lru_ranges():
        for start in range(0, ts, sub):
            yield from _lru_steps(slice(start, start + sub), *lru_in, lru_out, buf_ref, h_ref)

    def fox_ranges():
        for tile in range(ts // sub):
            yield from _fox_prep_steps(slice(tile * sub, (tile + 1) * sub), tile, *fox_in, qa_ref, ka_ref, vt_ref,
                                       cum_ref)

    _interleave(_ret_steps(*ret_in, ret_out, state_ref, ts=ts, chunk=chunk), lru_ranges(), fox_ranges())


def side_mixers(proj_fox3, proj3, f_bias_pad, conv_w, conv_b, wra_bd, ra_b, wri_bd, ri_b, lam, cos_t, sin_t, dmat, xi,
                zeta, cd, gn_w, *, ts, sub, chunk):
    b, s, _ = proj3.shape
    slab = lambda off: pl.BlockSpec((1, ts, GROUP), lambda i, j: (i, j, off // GROUP))
    full = lambda shape: pl.BlockSpec(shape, lambda i, j: tuple(0 for _ in shape))
    vec, mat = full((1, GROUP)), full((GROUP, GROUP))
    table = pl.BlockSpec((ts, GROUP), lambda i, j: (j, 0))
    out = pl.BlockSpec((1, ts, GROUP), lambda i, j: (i, j, 0))
    aug = pl.BlockSpec((1, HEADS, ts, FOX_AUG), lambda i, j: (i, 0, j, 0))
    f_width = FOX_WIDTH - FOX_F
    row = lambda t: t.reshape(1, GROUP)
    return pl.pallas_call(
        functools.partial(_side_mixers_kernel, ts=ts, sub=sub, chunk=chunk),
        grid=(b, s // ts),
        in_specs=[slab(LRU_X), slab(LRU_Y), full((CONV_WIDTH, GROUP)), vec, mat, vec, mat, vec, vec,
                  slab(RET_Q), slab(RET_K), slab(RET_V), slab(RET_G), table, table,
                  full((HEADS, chunk, chunk)), full((chunk, GROUP)), full((chunk, GROUP)), vec, vec,
                  slab(FOX_Q), slab(FOX_K), slab(FOX_V),
                  pl.BlockSpec((1, ts, f_width), lambda i, j: (i, j, FOX_F // f_width)), full((1, f_width))],
        out_specs=[out, out, aug, aug,
                   pl.BlockSpec((1, HEADS, ts // sub, FOX_VROWS, sub), lambda i, j: (i, 0, j, 0, 0))],
        out_shape=[jax.ShapeDtypeStruct((b, s, GROUP), BF16)] * 2
                  + [jax.ShapeDtypeStruct((b, HEADS, s, FOX_AUG), BF16)] * 2
                  + [jax.ShapeDtypeStruct((b, HEADS, s // sub, FOX_VROWS, sub), BF16)],
        scratch_shapes=[pltpu.VMEM((sub + 8, GROUP), F32), pltpu.VMEM((1, GROUP), F32),
                        pltpu.VMEM((HEADS, HEAD_DIM, HEAD_DIM), F32), pltpu.VMEM((1, f_width), F32)],
        compiler_params=_cparams(("parallel", "arbitrary")),
        name="side_mixers",
    )(proj3, proj3, conv_w, row(conv_b), wra_bd, row(ra_b), wri_bd, row(ri_b), row(lam),
      proj3, proj3, proj3, proj3, cos_t, sin_t, dmat, xi, zeta, cd, row(gn_w),
      proj_fox3, proj_fox3, proj_fox3, proj_fox3, f_bias_pad)


def _retention_tables(s, chunk):
    half = HEAD_DIM // 2
    inv = 1.0 / (RET_THETA ** jnp.linspace(0.0, 1.0, half, dtype=F32))
    ang = jnp.arange(s, dtype=F32)[:, None] * inv[None, :]
    cos, sin = jnp.cos(ang), jnp.sin(ang)
    cos_t = jnp.tile(jnp.concatenate([cos, cos], axis=-1), (1, HEADS))
    sin_t = jnp.tile(jnp.concatenate([-sin, sin], axis=-1), (1, HEADS))
    lg = jnp.log(1.0 - 2.0 ** (-5.0 - jnp.arange(HEADS, dtype=F32)))
    n = jnp.arange(chunk, dtype=F32)
    diff = n[:, None] - n[None, :]
    dmat = jnp.where(diff >= 0, jnp.exp(lg[:, None, None] * jnp.maximum(diff, 0.0)), 0.0)
    zeta = jnp.exp(lg[:, None] * (chunk - 1.0 - n)[None, :])
    xi = jnp.exp(lg[:, None] * (n + 1.0)[None, :])
    per_lane = lambda t: jnp.repeat(t.T, HEAD_DIM, axis=1)
    cd = jnp.repeat(jnp.exp(lg * chunk), HEAD_DIM)[None, :]
    return cos_t, sin_t, dmat, per_lane(xi), per_lane(zeta), cd


MIX_XATTN_PARTS = 4


def _mix_xattn_steps(rows, m_refs, x_ref, kv_ref, wout_ref, wq_ref, wo_ref, gmix_ref, gpre_ref, gpost_ref, o_ref):
    acc = _dot(m_refs[0][0, rows, :], wout_ref[0:GROUP, :])
    for idx, m_ref in enumerate(m_refs[1:], start=1):
        acc = acc + _dot(m_ref[0, rows, :], wout_ref[idx * GROUP:(idx + 1) * GROUP, :])
    yield
    x = x_ref[0, rows, :] + _rms(acc, gmix_ref[...])
    xn = _rms(x, gpre_ref[...]).astype(BF16)
    q = (_dot(xn, wq_ref[...]) * (XATTN_HEAD_DIM ** -0.5)).astype(BF16)
    yield
    heads = lambda t, off: jnp.stack([t[:, off + h * XATTN_HEAD_DIM:off + (h + 1) * XATTN_HEAD_DIM]
                                      for h in range(XATTN_HEADS)])
    kv = kv_ref[0]
    s = _bnt(heads(q, 0), heads(kv, 0))
    yield
    e = jnp.exp(s - jnp.max(s, axis=-1, keepdims=True))
    p = e / jnp.sum(e, axis=-1, keepdims=True)
    o = _bnn(p.astype(BF16), heads(kv, D_MODEL)).astype(BF16)
    yield
    o = jnp.concatenate([o[h] for h in range(XATTN_HEADS)], axis=-1)
    o_ref[0, rows, :] = x + _rms(_dot(o, wo_ref[...]), gpost_ref[...])


def _mix_xattn_kernel(m0_ref, m1_ref, m2_ref, m3_ref, x_ref, kv_ref, wout_ref, wq_ref, wo_ref, gmix_ref, gpre_ref,
                      gpost_ref, o_ref):
    n = x_ref.shape[1] // MIX_XATTN_PARTS
    _interleave(*[_mix_xattn_steps(slice(part * n, (part + 1) * n), (m0_ref, m1_ref, m2_ref, m3_ref), x_ref,
                                   kv_ref, wout_ref, wq_ref, wo_ref, gmix_ref, gpre_ref, gpost_ref, o_ref)
                  for part in range(MIX_XATTN_PARTS)])


def mix_xattn(mixed, x3, kv, w_out, wq, wo, layer, g_mix, g_pre, g_post, *, tm):
    b, s, d = x3.shape
    m = kv.shape[1]
    tok = lambda width: pl.BlockSpec((1, tm, width), lambda bi, i: (bi, i, 0))
    weight = lambda: pl.BlockSpec((None, d, d), lambda bi, i: (layer, 0, 0))
    vec = lambda: pl.BlockSpec((1, d), lambda bi, i: (0, 0))
    return pl.pallas_call(
        _mix_xattn_kernel,
        grid=(b, s // tm),
        in_specs=[tok(GROUP)] * 4 + [tok(d), pl.BlockSpec((1, m, 2 * d), lambda bi, i: (bi, 0, 0)),
                                     weight(), weight(), weight(), vec(), vec(), vec()],
        out_specs=tok(d),
        out_shape=jax.ShapeDtypeStruct((b, s, d), F32),
        compiler_params=_cparams(("parallel", "arbitrary")),
        name="mix_xattn",
    )(*mixed, x3, kv, w_out, wq, wo, g_mix.reshape(1, d), g_pre.reshape(1, d), g_post.reshape(1, d))


def _mlp_steps(rows, x_ref, w1_ref, w2_ref, gpre_ref, gpost_ref, o_ref, ff_tile):
    x = x_ref[rows, :]
    xn = _rms(x, gpre_ref[...]).astype(BF16)
    yield
    d_ff = w1_ref.shape[1]
    acc = None
    for c in range(d_ff // ff_tile):
        hid = jnp.square(jnp.maximum(_dot(xn, w1_ref[:, c * ff_tile:(c + 1) * ff_tile]), 0.0)).astype(BF16)
        yield
        part = _dot(hid, w2_ref[c * ff_tile:(c + 1) * ff_tile, :])
        acc = part if acc is None else acc + part
        yield
    o_ref[rows, :] = x + _rms(acc, gpost_ref[...])


def _mlp_kernel(x_ref, w1_ref, w2_ref, gpre_ref, gpost_ref, o_ref, *, ff_tile):
    half = x_ref.shape[0] // 2
    _interleave(*[_mlp_steps(slice(p * half, (p + 1) * half), x_ref, w1_ref, w2_ref, gpre_ref, gpost_ref, o_ref,
                             ff_tile) for p in range(2)])


def mlp(x, w1, w2, layer, g_pre, g_post, *, tm, ff_tile):
    n, d = x.shape
    d_ff = w1.shape[2]
    return pl.pallas_call(
        functools.partial(_mlp_kernel, ff_tile=ff_tile),
        grid=(n // tm,),
        in_specs=[pl.BlockSpec((tm, d), lambda i: (i, 0)),
                  pl.BlockSpec((None, d, d_ff), lambda i: (layer, 0, 0), pipeline_mode=pl.Buffered(1)),
                  pl.BlockSpec((None, d_ff, d), lambda i: (layer, 0, 0), pipeline_mode=pl.Buffered(1)),
                  pl.BlockSpec((1, d), lambda i: (0, 0)),
                  pl.BlockSpec((1, d), lambda i: (0, 0))],
        out_specs=pl.BlockSpec((tm, d), lambda i: (i, 0)),
        out_shape=jax.ShapeDtypeStruct((n, d), F32),
        compiler_params=_cparams(("parallel",)),
        name="mlp",
    )(x, w1, w2, g_pre.reshape(1, d), g_post.reshape(1, d))


def _block_diag(w):
    h, n, _ = w.shape
    eye = jnp.eye(h, dtype=w.dtype)
    return (eye[:, None, :, None] * w[:, :, None, :]).reshape(h * n, h * n)


def _pad_rows(w, start, total):
    return jnp.zeros((total, w.shape[1]), w.dtype).at[start:start + w.shape[0]].set(w)


def _tile(n, pref):
    return pref if n % pref == 0 else n


def kernel(x, mem, norm_mix_pre, norm_mix_post, norm_xa_pre, norm_xa_post, norm_mem, norm_mlp_pre, norm_mlp_post, w_in, w_out, fox_f_bias, lru_conv_w, lru_conv_b, lru_ra_w, lru_ra_b, lru_ri_w, lru_ri_b, lru_lambda, rwkv_mu, rwkv_w0, rwkv_w2, rwkv_a0, rwkv_a2, rwkv_g2, rwkv_k_k, rwkv_k_a, rwkv_r_k, rwkv_gn_w, rwkv_gn_b, rwkv_v0, rwkv_v1, rwkv_v2, ret_gn_w, xa_wq, xa_wk, xa_wv, xa_wo, mlp_w1, mlp_w2):
    bsz, seq, d = x.shape
    depth = w_in.shape[0]
    n_tok = bsz * seq
    mem_len = mem.shape[1]
    tm = _tile(n_tok, 512)
    tq = _tile(seq, 512)
    ret_tables = _retention_tables(seq, RET_CHUNK)
    head_ones = _block_diag(jnp.ones((HEADS, HEAD_DIM, HEAD_DIM), BF16))

    w_in_t = w_in.astype(BF16).swapaxes(1, 2)
    w_fox_b, w_mix_b = w_in_t[:, :FOX_WIDTH], w_in_t[:, FOX_REAL:]
    w_out_b = w_out.astype(BF16)
    wq_b, wo_b = xa_wq.astype(BF16), xa_wo.astype(BF16)
    wkv_b = jnp.concatenate([xa_wk, xa_wv], axis=-1).astype(BF16)
    w1_b, w2_b = mlp_w1.astype(BF16), mlp_w2.astype(BF16)

    x2 = x.reshape(n_tok, d)
    v_first = None
    for l in range(depth):
        proj_fox, proj_mix = in_proj(x2, norm_mix_pre[l], w_fox_b, w_mix_b, l, tm=_tile(n_tok, 1024),
                                     tn=MIX_WIDTH // 5)
        proj3 = proj_mix.reshape(bsz, seq, MIX_WIDTH)

        f_bias = jnp.zeros((1, FOX_WIDTH - FOX_F), F32).at[0, :HEADS].set(fox_f_bias[l])
        side_ts = _tile(seq, 8 * RET_CHUNK)
        lru_out, ret_out, qa, ka, vt = side_mixers(
            proj_fox.reshape(bsz, seq, FOX_WIDTH), proj3, f_bias, lru_conv_w[l], lru_conv_b[l],
            _block_diag(lru_ra_w[l]).astype(BF16), lru_ra_b[l], _block_diag(lru_ri_w[l]).astype(BF16), lru_ri_b[l],
            lru_lambda[l], *ret_tables, ret_gn_w[l], ts=side_ts, sub=_tile(side_ts, tq), chunk=RET_CHUNK)
        fox_out = fox_attention(qa, ka, vt)

        vres = None
        if l > 0:
            vres = (v_first, rwkv_v0[l - 1],
                    jnp.pad(rwkv_v1[l - 1], ((0, 0), (0, 128 - RWKV_V_RANK))).astype(BF16),
                    _pad_rows(rwkv_v2[l - 1], 0, 128).astype(BF16))
        mm_, gm_, qm_, y0_, bonus_, g_, v_ = rwkv_chunk(
            proj3, rwkv_mu[l].reshape(4, GROUP), rwkv_w0[l],
            _pad_rows(rwkv_w2[l], 0, GROUP).astype(BF16), rwkv_a0[l],
            _pad_rows(rwkv_a2[l], RWKV_W_RANK, GROUP).astype(BF16),
            _pad_rows(rwkv_g2[l], RWKV_W_RANK + RWKV_A_RANK, GROUP).astype(BF16),
            rwkv_k_k[l], rwkv_k_a[l], rwkv_r_k[l], head_ones, vres, ts=_tile(seq, 8 * RWKV_CHUNK), chunk=RWKV_CHUNK)
        if l == 0:
            v_first = v_
        rwkv_out = rwkv_state(mm_, gm_, qm_, y0_, bonus_, g_, rwkv_gn_w[l], rwkv_gn_b[l],
                              ts=_tile(seq, 4 * RWKV_CHUNK), chunk=RWKV_CHUNK)

        kv = norm_matmul(mem.reshape(bsz * mem_len, d), norm_mem[l], wkv_b, l, tm=_tile(bsz * mem_len, 512),
                         tn=1024, out_dtype=BF16).reshape(bsz, mem_len, 2 * d)
        x2 = mix_xattn((fox_out, lru_out, rwkv_out, ret_out), x2.reshape(bsz, seq, d), kv, w_out_b, wq_b, wo_b, l,
                       norm_mix_post[l], norm_xa_pre[l], norm_xa_post[l], tm=_tile(seq, 1024)).reshape(n_tok, d)

        x2 = mlp(x2, w1_b, w2_b, l, norm_mlp_pre[l], norm_mlp_post[l], tm=_tile(n_tok, 1024), ff_tile=1024)
    return x2.reshape(bsz, seq, d)
```

```python
import functools
import math

import jax
import jax.numpy as jnp
from jax import lax
from jax.experimental import pallas as pl
from jax.experimental.pallas import tpu as pltpu

F32 = jnp.float32
BF16 = jnp.bfloat16

D_MODEL = 1024
GROUP = 256
HEADS = 4
HEAD_DIM = 64
CONV_WIDTH = 4
LRU_C = 8.0
RET_THETA = 10000.0
RET_CHUNK = 128
RWKV_W_RANK, RWKV_A_RANK, RWKV_G_RANK, RWKV_V_RANK = 64, 64, 128, 32
RWKV_GN_EPS = 64e-5
XATTN_HEADS = 4
XATTN_HEAD_DIM = D_MODEL // XATTN_HEADS
NORM_EPS = 1e-6
NEG_BIG = -1e30

FOX_Q, FOX_K, FOX_V, FOX_F = 0, 256, 512, 768
FOX_REAL = 3 * GROUP + HEADS
FOX_WIDTH = 896
LRU_X, LRU_Y = 0, 256
RWKV_R, RWKV_K, RWKV_V, RWKV_LR = 512, 768, 1024, 1280
RET_Q, RET_K, RET_V, RET_G = 1536, 1792, 2048, 2304
MIX_WIDTH = 2560

VMEM_LIMIT = 56 * 1024 * 1024
RWKV_CHUNK = 64


def _cparams(sem):
    return pltpu.CompilerParams(dimension_semantics=sem, vmem_limit_bytes=VMEM_LIMIT)


def _rms(x, g):
    return x * lax.rsqrt(jnp.mean(x * x, axis=-1, keepdims=True) + NORM_EPS) * g


def _log_sigmoid(x):
    return jnp.minimum(x, 0.0) - jnp.log1p(jnp.exp(-jnp.abs(x)))


def _sigmoid(x):
    return 1.0 / (1.0 + jnp.exp(-x))


def _dot(a, b, **kw):
    return jnp.dot(a, b, preferred_element_type=F32, **kw)


def _dot_nt(a, b, **kw):
    return lax.dot_general(a, b, (((1,), (1,)), ((), ())), preferred_element_type=F32, **kw)


def _split_bf16(x):
    hi = x.astype(BF16)
    return hi, (x - hi.astype(F32)).astype(BF16)


def _interleave(*step_generators):
    live = list(step_generators)
    while live:
        for gen in list(live):
            try:
                next(gen)
            except StopIteration:
                live.remove(gen)


def _bf16_terms(x, n):
    terms = []
    for _ in range(n - 1):
        t = x.astype(BF16)
        terms.append(t)
        x = x - t.astype(F32)
    return terms + [x.astype(BF16)]


def _dot_sel(a, b, *, split, n=3):
    if split == "a":
        parts = [_dot(t, b.astype(BF16)) for t in _bf16_terms(a, n)]
    else:
        parts = [_dot(a.astype(BF16), t) for t in _bf16_terms(b, n)]
    out = parts[-1]
    for p in reversed(parts[:-1]):
        out = out + p
    return out


def _tri(n, strict=False):
    r = lax.broadcasted_iota(jnp.int32, (n, n), 0)
    c = lax.broadcasted_iota(jnp.int32, (n, n), 1)
    return (r > c) if strict else (r >= c)


def _norm_matmul_kernel(x_ref, g_ref, w_ref, o_ref, xn_ref):
    @pl.when(pl.program_id(1) == 0)
    def _():
        xn_ref[...] = _rms(x_ref[...], g_ref[...]).astype(BF16)

    o_ref[...] = _dot(xn_ref[...], w_ref[...]).astype(o_ref.dtype)


def norm_matmul(x, g, w, layer, *, tm, tn, out_dtype):
    n, d = x.shape
    width = w.shape[2]
    return pl.pallas_call(
        _norm_matmul_kernel,
        grid=(n // tm, width // tn),
        in_specs=[pl.BlockSpec((tm, d), lambda i, j: (i, 0)),
                  pl.BlockSpec((1, d), lambda i, j: (0, 0)),
                  pl.BlockSpec((None, d, tn), lambda i, j: (layer, 0, j))],
        out_specs=pl.BlockSpec((tm, tn), lambda i, j: (i, j)),
        out_shape=jax.ShapeDtypeStruct((n, width), out_dtype),
        scratch_shapes=[pltpu.VMEM((tm, d), BF16)],
        compiler_params=_cparams(("parallel", "arbitrary")),
        name="norm_matmul",
    )(x, g.reshape(1, d), w)


def _in_proj_steps(rows, x_ref, g_ref, wf_ref, wm_ref, of_ref, om_ref, tn):
    xn = _rms(x_ref[rows, :], g_ref[...]).astype(BF16)
    yield
    of_ref[rows, :] = _dot_nt(xn, wf_ref[...])
    for c in range(wm_ref.shape[0] // tn):
        yield
        om_ref[rows, c * tn:(c + 1) * tn] = _dot_nt(xn, wm_ref[c * tn:(c + 1) * tn, :])


def _in_proj_kernel(x_ref, g_ref, wf_ref, wm_ref, of_ref, om_ref, *, tn):
    half = x_ref.shape[0] // 2
    _interleave(*[_in_proj_steps(slice(p * half, (p + 1) * half), x_ref, g_ref, wf_ref, wm_ref, of_ref, om_ref, tn)
                  for p in range(2)])


def in_proj(x, g, w_fox, w_mix, layer, *, tm, tn):
    n, d = x.shape
    weight = lambda w: pl.BlockSpec((None, w.shape[1], d), lambda i: (layer, 0, 0), pipeline_mode=pl.Buffered(1))
    out = lambda w: pl.BlockSpec((tm, w.shape[1]), lambda i: (i, 0))
    return pl.pallas_call(
        functools.partial(_in_proj_kernel, tn=tn),
        grid=(n // tm,),
        in_specs=[pl.BlockSpec((tm, d), lambda i: (i, 0)), pl.BlockSpec((1, d), lambda i: (0, 0)),
                  weight(w_fox), weight(w_mix)],
        out_specs=[out(w_fox), out(w_mix)],
        out_shape=[jax.ShapeDtypeStruct((n, w.shape[1]), F32) for w in (w_fox, w_mix)],
        compiler_params=_cparams(("parallel",)),
        name="in_proj",
    )(x, g.reshape(1, d), w_fox, w_mix)


FOX_AUG = 128
FOX_VROWS = HEAD_DIM + 16
LOG2E = 1.4426950408889634


def _fox_prep_steps(rows, tile, q_ref, k_ref, v_ref, f_ref, b_ref, qa_ref, ka_ref, vt_ref, carry_ref):
    lf = _log_sigmoid(f_ref[0, rows, :] + b_ref[...])
    tc = lf.shape[0]
    cum = _dot_sel(_tri(tc), lf, split="b") + carry_ref[...]
    carry_ref[...] = cum[tc - 1:tc, :]
    yield

    lane = lax.broadcasted_iota(jnp.int32, (tc, FOX_AUG), 1)
    feat = lane < HEAD_DIM
    ones = jnp.where((lane >= HEAD_DIM) & (lane < HEAD_DIM + 3), 1.0, 0.0)
    q = q_ref[0, rows, :] * (HEAD_DIM ** -0.5 * LOG2E)
    k = k_ref[0, rows, :]
    for h in range(HEADS):
        pair = slice((h // 2) * FOX_AUG, (h // 2 + 1) * FOX_AUG)
        qt, kt = q[:, pair], k[:, pair]
        if h % 2:
            qt, kt = pltpu.roll(qt, HEAD_DIM, axis=1), pltpu.roll(kt, HEAD_DIM, axis=1)
        neg_c = jnp.broadcast_to(cum[:, h:h + 1] * -LOG2E, (tc, FOX_AUG))
        c_hi = neg_c.astype(BF16).astype(F32)
        rest = neg_c - c_hi
        c_mid = rest.astype(BF16).astype(F32)
        bias = jnp.where(lane == HEAD_DIM, c_hi, jnp.where(lane == HEAD_DIM + 1, c_mid,
                                                           jnp.where(lane == HEAD_DIM + 2, rest - c_mid, 0.0)))
        qa_ref[0, h, rows, :] = jnp.where(feat, qt, ones).astype(BF16)
        ka_ref[0, h, rows, :] = jnp.where(feat, kt, bias).astype(BF16)
        yield
    ones_row = lax.broadcasted_iota(jnp.int32, (HEADS, FOX_VROWS - HEAD_DIM, tc), 1) == 0
    vt = jnp.concatenate([v_ref[0, rows, :].T.reshape(HEADS, HEAD_DIM, tc), ones_row.astype(F32)], axis=1)
    vt_ref[0, :, tile] = vt.astype(BF16)


def _fox_attn_kernel(q_ref, k_ref, v_ref, o_ref, s_ref, *, t):
    i = pl.program_id(1)
    qa = q_ref[0]

    def scores(j):
        return _bnt(k_ref[0, :, pl.ds(pl.multiple_of(j * t, t), t), :], qa)

    def update(carry, slot, j):
        m, acc = carry
        s = s_ref[slot]
        m_new = jnp.maximum(m, jnp.max(s, axis=1, keepdims=True))
        p = jnp.exp2(s - m_new).astype(BF16)
        return m_new, jnp.exp2(m - m_new) * acc + _bnn(v_ref[0, :, j], p)

    h = t // 2
    k_diag = k_ref[0, :, pl.ds(pl.multiple_of(i * t, t), t), :]
    v_diag = v_ref[0, :, i]
    tri = (lax.broadcasted_iota(jnp.int32, (1, h, h), 1) <= lax.broadcasted_iota(jnp.int32, (1, h, h), 2))
    s_q = jnp.where(tri, _bnt(k_diag[:, h:], qa[:, h:]), NEG_BIG)
    m_q = jnp.max(s_q, axis=1, keepdims=True)
    acc_q = _bnn(v_diag[:, :, h:], jnp.exp2(s_q - m_q).astype(BF16))
    carry = (jnp.concatenate([jnp.full((HEADS, 1, h), NEG_BIG, F32), m_q], axis=-1),
             jnp.concatenate([jnp.zeros((HEADS, FOX_VROWS, h), F32), acc_q], axis=-1))
    key = lax.broadcasted_iota(jnp.int32, (1, h, t), 1)
    qry = lax.broadcasted_iota(jnp.int32, (1, h, t), 2)
    s_ref[0, :, :h, :] = jnp.where(key <= qry, _bnt(k_diag[:, :h], qa), NEG_BIG)

    def update_half_diag(carry):
        m, acc = carry
        s = s_ref[0, :, :h, :]
        m_new = jnp.maximum(m, jnp.max(s, axis=1, keepdims=True))
        p = jnp.exp2(s - m_new).astype(BF16)
        return m_new, jnp.exp2(m - m_new) * acc + _bnn(v_diag[:, :, :h], p)

    def earlier_tiles(carry):
        s_ref[1] = scores(0)
        carry = update_half_diag(carry)

        def pair(jj, carry):
            s_ref[0] = scores(2 * jj + 1)
            carry = update(carry, 1, 2 * jj)
            s_ref[1] = scores(2 * jj + 2)
            return update(carry, 0, 2 * jj + 1)

        carry = lax.fori_loop(0, (i - 1) // 2, pair, carry)
        pending = 2 * ((i - 1) // 2)

        def one_more(carry):
            s_ref[0] = scores(i - 1)
            return update(update(carry, 1, pending), 0, i - 1)

        return lax.cond((i - 1) % 2 == 1, one_more, lambda c: update(c, 1, pending), carry)

    _, acc = lax.cond(i > 0, earlier_tiles, update_half_diag, carry)
    out = acc[:, :HEAD_DIM] / acc[:, HEAD_DIM:HEAD_DIM + 1]
    o_ref[0] = out.reshape(GROUP, t).T.astype(o_ref.dtype)


def fox_attention(qa, ka, vt):
    b, h, s, _ = qa.shape
    nk, t = vt.shape[2], vt.shape[4]
    return pl.pallas_call(
        functools.partial(_fox_attn_kernel, t=t),
        grid=(b, nk),
        in_specs=[pl.BlockSpec((1, h, t, FOX_AUG), lambda bi, i: (bi, 0, i, 0)),
                  pl.BlockSpec((1, h, s, FOX_AUG), lambda bi, i: (bi, 0, 0, 0)),
                  pl.BlockSpec((1, h, nk, FOX_VROWS, t), lambda bi, i: (bi, 0, 0, 0, 0))],
        out_specs=pl.BlockSpec((1, t, GROUP), lambda bi, i: (bi, i, 0)),
        out_shape=jax.ShapeDtypeStruct((b, s, GROUP), BF16),
        scratch_shapes=[pltpu.VMEM((2, h, t, t), F32)],
        compiler_params=_cparams(("parallel", "arbitrary")),
        name="fox_attention",
    )(qa, ka, vt)


def _lru_steps(rows, x_ref, y_ref, cw_ref, cb_ref, wra_ref, bra_ref, wri_ref, bri_ref, lam_ref, o_ref, buf_ref,
               h_ref):
    ts = rows.stop - rows.start
    xb = x_ref[0, rows, :]
    buf_ref[8:8 + ts, :] = xb
    xc = cb_ref[...] + buf_ref[5:5 + ts, :] * cw_ref[0:1, :]
    for j in range(1, CONV_WIDTH):
        xc = xc + buf_ref[5 + j:5 + j + ts, :] * cw_ref[j:j + 1, :]
    buf_ref[0:8, :] = xb[ts - 8:ts, :]
    yield

    xcb = xc.astype(BF16)
    r = _sigmoid(_dot(xcb, wra_ref[...]) + bra_ref[...])
    yield
    gate_i = _sigmoid(_dot(xcb, wri_ref[...]) + bri_ref[...])
    yield
    log_a = LRU_C * r * _log_sigmoid(lam_ref[...])
    a = jnp.exp(log_a)
    z = 2.0 * log_a
    mult = jnp.sqrt(jnp.maximum(-jnp.tanh(0.5 * z) * (jnp.exp(z) + 1.0), 0.0))
    u = mult * (gate_i * xc)
    yield

    row = lax.broadcasted_iota(jnp.int32, (ts, GROUP), 0)
    pa, pb = a, u
    d = 1
    while d < ts:
        sa = pltpu.roll(pa, d, axis=0)
        sb = pltpu.roll(pb, d, axis=0)
        valid = row >= d
        pb = jnp.where(valid, pa * sb + pb, pb)
        pa = jnp.where(valid, pa * sa, pa)
        d *= 2
        yield
    hseq = pa * h_ref[...] + pb
    h_ref[...] = hseq[ts - 1:ts, :]

    y = y_ref[0, rows, :]
    gelu = 0.5 * y * (1.0 + jnp.tanh(math.sqrt(2.0 / math.pi) * (y + 0.044715 * (y * y * y))))
    o_ref[0, rows, :] = (hseq * gelu).astype(o_ref.dtype)


def _bdot(dims):
    return lambda a, b: lax.dot_general(a, b, (dims, ((0,), (0,))), preferred_element_type=F32)


_bnn = _bdot(((2,), (1,)))
_bnt = _bdot(((2,), (2,)))
_btn = _bdot(((1,), (1,)))


def _unit_lower_inverse(a_strict, n):
    r = lax.broadcasted_iota(jnp.int32, (1, n, n), 1)
    c = lax.broadcasted_iota(jnp.int32, (1, n, n), 2)
    a_b = a_strict.astype(BF16)
    zero, one = jnp.zeros((), BF16), jnp.ones((), BF16)
    t = jnp.where(r == c, one, jnp.where((r // 2 == c // 2) & (r > c), a_b, zero))
    m = 2
    while m < n:
        off = (r // (2 * m) == c // (2 * m)) & (r % (2 * m) >= m) & (c % (2 * m) < m)
        t = t + _bnn(t, _bnn(jnp.where(off, a_b, zero), t).astype(BF16)).astype(BF16)
        m *= 2
    return t


def _to_heads(x):
    n, rows, _ = x.shape
    parts = [x[:, :, h * HEAD_DIM:(h + 1) * HEAD_DIM] for h in range(HEADS)]
    return jnp.stack(parts, axis=1).reshape(n * HEADS, rows, HEAD_DIM)


def _from_heads(x):
    nh, rows, _ = x.shape
    x = x.reshape(nh // HEADS, HEADS, rows, HEAD_DIM)
    return jnp.concatenate([x[:, h] for h in range(HEADS)], axis=-1)


def _rwkv_chunk_kernel(*refs, ts, chunk, has_vres):
    (sr_ref, sk_ref, sv_ref, sl_ref, mu_ref, w0_ref, w2_ref, a0_ref, a2_ref, g2_ref, kk_ref, ka_ref, rk_ref,
     ones_ref) = refs[:14]
    refs = refs[14:]
    if has_vres:
        vf_ref, v0_ref, v1_ref, v2_ref = refs[:4]
        refs = refs[4:]
    mm_out, gm_out, qm_out, y0_out, bonus_out, g_out, v_out, carry_ref = refs

    @pl.when(pl.program_id(1) == 0)
    def _():
        carry_ref[...] = jnp.zeros_like(carry_ref)

    row0 = lax.broadcasted_iota(jnp.int32, (ts, GROUP), 0) == 0

    def shift_mix(ref, idx):
        s = ref[0]
        prev = jnp.where(row0, carry_ref[idx:idx + 1, :], pltpu.roll(s, 1, axis=0))
        carry_ref[idx:idx + 1, :] = s[ts - 1:ts, :]
        return s + (prev - s) * mu_ref[idx:idx + 1, :]

    r = shift_mix(sr_ref, 0)
    k = shift_mix(sk_ref, 1)
    v = shift_mix(sv_ref, 2)
    low = shift_mix(sl_ref, 3)

    zw = w0_ref[...] + _dot(jnp.tanh(low).astype(BF16), w2_ref[...])
    lw = -math.exp(-0.5) * _sigmoid(zw)
    a = _sigmoid(a0_ref[...] + _dot(low.astype(BF16), a2_ref[...]))
    g_out[0] = _dot(_sigmoid(low).astype(BF16), g2_ref[...])
    if has_vres:
        mix = _dot(_dot(v.astype(BF16), v1_ref[...]).astype(BF16), v2_ref[...])
        v = v + (vf_ref[0] - v) * _sigmoid(v0_ref[...] + mix)
    v_out[0] = v
    kk = k * kk_ref[...]
    ss = _dot_sel(kk * kk, ones_ref[...], split="a")
    kk = kk / jnp.maximum(jnp.sqrt(ss), 1e-12)
    k = k * (1.0 + (a - 1.0) * ka_ref[...])
    bonus_out[0] = _dot_sel(r * k * rk_ref[...], ones_ref[...], split="a") * v
    a_vec, b_vec = -kk, kk * a

    c = chunk
    rr = lax.broadcasted_iota(jnp.int32, (ts, ts), 0)
    cc = lax.broadcasted_iota(jnp.int32, (ts, ts), 1)
    chunk_tri = ((rr >= cc) & (rr // c == cc // c)).astype(F32)
    cum_all = _dot_sel(chunk_tri, lw, split="b")
    strict = _tri(c, strict=True)[None]
    incl = _tri(c)[None]
    eye = (lax.broadcasted_iota(jnp.int32, (1, HEAD_DIM, HEAD_DIM), 1)
           == lax.broadcasted_iota(jnp.int32, (1, HEAD_DIM, HEAD_DIM), 2)).astype(F32)

    nq = ts // c
    per_chunk = lambda t: t.reshape(nq, c, GROUP)
    cum = per_chunk(cum_all)
    rq, kq, vq, aq, bq = (per_chunk(t) for t in (r, k, v, a_vec, b_vec))
    cum_ex = cum - per_chunk(lw)
    mid = cum[:, c // 2 - 1:c // 2, :]
    tot = cum[:, c - 1:c, :]
    e_fwd = jnp.exp(cum - mid)
    e_bwd = jnp.exp(mid - cum)
    e_end = jnp.exp(tot - cum)
    mxu = lambda t: _to_heads(t.astype(BF16))
    r_rel, k_rel = mxu(rq * e_fwd), mxu(kq * e_bwd)
    a_rel, b_rel = mxu(aq * jnp.exp(cum_ex - mid)), mxu(bq * e_bwd)
    a_abs, r_abs = mxu(aq * jnp.exp(cum_ex)), rq * jnp.exp(cum)
    k_end, b_end = mxu(kq * e_end), mxu(bq * e_end)
    gam = _to_heads(jnp.exp(tot))
    vh = mxu(vq)

    ar_rel = jnp.concatenate([a_rel, r_rel], axis=1)
    s_b, s_k = _bnt(ar_rel, b_rel), _bnt(ar_rel, k_rel)
    a_ab = jnp.where(strict, s_b[:, :c], 0.0)
    zero = jnp.zeros((), BF16)
    a_ak = jnp.where(strict, s_k[:, :c].astype(BF16), zero)
    a_rb = jnp.where(incl, s_b[:, c:].astype(BF16), zero)
    a_rk = jnp.where(incl, s_k[:, c:].astype(BF16), zero)
    t_inv = _unit_lower_inverse(a_ab, c)
    pu = _bnn(t_inv, jnp.concatenate([a_abs, _bnn(a_ak, vh).astype(BF16)], axis=-1)).astype(BF16)
    z = _bnn(a_rb, pu)
    y0 = z[..., HEAD_DIM:] + _bnn(a_rk, vh)
    xtb = _btn(pu, b_end)
    mm_mat = eye * gam + xtb[:, :HEAD_DIM]
    gm = xtb[:, HEAD_DIM:] + _btn(vh, k_end)
    dense = lambda t: _from_heads(t).reshape(ts, GROUP)
    mm_out[0] = dense(mm_mat).astype(mm_out.dtype)
    gm_out[0] = dense(gm)
    qm_out[0] = (dense(z[..., :HEAD_DIM]) + r_abs.reshape(ts, GROUP)).astype(qm_out.dtype)
    y0_out[0] = dense(y0)


def rwkv_chunk(proj3, mu4, w0, w2p, a0, a2p, g2p, k_k, k_a, r_k, head_ones, vres, *, ts, chunk):
    b, s, _ = proj3.shape
    slab = lambda off: pl.BlockSpec((1, ts, GROUP), lambda i, j: (i, j, off // GROUP))
    vec = lambda: pl.BlockSpec((1, GROUP), lambda i, j: (0, 0))
    full = lambda shape: pl.BlockSpec(shape, lambda i, j: tuple(0 for _ in shape))
    tok = pl.BlockSpec((1, ts, GROUP), lambda i, j: (i, j, 0))
    in_specs = [slab(RWKV_R), slab(RWKV_K), slab(RWKV_V), slab(RWKV_LR), full((4, GROUP)),
                vec(), full((GROUP, GROUP)), vec(), full((GROUP, GROUP)), full((GROUP, GROUP)), vec(), vec(), vec(),
                full((GROUP, GROUP))]
    args = [proj3, proj3, proj3, proj3, mu4, w0.reshape(1, GROUP), w2p, a0.reshape(1, GROUP), a2p, g2p,
            k_k.reshape(1, GROUP), k_a.reshape(1, GROUP), r_k.reshape(1, GROUP), head_ones]
    if vres is not None:
        v_first, v0, v1p, v2p = vres
        in_specs += [tok, vec(), full((GROUP, 128)), full((128, GROUP))]
        args += [v_first, v0.reshape(1, GROUP), v1p, v2p]
    return pl.pallas_call(
        functools.partial(_rwkv_chunk_kernel, ts=ts, chunk=chunk, has_vres=vres is not None),
        grid=(b, s // ts),
        in_specs=in_specs,
        out_specs=[tok] * 7,
        out_shape=[jax.ShapeDtypeStruct((b, s, GROUP), dt) for dt in (BF16, F32, BF16, F32, F32, F32, F32)],
        scratch_shapes=[pltpu.VMEM((4, GROUP), F32)],
        compiler_params=_cparams(("parallel", "arbitrary")),
        name="rwkv_chunk",
    )(*args)


def _rwkv_state_kernel(mm_ref, gm_ref, qm_ref, y0_ref, bonus_ref, g_ref, gw_ref, gb_ref, o_ref, state_ref, *, chunk):
    @pl.when(pl.program_id(0) == 0)
    def _():
        state_ref[...] = jnp.zeros_like(state_ref)

    state = state_ref[...]
    for c in range(mm_ref.shape[1] // chunk):
        rows = slice(c * chunk, (c + 1) * chunk)
        s_hi, s_lo = _split_bf16(state)
        qm, mm = _to_heads(qm_ref[:, rows, :]), _to_heads(mm_ref[:, rows, :])
        y = _bnt(qm, s_hi) + _bnt(qm, s_lo) + _to_heads(y0_ref[:, rows, :])
        state = _bnn(s_hi, mm) + _bnn(s_lo, mm) + _to_heads(gm_ref[:, rows, :])
        mu = jnp.mean(y, axis=-1, keepdims=True)
        var = jnp.mean(jnp.square(y - mu), axis=-1, keepdims=True)
        yn = _from_heads((y - mu) * lax.rsqrt(var + RWKV_GN_EPS)) * gw_ref[...] + gb_ref[...]
        o_ref[:, rows, :] = ((yn + bonus_ref[:, rows, :]) * g_ref[:, rows, :]).astype(o_ref.dtype)
    state_ref[...] = state


def rwkv_state(mm, gm, qm, y0, bonus, g, gn_w, gn_b, *, ts, chunk):
    bsz, s, _ = mm.shape
    tok = pl.BlockSpec((bsz, ts, GROUP), lambda j: (0, j, 0))
    vec = pl.BlockSpec((1, GROUP), lambda j: (0, 0))
    return pl.pallas_call(
        functools.partial(_rwkv_state_kernel, chunk=chunk),
        grid=(s // ts,),
        in_specs=[tok] * 6 + [vec] * 2,
        out_specs=tok,
        out_shape=jax.ShapeDtypeStruct((bsz, s, GROUP), BF16),
        scratch_shapes=[pltpu.VMEM((bsz * HEADS, HEAD_DIM, HEAD_DIM), F32)],
        compiler_params=_cparams(("arbitrary",)),
        name="rwkv_state",
    )(mm, gm, qm, y0, bonus, g, gn_w.reshape(1, GROUP), gn_b.reshape(1, GROUP))


def _ret_steps(q_ref, k_ref, v_ref, g_ref, cos_ref, sin_ref, dmat_ref, xi_ref, zeta_ref, cd_ref, gw_ref, o_ref,
               state_ref, *, ts, chunk):
    lane = lax.broadcasted_iota(jnp.int32, (ts, GROUP), 1)
    first_half = (lane % HEAD_DIM) < (HEAD_DIM // 2)
    cos, sin = cos_ref[...], sin_ref[...]

    def rotary(t):
        partner = jnp.where(first_half, pltpu.roll(t, GROUP - HEAD_DIM // 2, axis=1),
                            pltpu.roll(t, HEAD_DIM // 2, axis=1))
        return t * cos + partner * sin

    nq = ts // chunk
    per_chunk = lambda t: t.reshape(nq, chunk, GROUP)
    mxu = lambda t: _to_heads(t).astype(BF16)
    q = per_chunk(rotary(q_ref[0]))
    yield
    k = per_chunk(rotary(k_ref[0]) * (HEAD_DIM ** -0.5))
    yield
    qb, kb, vb = mxu(q), mxu(k), mxu(per_chunk(v_ref[0]))
    yield
    q_cross, k_decay = mxu(q * xi_ref[...]), mxu(k * zeta_ref[...])
    yield
    inner = _bnt(qb, kb).reshape(nq, HEADS, chunk, chunk) * dmat_ref[...]
    yield
    intra = _bnn(inner.reshape(nq * HEADS, chunk, chunk).astype(BF16), vb)
    yield
    kv = _btn(k_decay, vb)
    yield

    decay = _to_heads(cd_ref[...][None])
    state = state_ref[...]
    incoming = []
    for c in range(nq):
        incoming.append(state)
        state = state * decay + kv[c * HEADS:(c + 1) * HEADS]
    state_ref[...] = state
    o = intra + _bnn(q_cross, jnp.concatenate(incoming, axis=0).astype(BF16))
    yield
    o = _from_heads(o * lax.rsqrt(jnp.mean(o * o, axis=-1, keepdims=True) + NORM_EPS)).reshape(ts, GROUP)
    g = g_ref[0]
    o_ref[0] = (o * gw_ref[...] * (g * _sigmoid(g))).astype(o_ref.dtype)


def _side_mixers_kernel(*refs, ts, sub, chunk):
    lru_in, ret_in, fox_in = refs[:9], refs[9:20], refs[20:25]
    lru_out, ret_out, qa_ref, ka_ref, vt_ref, buf_ref, h_ref, state_ref, cum_ref = refs[25:]

    @pl.when(pl.program_id(1) == 0)
    def _():
        buf_ref[0:8, :] = jnp.zeros((8, GROUP), F32)
        h_ref[...] = jnp.zeros_like(h_ref)
        state_ref[...] = jnp.zeros_like(state_ref)
        cum_ref[...] = jnp.zeros_like(cum_ref)

    def lru_ranges():
        for start in range(0, ts, sub):
            yield from _lru_steps(slice(start, start + sub), *lru_in, lru_out, buf_ref, h_ref)

    def fox_ranges():
        for tile in range(ts // sub):
            yield from _fox_prep_steps(slice(tile * sub, (tile + 1) * sub), tile, *fox_in, qa_ref, ka_ref, vt_ref,
                                       cum_ref)

    _interleave(_ret_steps(*ret_in, ret_out, state_ref, ts=ts, chunk=chunk), lru_ranges(), fox_ranges())


def side_mixers(proj_fox3, proj3, f_bias_pad, conv_w, conv_b, wra_bd, ra_b, wri_bd, ri_b, lam, cos_t, sin_t, dmat, xi,
                zeta, cd, gn_w, *, ts, sub, chunk):
    b, s, _ = proj3.shape
    slab = lambda off: pl.BlockSpec((1, ts, GROUP), lambda i, j: (i, j, off // GROUP))
    full = lambda shape: pl.BlockSpec(shape, lambda i, j: tuple(0 for _ in shape))
    vec, mat = full((1, GROUP)), full((GROUP, GROUP))
    table = pl.BlockSpec((ts, GROUP), lambda i, j: (j, 0))
    out = pl.BlockSpec((1, ts, GROUP), lambda i, j: (i, j, 0))
    aug = pl.BlockSpec((1, HEADS, ts, FOX_AUG), lambda i, j: (i, 0, j, 0))
    f_width = FOX_WIDTH - FOX_F
    row = lambda t: t.reshape(1, GROUP)
    return pl.pallas_call(
        functools.partial(_side_mixers_kernel, ts=ts, sub=sub, chunk=chunk),
        grid=(b, s // ts),
        in_specs=[slab(LRU_X), slab(LRU_Y), full((CONV_WIDTH, GROUP)), vec, mat, vec, mat, vec, vec,
                  slab(RET_Q), slab(RET_K), slab(RET_V), slab(RET_G), table, table,
                  full((HEADS, chunk, chunk)), full((chunk, GROUP)), full((chunk, GROUP)), vec, vec,
                  slab(FOX_Q), slab(FOX_K), slab(FOX_V),
                  pl.BlockSpec((1, ts, f_width), lambda i, j: (i, j, FOX_F // f_width)), full((1, f_width))],
        out_specs=[out, out, aug, aug,
                   pl.BlockSpec((1, HEADS, ts // sub, FOX_VROWS, sub), lambda i, j: (i, 0, j, 0, 0))],
        out_shape=[jax.ShapeDtypeStruct((b, s, GROUP), BF16)] * 2
                  + [jax.ShapeDtypeStruct((b, HEADS, s, FOX_AUG), BF16)] * 2
                  + [jax.ShapeDtypeStruct((b, HEADS, s // sub, FOX_VROWS, sub), BF16)],
        scratch_shapes=[pltpu.VMEM((sub + 8, GROUP), F32), pltpu.VMEM((1, GROUP), F32),
                        pltpu.VMEM((HEADS, HEAD_DIM, HEAD_DIM), F32), pltpu.VMEM((1, f_width), F32)],
        compiler_params=_cparams(("parallel", "arbitrary")),
        name="side_mixers",
    )(proj3, proj3, conv_w, row(conv_b), wra_bd, row(ra_b), wri_bd, row(ri_b), row(lam),
      proj3, proj3, proj3, proj3, cos_t, sin_t, dmat, xi, zeta, cd, row(gn_w),
      proj_fox3, proj_fox3, proj_fox3, proj_fox3, f_bias_pad)


def _retention_tables(s, chunk):
    half = HEAD_DIM // 2
    inv = 1.0 / (RET_THETA ** jnp.linspace(0.0, 1.0, half, dtype=F32))
    ang = jnp.arange(s, dtype=F32)[:, None] * inv[None, :]
    cos, sin = jnp.cos(ang), jnp.sin(ang)
    cos_t = jnp.tile(jnp.concatenate([cos, cos], axis=-1), (1, HEADS))
    sin_t = jnp.tile(jnp.concatenate([-sin, sin], axis=-1), (1, HEADS))
    lg = jnp.log(1.0 - 2.0 ** (-5.0 - jnp.arange(HEADS, dtype=F32)))
    n = jnp.arange(chunk, dtype=F32)
    diff = n[:, None] - n[None, :]
    dmat = jnp.where(diff >= 0, jnp.exp(lg[:, None, None] * jnp.maximum(diff, 0.0)), 0.0)
    zeta = jnp.exp(lg[:, None] * (chunk - 1.0 - n)[None, :])
    xi = jnp.exp(lg[:, None] * (n + 1.0)[None, :])
    per_lane = lambda t: jnp.repeat(t.T, HEAD_DIM, axis=1)
    cd = jnp.repeat(jnp.exp(lg * chunk), HEAD_DIM)[None, :]
    return cos_t, sin_t, dmat, per_lane(xi), per_lane(zeta), cd


MIX_XATTN_PARTS = 4


def _mix_xattn_steps(rows, m_refs, x_ref, kv_ref, wout_ref, wq_ref, wo_ref, gmix_ref, gpre_ref, gpost_ref, o_ref):
    acc = _dot(m_refs[0][0, rows, :], wout_ref[0:GROUP, :])
    for idx, m_ref in enumerate(m_refs[1:], start=1):
        acc = acc + _dot(m_ref[0, rows, :], wout_ref[idx * GROUP:(idx + 1) * GROUP, :])
    yield
    x = x_ref[0, rows, :] + _rms(acc, gmix_ref[...])
    xn = _rms(x, gpre_ref[...]).astype(BF16)
    q = (_dot(xn, wq_ref[...]) * (XATTN_HEAD_DIM ** -0.5)).astype(BF16)
    yield
    heads = lambda t, off: jnp.stack([t[:, off + h * XATTN_HEAD_DIM:off + (h + 1) * XATTN_HEAD_DIM]
                                      for h in range(XATTN_HEADS)])
    kv = kv_ref[0]
    s = _bnt(heads(q, 0), heads(kv, 0))
    yield
    e = jnp.exp(s - jnp.max(s, axis=-1, keepdims=True))
    p = e / jnp.sum(e, axis=-1, keepdims=True)
    o = _bnn(p.astype(BF16), heads(kv, D_MODEL)).astype(BF16)
    yield
    o = jnp.concatenate([o[h] for h in range(XATTN_HEADS)], axis=-1)
    o_ref[0, rows, :] = x + _rms(_dot(o, wo_ref[...]), gpost_ref[...])


def _mix_xattn_kernel(m0_ref, m1_ref, m2_ref, m3_ref, x_ref, kv_ref, wout_ref, wq_ref, wo_ref, gmix_ref, gpre_ref,
                      gpost_ref, o_ref):
    n = x_ref.shape[1] // MIX_XATTN_PARTS
    _interleave(*[_mix_xattn_steps(slice(part * n, (part + 1) * n), (m0_ref, m1_ref, m2_ref, m3_ref), x_ref,
                                   kv_ref, wout_ref, wq_ref, wo_ref, gmix_ref, gpre_ref, gpost_ref, o_ref)
                  for part in range(MIX_XATTN_PARTS)])


def mix_xattn(mixed, x3, kv, w_out, wq, wo, layer, g_mix, g_pre, g_post, *, tm):
    b, s, d = x3.shape
    m = kv.shape[1]
    tok = lambda width: pl.BlockSpec((1, tm, width), lambda bi, i: (bi, i, 0))
    weight = lambda: pl.BlockSpec((None, d, d), lambda bi, i: (layer, 0, 0))
    vec = lambda: pl.BlockSpec((1, d), lambda bi, i: (0, 0))
    return pl.pallas_call(
        _mix_xattn_kernel,
        grid=(b, s // tm),
        in_specs=[tok(GROUP)] * 4 + [tok(d), pl.BlockSpec((1, m, 2 * d), lambda bi, i: (bi, 0, 0)),
                                     weight(), weight(), weight(), vec(), vec(), vec()],
        out_specs=tok(d),
        out_shape=jax.ShapeDtypeStruct((b, s, d), F32),
        compiler_params=_cparams(("parallel", "arbitrary")),
        name="mix_xattn",
    )(*mixed, x3, kv, w_out, wq, wo, g_mix.reshape(1, d), g_pre.reshape(1, d), g_post.reshape(1, d))


def _mlp_steps(rows, x_ref, w1_ref, w2_ref, gpre_ref, gpost_ref, o_ref, ff_tile):
    x = x_ref[rows, :]
    xn = _rms(x, gpre_ref[...]).astype(BF16)
    yield
    d_ff = w1_ref.shape[1]
    acc = None
    for c in range(d_ff // ff_tile):
        hid = jnp.square(jnp.maximum(_dot(xn, w1_ref[:, c * ff_tile:(c + 1) * ff_tile]), 0.0)).astype(BF16)
        yield
        part = _dot(hid, w2_ref[c * ff_tile:(c + 1) * ff_tile, :])
        acc = part if acc is None else acc + part
        yield
    o_ref[rows, :] = x + _rms(acc, gpost_ref[...])


def _mlp_kernel(x_ref, w1_ref, w2_ref, gpre_ref, gpost_ref, o_ref, *, ff_tile):
    half = x_ref.shape[0] // 2
    _interleave(*[_mlp_steps(slice(p * half, (p + 1) * half), x_ref, w1_ref, w2_ref, gpre_ref, gpost_ref, o_ref,
                             ff_tile) for p in range(2)])


def mlp(x, w1, w2, layer, g_pre, g_post, *, tm, ff_tile):
    n, d = x.shape
    d_ff = w1.shape[2]
    return pl.pallas_call(
        functools.partial(_mlp_kernel, ff_tile=ff_tile),
        grid=(n // tm,),
        in_specs=[pl.BlockSpec((tm, d), lambda i: (i, 0)),
                  pl.BlockSpec((None, d, d_ff), lambda i: (layer, 0, 0), pipeline_mode=pl.Buffered(1)),
                  pl.BlockSpec((None, d_ff, d), lambda i: (layer, 0, 0), pipeline_mode=pl.Buffered(1)),
                  pl.BlockSpec((1, d), lambda i: (0, 0)),
                  pl.BlockSpec((1, d), lambda i: (0, 0))],
        out_specs=pl.BlockSpec((tm, d), lambda i: (i, 0)),
        out_shape=jax.ShapeDtypeStruct((n, d), F32),
        compiler_params=_cparams(("parallel",)),
        name="mlp",
    )(x, w1, w2, g_pre.reshape(1, d), g_post.reshape(1, d))


def _block_diag(w):
    h, n, _ = w.shape
    eye = jnp.eye(h, dtype=w.dtype)
    return (eye[:, None, :, None] * w[:, :, None, :]).reshape(h * n, h * n)


def _pad_rows(w, start, total):
    return jnp.zeros((total, w.shape[1]), w.dtype).at[start:start + w.shape[0]].set(w)


def _tile(n, pref):
    return pref if n % pref == 0 else n


def kernel(x, mem, norm_mix_pre, norm_mix_post, norm_xa_pre, norm_xa_post, norm_mem, norm_mlp_pre, norm_mlp_post, w_in, w_out, fox_f_bias, lru_conv_w, lru_conv_b, lru_ra_w, lru_ra_b, lru_ri_w, lru_ri_b, lru_lambda, rwkv_mu, rwkv_w0, rwkv_w2, rwkv_a0, rwkv_a2, rwkv_g2, rwkv_k_k, rwkv_k_a, rwkv_r_k, rwkv_gn_w, rwkv_gn_b, rwkv_v0, rwkv_v1, rwkv_v2, ret_gn_w, xa_wq, xa_wk, xa_wv, xa_wo, mlp_w1, mlp_w2):
    bsz, seq, d = x.shape
    depth = w_in.shape[0]
    n_tok = bsz * seq
    mem_len = mem.shape[1]
    tm = _tile(n_tok, 512)
    tq = _tile(seq, 512)
    ret_tables = _retention_tables(seq, RET_CHUNK)
    head_ones = _block_diag(jnp.ones((HEADS, HEAD_DIM, HEAD_DIM), BF16))

    w_in_t = w_in.astype(BF16).swapaxes(1, 2)
    w_fox_b, w_mix_b = w_in_t[:, :FOX_WIDTH], w_in_t[:, FOX_REAL:]
    w_out_b = w_out.astype(BF16)
    wq_b, wo_b = xa_wq.astype(BF16), xa_wo.astype(BF16)
    wkv_b = jnp.concatenate([xa_wk, xa_wv], axis=-1).astype(BF16)
    w1_b, w2_b = mlp_w1.astype(BF16), mlp_w2.astype(BF16)

    x2 = x.reshape(n_tok, d)
    v_first = None
    for l in range(depth):
        proj_fox, proj_mix = in_proj(x2, norm_mix_pre[l], w_fox_b, w_mix_b, l, tm=_tile(n_tok, 1024),
                                     tn=MIX_WIDTH // 5)
        proj3 = proj_mix.reshape(bsz, seq, MIX_WIDTH)

        f_bias = jnp.zeros((1, FOX_WIDTH - FOX_F), F32).at[0, :HEADS].set(fox_f_bias[l])
        side_ts = _tile(seq, 8 * RET_CHUNK)
        lru_out, ret_out, qa, ka, vt = side_mixers(
            proj_fox.reshape(bsz, seq, FOX_WIDTH), proj3, f_bias, lru_conv_w[l], lru_conv_b[l],
            _block_diag(lru_ra_w[l]).astype(BF16), lru_ra_b[l], _block_diag(lru_ri_w[l]).astype(BF16), lru_ri_b[l],
            lru_lambda[l], *ret_tables, ret_gn_w[l], ts=side_ts, sub=_tile(side_ts, tq), chunk=RET_CHUNK)
        fox_out = fox_attention(qa, ka, vt)

        vres = None
        if l > 0:
            vres = (v_first, rwkv_v0[l - 1],
                    jnp.pad(rwkv_v1[l - 1], ((0, 0), (0, 128 - RWKV_V_RANK))).astype(BF16),
                    _pad_rows(rwkv_v2[l - 1], 0, 128).astype(BF16))
        mm_, gm_, qm_, y0_, bonus_, g_, v_ = rwkv_chunk(
            proj3, rwkv_mu[l].reshape(4, GROUP), rwkv_w0[l],
            _pad_rows(rwkv_w2[l], 0, GROUP).astype(BF16), rwkv_a0[l],
            _pad_rows(rwkv_a2[l], RWKV_W_RANK, GROUP).astype(BF16),
            _pad_rows(rwkv_g2[l], RWKV_W_RANK + RWKV_A_RANK, GROUP).astype(BF16),
            rwkv_k_k[l], rwkv_k_a[l], rwkv_r_k[l], head_ones, vres, ts=_tile(seq, 8 * RWKV_CHUNK), chunk=RWKV_CHUNK)
        if l == 0:
            v_first = v_
        rwkv_out = rwkv_state(mm_, gm_, qm_, y0_, bonus_, g_, rwkv_gn_w[l], rwkv_gn_b[l],
                              ts=_tile(seq, 4 * RWKV_CHUNK), chunk=RWKV_CHUNK)

        kv = norm_matmul(mem.reshape(bsz * mem_len, d), norm_mem[l], wkv_b, l, tm=_tile(bsz * mem_len, 512),
                         tn=1024, out_dtype=BF16).reshape(bsz, mem_len, 2 * d)
        x2 = mix_xattn((fox_out, lru_out, rwkv_out, ret_out), x2.reshape(bsz, seq, d), kv, w_out_b, wq_b, wo_b, l,
                       norm_mix_post[l], norm_xa_pre[l], norm_xa_post[l], tm=_tile(seq, 1024)).reshape(n_tok, d)

        x2 = mlp(x2, w1_b, w2_b, l, norm_mlp_pre[l], norm_mlp_post[l], tm=_tile(n_tok, 1024), ff_tile=1024)
    return x2.reshape(bsz, seq, d)
```

```python
import functools
import math

import jax
import jax.numpy as jnp
from jax import lax
from jax.experimental import pallas as pl
from jax.experimental.pallas import tpu as pltpu

F32 = jnp.float32
BF16 = jnp.bfloat16

D_MODEL = 1024
GROUP = 256
HEADS = 4
HEAD_DIM = 64
CONV_WIDTH = 4
LRU_C = 8.0
RET_THETA = 10000.0
RET_CHUNK = 128
RWKV_W_RANK, RWKV_A_RANK, RWKV_G_RANK, RWKV_V_RANK = 64, 64, 128, 32
RWKV_GN_EPS = 64e-5
XATTN_HEADS = 4
XATTN_HEAD_DIM = D_MODEL // XATTN_HEADS
NORM_EPS = 1e-6
NEG_BIG = -1e30

FOX_Q, FOX_K, FOX_V, FOX_F = 0, 256, 512, 768
FOX_REAL = 3 * GROUP + HEADS
FOX_WIDTH = 896
LRU_X, LRU_Y = 0, 256
RWKV_R, RWKV_K, RWKV_V, RWKV_LR = 512, 768, 1024, 1280
RET_Q, RET_K, RET_V, RET_G = 1536, 1792, 2048, 2304
MIX_WIDTH = 2560

VMEM_LIMIT = 56 * 1024 * 1024
RWKV_CHUNK = 64


def _cparams(sem):
    return pltpu.CompilerParams(dimension_semantics=sem, vmem_limit_bytes=VMEM_LIMIT)


def _rms(x, g):
    return x * lax.rsqrt(jnp.mean(x * x, axis=-1, keepdims=True) + NORM_EPS) * g


def _log_sigmoid(x):
    return jnp.minimum(x, 0.0) - jnp.log1p(jnp.exp(-jnp.abs(x)))


def _sigmoid(x):
    return 1.0 / (1.0 + jnp.exp(-x))


def _dot(a, b, **kw):
    return jnp.dot(a, b, preferred_element_type=F32, **kw)


def _dot_nt(a, b, **kw):
    return lax.dot_general(a, b, (((1,), (1,)), ((), ())), preferred_element_type=F32, **kw)


def _split_bf16(x):
    hi = x.astype(BF16)
    return hi, (x - hi.astype(F32)).astype(BF16)


def _interleave(*step_generators):
    live = list(step_generators)
    while live:
        for gen in list(live):
            try:
                next(gen)
            except StopIteration:
                live.remove(gen)


def _bf16_terms(x, n):
    terms = []
    for _ in range(n - 1):
        t = x.astype(BF16)
        terms.append(t)
        x = x - t.astype(F32)
    return terms + [x.astype(BF16)]


def _dot_sel(a, b, *, split, n=3):
    if split == "a":
        parts = [_dot(t, b.astype(BF16)) for t in _bf16_terms(a, n)]
    else:
        parts = [_dot(a.astype(BF16), t) for t in _bf16_terms(b, n)]
    out = parts[-1]
    for p in reversed(parts[:-1]):
        out = out + p
    return out


def _tri(n, strict=False):
    r = lax.broadcasted_iota(jnp.int32, (n, n), 0)
    c = lax.broadcasted_iota(jnp.int32, (n, n), 1)
    return (r > c) if strict else (r >= c)


def _norm_matmul_kernel(x_ref, g_ref, w_ref, o_ref, xn_ref):
    @pl.when(pl.program_id(1) == 0)
    def _():
        xn_ref[...] = _rms(x_ref[...], g_ref[...]).astype(BF16)

    o_ref[...] = _dot(xn_ref[...], w_ref[...]).astype(o_ref.dtype)


def norm_matmul(x, g, w, layer, *, tm, tn, out_dtype):
    n, d = x.shape
    width = w.shape[2]
    return pl.pallas_call(
        _norm_matmul_kernel,
        grid=(n // tm, width // tn),
        in_specs=[pl.BlockSpec((tm, d), lambda i, j: (i, 0)),
                  pl.BlockSpec((1, d), lambda i, j: (0, 0)),
                  pl.BlockSpec((None, d, tn), lambda i, j: (layer, 0, j))],
        out_specs=pl.BlockSpec((tm, tn), lambda i, j: (i, j)),
        out_shape=jax.ShapeDtypeStruct((n, width), out_dtype),
        scratch_shapes=[pltpu.VMEM((tm, d), BF16)],
        compiler_params=_cparams(("parallel", "arbitrary")),
        name="norm_matmul",
    )(x, g.reshape(1, d), w)


def _in_proj_steps(rows, x_ref, g_ref, wf_ref, wm_ref, of_ref, om_ref, tn):
    xn = _rms(x_ref[rows, :], g_ref[...]).astype(BF16)
    yield
    of_ref[rows, :] = _dot_nt(xn, wf_ref[...])
    for c in range(wm_ref.shape[0] // tn):
        yield
        om_ref[rows, c * tn:(c + 1) * tn] = _dot_nt(xn, wm_ref[c * tn:(c + 1) * tn, :])


def _in_proj_kernel(x_ref, g_ref, wf_ref, wm_ref, of_ref, om_ref, *, tn):
    half = x_ref.shape[0] // 2
    _interleave(*[_in_proj_steps(slice(p * half, (p + 1) * half), x_ref, g_ref, wf_ref, wm_ref, of_ref, om_ref, tn)
                  for p in range(2)])


def in_proj(x, g, w_fox, w_mix, layer, *, tm, tn):
    n, d = x.shape
    weight = lambda w: pl.BlockSpec((None, w.shape[1], d), lambda i: (layer, 0, 0), pipeline_mode=pl.Buffered(1))
    out = lambda w: pl.BlockSpec((tm, w.shape[1]), lambda i: (i, 0))
    return pl.pallas_call(
        functools.partial(_in_proj_kernel, tn=tn),
        grid=(n // tm,),
        in_specs=[pl.BlockSpec((tm, d), lambda i: (i, 0)), pl.BlockSpec((1, d), lambda i: (0, 0)),
                  weight(w_fox), weight(w_mix)],
        out_specs=[out(w_fox), out(w_mix)],
        out_shape=[jax.ShapeDtypeStruct((n, w.shape[1]), F32) for w in (w_fox, w_mix)],
        compiler_params=_cparams(("parallel",)),
        name="in_proj",
    )(x, g.reshape(1, d), w_fox, w_mix)


FOX_AUG = 128
FOX_VROWS = HEAD_DIM + 16
LOG2E = 1.4426950408889634


def _fox_prep_steps(rows, tile, q_ref, k_ref, v_ref, f_ref, b_ref, qa_ref, ka_ref, vt_ref, carry_ref):
    lf = _log_sigmoid(f_ref[0, rows, :] + b_ref[...])
    tc = lf.shape[0]
    cum = _dot_sel(_tri(tc), lf, split="b") + carry_ref[...]
    carry_ref[...] = cum[tc - 1:tc, :]
    yield

    lane = lax.broadcasted_iota(jnp.int32, (tc, FOX_AUG), 1)
    feat = lane < HEAD_DIM
    ones = jnp.where((lane >= HEAD_DIM) & (lane < HEAD_DIM + 3), 1.0, 0.0)
    q = q_ref[0, rows, :] * (HEAD_DIM ** -0.5 * LOG2E)
    k = k_ref[0, rows, :]
    for h in range(HEADS):
        pair = slice((h // 2) * FOX_AUG, (h // 2 + 1) * FOX_AUG)
        qt, kt = q[:, pair], k[:, pair]
        if h % 2:
            qt, kt = pltpu.roll(qt, HEAD_DIM, axis=1), pltpu.roll(kt, HEAD_DIM, axis=1)
        neg_c = jnp.broadcast_to(cum[:, h:h + 1] * -LOG2E, (tc, FOX_AUG))
        c_hi = neg_c.astype(BF16).astype(F32)
        rest = neg_c - c_hi
        c_mid = rest.astype(BF16).astype(F32)
        bias = jnp.where(lane == HEAD_DIM, c_hi, jnp.where(lane == HEAD_DIM + 1, c_mid,
                                                           jnp.where(lane == HEAD_DIM + 2, rest - c_mid, 0.0)))
        qa_ref[0, h, rows, :] = jnp.where(feat, qt, ones).astype(BF16)
        ka_ref[0, h, rows, :] = jnp.where(feat, kt, bias).astype(BF16)
        yield
    ones_row = lax.broadcasted_iota(jnp.int32, (HEADS, FOX_VROWS - HEAD_DIM, tc), 1) == 0
    vt = jnp.concatenate([v_ref[0, rows, :].T.reshape(HEADS, HEAD_DIM, tc), ones_row.astype(F32)], axis=1)
    vt_ref[0, :, tile] = vt.astype(BF16)


def _fox_attn_kernel(q_ref, k_ref, v_ref, o_ref, s_ref, *, t):
    i = pl.program_id(1)
    qa = q_ref[0]

    def scores(j):
        return _bnt(k_ref[0, :, pl.ds(pl.multiple_of(j * t, t), t), :], qa)

    def update(carry, slot, j):
        m, acc = carry
        s = s_ref[slot]
        m_new = jnp.maximum(m, jnp.max(s, axis=1, keepdims=True))
        p = jnp.exp2(s - m_new).astype(BF16)
        return m_new, jnp.exp2(m - m_new) * acc + _bnn(v_ref[0, :, j], p)

    h = t // 2
    k_diag = k_ref[0, :, pl.ds(pl.multiple_of(i * t, t), t), :]
    v_diag = v_ref[0, :, i]
    tri = (lax.broadcasted_iota(jnp.int32, (1, h, h), 1) <= lax.broadcasted_iota(jnp.int32, (1, h, h), 2))
    s_q = jnp.where(tri, _bnt(k_diag[:, h:], qa[:, h:]), NEG_BIG)
    m_q = jnp.max(s_q, axis=1, keepdims=True)
    acc_q = _bnn(v_diag[:, :, h:], jnp.exp2(s_q - m_q).astype(BF16))
    carry = (jnp.concatenate([jnp.full((HEADS, 1, h), NEG_BIG, F32), m_q], axis=-1),
             jnp.concatenate([jnp.zeros((HEADS, FOX_VROWS, h), F32), acc_q], axis=-1))
    key = lax.broadcasted_iota(jnp.int32, (1, h, t), 1)
    qry = lax.broadcasted_iota(jnp.int32, (1, h, t), 2)
    s_ref[0, :, :h, :] = jnp.where(key <= qry, _bnt(k_diag[:, :h], qa), NEG_BIG)

    def update_half_diag(carry):
        m, acc = carry
        s = s_ref[0, :, :h, :]
        m_new = jnp.maximum(m, jnp.max(s, axis=1, keepdims=True))
        p = jnp.exp2(s - m_new).astype(BF16)
        return m_new, jnp.exp2(m - m_new) * acc + _bnn(v_diag[:, :, :h], p)

    def earlier_tiles(carry):
        s_ref[1] = scores(0)
        carry = update_half_diag(carry)

        def pair(jj, carry):
            s_ref[0] = scores(2 * jj + 1)
            carry = update(carry, 1, 2 * jj)
            s_ref[1] = scores(2 * jj + 2)
            return update(carry, 0, 2 * jj + 1)

        carry = lax.fori_loop(0, (i - 1) // 2, pair, carry)
        pending = 2 * ((i - 1) // 2)

        def one_more(carry):
            s_ref[0] = scores(i - 1)
            return update(update(carry, 1, pending), 0, i - 1)

        return lax.cond((i - 1) % 2 == 1, one_more, lambda c: update(c, 1, pending), carry)

    _, acc = lax.cond(i > 0, earlier_tiles, update_half_diag, carry)
    out = acc[:, :HEAD_DIM] / acc[:, HEAD_DIM:HEAD_DIM + 1]
    o_ref[0] = out.reshape(GROUP, t).T.astype(o_ref.dtype)


def fox_attention(qa, ka, vt):
    b, h, s, _ = qa.shape
    nk, t = vt.shape[2], vt.shape[4]
    return pl.pallas_call(
        functools.partial(_fox_attn_kernel, t=t),
        grid=(b, nk),
        in_specs=[pl.BlockSpec((1, h, t, FOX_AUG), lambda bi, i: (bi, 0, i, 0)),
                  pl.BlockSpec((1, h, s, FOX_AUG), lambda bi, i: (bi, 0, 0, 0)),
                  pl.BlockSpec((1, h, nk, FOX_VROWS, t), lambda bi, i: (bi, 0, 0, 0, 0))],
        out_specs=pl.BlockSpec((1, t, GROUP), lambda bi, i: (bi, i, 0)),
        out_shape=jax.ShapeDtypeStruct((b, s, GROUP), BF16),
        scratch_shapes=[pltpu.VMEM((2, h, t, t), F32)],
        compiler_params=_cparams(("parallel", "arbitrary")),
        name="fox_attention",
    )(qa, ka, vt)


def _lru_steps(rows, x_ref, y_ref, cw_ref, cb_ref, wra_ref, bra_ref, wri_ref, bri_ref, lam_ref, o_ref, buf_ref,
               h_ref):
    ts = rows.stop - rows.start
    xb = x_ref[0, rows, :]
    buf_ref[8:8 + ts, :] = xb
    xc = cb_ref[...] + buf_ref[5:5 + ts, :] * cw_ref[0:1, :]
    for j in range(1, CONV_WIDTH):
        xc = xc + buf_ref[5 + j:5 + j + ts, :] * cw_ref[j:j + 1, :]
    buf_ref[0:8, :] = xb[ts - 8:ts, :]
    yield

    xcb = xc.astype(BF16)
    r = _sigmoid(_dot(xcb, wra_ref[...]) + bra_ref[...])
    yield
    gate_i = _sigmoid(_dot(xcb, wri_ref[...]) + bri_ref[...])
    yield
    log_a = LRU_C * r * _log_sigmoid(lam_ref[...])
    a = jnp.exp(log_a)
    mult = jnp.sqrt(jnp.maximum(-jnp.tanh(log_a) * (a * a + 1.0), 0.0))
    u = mult * (gate_i * xc)
    yield

    row = lax.broadcasted_iota(jnp.int32, (ts, GROUP), 0)
    pa, pb = a, u
    d = 1
    while d < ts:
        sa = pltpu.roll(pa, d, axis=0)
        sb = pltpu.roll(pb, d, axis=0)
        valid = row >= d
        pb = jnp.where(valid, pa * sb + pb, pb)
        pa = jnp.where(valid, pa * sa, pa)
        d *= 2
        yield
    hseq = pa * h_ref[...] + pb
    h_ref[...] = hseq[ts - 1:ts, :]

    y = y_ref[0, rows, :]
    gelu = 0.5 * y * (1.0 + jnp.tanh(math.sqrt(2.0 / math.pi) * (y + 0.044715 * (y * y * y))))
    o_ref[0, rows, :] = (hseq * gelu).astype(o_ref.dtype)


def _bdot(dims):
    return lambda a, b: lax.dot_general(a, b, (dims, ((0,), (0,))), preferred_element_type=F32)


_bnn = _bdot(((2,), (1,)))
_bnt = _bdot(((2,), (2,)))
_btn = _bdot(((1,), (1,)))


def _unit_lower_inverse(a_strict, n):
    r = lax.broadcasted_iota(jnp.int32, (1, n, n), 1)
    c = lax.broadcasted_iota(jnp.int32, (1, n, n), 2)
    a_b = a_strict.astype(BF16)
    zero, one = jnp.zeros((), BF16), jnp.ones((), BF16)
    t = jnp.where(r == c, one, jnp.where((r // 2 == c // 2) & (r > c), a_b, zero))
    m = 2
    while m < n:
        off = (r // (2 * m) == c // (2 * m)) & (r % (2 * m) >= m) & (c % (2 * m) < m)
        t = t + _bnn(t, _bnn(jnp.where(off, a_b, zero), t).astype(BF16)).astype(BF16)
        m *= 2
    return t


def _to_heads(x):
    n, rows, _ = x.shape
    parts = [x[:, :, h * HEAD_DIM:(h + 1) * HEAD_DIM] for h in range(HEADS)]
    return jnp.stack(parts, axis=1).reshape(n * HEADS, rows, HEAD_DIM)


def _from_heads(x):
    nh, rows, _ = x.shape
    x = x.reshape(nh // HEADS, HEADS, rows, HEAD_DIM)
    return jnp.concatenate([x[:, h] for h in range(HEADS)], axis=-1)


def _rwkv_chunk_kernel(*refs, ts, chunk, has_vres):
    (sr_ref, sk_ref, sv_ref, sl_ref, mu_ref, w0_ref, w2_ref, a0_ref, a2_ref, g2_ref, kk_ref, ka_ref, rk_ref,
     ones_ref) = refs[:14]
    refs = refs[14:]
    if has_vres:
        vf_ref, v0_ref, v1_ref, v2_ref = refs[:4]
        refs = refs[4:]
    mm_out, gm_out, qm_out, y0_out, bonus_out, g_out, v_out, carry_ref = refs

    @pl.when(pl.program_id(1) == 0)
    def _():
        carry_ref[...] = jnp.zeros_like(carry_ref)

    row0 = lax.broadcasted_iota(jnp.int32, (ts, GROUP), 0) == 0

    def shift_mix(ref, idx):
        s = ref[0]
        prev = jnp.where(row0, carry_ref[idx:idx + 1, :], pltpu.roll(s, 1, axis=0))
        carry_ref[idx:idx + 1, :] = s[ts - 1:ts, :]
        return s + (prev - s) * mu_ref[idx:idx + 1, :]

    r = shift_mix(sr_ref, 0)
    k = shift_mix(sk_ref, 1)
    v = shift_mix(sv_ref, 2)
    low = shift_mix(sl_ref, 3)

    zw = w0_ref[...] + _dot(jnp.tanh(low).astype(BF16), w2_ref[...])
    lw = -math.exp(-0.5) * _sigmoid(zw)
    a = _sigmoid(a0_ref[...] + _dot(low.astype(BF16), a2_ref[...]))
    g_out[0] = _dot(_sigmoid(low).astype(BF16), g2_ref[...])
    if has_vres:
        mix = _dot(_dot(v.astype(BF16), v1_ref[...]).astype(BF16), v2_ref[...])
        v = v + (vf_ref[0] - v) * _sigmoid(v0_ref[...] + mix)
    v_out[0] = v
    kk = k * kk_ref[...]
    ss = _dot_sel(kk * kk, ones_ref[...], split="a")
    kk = kk / jnp.maximum(jnp.sqrt(ss), 1e-12)
    k = k * (1.0 + (a - 1.0) * ka_ref[...])
    bonus_out[0] = _dot_sel(r * k * rk_ref[...], ones_ref[...], split="a") * v
    a_vec, b_vec = -kk, kk * a

    c = chunk
    rr = lax.broadcasted_iota(jnp.int32, (ts, ts), 0)
    cc = lax.broadcasted_iota(jnp.int32, (ts, ts), 1)
    chunk_tri = ((rr >= cc) & (rr // c == cc // c)).astype(F32)
    cum_all = _dot_sel(chunk_tri, lw, split="b")
    strict = _tri(c, strict=True)[None]
    incl = _tri(c)[None]
    eye = (lax.broadcasted_iota(jnp.int32, (1, HEAD_DIM, HEAD_DIM), 1)
           == lax.broadcasted_iota(jnp.int32, (1, HEAD_DIM, HEAD_DIM), 2)).astype(F32)

    nq = ts // c
    per_chunk = lambda t: t.reshape(nq, c, GROUP)
    cum = per_chunk(cum_all)
    rq, kq, vq, aq, bq = (per_chunk(t) for t in (r, k, v, a_vec, b_vec))
    cum_ex = cum - per_chunk(lw)
    mid = cum[:, c // 2 - 1:c // 2, :]
    tot = cum[:, c - 1:c, :]
    e_fwd = jnp.exp(cum - mid)
    e_bwd = jnp.exp(mid - cum)
    e_end = jnp.exp(tot - cum)
    mxu = lambda t: _to_heads(t.astype(BF16))
    r_rel, k_rel = mxu(rq * e_fwd), mxu(kq * e_bwd)
    a_rel, b_rel = mxu(aq * jnp.exp(cum_ex - mid)), mxu(bq * e_bwd)
    a_abs, r_abs = mxu(aq * jnp.exp(cum_ex)), rq * jnp.exp(cum)
    k_end, b_end = mxu(kq * e_end), mxu(bq * e_end)
    gam = _to_heads(jnp.exp(tot))
    vh = mxu(vq)

    ar_rel = jnp.concatenate([a_rel, r_rel], axis=1)
    s_b, s_k = _bnt(ar_rel, b_rel), _bnt(ar_rel, k_rel)
    a_ab = jnp.where(strict, s_b[:, :c], 0.0)
    zero = jnp.zeros((), BF16)
    a_ak = jnp.where(strict, s_k[:, :c].astype(BF16), zero)
    a_rb = jnp.where(incl, s_b[:, c:].astype(BF16), zero)
    a_rk = jnp.where(incl, s_k[:, c:].astype(BF16), zero)
    t_inv = _unit_lower_inverse(a_ab, c)
    pu = _bnn(t_inv, jnp.concatenate([a_abs, _bnn(a_ak, vh).astype(BF16)], axis=-1)).astype(BF16)
    z = _bnn(a_rb, pu)
    y0 = z[..., HEAD_DIM:] + _bnn(a_rk, vh)
    xtb = _btn(pu, b_end)
    mm_mat = eye * gam + xtb[:, :HEAD_DIM]
    gm = xtb[:, HEAD_DIM:] + _btn(vh, k_end)
    dense = lambda t: _from_heads(t).reshape(ts, GROUP)
    mm_out[0] = dense(mm_mat).astype(mm_out.dtype)
    gm_out[0] = dense(gm)
    qm_out[0] = (dense(z[..., :HEAD_DIM]) + r_abs.reshape(ts, GROUP)).astype(qm_out.dtype)
    y0_out[0] = dense(y0)


def rwkv_chunk(proj3, mu4, w0, w2p, a0, a2p, g2p, k_k, k_a, r_k, head_ones, vres, *, ts, chunk):
    b, s, _ = proj3.shape
    slab = lambda off: pl.BlockSpec((1, ts, GROUP), lambda i, j: (i, j, off // GROUP))
    vec = lambda: pl.BlockSpec((1, GROUP), lambda i, j: (0, 0))
    full = lambda shape: pl.BlockSpec(shape, lambda i, j: tuple(0 for _ in shape))
    tok = pl.BlockSpec((1, ts, GROUP), lambda i, j: (i, j, 0))
    in_specs = [slab(RWKV_R), slab(RWKV_K), slab(RWKV_V), slab(RWKV_LR), full((4, GROUP)),
                vec(), full((GROUP, GROUP)), vec(), full((GROUP, GROUP)), full((GROUP, GROUP)), vec(), vec(), vec(),
                full((GROUP, GROUP))]
    args = [proj3, proj3, proj3, proj3, mu4, w0.reshape(1, GROUP), w2p, a0.reshape(1, GROUP), a2p, g2p,
            k_k.reshape(1, GROUP), k_a.reshape(1, GROUP), r_k.reshape(1, GROUP), head_ones]
    if vres is not None:
        v_first, v0, v1p, v2p = vres
        in_specs += [tok, vec(), full((GROUP, 128)), full((128, GROUP))]
        args += [v_first, v0.reshape(1, GROUP), v1p, v2p]
    return pl.pallas_call(
        functools.partial(_rwkv_chunk_kernel, ts=ts, chunk=chunk, has_vres=vres is not None),
        grid=(b, s // ts),
        in_specs=in_specs,
        out_specs=[tok] * 7,
        out_shape=[jax.ShapeDtypeStruct((b, s, GROUP), dt) for dt in (BF16, F32, BF16, F32, F32, F32, F32)],
        scratch_shapes=[pltpu.VMEM((4, GROUP), F32)],
        compiler_params=_cparams(("parallel", "arbitrary")),
        name="rwkv_chunk",
    )(*args)


def _rwkv_state_kernel(mm_ref, gm_ref, qm_ref, y0_ref, bonus_ref, g_ref, gw_ref, gb_ref, o_ref, state_ref, *, chunk):
    @pl.when(pl.program_id(0) == 0)
    def _():
        state_ref[...] = jnp.zeros_like(state_ref)

    state = state_ref[...]
    for c in range(mm_ref.shape[1] // chunk):
        rows = slice(c * chunk, (c + 1) * chunk)
        s_hi, s_lo = _split_bf16(state)
        qm, mm = _to_heads(qm_ref[:, rows, :]), _to_heads(mm_ref[:, rows, :])
        y = _bnt(qm, s_hi) + _bnt(qm, s_lo) + _to_heads(y0_ref[:, rows, :])
        state = _bnn(s_hi, mm) + _bnn(s_lo, mm) + _to_heads(gm_ref[:, rows, :])
        mu = jnp.mean(y, axis=-1, keepdims=True)
        var = jnp.mean(jnp.square(y - mu), axis=-1, keepdims=True)
        yn = _from_heads((y - mu) * lax.rsqrt(var + RWKV_GN_EPS)) * gw_ref[...] + gb_ref[...]
        o_ref[:, rows, :] = ((yn + bonus_ref[:, rows, :]) * g_ref[:, rows, :]).astype(o_ref.dtype)
    state_ref[...] = state


def rwkv_state(mm, gm, qm, y0, bonus, g, gn_w, gn_b, *, ts, chunk):
    bsz, s, _ = mm.shape
    tok = pl.BlockSpec((bsz, ts, GROUP), lambda j: (0, j, 0))
    vec = pl.BlockSpec((1, GROUP), lambda j: (0, 0))
    return pl.pallas_call(
        functools.partial(_rwkv_state_kernel, chunk=chunk),
        grid=(s // ts,),
        in_specs=[tok] * 6 + [vec] * 2,
        out_specs=tok,
        out_shape=jax.ShapeDtypeStruct((bsz, s, GROUP), BF16),
        scratch_shapes=[pltpu.VMEM((bsz * HEADS, HEAD_DIM, HEAD_DIM), F32)],
        compiler_params=_cparams(("arbitrary",)),
        name="rwkv_state",
    )(mm, gm, qm, y0, bonus, g, gn_w.reshape(1, GROUP), gn_b.reshape(1, GROUP))


def _ret_steps(q_ref, k_ref, v_ref, g_ref, cos_ref, sin_ref, dmat_ref, xi_ref, zeta_ref, cd_ref, gw_ref, o_ref,
               state_ref, *, ts, chunk):
    lane = lax.broadcasted_iota(jnp.int32, (ts, GROUP), 1)
    first_half = (lane % HEAD_DIM) < (HEAD_DIM // 2)
    cos, sin = cos_ref[...], sin_ref[...]

    def rotary(t):
        partner = jnp.where(first_half, pltpu.roll(t, GROUP - HEAD_DIM // 2, axis=1),
                            pltpu.roll(t, HEAD_DIM // 2, axis=1))
        return t * cos + partner * sin

    nq = ts // chunk
    per_chunk = lambda t: t.reshape(nq, chunk, GROUP)
    mxu = lambda t: _to_heads(t).astype(BF16)
    q = per_chunk(rotary(q_ref[0]))
    yield
    k = per_chunk(rotary(k_ref[0]) * (HEAD_DIM ** -0.5))
    yield
    qb, kb, vb = mxu(q), mxu(k), mxu(per_chunk(v_ref[0]))
    yield
    q_cross, k_decay = mxu(q * xi_ref[...]), mxu(k * zeta_ref[...])
    yield
    inner = _bnt(qb, kb).reshape(nq, HEADS, chunk, chunk) * dmat_ref[...]
    yield
    intra = _bnn(inner.reshape(nq * HEADS, chunk, chunk).astype(BF16), vb)
    yield
    kv = _btn(k_decay, vb)
    yield

    decay = _to_heads(cd_ref[...][None])
    state = state_ref[...]
    incoming = []
    for c in range(nq):
        incoming.append(state)
        state = state * decay + kv[c * HEADS:(c + 1) * HEADS]
    state_ref[...] = state
    o = intra + _bnn(q_cross, jnp.concatenate(incoming, axis=0).astype(BF16))
    yield
    o = _from_heads(o * lax.rsqrt(jnp.mean(o * o, axis=-1, keepdims=True) + NORM_EPS)).reshape(ts, GROUP)
    g = g_ref[0]
    o_ref[0] = (o * gw_ref[...] * (g * _sigmoid(g))).astype(o_ref.dtype)


def _side_mixers_kernel(*refs, ts, sub, chunk):
    lru_in, ret_in, fox_in = refs[:9], refs[9:20], refs[20:25]
    lru_out, ret_out, qa_ref, ka_ref, vt_ref, buf_ref, h_ref, state_ref, cum_ref = refs[25:]

    @pl.when(pl.program_id(1) == 0)
    def _():
        buf_ref[0:8, :] = jnp.zeros((8, GROUP), F32)
        h_ref[...] = jnp.zeros_like(h_ref)
        state_ref[...] = jnp.zeros_like(state_ref)
        cum_ref[...] = jnp.zeros_like(cum_ref)

    def lru_ranges():
        for start in range(0, ts, sub):
            yield from _lru_steps(slice(start, start + sub), *lru_in, lru_out, buf_ref, h_ref)

    def fox_ranges():
        for tile in range(ts // sub):
            yield from _fox_prep_steps(slice(tile * sub, (tile + 1) * sub), tile, *fox_in, qa_ref, ka_ref, vt_ref,
                                       cum_ref)

    _interleave(_ret_steps(*ret_in, ret_out, state_ref, ts=ts, chunk=chunk), lru_ranges(), fox_ranges())


def side_mixers(proj_fox3, proj3, f_bias_pad, conv_w, conv_b, wra_bd, ra_b, wri_bd, ri_b, lam, cos_t, sin_t, dmat, xi,
                zeta, cd, gn_w, *, ts, sub, chunk):
    b, s, _ = proj3.shape
    slab = lambda off: pl.BlockSpec((1, ts, GROUP), lambda i, j: (i, j, off // GROUP))
    full = lambda shape: pl.BlockSpec(shape, lambda i, j: tuple(0 for _ in shape))
    vec, mat = full((1, GROUP)), full((GROUP, GROUP))
    table = pl.BlockSpec((ts, GROUP), lambda i, j: (j, 0))
    out = pl.BlockSpec((1, ts, GROUP), lambda i, j: (i, j, 0))
    aug = pl.BlockSpec((1, HEADS, ts, FOX_AUG), lambda i, j: (i, 0, j, 0))
    f_width = FOX_WIDTH - FOX_F
    row = lambda t: t.reshape(1, GROUP)
    return pl.pallas_call(
        functools.partial(_side_mixers_kernel, ts=ts, sub=sub, chunk=chunk),
        grid=(b, s // ts),
        in_specs=[slab(LRU_X), slab(LRU_Y), full((CONV_WIDTH, GROUP)), vec, mat, vec, mat, vec, vec,
                  slab(RET_Q), slab(RET_K), slab(RET_V), slab(RET_G), table, table,
                  full((HEADS, chunk, chunk)), full((chunk, GROUP)), full((chunk, GROUP)), vec, vec,
                  slab(FOX_Q), slab(FOX_K), slab(FOX_V),
                  pl.BlockSpec((1, ts, f_width), lambda i, j: (i, j, FOX_F // f_width)), full((1, f_width))],
        out_specs=[out, out, aug, aug,
                   pl.BlockSpec((1, HEADS, ts // sub, FOX_VROWS, sub), lambda i, j: (i, 0, j, 0, 0))],
        out_shape=[jax.ShapeDtypeStruct((b, s, GROUP), BF16)] * 2
                  + [jax.ShapeDtypeStruct((b, HEADS, s, FOX_AUG), BF16)] * 2
                  + [jax.ShapeDtypeStruct((b, HEADS, s // sub, FOX_VROWS, sub), BF16)],
        scratch_shapes=[pltpu.VMEM((sub + 8, GROUP), F32), pltpu.VMEM((1, GROUP), F32),
                        pltpu.VMEM((HEADS, HEAD_DIM, HEAD_DIM), F32), pltpu.VMEM((1, f_width), F32)],
        compiler_params=_cparams(("parallel", "arbitrary")),
        name="side_mixers",
    )(proj3, proj3, conv_w, row(conv_b), wra_bd, row(ra_b), wri_bd, row(ri_b), row(lam),
      proj3, proj3, proj3, proj3, cos_t, sin_t, dmat, xi, zeta, cd, row(gn_w),
      proj_fox3, proj_fox3, proj_fox3, proj_fox3, f_bias_pad)


def _retention_tables(s, chunk):
    half = HEAD_DIM // 2
    inv = 1.0 / (RET_THETA ** jnp.linspace(0.0, 1.0, half, dtype=F32))
    ang = jnp.arange(s, dtype=F32)[:, None] * inv[None, :]
    cos, sin = jnp.cos(ang), jnp.sin(ang)
    cos_t = jnp.tile(jnp.concatenate([cos, cos], axis=-1), (1, HEADS))
    sin_t = jnp.tile(jnp.concatenate([-sin, sin], axis=-1), (1, HEADS))
    lg = jnp.log(1.0 - 2.0 ** (-5.0 - jnp.arange(HEADS, dtype=F32)))
    n = jnp.arange(chunk, dtype=F32)
    diff = n[:, None] - n[None, :]
    dmat = jnp.where(diff >= 0, jnp.exp(lg[:, None, None] * jnp.maximum(diff, 0.0)), 0.0)
    zeta = jnp.exp(lg[:, None] * (chunk - 1.0 - n)[None, :])
    xi = jnp.exp(lg[:, None] * (n + 1.0)[None, :])
    per_lane = lambda t: jnp.repeat(t.T, HEAD_DIM, axis=1)
    cd = jnp.repeat(jnp.exp(lg * chunk), HEAD_DIM)[None, :]
    return cos_t, sin_t, dmat, per_lane(xi), per_lane(zeta), cd


MIX_XATTN_PARTS = 4


def _mix_xattn_steps(rows, m_refs, x_ref, kv_ref, wout_ref, wq_ref, wo_ref, gmix_ref, gpre_ref, gpost_ref, o_ref):
    acc = _dot(m_refs[0][0, rows, :], wout_ref[0:GROUP, :])
    for idx, m_ref in enumerate(m_refs[1:], start=1):
        acc = acc + _dot(m_ref[0, rows, :], wout_ref[idx * GROUP:(idx + 1) * GROUP, :])
    yield
    x = x_ref[0, rows, :] + _rms(acc, gmix_ref[...])
    xn = _rms(x, gpre_ref[...]).astype(BF16)
    q = (_dot(xn, wq_ref[...]) * (XATTN_HEAD_DIM ** -0.5 * LOG2E)).astype(BF16)
    yield
    heads = lambda t, off: jnp.stack([t[:, off + h * XATTN_HEAD_DIM:off + (h + 1) * XATTN_HEAD_DIM]
                                      for h in range(XATTN_HEADS)])
    kv = kv_ref[0]
    s = _bnt(heads(q, 0), heads(kv, 0))
    yield
    e = jnp.exp2(s - jnp.max(s, axis=-1, keepdims=True))
    p = e / jnp.sum(e, axis=-1, keepdims=True)
    o = _bnn(p.astype(BF16), heads(kv, D_MODEL)).astype(BF16)
    yield
    o = jnp.concatenate([o[h] for h in range(XATTN_HEADS)], axis=-1)
    o_ref[0, rows, :] = x + _rms(_dot(o, wo_ref[...]), gpost_ref[...])


def _mix_xattn_kernel(m0_ref, m1_ref, m2_ref, m3_ref, x_ref, kv_ref, wout_ref, wq_ref, wo_ref, gmix_ref, gpre_ref,
                      gpost_ref, o_ref):
    n = x_ref.shape[1] // MIX_XATTN_PARTS
    _interleave(*[_mix_xattn_steps(slice(part * n, (part + 1) * n), (m0_ref, m1_ref, m2_ref, m3_ref), x_ref,
                                   kv_ref, wout_ref, wq_ref, wo_ref, gmix_ref, gpre_ref, gpost_ref, o_ref)
                  for part in range(MIX_XATTN_PARTS)])


def mix_xattn(mixed, x3, kv, w_out, wq, wo, layer, g_mix, g_pre, g_post, *, tm):
    b, s, d = x3.shape
    m = kv.shape[1]
    tok = lambda width: pl.BlockSpec((1, tm, width), lambda bi, i: (bi, i, 0))
    weight = lambda: pl.BlockSpec((None, d, d), lambda bi, i: (layer, 0, 0))
    vec = lambda: pl.BlockSpec((1, d), lambda bi, i: (0, 0))
    return pl.pallas_call(
        _mix_xattn_kernel,
        grid=(b, s // tm),
        in_specs=[tok(GROUP)] * 4 + [tok(d), pl.BlockSpec((1, m, 2 * d), lambda bi, i: (bi, 0, 0)),
                                     weight(), weight(), weight(), vec(), vec(), vec()],
        out_specs=tok(d),
        out_shape=jax.ShapeDtypeStruct((b, s, d), F32),
        compiler_params=_cparams(("parallel", "arbitrary")),
        name="mix_xattn",
    )(*mixed, x3, kv, w_out, wq, wo, g_mix.reshape(1, d), g_pre.reshape(1, d), g_post.reshape(1, d))


def _mlp_steps(rows, x_ref, w1_ref, w2_ref, gpre_ref, gpost_ref, o_ref, ff_tile):
    x = x_ref[rows, :]
    xn = _rms(x, gpre_ref[...]).astype(BF16)
    yield
    d_ff = w1_ref.shape[1]
    acc = None
    for c in range(d_ff // ff_tile):
        hid = jnp.square(jnp.maximum(_dot(xn, w1_ref[:, c * ff_tile:(c + 1) * ff_tile]), 0.0)).astype(BF16)
        yield
        part = _dot(hid, w2_ref[c * ff_tile:(c + 1) * ff_tile, :])
        acc = part if acc is None else acc + part
        yield
    o_ref[rows, :] = x + _rms(acc, gpost_ref[...])


def _mlp_kernel(x_ref, w1_ref, w2_ref, gpre_ref, gpost_ref, o_ref, *, ff_tile):
    half = x_ref.shape[0] // 2
    _interleave(*[_mlp_steps(slice(p * half, (p + 1) * half), x_ref, w1_ref, w2_ref, gpre_ref, gpost_ref, o_ref,
                             ff_tile) for p in range(2)])


def mlp(x, w1, w2, layer, g_pre, g_post, *, tm, ff_tile):
    n, d = x.shape
    d_ff = w1.shape[2]
    return pl.pallas_call(
        functools.partial(_mlp_kernel, ff_tile=ff_tile),
        grid=(n // tm,),
        in_specs=[pl.BlockSpec((tm, d), lambda i: (i, 0)),
                  pl.BlockSpec((None, d, d_ff), lambda i: (layer, 0, 0), pipeline_mode=pl.Buffered(1)),
                  pl.BlockSpec((None, d_ff, d), lambda i: (layer, 0, 0), pipeline_mode=pl.Buffered(1)),
                  pl.BlockSpec((1, d), lambda i: (0, 0)),
                  pl.BlockSpec((1, d), lambda i: (0, 0))],
        out_specs=pl.BlockSpec((tm, d), lambda i: (i, 0)),
        out_shape=jax.ShapeDtypeStruct((n, d), F32),
        compiler_params=_cparams(("parallel",)),
        name="mlp",
    )(x, w1, w2, g_pre.reshape(1, d), g_post.reshape(1, d))


def _block_diag(w):
    h, n, _ = w.shape
    eye = jnp.eye(h, dtype=w.dtype)
    return (eye[:, None, :, None] * w[:, :, None, :]).reshape(h * n, h * n)


def _pad_rows(w, start, total):
    return jnp.zeros((total, w.shape[1]), w.dtype).at[start:start + w.shape[0]].set(w)


def _tile(n, pref):
    return pref if n % pref == 0 else n


def kernel(x, mem, norm_mix_pre, norm_mix_post, norm_xa_pre, norm_xa_post, norm_mem, norm_mlp_pre, norm_mlp_post, w_in, w_out, fox_f_bias, lru_conv_w, lru_conv_b, lru_ra_w, lru_ra_b, lru_ri_w, lru_ri_b, lru_lambda, rwkv_mu, rwkv_w0, rwkv_w2, rwkv_a0, rwkv_a2, rwkv_g2, rwkv_k_k, rwkv_k_a, rwkv_r_k, rwkv_gn_w, rwkv_gn_b, rwkv_v0, rwkv_v1, rwkv_v2, ret_gn_w, xa_wq, xa_wk, xa_wv, xa_wo, mlp_w1, mlp_w2):
    bsz, seq, d = x.shape
    depth = w_in.shape[0]
    n_tok = bsz * seq
    mem_len = mem.shape[1]
    tm = _tile(n_tok, 512)
    tq = _tile(seq, 512)
    ret_tables = _retention_tables(seq, RET_CHUNK)
    head_ones = _block_diag(jnp.ones((HEADS, HEAD_DIM, HEAD_DIM), BF16))

    w_in_t = w_in.astype(BF16).swapaxes(1, 2)
    w_fox_b, w_mix_b = w_in_t[:, :FOX_WIDTH], w_in_t[:, FOX_REAL:]
    w_out_b = w_out.astype(BF16)
    wq_b, wo_b = xa_wq.astype(BF16), xa_wo.astype(BF16)
    wkv_b = jnp.concatenate([xa_wk, xa_wv], axis=-1).astype(BF16)
    w1_b, w2_b = mlp_w1.astype(BF16), mlp_w2.astype(BF16)

    x2 = x.reshape(n_tok, d)
    v_first = None
    for l in range(depth):
        proj_fox, proj_mix = in_proj(x2, norm_mix_pre[l], w_fox_b, w_mix_b, l, tm=_tile(n_tok, 1024),
                                     tn=MIX_WIDTH // 5)
        proj3 = proj_mix.reshape(bsz, seq, MIX_WIDTH)

        f_bias = jnp.zeros((1, FOX_WIDTH - FOX_F), F32).at[0, :HEADS].set(fox_f_bias[l])
        side_ts = _tile(seq, 8 * RET_CHUNK)
        lru_out, ret_out, qa, ka, vt = side_mixers(
            proj_fox.reshape(bsz, seq, FOX_WIDTH), proj3, f_bias, lru_conv_w[l], lru_conv_b[l],
            _block_diag(lru_ra_w[l]).astype(BF16), lru_ra_b[l], _block_diag(lru_ri_w[l]).astype(BF16), lru_ri_b[l],
            lru_lambda[l], *ret_tables, ret_gn_w[l], ts=side_ts, sub=_tile(side_ts, tq), chunk=RET_CHUNK)
        fox_out = fox_attention(qa, ka, vt)

        vres = None
        if l > 0:
            vres = (v_first, rwkv_v0[l - 1],
                    jnp.pad(rwkv_v1[l - 1], ((0, 0), (0, 128 - RWKV_V_RANK))).astype(BF16),
                    _pad_rows(rwkv_v2[l - 1], 0, 128).astype(BF16))
        mm_, gm_, qm_, y0_, bonus_, g_, v_ = rwkv_chunk(
            proj3, rwkv_mu[l].reshape(4, GROUP), rwkv_w0[l],
            _pad_rows(rwkv_w2[l], 0, GROUP).astype(BF16), rwkv_a0[l],
            _pad_rows(rwkv_a2[l], RWKV_W_RANK, GROUP).astype(BF16),
            _pad_rows(rwkv_g2[l], RWKV_W_RANK + RWKV_A_RANK, GROUP).astype(BF16),
            rwkv_k_k[l], rwkv_k_a[l], rwkv_r_k[l], head_ones, vres, ts=_tile(seq, 8 * RWKV_CHUNK), chunk=RWKV_CHUNK)
        if l == 0:
            v_first = v_
        rwkv_out = rwkv_state(mm_, gm_, qm_, y0_, bonus_, g_, rwkv_gn_w[l], rwkv_gn_b[l],
                              ts=_tile(seq, 4 * RWKV_CHUNK), chunk=RWKV_CHUNK)

        kv = norm_matmul(mem.reshape(bsz * mem_len, d), norm_mem[l], wkv_b, l, tm=_tile(bsz * mem_len, 512),
                         tn=1024, out_dtype=BF16).reshape(bsz, mem_len, 2 * d)
        x2 = mix_xattn((fox_out, lru_out, rwkv_out, ret_out), x2.reshape(bsz, seq, d), kv, w_out_b, wq_b, wo_b, l,
                       norm_mix_post[l], norm_xa_pre[l], norm_xa_post[l], tm=_tile(seq, 1024)).reshape(n_tok, d)

        x2 = mlp(x2, w1_b, w2_b, l, norm_mlp_pre[l], norm_mlp_post[l], tm=_tile(n_tok, 1024), ff_tile=1024)
    return x2.reshape(bsz, seq, d)
```

```python
import functools
import math

import jax
import jax.numpy as jnp
from jax import lax
from jax.experimental import pallas as pl
from jax.experimental.pallas import tpu as pltpu

F32 = jnp.float32
BF16 = jnp.bfloat16

D_MODEL = 1024
GROUP = 256
HEADS = 4
HEAD_DIM = 64
CONV_WIDTH = 4
LRU_C = 8.0
RET_THETA = 10000.0
RET_CHUNK = 128
RWKV_W_RANK, RWKV_A_RANK, RWKV_G_RANK, RWKV_V_RANK = 64, 64, 128, 32
RWKV_GN_EPS = 64e-5
XATTN_HEADS = 4
XATTN_HEAD_DIM = D_MODEL // XATTN_HEADS
NORM_EPS = 1e-6
NEG_BIG = -1e30

FOX_Q, FOX_K, FOX_V, FOX_F = 0, 256, 512, 768
FOX_REAL = 3 * GROUP + HEADS
FOX_WIDTH = 896
LRU_X, LRU_Y = 0, 256
RWKV_R, RWKV_K, RWKV_V, RWKV_LR = 512, 768, 1024, 1280
RET_Q, RET_K, RET_V, RET_G = 1536, 1792, 2048, 2304
MIX_WIDTH = 2560

VMEM_LIMIT = 56 * 1024 * 1024
RWKV_CHUNK = 64


def _cparams(sem):
    return pltpu.CompilerParams(dimension_semantics=sem, vmem_limit_bytes=VMEM_LIMIT)


def _rms(x, g):
    return x * lax.rsqrt(jnp.mean(x * x, axis=-1, keepdims=True) + NORM_EPS) * g


def _log_sigmoid(x):
    return jnp.minimum(x, 0.0) - jnp.log1p(jnp.exp(-jnp.abs(x)))


def _sigmoid(x):
    return 1.0 / (1.0 + jnp.exp(-x))


def _dot(a, b, **kw):
    return jnp.dot(a, b, preferred_element_type=F32, **kw)


def _dot_nt(a, b, **kw):
    return lax.dot_general(a, b, (((1,), (1,)), ((), ())), preferred_element_type=F32, **kw)


def _split_bf16(x):
    hi = x.astype(BF16)
    return hi, (x - hi.astype(F32)).astype(BF16)


def _interleave(*step_generators):
    live = list(step_generators)
    while live:
        for gen in list(live):
            try:
                next(gen)
            except StopIteration:
                live.remove(gen)


def _bf16_terms(x, n):
    terms = []
    for _ in range(n - 1):
        t = x.astype(BF16)
        terms.append(t)
        x = x - t.astype(F32)
    return terms + [x.astype(BF16)]


def _dot_sel(a, b, *, split, n=3):
    if split == "a":
        parts = [_dot(t, b.astype(BF16)) for t in _bf16_terms(a, n)]
    else:
        parts = [_dot(a.astype(BF16), t) for t in _bf16_terms(b, n)]
    out = parts[-1]
    for p in reversed(parts[:-1]):
        out = out + p
    return out


def _tri(n, strict=False):
    r = lax.broadcasted_iota(jnp.int32, (n, n), 0)
    c = lax.broadcasted_iota(jnp.int32, (n, n), 1)
    return (r > c) if strict else (r >= c)


def _norm_matmul_kernel(x_ref, g_ref, w_ref, o_ref, xn_ref):
    @pl.when(pl.program_id(1) == 0)
    def _():
        xn_ref[...] = _rms(x_ref[...], g_ref[...]).astype(BF16)

    o_ref[...] = _dot(xn_ref[...], w_ref[...]).astype(o_ref.dtype)


def norm_matmul(x, g, w, layer, *, tm, tn, out_dtype):
    n, d = x.shape
    width = w.shape[2]
    return pl.pallas_call(
        _norm_matmul_kernel,
        grid=(n // tm, width // tn),
        in_specs=[pl.BlockSpec((tm, d), lambda i, j: (i, 0)),
                  pl.BlockSpec((1, d), lambda i, j: (0, 0)),
                  pl.BlockSpec((None, d, tn), lambda i, j: (layer, 0, j))],
        out_specs=pl.BlockSpec((tm, tn), lambda i, j: (i, j)),
        out_shape=jax.ShapeDtypeStruct((n, width), out_dtype),
        scratch_shapes=[pltpu.VMEM((tm, d), BF16)],
        compiler_params=_cparams(("parallel", "arbitrary")),
        name="norm_matmul",
    )(x, g.reshape(1, d), w)


def _in_proj_steps(rows, x_ref, g_ref, wf_ref, wm_ref, of_ref, om_ref, tn):
    xn = _rms(x_ref[rows, :], g_ref[...]).astype(BF16)
    yield
    of_ref[rows, :] = _dot_nt(xn, wf_ref[...])
    for c in range(wm_ref.shape[0] // tn):
        yield
        om_ref[rows, c * tn:(c + 1) * tn] = _dot_nt(xn, wm_ref[c * tn:(c + 1) * tn, :])


def _in_proj_kernel(x_ref, g_ref, wf_ref, wm_ref, of_ref, om_ref, *, tn):
    half = x_ref.shape[0] // 2
    _interleave(*[_in_proj_steps(slice(p * half, (p + 1) * half), x_ref, g_ref, wf_ref, wm_ref, of_ref, om_ref, tn)
                  for p in range(2)])


def in_proj(x, g, w_fox, w_mix, layer, *, tm, tn):
    n, d = x.shape
    weight = lambda w: pl.BlockSpec((None, w.shape[1], d), lambda i: (layer, 0, 0), pipeline_mode=pl.Buffered(1))
    out = lambda w: pl.BlockSpec((tm, w.shape[1]), lambda i: (i, 0))
    return pl.pallas_call(
        functools.partial(_in_proj_kernel, tn=tn),
        grid=(n // tm,),
        in_specs=[pl.BlockSpec((tm, d), lambda i: (i, 0)), pl.BlockSpec((1, d), lambda i: (0, 0)),
                  weight(w_fox), weight(w_mix)],
        out_specs=[out(w_fox), out(w_mix)],
        out_shape=[jax.ShapeDtypeStruct((n, w.shape[1]), F32) for w in (w_fox, w_mix)],
        compiler_params=_cparams(("parallel",)),
        name="in_proj",
    )(x, g.reshape(1, d), w_fox, w_mix)


FOX_AUG = 128
FOX_VROWS = HEAD_DIM + 16
LOG2E = 1.4426950408889634


def _fox_prep_steps(rows, tile, q_ref, k_ref, v_ref, f_ref, b_ref, qa_ref, ka_ref, vt_ref, carry_ref):
    lf = _log_sigmoid(f_ref[0, rows, :] + b_ref[...])
    tc = lf.shape[0]
    cum = _dot_sel(_tri(tc), lf, split="b") + carry_ref[...]
    carry_ref[...] = cum[tc - 1:tc, :]
    yield

    lane = lax.broadcasted_iota(jnp.int32, (tc, FOX_AUG), 1)
    feat = lane < HEAD_DIM
    ones = jnp.where((lane >= HEAD_DIM) & (lane < HEAD_DIM + 3), 1.0, 0.0)
    q = q_ref[0, rows, :] * (HEAD_DIM ** -0.5 * LOG2E)
    k = k_ref[0, rows, :]
    for h in range(HEADS):
        pair = slice((h // 2) * FOX_AUG, (h // 2 + 1) * FOX_AUG)
        qt, kt = q[:, pair], k[:, pair]
        if h % 2:
            qt, kt = pltpu.roll(qt, HEAD_DIM, axis=1), pltpu.roll(kt, HEAD_DIM, axis=1)
        neg_c = jnp.broadcast_to(cum[:, h:h + 1] * -LOG2E, (tc, FOX_AUG))
        c_hi = neg_c.astype(BF16).astype(F32)
        rest = neg_c - c_hi
        c_mid = rest.astype(BF16).astype(F32)
        bias = jnp.where(lane == HEAD_DIM, c_hi, jnp.where(lane == HEAD_DIM + 1, c_mid,
                                                           jnp.where(lane == HEAD_DIM + 2, rest - c_mid, 0.0)))
        qa_ref[0, h, rows, :] = jnp.where(feat, qt, ones).astype(BF16)
        ka_ref[0, h, rows, :] = jnp.where(feat, kt, bias).astype(BF16)
        yield
    ones_row = lax.broadcasted_iota(jnp.int32, (HEADS, FOX_VROWS - HEAD_DIM, tc), 1) == 0
    vt = jnp.concatenate([v_ref[0, rows, :].T.reshape(HEADS, HEAD_DIM, tc), ones_row.astype(F32)], axis=1)
    vt_ref[0, :, tile] = vt.astype(BF16)


def _fox_attn_kernel(q_ref, k_ref, v_ref, o_ref, s_ref, *, t):
    i = pl.program_id(1)
    qa = q_ref[0]

    def scores(j):
        return _bnt(k_ref[0, :, pl.ds(pl.multiple_of(j * t, t), t), :], qa)

    def update(carry, slot, j):
        m, acc = carry
        s = s_ref[slot]
        m_new = jnp.maximum(m, jnp.max(s, axis=1, keepdims=True))
        p = jnp.exp2(s - m_new).astype(BF16)
        return m_new, jnp.exp2(m - m_new) * acc + _bnn(v_ref[0, :, j], p)

    h = t // 2
    k_diag = k_ref[0, :, pl.ds(pl.multiple_of(i * t, t), t), :]
    v_diag = v_ref[0, :, i]
    tri = (lax.broadcasted_iota(jnp.int32, (1, h, h), 1) <= lax.broadcasted_iota(jnp.int32, (1, h, h), 2))
    s_q = jnp.where(tri, _bnt(k_diag[:, h:], qa[:, h:]), NEG_BIG)
    m_q = jnp.max(s_q, axis=1, keepdims=True)
    acc_q = _bnn(v_diag[:, :, h:], jnp.exp2(s_q - m_q).astype(BF16))
    carry = (jnp.concatenate([jnp.full((HEADS, 1, h), NEG_BIG, F32), m_q], axis=-1),
             jnp.concatenate([jnp.zeros((HEADS, FOX_VROWS, h), F32), acc_q], axis=-1))
    key = lax.broadcasted_iota(jnp.int32, (1, h, t), 1)
    qry = lax.broadcasted_iota(jnp.int32, (1, h, t), 2)
    s_ref[0, :, :h, :] = jnp.where(key <= qry, _bnt(k_diag[:, :h], qa), NEG_BIG)

    def update_half_diag(carry):
        m, acc = carry
        s = s_ref[0, :, :h, :]
        m_new = jnp.maximum(m, jnp.max(s, axis=1, keepdims=True))
        p = jnp.exp2(s - m_new).astype(BF16)
        return m_new, jnp.exp2(m - m_new) * acc + _bnn(v_diag[:, :, :h], p)

    def earlier_tiles(carry):
        s_ref[1] = scores(0)
        carry = update_half_diag(carry)

        def pair(jj, carry):
            s_ref[0] = scores(2 * jj + 1)
            carry = update(carry, 1, 2 * jj)
            s_ref[1] = scores(2 * jj + 2)
            return update(carry, 0, 2 * jj + 1)

        carry = lax.fori_loop(0, (i - 1) // 2, pair, carry)
        pending = 2 * ((i - 1) // 2)

        def one_more(carry):
            s_ref[0] = scores(i - 1)
            return update(update(carry, 1, pending), 0, i - 1)

        return lax.cond((i - 1) % 2 == 1, one_more, lambda c: update(c, 1, pending), carry)

    _, acc = lax.cond(i > 0, earlier_tiles, update_half_diag, carry)
    out = acc[:, :HEAD_DIM] / acc[:, HEAD_DIM:HEAD_DIM + 1]
    o_ref[0] = out.reshape(GROUP, t).T.astype(o_ref.dtype)


def fox_attention(qa, ka, vt):
    b, h, s, _ = qa.shape
    nk, t = vt.shape[2], vt.shape[4]
    return pl.pallas_call(
        functools.partial(_fox_attn_kernel, t=t),
        grid=(b, nk),
        in_specs=[pl.BlockSpec((1, h, t, FOX_AUG), lambda bi, i: (bi, 0, i, 0)),
                  pl.BlockSpec((1, h, s, FOX_AUG), lambda bi, i: (bi, 0, 0, 0)),
                  pl.BlockSpec((1, h, nk, FOX_VROWS, t), lambda bi, i: (bi, 0, 0, 0, 0))],
        out_specs=pl.BlockSpec((1, t, GROUP), lambda bi, i: (bi, i, 0)),
        out_shape=jax.ShapeDtypeStruct((b, s, GROUP), BF16),
        scratch_shapes=[pltpu.VMEM((2, h, t, t), F32)],
        compiler_params=_cparams(("parallel", "arbitrary")),
        name="fox_attention",
    )(qa, ka, vt)


def _lru_steps(rows, x_ref, y_ref, cw_ref, cb_ref, wra_ref, bra_ref, wri_ref, bri_ref, lam_ref, o_ref, buf_ref,
               h_ref):
    ts = rows.stop - rows.start
    xb = x_ref[0, rows, :]
    buf_ref[8:8 + ts, :] = xb
    xc = cb_ref[...] + buf_ref[5:5 + ts, :] * cw_ref[0:1, :]
    for j in range(1, CONV_WIDTH):
        xc = xc + buf_ref[5 + j:5 + j + ts, :] * cw_ref[j:j + 1, :]
    buf_ref[0:8, :] = xb[ts - 8:ts, :]
    yield

    xcb = xc.astype(BF16)
    r = _sigmoid(_dot(xcb, wra_ref[...]) + bra_ref[...])
    yield
    gate_i = _sigmoid(_dot(xcb, wri_ref[...]) + bri_ref[...])
    yield
    log_a = LRU_C * r * _log_sigmoid(lam_ref[...])
    a = jnp.exp(log_a)
    z = 2.0 * log_a
    mult = jnp.sqrt(jnp.maximum(-jnp.tanh(0.5 * z) * (jnp.exp(z) + 1.0), 0.0))
    u = mult * (gate_i * xc)
    yield

    row = lax.broadcasted_iota(jnp.int32, (ts, GROUP), 0)
    pa, pb = a, u
    d = 1
    while d < ts:
        sa = pltpu.roll(pa, d, axis=0)
        sb = pltpu.roll(pb, d, axis=0)
        valid = row >= d
        pb = jnp.where(valid, pa * sb + pb, pb)
        pa = jnp.where(valid, pa * sa, pa)
        d *= 2
        yield
    hseq = pa * h_ref[...] + pb
    h_ref[...] = hseq[ts - 1:ts, :]

    y = y_ref[0, rows, :]
    gelu = 0.5 * y * (1.0 + jnp.tanh(math.sqrt(2.0 / math.pi) * (y + 0.044715 * (y * y * y))))
    o_ref[0, rows, :] = (hseq * gelu).astype(o_ref.dtype)


def _bdot(dims):
    return lambda a, b: lax.dot_general(a, b, (dims, ((0,), (0,))), preferred_element_type=F32)


_bnn = _bdot(((2,), (1,)))
_bnt = _bdot(((2,), (2,)))
_btn = _bdot(((1,), (1,)))


def _unit_lower_inverse(a_strict, n):
    r = lax.broadcasted_iota(jnp.int32, (1, n, n), 1)
    c = lax.broadcasted_iota(jnp.int32, (1, n, n), 2)
    a_b = a_strict.astype(BF16)
    zero, one = jnp.zeros((), BF16), jnp.ones((), BF16)
    t = jnp.where(r == c, one, jnp.where((r // 2 == c // 2) & (r > c), a_b, zero))
    m = 2
    while m < n:
        off = (r // (2 * m) == c // (2 * m)) & (r % (2 * m) >= m) & (c % (2 * m) < m)
        t = t + _bnn(t, _bnn(jnp.where(off, a_b, zero), t).astype(BF16)).astype(BF16)
        m *= 2
    return t


def _to_heads(x):
    n, rows, _ = x.shape
    parts = [x[:, :, h * HEAD_DIM:(h + 1) * HEAD_DIM] for h in range(HEADS)]
    return jnp.stack(parts, axis=1).reshape(n * HEADS, rows, HEAD_DIM)


def _from_heads(x):
    nh, rows, _ = x.shape
    x = x.reshape(nh // HEADS, HEADS, rows, HEAD_DIM)
    return jnp.concatenate([x[:, h] for h in range(HEADS)], axis=-1)


def _rwkv_chunk_kernel(*refs, ts, chunk, has_vres):
    (sr_ref, sk_ref, sv_ref, sl_ref, mu_ref, w0_ref, w2_ref, a0_ref, a2_ref, g2_ref, kk_ref, ka_ref, rk_ref,
     ones_ref) = refs[:14]
    refs = refs[14:]
    if has_vres:
        vf_ref, v0_ref, v1_ref, v2_ref = refs[:4]
        refs = refs[4:]
    mm_out, gm_out, qm_out, y0_out, bonus_out, g_out, v_out, carry_ref = refs

    @pl.when(pl.program_id(1) == 0)
    def _():
        carry_ref[...] = jnp.zeros_like(carry_ref)

    row0 = lax.broadcasted_iota(jnp.int32, (ts, GROUP), 0) == 0

    def shift_mix(ref, idx):
        s = ref[0]
        prev = jnp.where(row0, carry_ref[idx:idx + 1, :], pltpu.roll(s, 1, axis=0))
        carry_ref[idx:idx + 1, :] = s[ts - 1:ts, :]
        return s + (prev - s) * mu_ref[idx:idx + 1, :]

    r = shift_mix(sr_ref, 0)
    k = shift_mix(sk_ref, 1)
    v = shift_mix(sv_ref, 2)
    low = shift_mix(sl_ref, 3)

    zw = w0_ref[...] + _dot(jnp.tanh(low).astype(BF16), w2_ref[...])
    lw = -math.exp(-0.5) * _sigmoid(zw)
    a = _sigmoid(a0_ref[...] + _dot(low.astype(BF16), a2_ref[...]))
    g_out[0] = _dot(_sigmoid(low).astype(BF16), g2_ref[...])
    if has_vres:
        mix = _dot(_dot(v.astype(BF16), v1_ref[...]).astype(BF16), v2_ref[...])
        v = v + (vf_ref[0] - v) * _sigmoid(v0_ref[...] + mix)
    v_out[0] = v
    kk = k * kk_ref[...]
    ss = _dot_sel(kk * kk, ones_ref[...], split="a")
    kk = kk / jnp.maximum(jnp.sqrt(ss), 1e-12)
    k = k * (1.0 + (a - 1.0) * ka_ref[...])
    bonus_out[0] = _dot_sel(r * k * rk_ref[...], ones_ref[...], split="a") * v
    a_vec, b_vec = -kk, kk * a

    c = chunk
    rr = lax.broadcasted_iota(jnp.int32, (ts, ts), 0)
    cc = lax.broadcasted_iota(jnp.int32, (ts, ts), 1)
    chunk_tri = ((rr >= cc) & (rr // c == cc // c)).astype(F32)
    cum_all = _dot_sel(chunk_tri, lw, split="b")
    strict = _tri(c, strict=True)[None]
    incl = _tri(c)[None]
    eye = (lax.broadcasted_iota(jnp.int32, (1, HEAD_DIM, HEAD_DIM), 1)
           == lax.broadcasted_iota(jnp.int32, (1, HEAD_DIM, HEAD_DIM), 2)).astype(F32)

    nq = ts // c
    per_chunk = lambda t: t.reshape(nq, c, GROUP)
    cum = per_chunk(cum_all)
    rq, kq, vq, aq, bq = (per_chunk(t) for t in (r, k, v, a_vec, b_vec))
    cum_ex = cum - per_chunk(lw)
    mid = cum[:, c // 2 - 1:c // 2, :]
    tot = cum[:, c - 1:c, :]
    e_fwd = jnp.exp(cum - mid)
    e_bwd = jnp.exp(mid - cum)
    e_end = jnp.exp(tot - cum)
    mxu = lambda t: _to_heads(t.astype(BF16))
    r_rel, k_rel = mxu(rq * e_fwd), mxu(kq * e_bwd)
    a_rel, b_rel = mxu(aq * jnp.exp(cum_ex - mid)), mxu(bq * e_bwd)
    a_abs, r_abs = mxu(aq * jnp.exp(cum_ex)), rq * jnp.exp(cum)
    k_end, b_end = mxu(kq * e_end), mxu(bq * e_end)
    gam = _to_heads(jnp.exp(tot))
    vh = mxu(vq)

    ar_rel = jnp.concatenate([a_rel, r_rel], axis=1)
    s_b, s_k = _bnt(ar_rel, b_rel), _bnt(ar_rel, k_rel)
    a_ab = jnp.where(strict, s_b[:, :c], 0.0)
    zero = jnp.zeros((), BF16)
    a_ak = jnp.where(strict, s_k[:, :c].astype(BF16), zero)
    a_rb = jnp.where(incl, s_b[:, c:].astype(BF16), zero)
    a_rk = jnp.where(incl, s_k[:, c:].astype(BF16), zero)
    t_inv = _unit_lower_inverse(a_ab, c)
    pu = _bnn(t_inv, jnp.concatenate([a_abs, _bnn(a_ak, vh).astype(BF16)], axis=-1)).astype(BF16)
    z = _bnn(a_rb, pu)
    y0 = z[..., HEAD_DIM:] + _bnn(a_rk, vh)
    xtb = _btn(pu, b_end)
    mm_mat = eye * gam + xtb[:, :HEAD_DIM]
    gm = xtb[:, HEAD_DIM:] + _btn(vh, k_end)
    dense = lambda t: _from_heads(t).reshape(ts, GROUP)
    mm_out[0] = dense(mm_mat).astype(mm_out.dtype)
    gm_out[0] = dense(gm)
    qm_out[0] = (dense(z[..., :HEAD_DIM]) + r_abs.reshape(ts, GROUP)).astype(qm_out.dtype)
    y0_out[0] = dense(y0)


def rwkv_chunk(proj3, mu4, w0, w2p, a0, a2p, g2p, k_k, k_a, r_k, head_ones, vres, *, ts, chunk):
    b, s, _ = proj3.shape
    slab = lambda off: pl.BlockSpec((1, ts, GROUP), lambda i, j: (i, j, off // GROUP))
    vec = lambda: pl.BlockSpec((1, GROUP), lambda i, j: (0, 0))
    full = lambda shape: pl.BlockSpec(shape, lambda i, j: tuple(0 for _ in shape))
    tok = pl.BlockSpec((1, ts, GROUP), lambda i, j: (i, j, 0))
    in_specs = [slab(RWKV_R), slab(RWKV_K), slab(RWKV_V), slab(RWKV_LR), full((4, GROUP)),
                vec(), full((GROUP, GROUP)), vec(), full((GROUP, GROUP)), full((GROUP, GROUP)), vec(), vec(), vec(),
                full((GROUP, GROUP))]
    args = [proj3, proj3, proj3, proj3, mu4, w0.reshape(1, GROUP), w2p, a0.reshape(1, GROUP), a2p, g2p,
            k_k.reshape(1, GROUP), k_a.reshape(1, GROUP), r_k.reshape(1, GROUP), head_ones]
    if vres is not None:
        v_first, v0, v1p, v2p = vres
        in_specs += [tok, vec(), full((GROUP, 128)), full((128, GROUP))]
        args += [v_first, v0.reshape(1, GROUP), v1p, v2p]
    return pl.pallas_call(
        functools.partial(_rwkv_chunk_kernel, ts=ts, chunk=chunk, has_vres=vres is not None),
        grid=(b, s // ts),
        in_specs=in_specs,
        out_specs=[tok] * 7,
        out_shape=[jax.ShapeDtypeStruct((b, s, GROUP), dt) for dt in (BF16, F32, BF16, F32, F32, F32, F32)],
        scratch_shapes=[pltpu.VMEM((4, GROUP), F32)],
        compiler_params=_cparams(("parallel", "arbitrary")),
        name="rwkv_chunk",
    )(*args)


def _rwkv_state_kernel(mm_ref, gm_ref, qm_ref, y0_ref, bonus_ref, g_ref, gw_ref, gb_ref, o_ref, state_ref, *, chunk):
    @pl.when(pl.program_id(0) == 0)
    def _():
        state_ref[...] = jnp.zeros_like(state_ref)

    state = state_ref[...]
    for c in range(mm_ref.shape[1] // chunk):
        rows = slice(c * chunk, (c + 1) * chunk)
        s_hi, s_lo = _split_bf16(state)
        qm, mm = _to_heads(qm_ref[:, rows, :]), _to_heads(mm_ref[:, rows, :])
        y = _bnt(qm, s_hi) + _bnt(qm, s_lo) + _to_heads(y0_ref[:, rows, :])
        state = _bnn(s_hi, mm) + _bnn(s_lo, mm) + _to_heads(gm_ref[:, rows, :])
        mu = jnp.mean(y, axis=-1, keepdims=True)
        var = jnp.mean(jnp.square(y - mu), axis=-1, keepdims=True)
        yn = _from_heads((y - mu) * lax.rsqrt(var + RWKV_GN_EPS)) * gw_ref[...] + gb_ref[...]
        o_ref[:, rows, :] = ((yn + bonus_ref[:, rows, :]) * g_ref[:, rows, :]).astype(o_ref.dtype)
    state_ref[...] = state


def rwkv_state(mm, gm, qm, y0, bonus, g, gn_w, gn_b, *, ts, chunk):
    bsz, s, _ = mm.shape
    tok = pl.BlockSpec((bsz, ts, GROUP), lambda j: (0, j, 0))
    vec = pl.BlockSpec((1, GROUP), lambda j: (0, 0))
    return pl.pallas_call(
        functools.partial(_rwkv_state_kernel, chunk=chunk),
        grid=(s // ts,),
        in_specs=[tok] * 6 + [vec] * 2,
        out_specs=tok,
        out_shape=jax.ShapeDtypeStruct((bsz, s, GROUP), BF16),
        scratch_shapes=[pltpu.VMEM((bsz * HEADS, HEAD_DIM, HEAD_DIM), F32)],
        compiler_params=_cparams(("arbitrary",)),
        name="rwkv_state",
    )(mm, gm, qm, y0, bonus, g, gn_w.reshape(1, GROUP), gn_b.reshape(1, GROUP))


def _ret_steps(q_ref, k_ref, v_ref, g_ref, cos_ref, sin_ref, dmat_ref, xi_ref, zeta_ref, cd_ref, gw_ref, o_ref,
               state_ref, *, ts, chunk):
    lane = lax.broadcasted_iota(jnp.int32, (ts, GROUP), 1)
    first_half = (lane % HEAD_DIM) < (HEAD_DIM // 2)
    cos, sin = cos_ref[...], sin_ref[...]

    def rotary(t):
        partner = jnp.where(first_half, pltpu.roll(t, GROUP - HEAD_DIM // 2, axis=1),
                            pltpu.roll(t, HEAD_DIM // 2, axis=1))
        return t * cos + partner * sin

    nq = ts // chunk
    per_chunk = lambda t: t.reshape(nq, chunk, GROUP)
    mxu = lambda t: _to_heads(t).astype(BF16)
    q = per_chunk(rotary(q_ref[0]))
    yield
    k = per_chunk(rotary(k_ref[0]) * (HEAD_DIM ** -0.5))
    yield
    qb, kb, vb = mxu(q), mxu(k), mxu(per_chunk(v_ref[0]))
    yield
    q_cross, k_decay = mxu(q * xi_ref[...]), mxu(k * zeta_ref[...])
    yield
    inner = _bnt(qb, kb).reshape(nq, HEADS, chunk, chunk) * dmat_ref[...]
    yield
    intra = _bnn(inner.reshape(nq * HEADS, chunk, chunk).astype(BF16), vb)
    yield
    kv = _btn(k_decay, vb)
    yield

    decay = _to_heads(cd_ref[...][None])
    state = state_ref[...]
    incoming = []
    for c in range(nq):
        incoming.append(state)
        state = state * decay + kv[c * HEADS:(c + 1) * HEADS]
    state_ref[...] = state
    o = intra + _bnn(q_cross, jnp.concatenate(incoming, axis=0).astype(BF16))
    yield
    o = _from_heads(o * lax.rsqrt(jnp.mean(o * o, axis=-1, keepdims=True) + NORM_EPS)).reshape(ts, GROUP)
    g = g_ref[0]
    o_ref[0] = (o * gw_ref[...] * (g * _sigmoid(g))).astype(o_ref.dtype)


def _side_mixers_kernel(*refs, ts, sub, chunk):
    lru_in, ret_in, fox_in = refs[:9], refs[9:20], refs[20:25]
    lru_out, ret_out, qa_ref, ka_ref, vt_ref, buf_ref, h_ref, state_ref, cum_ref = refs[25:]

    @pl.when(pl.program_id(1) == 0)
    def _():
        buf_ref[0:8, :] = jnp.zeros((8, GROUP), F32)
        h_ref[...] = jnp.zeros_like(h_ref)
        state_ref[...] = jnp.zeros_like(state_ref)
        cum_ref[...] = jnp.zeros_like(cum_ref)

    def lru_ranges():
        for start in range(0, ts, sub):
            yield from _lru_steps(slice(start, start + sub), *lru_in, lru_out, buf_ref, h_ref)

    def fox_ranges():
        for tile in range(ts // sub):
            yield from _fox_prep_steps(slice(tile * sub, (tile + 1) * sub), tile, *fox_in, qa_ref, ka_ref, vt_ref,
                                       cum_ref)

    _interleave(_ret_steps(*ret_in, ret_out, state_ref, ts=ts, chunk=chunk), lru_ranges(), fox_ranges())


def side_mixers(proj_fox3, proj3, f_bias_pad, conv_w, conv_b, wra_bd, ra_b, wri_bd, ri_b, lam, cos_t, sin_t, dmat, xi,
                zeta, cd, gn_w, *, ts, sub, chunk):
    b, s, _ = proj3.shape
    slab = lambda off: pl.BlockSpec((1, ts, GROUP), lambda i, j: (i, j, off // GROUP))
    full = lambda shape: pl.BlockSpec(shape, lambda i, j: tuple(0 for _ in shape))
    vec, mat = full((1, GROUP)), full((GROUP, GROUP))
    table = pl.BlockSpec((ts, GROUP), lambda i, j: (j, 0))
    out = pl.BlockSpec((1, ts, GROUP), lambda i, j: (i, j, 0))
    aug = pl.BlockSpec((1, HEADS, ts, FOX_AUG), lambda i, j: (i, 0, j, 0))
    f_width = FOX_WIDTH - FOX_F
    row = lambda t: t.reshape(1, GROUP)
    return pl.pallas_call(
        functools.partial(_side_mixers_kernel, ts=ts, sub=sub, chunk=chunk),
        grid=(b, s // ts),
        in_specs=[slab(LRU_X), slab(LRU_Y), full((CONV_WIDTH, GROUP)), vec, mat, vec, mat, vec, vec,
                  slab(RET_Q), slab(RET_K), slab(RET_V), slab(RET_G), table, table,
                  full((HEADS, chunk, chunk)), full((chunk, GROUP)), full((chunk, GROUP)), vec, vec,
                  slab(FOX_Q), slab(FOX_K), slab(FOX_V),
                  pl.BlockSpec((1, ts, f_width), lambda i, j: (i, j, FOX_F // f_width)), full((1, f_width))],
        out_specs=[out, out, aug, aug,
                   pl.BlockSpec((1, HEADS, ts // sub, FOX_VROWS, sub), lambda i, j: (i, 0, j, 0, 0))],
        out_shape=[jax.ShapeDtypeStruct((b, s, GROUP), BF16)] * 2
                  + [jax.ShapeDtypeStruct((b, HEADS, s, FOX_AUG), BF16)] * 2
                  + [jax.ShapeDtypeStruct((b, HEADS, s // sub, FOX_VROWS, sub), BF16)],
        scratch_shapes=[pltpu.VMEM((sub + 8, GROUP), F32), pltpu.VMEM((1, GROUP), F32),
                        pltpu.VMEM((HEADS, HEAD_DIM, HEAD_DIM), F32), pltpu.VMEM((1, f_width), F32)],
        compiler_params=_cparams(("parallel", "arbitrary")),
        name="side_mixers",
    )(proj3, proj3, conv_w, row(conv_b), wra_bd, row(ra_b), wri_bd, row(ri_b), row(lam),
      proj3, proj3, proj3, proj3, cos_t, sin_t, dmat, xi, zeta, cd, row(gn_w),
      proj_fox3, proj_fox3, proj_fox3, proj_fox3, f_bias_pad)


def _retention_tables(s, chunk):
    half = HEAD_DIM // 2
    inv = 1.0 / (RET_THETA ** jnp.linspace(0.0, 1.0, half, dtype=F32))
    ang = jnp.arange(s, dtype=F32)[:, None] * inv[None, :]
    cos, sin = jnp.cos(ang), jnp.sin(ang)
    cos_t = jnp.tile(jnp.concatenate([cos, cos], axis=-1), (1, HEADS))
    sin_t = jnp.tile(jnp.concatenate([-sin, sin], axis=-1), (1, HEADS))
    lg = jnp.log(1.0 - 2.0 ** (-5.0 - jnp.arange(HEADS, dtype=F32)))
    n = jnp.arange(chunk, dtype=F32)
    diff = n[:, None] - n[None, :]
    dmat = jnp.where(diff >= 0, jnp.exp(lg[:, None, None] * jnp.maximum(diff, 0.0)), 0.0)
    zeta = jnp.exp(lg[:, None] * (chunk - 1.0 - n)[None, :])
    xi = jnp.exp(lg[:, None] * (n + 1.0)[None, :])
    per_lane = lambda t: jnp.repeat(t.T, HEAD_DIM, axis=1)
    cd = jnp.repeat(jnp.exp(lg * chunk), HEAD_DIM)[None, :]
    return cos_t, sin_t, dmat, per_lane(xi), per_lane(zeta), cd


MIX_XATTN_PARTS = 4


def _mix_xattn_steps(rows, m_refs, x_ref, kv_ref, wout_ref, wq_ref, wo_ref, gmix_ref, gpre_ref, gpost_ref, o_ref):
    acc = _dot(m_refs[0][0, rows, :], wout_ref[0:GROUP, :])
    for idx, m_ref in enumerate(m_refs[1:], start=1):
        acc = acc + _dot(m_ref[0, rows, :], wout_ref[idx * GROUP:(idx + 1) * GROUP, :])
    yield
    x = x_ref[0, rows, :] + _rms(acc, gmix_ref[...])
    xn = _rms(x, gpre_ref[...]).astype(BF16)
    q = (_dot(xn, wq_ref[...]) * (XATTN_HEAD_DIM ** -0.5)).astype(BF16)
    yield
    heads = lambda t, off: jnp.stack([t[:, off + h * XATTN_HEAD_DIM:off + (h + 1) * XATTN_HEAD_DIM]
                                      for h in range(XATTN_HEADS)])
    kv = kv_ref[0]
    s = _bnt(heads(q, 0), heads(kv, 0))
    yield
    e = jnp.exp(s - jnp.max(s, axis=-1, keepdims=True))
    p = e / jnp.sum(e, axis=-1, keepdims=True)
    o = _bnn(p.astype(BF16), heads(kv, D_MODEL)).astype(BF16)
    yield
    o = jnp.concatenate([o[h] for h in range(XATTN_HEADS)], axis=-1)
    o_ref[0, rows, :] = x + _rms(_dot(o, wo_ref[...]), gpost_ref[...])


def _mix_xattn_kernel(m0_ref, m1_ref, m2_ref, m3_ref, x_ref, kv_ref, wout_ref, wq_ref, wo_ref, gmix_ref, gpre_ref,
                      gpost_ref, o_ref):
    n = x_ref.shape[1] // MIX_XATTN_PARTS
    _interleave(*[_mix_xattn_steps(slice(part * n, (part + 1) * n), (m0_ref, m1_ref, m2_ref, m3_ref), x_ref,
                                   kv_ref, wout_ref, wq_ref, wo_ref, gmix_ref, gpre_ref, gpost_ref, o_ref)
                  for part in range(MIX_XATTN_PARTS)])


def mix_xattn(mixed, x3, kv, w_out, wq, wo, layer, g_mix, g_pre, g_post, *, tm):
    b, s, d = x3.shape
    m = kv.shape[1]
    tok = lambda width: pl.BlockSpec((1, tm, width), lambda bi, i: (bi, i, 0))
    weight = lambda: pl.BlockSpec((None, d, d), lambda bi, i: (layer, 0, 0), pipeline_mode=pl.Buffered(1))
    vec = lambda: pl.BlockSpec((1, d), lambda bi, i: (0, 0))
    return pl.pallas_call(
        _mix_xattn_kernel,
        grid=(b, s // tm),
        in_specs=[tok(GROUP)] * 4 + [tok(d), pl.BlockSpec((1, m, 2 * d), lambda bi, i: (bi, 0, 0)),
                                     weight(), weight(), weight(), vec(), vec(), vec()],
        out_specs=tok(d),
        out_shape=jax.ShapeDtypeStruct((b, s, d), F32),
        compiler_params=_cparams(("parallel", "arbitrary")),
        name="mix_xattn",
    )(*mixed, x3, kv, w_out, wq, wo, g_mix.reshape(1, d), g_pre.reshape(1, d), g_post.reshape(1, d))


def _mlp_steps(rows, x_ref, w1_ref, w2_ref, gpre_ref, gpost_ref, o_ref, ff_tile):
    x = x_ref[rows, :]
    xn = _rms(x, gpre_ref[...]).astype(BF16)
    yield
    d_ff = w1_ref.shape[1]
    acc = None
    for c in range(d_ff // ff_tile):
        hid = jnp.square(jnp.maximum(_dot(xn, w1_ref[:, c * ff_tile:(c + 1) * ff_tile]), 0.0)).astype(BF16)
        yield
        part = _dot(hid, w2_ref[c * ff_tile:(c + 1) * ff_tile, :])
        acc = part if acc is None else acc + part
        yield
    o_ref[rows, :] = x + _rms(acc, gpost_ref[...])


def _mlp_kernel(x_ref, w1_ref, w2_ref, gpre_ref, gpost_ref, o_ref, *, ff_tile):
    half = x_ref.shape[0] // 2
    _interleave(*[_mlp_steps(slice(p * half, (p + 1) * half), x_ref, w1_ref, w2_ref, gpre_ref, gpost_ref, o_ref,
                             ff_tile) for p in range(2)])


def mlp(x, w1, w2, layer, g_pre, g_post, *, tm, ff_tile):
    n, d = x.shape
    d_ff = w1.shape[2]
    return pl.pallas_call(
        functools.partial(_mlp_kernel, ff_tile=ff_tile),
        grid=(n // tm,),
        in_specs=[pl.BlockSpec((tm, d), lambda i: (i, 0)),
                  pl.BlockSpec((None, d, d_ff), lambda i: (layer, 0, 0), pipeline_mode=pl.Buffered(1)),
                  pl.BlockSpec((None, d_ff, d), lambda i: (layer, 0, 0), pipeline_mode=pl.Buffered(1)),
                  pl.BlockSpec((1, d), lambda i: (0, 0)),
                  pl.BlockSpec((1, d), lambda i: (0, 0))],
        out_specs=pl.BlockSpec((tm, d), lambda i: (i, 0)),
        out_shape=jax.ShapeDtypeStruct((n, d), F32),
        compiler_params=_cparams(("parallel",)),
        name="mlp",
    )(x, w1, w2, g_pre.reshape(1, d), g_post.reshape(1, d))


def _block_diag(w):
    h, n, _ = w.shape
    eye = jnp.eye(h, dtype=w.dtype)
    return (eye[:, None, :, None] * w[:, :, None, :]).reshape(h * n, h * n)


def _pad_rows(w, start, total):
    return jnp.zeros((total, w.shape[1]), w.dtype).at[start:start + w.shape[0]].set(w)


def _tile(n, pref):
    return pref if n % pref == 0 else n


def kernel(x, mem, norm_mix_pre, norm_mix_post, norm_xa_pre, norm_xa_post, norm_mem, norm_mlp_pre, norm_mlp_post, w_in, w_out, fox_f_bias, lru_conv_w, lru_conv_b, lru_ra_w, lru_ra_b, lru_ri_w, lru_ri_b, lru_lambda, rwkv_mu, rwkv_w0, rwkv_w2, rwkv_a0, rwkv_a2, rwkv_g2, rwkv_k_k, rwkv_k_a, rwkv_r_k, rwkv_gn_w, rwkv_gn_b, rwkv_v0, rwkv_v1, rwkv_v2, ret_gn_w, xa_wq, xa_wk, xa_wv, xa_wo, mlp_w1, mlp_w2):
    bsz, seq, d = x.shape
    depth = w_in.shape[0]
    n_tok = bsz * seq
    mem_len = mem.shape[1]
    tm = _tile(n_tok, 512)
    tq = _tile(seq, 512)
    ret_tables = _retention_tables(seq, RET_CHUNK)
    head_ones = _block_diag(jnp.ones((HEADS, HEAD_DIM, HEAD_DIM), BF16))

    w_in_t = w_in.astype(BF16).swapaxes(1, 2)
    w_fox_b, w_mix_b = w_in_t[:, :FOX_WIDTH], w_in_t[:, FOX_REAL:]
    w_out_b = w_out.astype(BF16)
    wq_b, wo_b = xa_wq.astype(BF16), xa_wo.astype(BF16)
    wkv_b = jnp.concatenate([xa_wk, xa_wv], axis=-1).astype(BF16)
    w1_b, w2_b = mlp_w1.astype(BF16), mlp_w2.astype(BF16)

    x2 = x.reshape(n_tok, d)
    v_first = None
    for l in range(depth):
        proj_fox, proj_mix = in_proj(x2, norm_mix_pre[l], w_fox_b, w_mix_b, l, tm=_tile(n_tok, 1024),
                                     tn=MIX_WIDTH // 5)
        proj3 = proj_mix.reshape(bsz, seq, MIX_WIDTH)

        f_bias = jnp.zeros((1, FOX_WIDTH - FOX_F), F32).at[0, :HEADS].set(fox_f_bias[l])
        side_ts = _tile(seq, 8 * RET_CHUNK)
        lru_out, ret_out, qa, ka, vt = side_mixers(
            proj_fox.reshape(bsz, seq, FOX_WIDTH), proj3, f_bias, lru_conv_w[l], lru_conv_b[l],
            _block_diag(lru_ra_w[l]).astype(BF16), lru_ra_b[l], _block_diag(lru_ri_w[l]).astype(BF16), lru_ri_b[l],
            lru_lambda[l], *ret_tables, ret_gn_w[l], ts=side_ts, sub=_tile(side_ts, tq), chunk=RET_CHUNK)
        fox_out = fox_attention(qa, ka, vt)

        vres = None
        if l > 0:
            vres = (v_first, rwkv_v0[l - 1],
                    jnp.pad(rwkv_v1[l - 1], ((0, 0), (0, 128 - RWKV_V_RANK))).astype(BF16),
                    _pad_rows(rwkv_v2[l - 1], 0, 128).astype(BF16))
        mm_, gm_, qm_, y0_, bonus_, g_, v_ = rwkv_chunk(
            proj3, rwkv_mu[l].reshape(4, GROUP), rwkv_w0[l],
            _pad_rows(rwkv_w2[l], 0, GROUP).astype(BF16), rwkv_a0[l],
            _pad_rows(rwkv_a2[l], RWKV_W_RANK, GROUP).astype(BF16),
            _pad_rows(rwkv_g2[l], RWKV_W_RANK + RWKV_A_RANK, GROUP).astype(BF16),
            rwkv_k_k[l], rwkv_k_a[l], rwkv_r_k[l], head_ones, vres, ts=_tile(seq, 8 * RWKV_CHUNK), chunk=RWKV_CHUNK)
        if l == 0:
            v_first = v_
        rwkv_out = rwkv_state(mm_, gm_, qm_, y0_, bonus_, g_, rwkv_gn_w[l], rwkv_gn_b[l],
                              ts=_tile(seq, 4 * RWKV_CHUNK), chunk=RWKV_CHUNK)

        kv = norm_matmul(mem.reshape(bsz * mem_len, d), norm_mem[l], wkv_b, l, tm=_tile(bsz * mem_len, 512),
                         tn=1024, out_dtype=BF16).reshape(bsz, mem_len, 2 * d)
        x2 = mix_xattn((fox_out, lru_out, rwkv_out, ret_out), x2.reshape(bsz, seq, d), kv, w_out_b, wq_b, wo_b, l,
                       norm_mix_post[l], norm_xa_pre[l], norm_xa_post[l], tm=_tile(seq, 1024)).reshape(n_tok, d)

        x2 = mlp(x2, w1_b, w2_b, l, norm_mlp_pre[l], norm_mlp_post[l], tm=_tile(n_tok, 1024), ff_tile=1024)
    return x2.reshape(bsz, seq, d)
```
